```python
import jax, jax.numpy as jnp
from jax import lax
import numpy as np

D_MODEL = 1024
BATCH = 8
SEQ = 2048
DEPTH = 2

HEAD_DIM = 64
ATTN_HEADS = 8
ATTN_KV_HEADS = 2
ATTN_GROUP = ATTN_HEADS // ATTN_KV_HEADS
WINDOW = 128
ATTN_BLOCK = 128
ROPE_THETA = 10000.0

RET_HEADS = 4
RET_DK = 64
RET_DV = 64
RET_CHUNK = 128

GLA_HEADS = 4
GLA_DK = 32
GLA_DV = 64
GLA_CHUNK = 64
GLA_GATE_RANK = 16
GLA_GATE_NORMALIZER = 16.0

EPS = 1e-6

ATTN_W = ATTN_HEADS * HEAD_DIM
RET_W = RET_HEADS * RET_DV
GLA_W = GLA_HEADS * GLA_DV
D_MIX = ATTN_W + RET_W + GLA_W

IN_SIZES = (
    ATTN_HEADS * HEAD_DIM, ATTN_KV_HEADS * HEAD_DIM, ATTN_KV_HEADS * HEAD_DIM, ATTN_W,
    RET_HEADS * RET_DK, RET_HEADS * RET_DK, RET_HEADS * RET_DV, RET_W,
    GLA_HEADS * GLA_DK, GLA_HEADS * GLA_DK, GLA_HEADS * GLA_DV, GLA_W,
    GLA_GATE_RANK,
)
D_IN = sum(IN_SIZES)
IN_SPLITS = [int(s) for s in np.cumsum(IN_SIZES)[:-1]]

kernel_name = "hymba_style_swa_retention_gla_hybrid"


def rms_norm(x, gain=None):
    xf = x.astype(jnp.float32)
    y = xf * lax.rsqrt(jnp.mean(xf * xf, axis=-1, keepdims=True) + EPS)
    if gain is not None:
        y = y * gain.astype(jnp.float32)
    return y.astype(x.dtype)


def apply_rotary(x, positions, inv_freq):
    ang = positions.astype(jnp.float32)[..., None] * inv_freq
    cos = jnp.cos(ang)[:, :, None, :]
    sin = jnp.sin(ang)[:, :, None, :]
    x1, x2 = jnp.split(x.astype(jnp.float32), 2, axis=-1)
    out = jnp.concatenate([x1 * cos - x2 * sin, x2 * cos + x1 * sin], axis=-1)
    return out.astype(x.dtype)


def sliding_window_attention(q, k, v, sinks):
    B, S, H, D = q.shape
    nb = S // ATTN_BLOCK
    qb = q.reshape(B, nb, ATTN_BLOCK, ATTN_KV_HEADS, ATTN_GROUP, D)
    kb = k.reshape(B, nb, ATTN_BLOCK, ATTN_KV_HEADS, D)
    vb = v.reshape(B, nb, ATTN_BLOCK, ATTN_KV_HEADS, D)

    def with_prev(t):
        prev = jnp.pad(t, ((0, 0), (1, 0), (0, 0), (0, 0), (0, 0)))[:, :-1]
        return jnp.concatenate([prev, t], axis=2)

    kk, vv = with_prev(kb), with_prev(vb)
    s = jnp.einsum('bnqhgd,bnkhd->bnhgqk', qb, kk).astype(jnp.float32) * (D ** -0.5)
    i = jnp.arange(ATTN_BLOCK)[:, None]
    j = jnp.arange(2 * ATTN_BLOCK)[None, :]
    rel = i + ATTN_BLOCK - j
    kpos = (jnp.arange(nb)[:, None] - 1) * ATTN_BLOCK + jnp.arange(2 * ATTN_BLOCK)[None, :]
    valid = ((rel >= 0) & (rel < WINDOW))[None] & (kpos >= 0)[:, None, :]
    s = jnp.where(valid[None, :, None, None], s, -1e30)
    sink = sinks.astype(jnp.float32).reshape(ATTN_KV_HEADS, ATTN_GROUP)[None, None, :, :, None, None]
    sink = jnp.broadcast_to(sink, s.shape[:-1] + (1,))
    p = jax.nn.softmax(jnp.concatenate([s, sink], axis=-1), axis=-1)[..., :-1]
    o = jnp.einsum('bnhgqk,bnkhd->bnqhgd', p.astype(v.dtype), vv)
    return o.reshape(B, S, H * D)


def retention(q, k, v):
    B, S, H, Dk = q.shape
    Dv = v.shape[-1]
    C = RET_CHUNK
    nc = S // C
    f32 = jnp.float32
    log_g = jnp.log(1.0 - 2.0 ** (-5.0 - jnp.arange(H, dtype=f32)))
    idx = jnp.arange(C, dtype=f32)
    diff = idx[:, None] - idx[None, :]
    dmask = jnp.where(diff >= 0, jnp.exp(log_g[:, None, None] * jnp.maximum(diff, 0.0)), 0.0)
    q_decay = jnp.exp(log_g[:, None] * (idx + 1.0))[..., None]
    k_decay = jnp.exp(log_g[:, None] * (C - 1.0 - idx))[..., None]
    chunk_decay = jnp.exp(log_g * C)[:, None, None]

    def to_chunks(t):
        return t.astype(f32).reshape(B, nc, C, H, t.shape[-1]).transpose(1, 0, 3, 2, 4)

    qc, kc, vc = to_chunks(q), to_chunks(k * (Dk ** -0.5)), to_chunks(v)

    def step(state, inp):
        qi, ki, vi = inp
        sc = jnp.einsum('bhid,bhjd->bhij', qi, ki) * dmask
        intra = jnp.einsum('bhij,bhjv->bhiv', sc, vi)
        inter = jnp.einsum('bhid,bhdv->bhiv', qi * q_decay, state)
        new_state = chunk_decay * state + jnp.einsum('bhjd,bhjv->bhdv', ki * k_decay, vi)
        return new_state, intra + inter

    state0 = jnp.zeros((B, H, Dk, Dv), f32)
    _, out = lax.scan(step, state0, (qc, kc, vc))
    return out.transpose(1, 0, 3, 2, 4).reshape(B, S, H, Dv).astype(v.dtype)


def gated_linear_attention(q, k, v, log_a):
    B, S, H, Dk = q.shape
    Dv = v.shape[-1]
    C = GLA_CHUNK
    nc = S // C
    f32 = jnp.float32

    def to_chunks(t):
        return t.astype(f32).reshape(B, nc, C, H, t.shape[-1]).transpose(1, 0, 3, 2, 4)

    qc, kc, vc, gc = to_chunks(q * (Dk ** -0.5)), to_chunks(k), to_chunks(v), to_chunks(log_a)
    causal = jnp.tril(jnp.ones((C, C), dtype=bool))[..., None]

    def step(state, inp):
        qi, ki, vi, gi = inp
        b = jnp.cumsum(gi, axis=2)
        rel = b[:, :, :, None, :] - b[:, :, None, :, :]
        decay = jnp.where(causal, jnp.exp(jnp.minimum(rel, 0.0)), 0.0)
        sc = jnp.einsum('bhid,bhijd,bhjd->bhij', qi, decay, ki)
        intra = jnp.einsum('bhij,bhjv->bhiv', sc, vi)
        inter = jnp.einsum('bhid,bhdv->bhiv', qi * jnp.exp(b), state)
        b_last = b[:, :, -1:, :]
        new_state = (jnp.exp(b_last)[:, :, 0, :, None] * state
                     + jnp.einsum('bhjd,bhjv->bhdv', ki * jnp.exp(b_last - b), vi))
        return new_state, intra + inter

    state0 = jnp.zeros((B, H, Dk, Dv), f32)
    _, out = lax.scan(step, state0, (qc, kc, vc, gc))
    return out.transpose(1, 0, 3, 2, 4).reshape(B, S, H, Dv).astype(v.dtype)


def hybrid_layer(x, c_act, positions, w_mod, b_mod, pre_gain, post_gain, w_in, sinks,
                 gla_gate_w, gla_gate_b, gla_norm_gain, w_out):
    B, S, _ = x.shape
    mod = c_act @ w_mod + b_mod
    shift, scale, gate = jnp.split(mod[:, None, :], 3, axis=-1)
    h = rms_norm(x, pre_gain) * (1.0 + scale) + shift
    proj = h @ w_in
    (aq, ak, av, ag, rq, rk, rv, rg, gq, gk, gv, gg, ga) = jnp.split(proj, IN_SPLITS, axis=-1)

    rope_freq = ROPE_THETA ** (-jnp.arange(0, HEAD_DIM, 2, dtype=jnp.float32) / HEAD_DIM)
    aq = apply_rotary(aq.reshape(B, S, ATTN_HEADS, HEAD_DIM), positions, rope_freq)
    ak = apply_rotary(ak.reshape(B, S, ATTN_KV_HEADS, HEAD_DIM), positions, rope_freq)
    av = av.reshape(B, S, ATTN_KV_HEADS, HEAD_DIM)
    a_out = sliding_window_attention(aq, ak, av, sinks) * jax.nn.silu(ag)

    ret_freq = 1.0 / (10000.0 ** jnp.linspace(0.0, 1.0, RET_DK // 2, dtype=jnp.float32))
    rq = apply_rotary(rq.reshape(B, S, RET_HEADS, RET_DK), positions, ret_freq)
    rk = apply_rotary(rk.reshape(B, S, RET_HEADS, RET_DK), positions, ret_freq)
    r = retention(rq, rk, rv.reshape(B, S, RET_HEADS, RET_DV))
    r_out = rms_norm(r).reshape(B, S, RET_W) * jax.nn.silu(rg)

    gate_logits = (ga @ gla_gate_w + gla_gate_b).astype(jnp.float32)
    log_a = (jax.nn.log_sigmoid(gate_logits) / GLA_GATE_NORMALIZER).reshape(B, S, GLA_HEADS, GLA_DK)
    g = gated_linear_attention(gq.reshape(B, S, GLA_HEADS, GLA_DK), gk.reshape(B, S, GLA_HEADS, GLA_DK),
                               gv.reshape(B, S, GLA_HEADS, GLA_DV), log_a)
    g_out = rms_norm(g, gla_norm_gain).reshape(B, S, GLA_W) * jax.nn.silu(gg)

    y = jnp.concatenate([a_out, r_out, g_out], axis=-1) @ w_out
    return x + gate * rms_norm(y, post_gain)


def setup_inputs(seed: int = 0) -> dict:
    key = jax.random.key(seed)
    ks = jax.random.split(key, 16)
    f32 = jnp.float32
    x = jax.random.normal(ks[0], (BATCH, SEQ, D_MODEL), f32)
    c = jax.random.normal(ks[1], (BATCH, D_MODEL), f32)
    positions = jnp.broadcast_to(jnp.arange(SEQ, dtype=jnp.int32)[None, :], (BATCH, SEQ))
    w_mod = jax.random.normal(ks[2], (DEPTH, D_MODEL, 3 * D_MODEL), f32) * (0.5 * D_MODEL ** -0.5)
    b_mod = jax.random.normal(ks[3], (DEPTH, 3 * D_MODEL), f32) * 0.01
    pre_norm_gain = 1.0 + 0.02 * jax.random.normal(ks[4], (DEPTH, D_MODEL), f32)
    post_norm_gain = 1.0 + 0.02 * jax.random.normal(ks[5], (DEPTH, D_MODEL), f32)
    w_in = jax.random.normal(ks[6], (DEPTH, D_MODEL, D_IN), f32) * (D_MODEL ** -0.5)
    attn_sinks = 0.5 * jax.random.normal(ks[7], (DEPTH, ATTN_HEADS), f32)
    gla_gate_w = jax.random.normal(ks[8], (DEPTH, GLA_GATE_RANK, GLA_HEADS * GLA_DK), f32) * (GLA_GATE_RANK ** -0.5)
    gla_gate_b = 0.01 * jax.random.normal(ks[9], (DEPTH, GLA_HEADS * GLA_DK), f32)
    gla_norm_gain = 1.0 + 0.02 * jax.random.normal(ks[10], (DEPTH, GLA_DV), f32)
    w_out = jax.random.normal(ks[11], (DEPTH, D_MIX, D_MODEL), f32) * (D_MIX ** -0.5)
    return {"x": x, "c": c, "positions": positions, "w_mod": w_mod, "b_mod": b_mod,
            "pre_norm_gain": pre_norm_gain, "post_norm_gain": post_norm_gain, "w_in": w_in,
            "attn_sinks": attn_sinks, "gla_gate_w": gla_gate_w, "gla_gate_b": gla_gate_b,
            "gla_norm_gain": gla_norm_gain, "w_out": w_out}


def reference(x, c, positions, w_mod, b_mod, pre_norm_gain, post_norm_gain, w_in,
              attn_sinks, gla_gate_w, gla_gate_b, gla_norm_gain, w_out):
    c_act = jax.nn.silu(c)
    for l in range(DEPTH):
        x = hybrid_layer(x, c_act, positions, w_mod[l], b_mod[l], pre_norm_gain[l], post_norm_gain[l],
                         w_in[l], attn_sinks[l], gla_gate_w[l], gla_gate_b[l], gla_norm_gain[l], w_out[l])
    return x
```

```python
import functools

import numpy as np
import jax
import jax.numpy as jnp
from jax import lax
from jax.experimental import pallas as pl
from jax.experimental.pallas import tpu as pltpu

D_MODEL = 1024
HEAD_DIM = 64
ATTN_HEADS = 8
ATTN_KV_HEADS = 2
ATTN_GROUP = ATTN_HEADS // ATTN_KV_HEADS
WINDOW = 128
ROPE_THETA = 10000.0
RET_HEADS = 4
RET_DK = 64
RET_DV = 64
GLA_HEADS = 4
GLA_DK = 32
GLA_DV = 64
GLA_GATE_RANK = 16
GLA_GATE_NORMALIZER = 16.0
EPS = 1e-6

ATTN_W = ATTN_HEADS * HEAD_DIM
RET_W = RET_HEADS * RET_DV
GLA_W = GLA_HEADS * GLA_DV

LANES = 128
BLK = 128
GLA_C = 64
GLA_LEVELS = 6
N_GLA_ROWS = (2 + GLA_LEVELS) * BLK
NEG = -1e30
VMEM_LIMIT = 56 * 1024 * 1024

O_AQ, O_AK, O_AV, O_AG = 0, 512, 640, 768
O_RQ, O_RK, O_RV, O_RG = 1280, 1536, 1792, 2048
O_GQ, O_GK, O_GV, O_GG, O_GA = 2304, 2432, 2560, 2816, 3072
N_PROJ = 3200


def _build_constants():
    f32 = np.float32
    lane = np.arange(LANES)
    part, grp, frq = lane // 64, (lane % 64) // 32, lane % 32

    perm = np.arange(N_PROJ)
    for m in range(4):
        perm[O_AQ + m * 128 + lane] = 0 + (m + 4 * grp) * 64 + part * 32 + frq
    perm[O_AK + lane] = 512 + grp * 64 + part * 32 + frq
    for m in range(4):
        perm[O_AG + m * 128 + lane] = 768 + (m + 4 * (lane // 64)) * 64 + lane % 64
    for p in range(2):
        perm[O_RQ + p * 128 + lane] = 1280 + (2 * p + grp) * 64 + part * 32 + frq
        perm[O_RK + p * 128 + lane] = 1536 + (2 * p + grp) * 64 + part * 32 + frq
    valid = perm < 3088
    perm = np.where(valid, perm, 0)
    colscale = np.ones(N_PROJ, f32)
    colscale[O_AQ:O_AQ + 512] = HEAD_DIM ** -0.5
    colscale[O_RK:O_RK + 256] = RET_DK ** -0.5
    colscale[~valid] = 0.0

    operm = np.arange(D_MODEL)
    for m in range(4):
        operm[m * 128 + lane] = (m + 4 * (lane // 64)) * 64 + lane % 64

    i = np.arange(BLK)[:, None]
    j = np.arange(2 * BLK)[None, :]
    rel = i + BLK - j
    ok = (rel >= 0) & (rel < WINDOW)
    bias = np.stack([np.where(ok & (j >= BLK), 0.0, NEG), np.where(ok, 0.0, NEG)]).astype(f32)
    vones = np.zeros((2 * 2 * BLK, LANES), f32)
    for g in range(2):
        vones[g * 2 * BLK:(g + 1) * 2 * BLK] = (lane // 64 == g)[None, :]

    log_g = np.log(1.0 - 2.0 ** (-5.0 - np.arange(RET_HEADS, dtype=np.float64)))
    idx = np.arange(BLK, dtype=np.float64)
    diff = idx[:, None] - idx[None, :]
    dmask = np.zeros((2, BLK, 2 * BLK), f32)
    qdec = np.zeros((2, BLK, LANES), f32)
    kdec = np.zeros((2, BLK, LANES), f32)
    cdm = np.zeros((2, LANES, LANES), f32)
    rbm = ((np.arange(LANES)[:, None] % 64) // 32 == (np.arange(LANES)[None, :] // 64)).astype(f32)
    for p in range(2):
        for hh in range(2):
            lg = log_g[2 * p + hh]
            dmask[p, :, hh * BLK:(hh + 1) * BLK] = np.where(diff >= 0, np.exp(lg * np.maximum(diff, 0.0)), 0.0)
        lg_lane = log_g[2 * p + grp]
        qdec[p] = np.exp(lg_lane[None, :] * (idx[:, None] + 1.0))
        kdec[p] = np.exp(lg_lane[None, :] * (BLK - 1.0 - idx[:, None]))
        cdm[p] = np.exp(lg_lane * BLK)[:, None] * rbm

    t = np.arange(BLK)
    same_chunk = (t[:, None] // GLA_C) == (t[None, :] // GLA_C)
    mall = np.zeros((N_GLA_ROWS, BLK), f32)
    mall[0:BLK] = same_chunk & (t[None, :] <= t[:, None])
    mall[BLK:2 * BLK] = same_chunk & (t[None, :] > t[:, None])
    for lv in range(1, GLA_LEVELS + 1):
        G = 1 << lv
        start = (t // G) * G
        mid = start + G // 2 - 1
        second = ((t >> (lv - 1)) & 1) == 1
        same_group = (t[:, None] // G) == (t[None, :] // G)
        as_q = second[:, None] & (t[None, :] > mid[:, None]) & (t[None, :] <= t[:, None])
        as_k = (~second)[:, None] & (t[None, :] > t[:, None]) & (t[None, :] <= mid[:, None])
        mall[(1 + lv) * BLK:(2 + lv) * BLK] = same_group & (as_q | as_k)
    ci = np.arange(GLA_C)[:, None]
    cj = np.arange(GLA_C)[None, :]
    x = ci ^ cj
    lvl = np.where(cj > ci, -1, np.where(x == 0, 0, np.floor(np.log2(np.maximum(x, 1))).astype(np.int64) + 1))
    lmask = np.stack([np.tile((lvl == lv).astype(f32), (1, GLA_HEADS)) for lv in range(GLA_LEVELS + 1)])
    smask = ((np.arange(GLA_W)[:, None] // GLA_DV) == (np.arange(LANES)[None, :] // GLA_DK)).astype(f32)
    e64 = ((np.arange(256)[:, None] // 64) == (np.arange(256)[None, :] // 64)).astype(f32) / 64.0

    return dict(perm=perm, colscale=colscale, operm=operm, bias=bias, vones=vones, dmask=dmask, qdec=qdec,
                kdec=kdec, cdm=cdm, rbm=rbm, mall=mall, lmask=lmask, smask=smask, e64=e64)


_C = _build_constants()


def _dot(a, b):
    return jnp.dot(a, b, preferred_element_type=jnp.float32)


def _dot_nt(a, b):
    return lax.dot_general(a, b, (((1,), (1,)), ((), ())), preferred_element_type=jnp.float32)


def _dot_tn(a, b):
    return lax.dot_general(a, b, (((0,), (0,)), ((), ())), preferred_element_type=jnp.float32)


def _bf(x):
    return x.astype(jnp.bfloat16)


def _split_hi_lo(x):
    hi = _bf(x)
    lo = _bf(x - hi.astype(jnp.float32))
    return hi, lo


def _silu(x):
    return x / (1.0 + jnp.exp(-x))


MOD_TN = 1024


def _mod_kernel(c_ref, w_ref, b_ref, o_ref):
    a = _silu(c_ref[...])
    a_hi, a_lo = _split_hi_lo(a)
    w_hi, w_lo = _split_hi_lo(w_ref[0])
    o_ref[0] = _dot(a_hi, w_hi) + _dot(a_lo, w_hi) + _dot(a_hi, w_lo) + b_ref[0]


def _modulation(c, w_mod, b_mod):
    depth, d, n = w_mod.shape
    bsz = c.shape[0]
    return pl.pallas_call(
        _mod_kernel,
        out_shape=jax.ShapeDtypeStruct((depth, bsz, n), jnp.float32),
        grid=(depth, n // MOD_TN),
        in_specs=[pl.BlockSpec((bsz, d), lambda l, j: (0, 0)),
                  pl.BlockSpec((1, d, MOD_TN), lambda l, j: (l, 0, j)),
                  pl.BlockSpec((1, 1, MOD_TN), lambda l, j: (l, 0, j))],
        out_specs=pl.BlockSpec((1, bsz, MOD_TN), lambda l, j: (l, 0, j)),
        compiler_params=pltpu.CompilerParams(dimension_semantics=("arbitrary", "arbitrary"),
                                             vmem_limit_bytes=VMEM_LIMIT),
        name="adaln_mod",
    )(c, w_mod, b_mod.reshape(depth, 1, n))


TRIG_T = 512


def _trig_kernel(pos_ref, freq_ref, ca_ref, sa_ref, cr_ref, sr_ref):
    ang = pos_ref[0].astype(jnp.float32) * freq_ref[...]
    c = jnp.cos(ang)
    s = jnp.sin(ang)
    lane = lax.broadcasted_iota(jnp.int32, ang.shape, 1)
    is_rope = (lane % 64) < 32
    sign = jnp.where(lane < 64, -1.0, 1.0)
    c_sh = pltpu.roll(c, 32, axis=1)
    s_sh = pltpu.roll(s, 32, axis=1)
    ca_ref[0] = jnp.where(is_rope, c, c_sh)
    sa_ref[0] = jnp.where(is_rope, s, s_sh) * sign
    cr_ref[0] = jnp.where(is_rope, c_sh, c)
    sr_ref[0] = jnp.where(is_rope, s_sh, s) * sign


def _trig_tables(positions, freq):
    bsz, seq = positions.shape
    out = jax.ShapeDtypeStruct((bsz, seq, LANES), jnp.float32)
    spec = pl.BlockSpec((1, TRIG_T, LANES), lambda b, t: (b, t, 0))
    return pl.pallas_call(
        _trig_kernel,
        out_shape=(out, out, out, out),
        grid=(bsz, seq // TRIG_T),
        in_specs=[pl.BlockSpec((1, TRIG_T, 1), lambda b, t: (b, t, 0)),
                  pl.BlockSpec((1, LANES), lambda b, t: (0, 0))],
        out_specs=(spec, spec, spec, spec),
        compiler_params=pltpu.CompilerParams(dimension_semantics=("arbitrary", "arbitrary"),
                                             vmem_limit_bytes=VMEM_LIMIT),
        name="rotary_tables",
    )(positions.reshape(bsz, seq, 1), freq)


def _seg_mean(x, e_ref):
    hi, lo = _split_hi_lo(x)
    e = e_ref[...]
    return _dot(hi, e) + _dot(lo, e)


def _layer_kernel(sinks_ref,
                  x_ref, mod_ref, pre_ref, post_ref, win_ref, wout_ref, gw_ref, gb_ref, gng_ref,
                  ca_ref, sa_ref, cr_ref, sr_ref,
                  bias_ref, vones_ref, dmask_ref, qdec_ref, kdec_ref, cdm_ref, rbm_ref,
                  mall_ref, lmask_ref, smask_ref, e64_ref,
                  o_ref,
                  hb_ref, mix_ref, kprev_ref, vprev_ref, rstate_ref, gstate_ref):
    t = pl.program_id(1)

    @pl.when(t == 0)
    def _():
        kprev_ref[...] = jnp.zeros_like(kprev_ref)
        vprev_ref[...] = jnp.zeros_like(vprev_ref)
        rstate_ref[...] = jnp.zeros_like(rstate_ref)
        gstate_ref[...] = jnp.zeros_like(gstate_ref)

    x = x_ref[0]
    shift = mod_ref[0, 0:1, :]
    scale = mod_ref[0, 1:2, :]
    gate = mod_ref[0, 2:3, :]
    inv = lax.rsqrt(jnp.mean(x * x, axis=-1, keepdims=True) + EPS)
    hb_ref[...] = _bf((x * inv) * pre_ref[...] * (1.0 + scale) + shift)

    def proj(off, width):
        return _dot(hb_ref[...], win_ref[:, off:off + width])

    lane = lax.broadcasted_iota(jnp.int32, (1, LANES), 1)

    ca, sa = ca_ref[0], sa_ref[0]

    def rot(v, c, s):
        return v * c + pltpu.roll(v, 64, axis=1) * s

    q = proj(O_AQ, 512)
    qstack = jnp.concatenate([_bf(rot(q[:, m * 128:(m + 1) * 128], ca, sa)) for m in range(4)], axis=0)
    k_new = _bf(rot(proj(O_AK, 128), ca, sa))
    v_new = _bf(proj(O_AV, 128))
    kcat = jnp.concatenate([kprev_ref[...], k_new], axis=0)
    vcat = jnp.concatenate([vprev_ref[...], v_new], axis=0)
    kprev_ref[...] = k_new
    vprev_ref[...] = v_new
    zero_bf = jnp.zeros((), jnp.bfloat16)
    kgrp = [(lane % 64) // 32 == g for g in range(2)]
    vgrp = [lane // 64 == g for g in range(2)]
    k2 = jnp.concatenate([jnp.where(kgrp[g], kcat, zero_bf) for g in range(2)], axis=0)
    v2 = jnp.concatenate([jnp.where(vgrp[g], vcat, zero_bf) for g in range(2)], axis=0)
    v2 = jnp.concatenate([v2, vones_ref[...]], axis=1)
    s_all = _dot_nt(qstack, k2)
    bias = bias_ref[jnp.minimum(t, 1)]
    ag = proj(O_AG, 512)
    for m in range(4):
        ps, mxs = [], []
        for g in range(2):
            s = s_all[m * BLK:(m + 1) * BLK, g * 2 * BLK:(g + 1) * 2 * BLK] + bias
            mx = jnp.maximum(jnp.max(s, axis=-1, keepdims=True), sinks_ref[m + 4 * g])
            ps.append(_bf(jnp.exp(s - mx)))
            mxs.append(mx)
        pv = _dot(jnp.concatenate(ps, axis=1), v2)
        mx_l = jnp.where(vgrp[0], mxs[0], mxs[1])
        sink_l = jnp.where(vgrp[0], sinks_ref[m], sinks_ref[m + 4])
        den = pv[:, 128:] + jnp.exp(sink_l - mx_l)
        a_out = pv[:, :128] / den
        mix_ref[:, m * 128:(m + 1) * 128] = _bf(a_out * _silu(ag[:, m * 128:(m + 1) * 128]))

    cr, sr = cr_ref[0], sr_ref[0]
    rq = proj(O_RQ, 256)
    rk = proj(O_RK, 256)
    rv = proj(O_RV, 256)
    r_parts = []
    for p in range(2):
        q_p = rot(rq[:, p * 128:(p + 1) * 128], cr, sr)
        k_p = rot(rk[:, p * 128:(p + 1) * 128], cr, sr)
        v_p = _bf(rv[:, p * 128:(p + 1) * 128])
        k_pb = _bf(k_p)
        kbd = jnp.concatenate([jnp.where(kgrp[h], k_pb, zero_bf) for h in range(2)], axis=0)
        vbd = jnp.concatenate([jnp.where(vgrp[h], v_p, zero_bf) for h in range(2)], axis=0)
        sc = _dot_nt(_bf(q_p), kbd) * dmask_ref[p]
        st = rstate_ref[p]
        r_parts.append(_dot(_bf(sc), vbd) + _dot(_bf(q_p * qdec_ref[p]), _bf(st)))
        upd = _dot_tn(_bf(k_p * kdec_ref[p]), v_p)
        rstate_ref[p] = st * cdm_ref[p] + upd * rbm_ref[...]
    r = jnp.concatenate(r_parts, axis=1)
    r = r * lax.rsqrt(_seg_mean(r * r, e64_ref) + EPS)
    mix_ref[:, ATTN_W:ATTN_W + RET_W] = _bf(r * _silu(proj(O_RG, 256)))

    gq = proj(O_GQ, 128) * (GLA_DK ** -0.5)
    gk = proj(O_GK, 128)
    gv = _bf(proj(O_GV, 256))
    logits = _dot(_bf(proj(O_GA, 128)), gw_ref[...]) + gb_ref[...]
    log_a = (jnp.minimum(logits, 0.0) - jnp.log1p(jnp.exp(-jnp.abs(logits)))) * (1.0 / GLA_GATE_NORMALIZER)
    la_hi, la_lo = _split_hi_lo(log_a)
    xr = _dot(mall_ref[...], jnp.concatenate([la_hi, la_lo], axis=1))
    xs = xr[:, :128] + xr[:, 128:]
    hgrp = [lane // GLA_DK == h for h in range(GLA_HEADS)]
    lane2 = lax.broadcasted_iota(jnp.int32, (1, GLA_W), 1)
    vhead = [lane2 // GLA_DV == h for h in range(GLA_HEADS)]
    g_parts = []
    for c in range(BLK // GLA_C):
        r0 = c * GLA_C
        q_c = gq[r0:r0 + GLA_C]
        k_c = gk[r0:r0 + GLA_C]
        v_c = gv[r0:r0 + GLA_C]
        sacc = jnp.zeros((GLA_C, GLA_HEADS * GLA_C), jnp.float32)
        for lv in range(GLA_LEVELS + 1):
            if lv == 0:
                qt, kt = _bf(q_c), _bf(k_c)
            else:
                f = jnp.exp(xs[(1 + lv) * BLK + r0:(1 + lv) * BLK + r0 + GLA_C])
                qt, kt = _bf(q_c * f), _bf(k_c * f)
            kbd = jnp.concatenate([jnp.where(hgrp[h], kt, zero_bf) for h in range(GLA_HEADS)], axis=0)
            sacc = sacc + _dot_nt(qt, kbd) * lmask_ref[lv]
        vbd = jnp.concatenate([jnp.where(vhead[h], v_c, zero_bf) for h in range(GLA_HEADS)], axis=0)
        st = gstate_ref[...]
        q_in = _bf(q_c * jnp.exp(xs[r0:r0 + GLA_C]))
        g_parts.append(_dot(_bf(sacc), vbd) + _dot_nt(q_in, _bf(st)))
        k_out = _bf(k_c * jnp.exp(xs[BLK + r0:BLK + r0 + GLA_C]))
        dec = jnp.exp(xs[r0 + GLA_C - 1:r0 + GLA_C])
        gstate_ref[...] = st * dec + _dot_tn(v_c, k_out) * smask_ref[...]
    g = jnp.concatenate(g_parts, axis=0)
    g = g * lax.rsqrt(_seg_mean(g * g, e64_ref) + EPS) * gng_ref[...]
    mix_ref[:, ATTN_W + RET_W:] = _bf(g * _silu(proj(O_GG, 256)))

    y = _dot(mix_ref[...], wout_ref[...])
    yn = y * lax.rsqrt(jnp.mean(y * y, axis=-1, keepdims=True) + EPS) * post_ref[...]
    o_ref[0] = x + gate * yn


def _const_spec(shape):
    nd = len(shape)
    return pl.BlockSpec(shape, lambda b, t, s, _n=nd: (0,) * _n)


def _layer(x, mod, pre_gain, post_gain, w_in_p, w_out_p, sinks, gate_w_p, gate_b, gng, tables, consts):
    bsz, seq, d = x.shape
    tok_spec = pl.BlockSpec((1, BLK, LANES), lambda b, t, s: (b, t, 0))
    small = [pre_gain, post_gain, w_in_p, w_out_p, gate_w_p, gate_b, gng]
    in_specs = ([pl.BlockSpec((1, BLK, d), lambda b, t, s: (b, t, 0)),
                 pl.BlockSpec((1, 3, d), lambda b, t, s: (b, 0, 0))]
                + [_const_spec(a.shape) for a in small]
                + [tok_spec] * 4
                + [_const_spec(a.shape) for a in consts])
    grid_spec = pltpu.PrefetchScalarGridSpec(
        num_scalar_prefetch=1,
        grid=(bsz, seq // BLK),
        in_specs=in_specs,
        out_specs=pl.BlockSpec((1, BLK, d), lambda b, t, s: (b, t, 0)),
        scratch_shapes=[pltpu.VMEM((BLK, d), jnp.bfloat16),
                        pltpu.VMEM((BLK, d), jnp.bfloat16),
                        pltpu.VMEM((BLK, LANES), jnp.bfloat16),
                        pltpu.VMEM((BLK, LANES), jnp.bfloat16),
                        pltpu.VMEM((2, LANES, LANES), jnp.float32),
                        pltpu.VMEM((GLA_W, LANES), jnp.float32)])
    return pl.pallas_call(
        _layer_kernel,
        out_shape=jax.ShapeDtypeStruct(x.shape, x.dtype),
        grid_spec=grid_spec,
        compiler_params=pltpu.CompilerParams(dimension_semantics=("arbitrary", "arbitrary"),
                                             vmem_limit_bytes=VMEM_LIMIT),
        name="hybrid_layer",
    )(sinks, x, mod, pre_gain, post_gain, w_in_p, w_out_p, gate_w_p, gate_b, gng, *tables, *consts)


def kernel(x, c, positions, w_mod, b_mod, pre_norm_gain, post_norm_gain, w_in, attn_sinks, gla_gate_w,
           gla_gate_b, gla_norm_gain, w_out):
    depth = w_mod.shape[0]
    bsz = x.shape[0]
    f32, bf16 = jnp.float32, jnp.bfloat16

    mod = _modulation(c, w_mod, b_mod)
    rope_freq = ROPE_THETA ** (-jnp.arange(0, HEAD_DIM, 2, dtype=f32) / HEAD_DIM)
    ret_freq = 1.0 / (10000.0 ** jnp.linspace(0.0, 1.0, RET_DK // 2, dtype=f32))
    freq = jnp.concatenate([rope_freq, ret_freq, rope_freq, ret_freq]).reshape(1, LANES)
    tables = _trig_tables(positions, freq)

    consts = [jnp.asarray(_C["bias"]), jnp.asarray(_C["vones"], bf16), jnp.asarray(_C["dmask"]),
              jnp.asarray(_C["qdec"]), jnp.asarray(_C["kdec"]), jnp.asarray(_C["cdm"]), jnp.asarray(_C["rbm"]),
              jnp.asarray(_C["mall"], bf16), jnp.asarray(_C["lmask"]), jnp.asarray(_C["smask"]),
              jnp.asarray(_C["e64"], bf16)]
    perm = jnp.asarray(_C["perm"])
    colscale = jnp.asarray(_C["colscale"])
    operm = jnp.asarray(_C["operm"])

    for l in range(depth):
        w_in_p = (jnp.take(w_in[l], perm, axis=1) * colscale).astype(bf16)
        w_out_p = jnp.take(w_out[l], operm, axis=0).astype(bf16)
        gate_w_p = jnp.zeros((LANES, LANES), f32).at[:GLA_GATE_RANK].set(gla_gate_w[l]).astype(bf16)
        x = _layer(x, mod[l].reshape(bsz, 3, D_MODEL),
                   pre_norm_gain[l].reshape(1, D_MODEL), post_norm_gain[l].reshape(1, D_MODEL),
                   w_in_p, w_out_p, attn_sinks[l], gate_w_p, gla_gate_b[l].reshape(1, LANES),
                   jnp.tile(gla_norm_gain[l], GLA_HEADS).reshape(1, GLA_W), tables, consts)
    return x
```

```python
import numpy as np
import jax
import jax.numpy as jnp
from jax import lax
from jax.experimental import pallas as pl
from jax.experimental.pallas import tpu as pltpu

D_MODEL = 1024
HEAD_DIM = 64
ATTN_HEADS = 8
ATTN_KV_HEADS = 2
WINDOW = 128
ROPE_THETA = 10000.0
RET_HEADS = 4
RET_DK = 64
RET_DV = 64
GLA_HEADS = 4
GLA_DK = 32
GLA_DV = 64
GLA_GATE_RANK = 16
GLA_GATE_NORMALIZER = 16.0
EPS = 1e-6

ATTN_W = ATTN_HEADS * HEAD_DIM
RET_W = RET_HEADS * RET_DV
GLA_W = GLA_HEADS * GLA_DV

LANES = 128
BLK = 128
NSUB = 2
GLA_C = 64
GLA_LEVELS = 6
N_GLA_ROWS = (2 + GLA_LEVELS) * BLK
NEG = -1e30
VMEM_LIMIT = 56 * 1024 * 1024

O_AQ, O_AK, O_AV, O_AG = 0, 512, 640, 768
O_RQ, O_RK, O_RV, O_RG = 1280, 1536, 1792, 2048
O_GQ, O_GK, O_GV, O_GG, O_GA = 2304, 2432, 2560, 2816, 3072
N_PROJ = 3200
PROJ_SEGMENTS = ((0, 512), (512, 768), (768, 1280), (1280, 1792), (1792, 2304), (2304, 2816), (2816, 3200))


def _build_constants():
    f32 = np.float32
    lane = np.arange(LANES)
    grp = (lane % 64) // 32

    i = np.arange(BLK)[:, None]
    j = np.arange(2 * BLK)[None, :]
    rel = i + BLK - j
    ok = (rel >= 0) & (rel < WINDOW)
    bias = np.stack([np.where(ok & (j >= BLK), 0.0, NEG), np.where(ok, 0.0, NEG)]).astype(f32)
    vones = np.zeros((2 * 2 * BLK, LANES), f32)
    for g in range(2):
        vones[g * 2 * BLK:(g + 1) * 2 * BLK] = (lane // 64 == g)[None, :]

    log_g = np.log(1.0 - 2.0 ** (-5.0 - np.arange(RET_HEADS, dtype=np.float64)))
    idx = np.arange(BLK, dtype=np.float64)
    diff = idx[:, None] - idx[None, :]
    dmask = np.zeros((2, BLK, 2 * BLK), f32)
    qdec = np.zeros((2, BLK, LANES), f32)
    kdec = np.zeros((2, BLK, LANES), f32)
    cdm = np.zeros((2, LANES, LANES), f32)
    rbm = ((np.arange(LANES)[:, None] % 64) // 32 == (np.arange(LANES)[None, :] // 64)).astype(f32)
    for p in range(2):
        for hh in range(2):
            lg = log_g[2 * p + hh]
            dmask[p, :, hh * BLK:(hh + 1) * BLK] = np.where(diff >= 0, np.exp(lg * np.maximum(diff, 0.0)), 0.0)
        lg_lane = log_g[2 * p + grp]
        qdec[p] = np.exp(lg_lane[None, :] * (idx[:, None] + 1.0))
        kdec[p] = np.exp(lg_lane[None, :] * (BLK - 1.0 - idx[:, None]))
        cdm[p] = np.exp(lg_lane * BLK)[:, None] * rbm

    t = np.arange(BLK)
    same_chunk = (t[:, None] // GLA_C) == (t[None, :] // GLA_C)
    mall = np.zeros((N_GLA_ROWS, BLK), f32)
    mall[0:BLK] = same_chunk & (t[None, :] <= t[:, None])
    mall[BLK:2 * BLK] = same_chunk & (t[None, :] > t[:, None])
    for lv in range(1, GLA_LEVELS + 1):
        G = 1 << lv
        mid = (t // G) * G + G // 2 - 1
        second = ((t >> (lv - 1)) & 1) == 1
        same_group = (t[:, None] // G) == (t[None, :] // G)
        as_q = second[:, None] & (t[None, :] > mid[:, None]) & (t[None, :] <= t[:, None])
        as_k = (~second)[:, None] & (t[None, :] > t[:, None]) & (t[None, :] <= mid[:, None])
        mall[(1 + lv) * BLK:(2 + lv) * BLK] = same_group & (as_q | as_k)
    ci = np.arange(GLA_C)[:, None]
    cj = np.arange(GLA_C)[None, :]
    x = ci ^ cj
    lvl = np.where(cj > ci, -1, np.where(x == 0, 0, np.floor(np.log2(np.maximum(x, 1))).astype(np.int64) + 1))
    lmask = np.stack([np.tile((lvl == lv).astype(f32), (1, GLA_HEADS)) for lv in range(GLA_LEVELS + 1)])
    smask = ((np.arange(GLA_W)[:, None] // GLA_DV) == (np.arange(LANES)[None, :] // GLA_DK)).astype(f32)
    e64 = ((np.arange(256)[:, None] // 64) == (np.arange(256)[None, :] // 64)).astype(f32) / 64.0

    return dict(bias=bias, vones=vones, dmask=dmask, qdec=qdec, kdec=kdec, cdm=cdm, rbm=rbm, mall=mall,
                lmask=lmask, smask=smask, e64=e64)


_C = _build_constants()


def _dot(a, b):
    return jnp.dot(a, b, preferred_element_type=jnp.float32)


def _dot_nt(a, b):
    return lax.dot_general(a, b, (((1,), (1,)), ((), ())), preferred_element_type=jnp.float32)


def _dot_tn(a, b):
    return lax.dot_general(a, b, (((0,), (0,)), ((), ())), preferred_element_type=jnp.float32)


def _bf(x):
    return x.astype(jnp.bfloat16)


def _split_hi_lo(x):
    hi = _bf(x)
    lo = _bf(x - hi.astype(jnp.float32))
    return hi, lo


def _silu(x):
    return x / (1.0 + jnp.exp(-x))


def _run_interleaved(gens):
    active = list(gens)
    while active:
        for g in list(active):
            try:
                next(g)
            except StopIteration:
                active.remove(g)


MOD_TN = 1024


def _mod_kernel(c_ref, w_ref, b_ref, o_ref):
    a = _silu(c_ref[...])
    a_hi, a_lo = _split_hi_lo(a)
    w_hi, w_lo = _split_hi_lo(w_ref[0])
    o_ref[0] = _dot(a_hi, w_hi) + _dot(a_lo, w_hi) + _dot(a_hi, w_lo) + b_ref[0]


def _modulation(c, w_mod, b_mod):
    depth, d, n = w_mod.shape
    bsz = c.shape[0]
    return pl.pallas_call(
        _mod_kernel,
        out_shape=jax.ShapeDtypeStruct((depth, bsz, n), jnp.float32),
        grid=(depth, n // MOD_TN),
        in_specs=[pl.BlockSpec((bsz, d), lambda l, j: (0, 0)),
                  pl.BlockSpec((1, d, MOD_TN), lambda l, j: (l, 0, j)),
                  pl.BlockSpec((1, 1, MOD_TN), lambda l, j: (l, 0, j))],
        out_specs=pl.BlockSpec((1, bsz, MOD_TN), lambda l, j: (l, 0, j)),
        compiler_params=pltpu.CompilerParams(dimension_semantics=("arbitrary", "arbitrary"),
                                             vmem_limit_bytes=VMEM_LIMIT),
        name="adaln_mod",
    )(c, w_mod, b_mod.reshape(depth, 1, n))


TRIG_T = 512


def _trig_kernel(pos_ref, freq_ref, ca_ref, sa_ref, cr_ref, sr_ref):
    ang = pos_ref[0].astype(jnp.float32) * freq_ref[...]
    c = jnp.cos(ang)
    s = jnp.sin(ang)
    lane = lax.broadcasted_iota(jnp.int32, ang.shape, 1)
    is_rope = (lane % 64) < 32
    sign = jnp.where(lane < 64, -1.0, 1.0)
    c_sh = pltpu.roll(c, 32, axis=1)
    s_sh = pltpu.roll(s, 32, axis=1)
    ca_ref[0] = jnp.where(is_rope, c, c_sh)
    sa_ref[0] = jnp.where(is_rope, s, s_sh) * sign
    cr_ref[0] = jnp.where(is_rope, c_sh, c)
    sr_ref[0] = jnp.where(is_rope, s_sh, s) * sign


def _trig_tables(positions, freq):
    bsz, seq = positions.shape
    out = jax.ShapeDtypeStruct((bsz, seq, LANES), jnp.float32)
    spec = pl.BlockSpec((1, TRIG_T, LANES), lambda b, t: (b, t, 0))
    return pl.pallas_call(
        _trig_kernel,
        out_shape=(out, out, out, out),
        grid=(bsz, seq // TRIG_T),
        in_specs=[pl.BlockSpec((1, TRIG_T, 1), lambda b, t: (b, t, 0)),
                  pl.BlockSpec((1, LANES), lambda b, t: (0, 0))],
        out_specs=(spec, spec, spec, spec),
        compiler_params=pltpu.CompilerParams(dimension_semantics=("arbitrary", "arbitrary"),
                                             vmem_limit_bytes=VMEM_LIMIT),
        name="rotary_tables",
    )(positions.reshape(bsz, seq, 1), freq)


def _seg_mean(x, e_ref):
    hi, lo = _split_hi_lo(x)
    e = e_ref[...]
    return _dot(hi, e) + _dot(lo, e)


def _rot(v, c, s):
    return v * c + pltpu.roll(v, 64, axis=1) * s


def _layer_kernel(sinks_ref,
                  x_ref, mod_ref, pre_ref, post_ref, win_ref, wout_ref, gw_ref, gb_ref, gng_ref,
                  ca_ref, sa_ref, cr_ref, sr_ref,
                  bias_ref, vones_ref, dmask_ref, qdec_ref, kdec_ref, cdm_ref, rbm_ref,
                  mall_ref, lmask_ref, smask_ref, e64_ref,
                  o_ref, *scratch):
    hb_refs = scratch[0:NSUB]
    proj_refs = scratch[NSUB:2 * NSUB]
    mix_refs = scratch[2 * NSUB:3 * NSUB]
    kprev_ref, vprev_ref, rstate_ref, gstate_ref = scratch[3 * NSUB:]
    t = pl.program_id(1)

    @pl.when(t == 0)
    def _():
        kprev_ref[...] = jnp.zeros_like(kprev_ref)
        vprev_ref[...] = jnp.zeros_like(vprev_ref)
        rstate_ref[...] = jnp.zeros_like(rstate_ref)
        gstate_ref[...] = jnp.zeros_like(gstate_ref)

    lane = lax.broadcasted_iota(jnp.int32, (1, LANES), 1)
    lane2 = lax.broadcasted_iota(jnp.int32, (1, GLA_W), 1)
    zero_bf = jnp.zeros((), jnp.bfloat16)
    kgrp = [(lane % 64) // 32 == g for g in range(2)]
    vgrp = [lane // 64 == g for g in range(2)]
    hgrp = [lane // GLA_DK == h for h in range(GLA_HEADS)]
    vhead = [lane2 // GLA_DV == h for h in range(GLA_HEADS)]

    def rows(sb):
        return slice(sb * BLK, (sb + 1) * BLK)

    def gen_norm(sb):
        x = x_ref[0, rows(sb), :]
        inv = lax.rsqrt(jnp.mean(x * x, axis=-1, keepdims=True) + EPS)
        hb_refs[sb][...] = _bf((x * inv) * pre_ref[...] * (1.0 + mod_ref[0, 1:2, :]) + mod_ref[0, 0:1, :])
        yield

    def gen_proj(sb):
        for a, b in PROJ_SEGMENTS:
            proj_refs[sb][:, a:b] = _dot(hb_refs[sb][...], win_ref[:, a:b])
            yield

    def gen_attn(sb):
        pr = proj_refs[sb]
        ca, sa = ca_ref[0, rows(sb), :], sa_ref[0, rows(sb), :]
        k_new = _bf(_rot(pr[:, O_AK:O_AK + 128], ca, sa))
        v_new = _bf(pr[:, O_AV:O_AV + 128])
        kcat = jnp.concatenate([kprev_ref[...], k_new], axis=0)
        vcat = jnp.concatenate([vprev_ref[...], v_new], axis=0)
        kprev_ref[...] = k_new
        vprev_ref[...] = v_new
        k2 = jnp.concatenate([jnp.where(kgrp[g], kcat, zero_bf) for g in range(2)], axis=0)
        v2 = jnp.concatenate([jnp.where(vgrp[g], vcat, zero_bf) for g in range(2)], axis=0)
        v2 = jnp.concatenate([v2, vones_ref[...]], axis=1)
        if sb == 0:
            bias = bias_ref[jnp.minimum(t, 1)]
        else:
            bias = bias_ref[1]

        def scores(m):
            return _dot_nt(_bf(_rot(pr[:, O_AQ + m * 128:O_AQ + (m + 1) * 128], ca, sa)), k2)

        s_m = scores(0)
        yield
        for m in range(4):
            ps, mxs = [], []
            for g in range(2):
                s = s_m[:, g * 2 * BLK:(g + 1) * 2 * BLK] + bias
                mx = jnp.maximum(jnp.max(s, axis=-1, keepdims=True), sinks_ref[m + 4 * g])
                ps.append(_bf(jnp.exp(s - mx)))
                mxs.append(mx)
            pv = _dot(jnp.concatenate(ps, axis=1), v2)
            if m < 3:
                s_m = scores(m + 1)
            yield
            mx_l = jnp.where(vgrp[0], mxs[0], mxs[1])
            sink_l = jnp.where(vgrp[0], sinks_ref[m], sinks_ref[m + 4])
            den = pv[:, 128:] + jnp.exp(sink_l - mx_l)
            gate_m = _silu(pr[:, O_AG + m * 128:O_AG + (m + 1) * 128])
            mix_refs[sb][:, m * 128:(m + 1) * 128] = _bf(pv[:, :128] / den * gate_m)

    def gen_ret(sb):
        pr = proj_refs[sb]
        cr, sr = cr_ref[0, rows(sb), :], sr_ref[0, rows(sb), :]
        held = []
        for p in range(2):
            q_p = _rot(pr[:, O_RQ + p * 128:O_RQ + (p + 1) * 128], cr, sr)
            k_p = _rot(pr[:, O_RK + p * 128:O_RK + (p + 1) * 128], cr, sr)
            v_p = _bf(pr[:, O_RV + p * 128:O_RV + (p + 1) * 128])
            k_pb = _bf(k_p)
            kbd = jnp.concatenate([jnp.where(kgrp[h], k_pb, zero_bf) for h in range(2)], axis=0)
            vbd = jnp.concatenate([jnp.where(vgrp[h], v_p, zero_bf) for h in range(2)], axis=0)
            sc_raw = _dot_nt(_bf(q_p), kbd)
            st = rstate_ref[p]
            inter = _dot(_bf(q_p * qdec_ref[p]), _bf(st))
            upd = _dot_tn(_bf(k_p * kdec_ref[p]), v_p)
            held.append((sc_raw, vbd, st, inter, upd))
        yield
        intras = []
        for p in range(2):
            sc_raw, vbd, st, inter, upd = held[p]
            intras.append(_dot(_bf(sc_raw * dmask_ref[p]), vbd) + inter)
            rstate_ref[p] = st * cdm_ref[p] + upd * rbm_ref[...]
        yield
        r = jnp.concatenate(intras, axis=1)
        ms = _seg_mean(r * r, e64_ref)
        yield
        mix_refs[sb][:, ATTN_W:ATTN_W + RET_W] = _bf(r * lax.rsqrt(ms + EPS) * _silu(pr[:, O_RG:O_RG + 256]))

    def gen_gla(sb):
        pr = proj_refs[sb]
        logits = _dot(_bf(pr[:, O_GA:O_GA + 128]), gw_ref[...])
        yield
        logits = logits + gb_ref[...]
        log_a = (jnp.minimum(logits, 0.0) - jnp.log1p(jnp.exp(-jnp.abs(logits)))) * (1.0 / GLA_GATE_NORMALIZER)
        la_hi, la_lo = _split_hi_lo(log_a)
        xr = _dot(mall_ref[...], jnp.concatenate([la_hi, la_lo], axis=1))
        yield
        xs = xr[:, :128] + xr[:, 128:]
        gq = pr[:, O_GQ:O_GQ + 128] * (GLA_DK ** -0.5)
        gk = pr[:, O_GK:O_GK + 128]
        gv = _bf(pr[:, O_GV:O_GV + 256])
        nch = BLK // GLA_C
        saccs = [jnp.zeros((GLA_C, GLA_HEADS * GLA_C), jnp.float32) for _ in range(nch)]
        for lv in range(GLA_LEVELS + 1):
            raw = []
            for c in range(nch):
                r0 = c * GLA_C
                if lv == 0:
                    qt, kt = _bf(gq[r0:r0 + GLA_C]), _bf(gk[r0:r0 + GLA_C])
                else:
                    f = jnp.exp(xs[(1 + lv) * BLK + r0:(1 + lv) * BLK + r0 + GLA_C])
                    qt, kt = _bf(gq[r0:r0 + GLA_C] * f), _bf(gk[r0:r0 + GLA_C] * f)
                kbd = jnp.concatenate([jnp.where(hgrp[h], kt, zero_bf) for h in range(GLA_HEADS)], axis=0)
                raw.append(_dot_nt(qt, kbd))
            yield
            for c in range(nch):
                saccs[c] = saccs[c] + raw[c] * lmask_ref[lv]
        g_parts = []
        st = gstate_ref[...]
        for c in range(nch):
            r0 = c * GLA_C
            v_c = gv[r0:r0 + GLA_C]
            vbd = jnp.concatenate([jnp.where(vhead[h], v_c, zero_bf) for h in range(GLA_HEADS)], axis=0)
            q_in = _bf(gq[r0:r0 + GLA_C] * jnp.exp(xs[r0:r0 + GLA_C]))
            o_c = _dot(_bf(saccs[c]), vbd) + _dot_nt(q_in, _bf(st))
            k_out = _bf(gk[r0:r0 + GLA_C] * jnp.exp(xs[BLK + r0:BLK + r0 + GLA_C]))
            upd = _dot_tn(v_c, k_out)
            yield
            g_parts.append(o_c)
            dec = jnp.exp(xs[r0 + GLA_C - 1:r0 + GLA_C])
            st = st * dec + upd * smask_ref[...]
        gstate_ref[...] = st
        g = jnp.concatenate(g_parts, axis=0)
        ms = _seg_mean(g * g, e64_ref)
        yield
        gate_g = _silu(pr[:, O_GG:O_GG + 256])
        mix_refs[sb][:, ATTN_W + RET_W:] = _bf(g * lax.rsqrt(ms + EPS) * gng_ref[...] * gate_g)

    def gen_out(sb):
        half = D_MODEL // 2
        y0 = _dot(mix_refs[sb][...], wout_ref[:, :half])
        yield
        y1 = _dot(mix_refs[sb][...], wout_ref[:, half:])
        yield
        ssq = jnp.sum(y0 * y0, axis=-1, keepdims=True) + jnp.sum(y1 * y1, axis=-1, keepdims=True)
        inv = lax.rsqrt(ssq * (1.0 / D_MODEL) + EPS)
        gate = mod_ref[0, 2:3, :]
        o_ref[0, rows(sb), :half] = x_ref[0, rows(sb), :half] + gate[:, :half] * (y0 * inv * post_ref[:, :half])
        yield
        o_ref[0, rows(sb), half:] = x_ref[0, rows(sb), half:] + gate[:, half:] * (y1 * inv * post_ref[:, half:])

    def chain(*gens):
        for g in gens:
            yield from g

    _run_interleaved([gen_norm(0)])
    _run_interleaved([gen_proj(0)] + ([gen_norm(1)] if NSUB > 1 else []))
    for sb in range(NSUB):
        gens = [gen_attn(sb), gen_ret(sb), gen_gla(sb)]
        if sb + 1 < NSUB:
            gens.append(gen_proj(sb + 1))
        if sb + 2 < NSUB:
            gens.append(gen_norm(sb + 2))
        if sb >= 1:
            gens.append(gen_out(sb - 1))
        _run_interleaved(gens)
    _run_interleaved([gen_out(NSUB - 1)])


def _const_spec(shape):
    nd = len(shape)
    return pl.BlockSpec(shape, lambda b, t, s, _n=nd: (0,) * _n)


def _layer(x, mod, pre_gain, post_gain, w_in_p, w_out_p, sinks, gate_w_p, gate_b, gng, tables, consts):
    bsz, seq, d = x.shape
    tb = NSUB * BLK
    tok_spec = pl.BlockSpec((1, tb, LANES), lambda b, t, s: (b, t, 0))
    small = [pre_gain, post_gain, w_in_p, w_out_p, gate_w_p, gate_b, gng]
    in_specs = ([pl.BlockSpec((1, tb, d), lambda b, t, s: (b, t, 0)),
                 pl.BlockSpec((1, 3, d), lambda b, t, s: (b, 0, 0))]
                + [_const_spec(a.shape) for a in small]
                + [tok_spec] * 4
                + [_const_spec(a.shape) for a in consts])
    scratch = ([pltpu.VMEM((BLK, d), jnp.bfloat16)] * NSUB
               + [pltpu.VMEM((BLK, N_PROJ), jnp.float32)] * NSUB
               + [pltpu.VMEM((BLK, d), jnp.bfloat16)] * NSUB
               + [pltpu.VMEM((BLK, LANES), jnp.bfloat16),
                  pltpu.VMEM((BLK, LANES), jnp.bfloat16),
                  pltpu.VMEM((2, LANES, LANES), jnp.float32),
                  pltpu.VMEM((GLA_W, LANES), jnp.float32)])
    grid_spec = pltpu.PrefetchScalarGridSpec(
        num_scalar_prefetch=1,
        grid=(bsz, seq // tb),
        in_specs=in_specs,
        out_specs=pl.BlockSpec((1, tb, d), lambda b, t, s: (b, t, 0)),
        scratch_shapes=scratch)
    return pl.pallas_call(
        _layer_kernel,
        out_shape=jax.ShapeDtypeStruct(x.shape, x.dtype),
        grid_spec=grid_spec,
        compiler_params=pltpu.CompilerParams(dimension_semantics=("arbitrary", "arbitrary"),
                                             vmem_limit_bytes=VMEM_LIMIT),
        name="hybrid_layer",
    )(sinks, x, mod, pre_gain, post_gain, w_in_p, w_out_p, gate_w_p, gate_b, gng, *tables, *consts)


def _relayout_weights(w_in, w_out):
    depth, d, _ = w_in.shape

    def cols(a, b):
        return w_in[:, :, a:b]

    aq = cols(0, 512).reshape(depth, d, 2, 4, 2, 32).transpose(0, 1, 3, 4, 2, 5).reshape(depth, d, 512)
    aq = aq * (HEAD_DIM ** -0.5)
    ak = cols(512, 640).reshape(depth, d, 2, 2, 32).transpose(0, 1, 3, 2, 4).reshape(depth, d, 128)
    av = cols(640, 768)
    ag = cols(768, 1280).reshape(depth, d, 2, 4, 64).transpose(0, 1, 3, 2, 4).reshape(depth, d, 512)
    rq = cols(1280, 1536).reshape(depth, d, 2, 2, 2, 32).transpose(0, 1, 2, 4, 3, 5).reshape(depth, d, 256)
    rk = cols(1536, 1792).reshape(depth, d, 2, 2, 2, 32).transpose(0, 1, 2, 4, 3, 5).reshape(depth, d, 256)
    rk = rk * (RET_DK ** -0.5)
    rest = cols(1792, 3088)
    pad = jnp.zeros((depth, d, N_PROJ - 3088), w_in.dtype)
    w_in_p = jnp.concatenate([aq, ak, av, ag, rq, rk, rest, pad], axis=-1).astype(jnp.bfloat16)
    wo_a = w_out[:, :ATTN_W].reshape(depth, 2, 4, 64, D_MODEL).transpose(0, 2, 1, 3, 4).reshape(depth, ATTN_W, D_MODEL)
    w_out_p = jnp.concatenate([wo_a, w_out[:, ATTN_W:]], axis=1).astype(jnp.bfloat16)
    return w_in_p, w_out_p


def kernel(x, c, positions, w_mod, b_mod, pre_norm_gain, post_norm_gain, w_in, attn_sinks, gla_gate_w,
           gla_gate_b, gla_norm_gain, w_out):
    depth = w_mod.shape[0]
    bsz = x.shape[0]
    f32, bf16 = jnp.float32, jnp.bfloat16

    mod = _modulation(c, w_mod, b_mod).reshape(depth, bsz, 3, D_MODEL)
    rope_freq = ROPE_THETA ** (-jnp.arange(0, HEAD_DIM, 2, dtype=f32) / HEAD_DIM)
    ret_freq = 1.0 / (10000.0 ** jnp.linspace(0.0, 1.0, RET_DK // 2, dtype=f32))
    freq = jnp.concatenate([rope_freq, ret_freq, rope_freq, ret_freq]).reshape(1, LANES)
    tables = _trig_tables(positions, freq)

    consts = [jnp.asarray(_C["bias"]), jnp.asarray(_C["vones"], bf16), jnp.asarray(_C["dmask"]),
              jnp.asarray(_C["qdec"]), jnp.asarray(_C["kdec"]), jnp.asarray(_C["cdm"]), jnp.asarray(_C["rbm"]),
              jnp.asarray(_C["mall"], bf16), jnp.asarray(_C["lmask"]), jnp.asarray(_C["smask"]),
              jnp.asarray(_C["e64"], bf16)]
    w_in_p, w_out_p = _relayout_weights(w_in, w_out)
    gate_w_p = jnp.pad(gla_gate_w, ((0, 0), (0, LANES - GLA_GATE_RANK), (0, 0))).astype(bf16)
    gng = jnp.tile(gla_norm_gain, (1, GLA_HEADS)).reshape(depth, 1, GLA_W)

    for l in range(depth):
        x = _layer(x, mod[l], pre_norm_gain[l].reshape(1, D_MODEL), post_norm_gain[l].reshape(1, D_MODEL),
                   w_in_p[l], w_out_p[l], attn_sinks[l], gate_w_p[l], gla_gate_b[l].reshape(1, LANES),
                   gng[l], tables, consts)
    return x
```

```python
import numpy as np
import jax
import jax.numpy as jnp
from jax import lax
from jax.experimental import pallas as pl
from jax.experimental.pallas import tpu as pltpu

D_MODEL = 1024
HEAD_DIM = 64
ATTN_HEADS = 8
ATTN_KV_HEADS = 2
WINDOW = 128
ROPE_THETA = 10000.0
RET_HEADS = 4
RET_DK = 64
RET_DV = 64
GLA_HEADS = 4
GLA_DK = 32
GLA_DV = 64
GLA_GATE_RANK = 16
GLA_GATE_NORMALIZER = 16.0
EPS = 1e-6

ATTN_W = ATTN_HEADS * HEAD_DIM
RET_W = RET_HEADS * RET_DV
GLA_W = GLA_HEADS * GLA_DV

LANES = 128
BLK = 128
NSUB = 4
PGRP = 2
NGRP = NSUB // PGRP
GLA_C = 64
GLA_LEVELS = 6
N_GLA_ROWS = (2 + GLA_LEVELS) * BLK
NEG = -1e30
VMEM_LIMIT = 56 * 1024 * 1024

O_AQ, O_AK, O_AV, O_AG = 0, 512, 640, 768
O_RQ, O_RK, O_RV, O_RG = 1280, 1536, 1792, 2048
O_GQ, O_GK, O_GV, O_GG, O_GA = 2304, 2432, 2560, 2816, 3072
N_PROJ = 3200
PROJ_SEGMENTS = (("gga", O_GG, N_PROJ), ("gqkv", O_GQ, O_GG), ("akv", O_AK, O_AG), ("aq", O_AQ, O_AK),
                 ("rqk", O_RQ, O_RV), ("rvg", O_RV, O_GQ), ("ag", O_AG, O_RQ))


def _build_constants():
    f32 = np.float32
    lane = np.arange(LANES)
    grp = lane // 64

    i = np.arange(BLK)[:, None]
    j = np.arange(2 * BLK)[None, :]
    rel = i + BLK - j
    ok = (rel >= 0) & (rel < WINDOW)
    bias = np.stack([np.where(ok & (j >= BLK), 0.0, NEG), np.where(ok, 0.0, NEG)]).astype(f32)
    vones = np.zeros((2 * 2 * BLK, LANES), f32)
    for g in range(2):
        vones[g * 2 * BLK:(g + 1) * 2 * BLK] = (lane // 64 == g)[None, :]

    kscale = RET_DK ** -0.5
    log_g = np.log(1.0 - 2.0 ** (-5.0 - np.arange(RET_HEADS, dtype=np.float64)))
    idx = np.arange(BLK, dtype=np.float64)
    diff = idx[:, None] - idx[None, :]
    dmask = np.zeros((2, BLK, 2 * BLK), f32)
    qdec = np.zeros((2, BLK, LANES), f32)
    kdec = np.zeros((2, BLK, LANES), f32)
    cdm = np.zeros((2, LANES, LANES), f32)
    rbm = ((np.arange(LANES)[:, None] // 64) == (np.arange(LANES)[None, :] // 64)).astype(f32)
    for p in range(2):
        for hh in range(2):
            lg = log_g[2 * p + hh]
            dmask[p, :, hh * BLK:(hh + 1) * BLK] = kscale * np.where(diff >= 0, np.exp(lg * np.maximum(diff, 0.0)), 0.0)
        lg_lane = log_g[2 * p + grp]
        qdec[p] = np.exp(lg_lane[None, :] * (idx[:, None] + 1.0))
        kdec[p] = kscale * np.exp(lg_lane[None, :] * (BLK - 1.0 - idx[:, None]))
        cdm[p] = np.exp(lg_lane * BLK)[:, None] * rbm

    t = np.arange(BLK)
    same_chunk = (t[:, None] // GLA_C) == (t[None, :] // GLA_C)
    mall = np.zeros((N_GLA_ROWS, BLK), f32)
    mall[0:BLK] = same_chunk & (t[None, :] <= t[:, None])
    mall[BLK:2 * BLK] = same_chunk & (t[None, :] > t[:, None])
    for lv in range(1, GLA_LEVELS + 1):
        G = 1 << lv
        mid = (t // G) * G + G // 2 - 1
        second = ((t >> (lv - 1)) & 1) == 1
        same_group = (t[:, None] // G) == (t[None, :] // G)
        as_q = second[:, None] & (t[None, :] > mid[:, None]) & (t[None, :] <= t[:, None])
        as_k = (~second)[:, None] & (t[None, :] > t[:, None]) & (t[None, :] <= mid[:, None])
        mall[(1 + lv) * BLK:(2 + lv) * BLK] = same_group & (as_q | as_k)
    ci = np.arange(GLA_C)[:, None]
    cj = np.arange(GLA_C)[None, :]
    x = ci ^ cj
    lvl = np.where(cj > ci, -1, np.where(x == 0, 0, np.floor(np.log2(np.maximum(x, 1))).astype(np.int64) + 1))
    lmask = np.stack([np.tile((lvl == lv).astype(f32), (1, GLA_HEADS)) for lv in range(GLA_LEVELS + 1)])
    smask = ((np.arange(GLA_W)[:, None] // GLA_DV) == (np.arange(LANES)[None, :] // GLA_DK)).astype(f32)
    e64 = ((np.arange(256)[:, None] // 64) == (np.arange(256)[None, :] // 64)).astype(f32) / 64.0

    return dict(bias=bias, vones=vones, dmask=dmask, qdec=qdec, kdec=kdec, cdm=cdm, rbm=rbm, mall=mall,
                lmask=lmask, smask=smask, e64=e64)


_C = _build_constants()


def _dot(a, b):
    return jnp.dot(a, b, preferred_element_type=jnp.float32)


def _dot_nt(a, b):
    return lax.dot_general(a, b, (((1,), (1,)), ((), ())), preferred_element_type=jnp.float32)


def _dot_tn(a, b):
    return lax.dot_general(a, b, (((0,), (0,)), ((), ())), preferred_element_type=jnp.float32)


def _bf(x):
    return x.astype(jnp.bfloat16)


def _split_hi_lo(x):
    hi = _bf(x)
    lo = _bf(x - hi.astype(jnp.float32))
    return hi, lo


def _silu(x):
    return x / (1.0 + jnp.exp(-x))


def _run_interleaved(gens, done):
    active = [[g, None] for g in gens]
    while active:
        progressed = False
        for item in list(active):
            if item[1] is not None and item[1] not in done:
                continue
            progressed = True
            try:
                need = next(item[0])
                while need is not None and need in done:
                    need = next(item[0])
                item[1] = need
            except StopIteration:
                active.remove(item)
        assert progressed, [item[1] for item in active]


MOD_TN = 1024


def _mod_kernel(c_ref, w_ref, b_ref, o_ref):
    a = _silu(c_ref[...])
    a_hi, a_lo = _split_hi_lo(a)
    w_hi, w_lo = _split_hi_lo(w_ref[0])
    o_ref[0] = _dot(a_hi, w_hi) + _dot(a_lo, w_hi) + _dot(a_hi, w_lo) + b_ref[0]


def _modulation(c, w_mod, b_mod):
    depth, d, n = w_mod.shape
    bsz = c.shape[0]
    return pl.pallas_call(
        _mod_kernel,
        out_shape=jax.ShapeDtypeStruct((depth, bsz, n), jnp.float32),
        grid=(depth, n // MOD_TN),
        in_specs=[pl.BlockSpec((bsz, d), lambda l, j: (0, 0)),
                  pl.BlockSpec((1, d, MOD_TN), lambda l, j: (l, 0, j)),
                  pl.BlockSpec((1, 1, MOD_TN), lambda l, j: (l, 0, j))],
        out_specs=pl.BlockSpec((1, bsz, MOD_TN), lambda l, j: (l, 0, j)),
        compiler_params=pltpu.CompilerParams(dimension_semantics=("arbitrary", "arbitrary"),
                                             vmem_limit_bytes=VMEM_LIMIT),
        name="adaln_mod",
    )(c, w_mod, b_mod.reshape(depth, 1, n))


TRIG_T = 512


def _trig_kernel(pos_ref, freq_ref, ca_ref, sa_ref, cr_ref, sr_ref):
    ang = pos_ref[0].astype(jnp.float32) * freq_ref[...]
    c = jnp.cos(ang)
    s = jnp.sin(ang)
    lane = lax.broadcasted_iota(jnp.int32, ang.shape, 1)
    first_half = (lane % 64) < 32
    sign = jnp.where(first_half, -1.0, 1.0)
    c_sh = pltpu.roll(c, 32, axis=1)
    s_sh = pltpu.roll(s, 32, axis=1)
    ca_ref[0] = jnp.where(first_half, c, c_sh)
    sa_ref[0] = jnp.where(first_half, s, s_sh) * sign
    cr_ref[0] = jnp.where(first_half, c_sh, c)
    sr_ref[0] = jnp.where(first_half, s_sh, s) * sign


def _trig_tables(positions, freq):
    bsz, seq = positions.shape
    out = jax.ShapeDtypeStruct((bsz, seq, LANES), jnp.float32)
    spec = pl.BlockSpec((1, TRIG_T, LANES), lambda b, t: (b, t, 0))
    return pl.pallas_call(
        _trig_kernel,
        out_shape=(out, out, out, out),
        grid=(bsz, seq // TRIG_T),
        in_specs=[pl.BlockSpec((1, TRIG_T, 1), lambda b, t: (b, t, 0)),
                  pl.BlockSpec((1, LANES), lambda b, t: (0, 0))],
        out_specs=(spec, spec, spec, spec),
        compiler_params=pltpu.CompilerParams(dimension_semantics=("arbitrary", "arbitrary"),
                                             vmem_limit_bytes=VMEM_LIMIT),
        name="rotary_tables",
    )(positions.reshape(bsz, seq, 1), freq)


def _seg_mean(x, e_ref):
    hi, lo = _split_hi_lo(x)
    e = e_ref[...]
    return _dot(hi, e) + _dot(lo, e)


def _rot(v, c, s, first_half):
    partner = jnp.where(first_half, pltpu.roll(v, 96, axis=1), pltpu.roll(v, 32, axis=1))
    return v * c + partner * s


def _layer_kernel(sinks_ref,
                  x_ref, mod_ref, pre_ref, post_ref, win_ref, wout_ref, gw_ref, gb_ref, gng_ref,
                  ca_ref, sa_ref, cr_ref, sr_ref,
                  bias_ref, vones_ref, dmask_ref, qdec_ref, kdec_ref, cdm_ref, rbm_ref,
                  mall_ref, lmask_ref, smask_ref, e64_ref,
                  o_ref, *scratch):
    hb_grp = scratch[0:NGRP]
    proj_grp = scratch[NGRP:2 * NGRP]
    mix_grp = scratch[2 * NGRP:3 * NGRP]
    kprev_ref, vprev_ref, rstate_ref, gstate_ref = scratch[3 * NGRP:]
    t = pl.program_id(1)

    def sub_view(refs, sb):
        return refs[sb // PGRP].at[pl.ds((sb % PGRP) * BLK, BLK)]

    hb_refs = [sub_view(hb_grp, sb) for sb in range(NSUB)]
    proj_refs = [sub_view(proj_grp, sb) for sb in range(NSUB)]
    mix_refs = [sub_view(mix_grp, sb) for sb in range(NSUB)]

    @pl.when(t == 0)
    def _():
        kprev_ref[...] = jnp.zeros_like(kprev_ref)
        vprev_ref[...] = jnp.zeros_like(vprev_ref)
        rstate_ref[...] = jnp.zeros_like(rstate_ref)
        gstate_ref[...] = jnp.zeros_like(gstate_ref)

    lane = lax.broadcasted_iota(jnp.int32, (1, LANES), 1)
    lane2 = lax.broadcasted_iota(jnp.int32, (1, GLA_W), 1)
    zero_bf = jnp.zeros((), jnp.bfloat16)
    first_half = (lane % 64) < 32
    lo_hi = [lane // 64 == g for g in range(2)]
    hgrp = [lane // GLA_DK == h for h in range(GLA_HEADS)]
    vhead = [lane2 // GLA_DV == h for h in range(GLA_HEADS)]

    def rows(sb):
        return slice(sb * BLK, (sb + 1) * BLK)

    done = set()

    def gen_norm(sb):
        if sb >= PGRP:
            yield f"projgrp{sb // PGRP - 1}.started"
        x = x_ref[0, rows(sb), :]
        inv = lax.rsqrt(jnp.mean(x * x, axis=-1, keepdims=True) + EPS)
        hb_refs[sb][...] = _bf((x * inv) * pre_ref[...] * (1.0 + mod_ref[1:2, :]) + mod_ref[0:1, :])
        done.add(f"norm{sb}")

    def gen_proj(g):
        subs = range(g * PGRP, (g + 1) * PGRP)
        for sb in subs:
            yield f"norm{sb}"
        if g > 0:
            yield f"projgrp{g - 1}.all"
        done.add(f"projgrp{g}.started")
        for name, a, b in PROJ_SEGMENTS:
            proj_grp[g][:, a:b] = _dot(hb_grp[g][...], win_ref[:, a:b])
            done.update(f"proj{sb}.{name}" for sb in subs)
            yield
        done.add(f"projgrp{g}.all")

    def gen_attn(sb):
        yield f"proj{sb}.akv"
        if sb > 0:
            yield f"attn{sb - 1}.kv"
        pr = proj_refs[sb]
        ca, sa = ca_ref[0, rows(sb), :], sa_ref[0, rows(sb), :]
        k_new = _rot(pr[:, O_AK:O_AK + 128], ca, sa, first_half) * (HEAD_DIM ** -0.5)
        v_new = pr[:, O_AV:O_AV + 128]
        k_sw = pltpu.roll(k_new, 64, axis=1)
        v_sw = pltpu.roll(v_new, 64, axis=1)
        k2, v2 = [], []
        for g in range(2):
            k_g = _bf(jnp.where(lo_hi[g], k_new, k_sw))
            v_g = _bf(jnp.where(lo_hi[g], v_new, v_sw))
            kcat = jnp.concatenate([kprev_ref[g], k_g], axis=0)
            vcat = jnp.concatenate([vprev_ref[g], v_g], axis=0)
            kprev_ref[g] = k_g
            vprev_ref[g] = v_g
            k2.append(jnp.concatenate([jnp.where(lo_hi[h], kcat, zero_bf) for h in range(2)], axis=0))
            v2_g = jnp.concatenate([jnp.where(lo_hi[h], vcat, zero_bf) for h in range(2)], axis=0)
            v2.append(jnp.concatenate([v2_g, vones_ref[...]], axis=1))
        done.add(f"attn{sb}.kv")
        if sb == 0:
            bias = bias_ref[jnp.minimum(t, 1)]
        else:
            bias = bias_ref[1]

        def scores(m):
            q_m = _rot(pr[:, O_AQ + m * 128:O_AQ + (m + 1) * 128], ca, sa, first_half)
            return _dot_nt(_bf(q_m), k2[m // 2])

        yield f"proj{sb}.aq"
        s_m = scores(0)
        yield
        for m in range(4):
            ps, mxs = [], []
            for h in range(2):
                s = s_m[:, h * 2 * BLK:(h + 1) * 2 * BLK] + bias
                mx = jnp.maximum(jnp.max(s, axis=-1, keepdims=True), sinks_ref[2 * m + h])
                ps.append(_bf(jnp.exp(s - mx)))
                mxs.append(mx)
            pv = _dot(jnp.concatenate(ps, axis=1), v2[m // 2])
            if m < 3:
                s_m = scores(m + 1)
            yield
            yield f"proj{sb}.ag"
            mx_l = jnp.where(lo_hi[0], mxs[0], mxs[1])
            sink_l = jnp.where(lo_hi[0], sinks_ref[2 * m], sinks_ref[2 * m + 1])
            den = pv[:, 128:] + jnp.exp(sink_l - mx_l)
            gate_m = _silu(pr[:, O_AG + m * 128:O_AG + (m + 1) * 128])
            mix_refs[sb][:, m * 128:(m + 1) * 128] = _bf(pv[:, :128] / den * gate_m)
        done.add(f"attn{sb}.done")

    def gen_ret(sb):
        yield f"proj{sb}.rqk"
        yield f"proj{sb}.rvg"
        if sb > 0:
            yield f"ret{sb - 1}.state"
        pr = proj_refs[sb]
        cr, sr = cr_ref[0, rows(sb), :], sr_ref[0, rows(sb), :]
        held = []
        for p in range(2):
            q_p = _rot(pr[:, O_RQ + p * 128:O_RQ + (p + 1) * 128], cr, sr, first_half)
            k_p = _rot(pr[:, O_RK + p * 128:O_RK + (p + 1) * 128], cr, sr, first_half)
            v_p = _bf(pr[:, O_RV + p * 128:O_RV + (p + 1) * 128])
            k_pb = _bf(k_p)
            kbd = jnp.concatenate([jnp.where(lo_hi[h], k_pb, zero_bf) for h in range(2)], axis=0)
            vbd = jnp.concatenate([jnp.where(lo_hi[h], v_p, zero_bf) for h in range(2)], axis=0)
            sc_raw = _dot_nt(_bf(q_p), kbd)
            st = rstate_ref[p]
            inter = _dot(_bf(q_p * qdec_ref[p]), _bf(st))
            upd = _dot_tn(_bf(k_p * kdec_ref[p]), v_p)
            held.append((sc_raw, vbd, st, inter, upd))
        yield
        intras = []
        for p in range(2):
            sc_raw, vbd, st, inter, upd = held[p]
            intras.append(_dot(_bf(sc_raw * dmask_ref[p]), vbd) + inter)
            rstate_ref[p] = st * cdm_ref[p] + upd * rbm_ref[...]
        done.add(f"ret{sb}.state")
        yield
        r = jnp.concatenate(intras, axis=1)
        ms = _seg_mean(r * r, e64_ref)
        yield
        mix_refs[sb][:, ATTN_W:ATTN_W + RET_W] = _bf(r * lax.rsqrt(ms + EPS) * _silu(pr[:, O_RG:O_RG + 256]))
        done.add(f"ret{sb}.done")

    def gen_gla(sb):
        yield f"proj{sb}.gga"
        pr = proj_refs[sb]
        logits = _dot(_bf(pr[:, O_GA:O_GA + 128]), gw_ref[...])
        yield
        logits = logits + gb_ref[...]
        log_a = (jnp.minimum(logits, 0.0) - jnp.log1p(jnp.exp(-jnp.abs(logits)))) * (1.0 / GLA_GATE_NORMALIZER)
        la_hi, la_lo = _split_hi_lo(log_a)
        xr = _dot(mall_ref[...], jnp.concatenate([la_hi, la_lo], axis=1))
        yield
        yield f"proj{sb}.gqkv"
        xs = xr[:, :128] + xr[:, 128:]
        gq = pr[:, O_GQ:O_GQ + 128] * (GLA_DK ** -0.5)
        gk = pr[:, O_GK:O_GK + 128]
        gv = _bf(pr[:, O_GV:O_GV + 256])
        nch = BLK // GLA_C
        saccs = [jnp.zeros((GLA_C, GLA_HEADS * GLA_C), jnp.float32) for _ in range(nch)]
        for lv in range(GLA_LEVELS + 1):
            raw = []
            for c in range(nch):
                r0 = c * GLA_C
                if lv == 0:
                    qt, kt = _bf(gq[r0:r0 + GLA_C]), _bf(gk[r0:r0 + GLA_C])
                else:
                    f = jnp.exp(xs[(1 + lv) * BLK + r0:(1 + lv) * BLK + r0 + GLA_C])
                    qt, kt = _bf(gq[r0:r0 + GLA_C] * f), _bf(gk[r0:r0 + GLA_C] * f)
                kbd = jnp.concatenate([jnp.where(hgrp[h], kt, zero_bf) for h in range(GLA_HEADS)], axis=0)
                raw.append(_dot_nt(qt, kbd))
            yield
            for c in range(nch):
                saccs[c] = saccs[c] + raw[c] * lmask_ref[lv]
        if sb > 0:
            yield f"gla{sb - 1}.state"
        g_parts = []
        st = gstate_ref[...]
        for c in range(nch):
            r0 = c * GLA_C
            v_c = gv[r0:r0 + GLA_C]
            vbd = jnp.concatenate([jnp.where(vhead[h], v_c, zero_bf) for h in range(GLA_HEADS)], axis=0)
            q_in = _bf(gq[r0:r0 + GLA_C] * jnp.exp(xs[r0:r0 + GLA_C]))
            o_c = _dot(_bf(saccs[c]), vbd) + _dot_nt(q_in, _bf(st))
            k_out = _bf(gk[r0:r0 + GLA_C] * jnp.exp(xs[BLK + r0:BLK + r0 + GLA_C]))
            upd = _dot_tn(v_c, k_out)
            yield
            g_parts.append(o_c)
            dec = jnp.exp(xs[r0 + GLA_C - 1:r0 + GLA_C])
            st = st * dec + upd * smask_ref[...]
        gstate_ref[...] = st
        done.add(f"gla{sb}.state")
        g = jnp.concatenate(g_parts, axis=0)
        ms = _seg_mean(g * g, e64_ref)
        yield
        gate_g = _silu(pr[:, O_GG:O_GG + 256])
        mix_refs[sb][:, ATTN_W + RET_W:] = _bf(g * lax.rsqrt(ms + EPS) * gng_ref[...] * gate_g)
        done.add(f"gla{sb}.done")

    def gen_out(g):
        for sb in range(g * PGRP, (g + 1) * PGRP):
            yield f"attn{sb}.done"
            yield f"ret{sb}.done"
            yield f"gla{sb}.done"
        rws = slice(g * PGRP * BLK, (g + 1) * PGRP * BLK)
        half = D_MODEL // 2
        y0 = _dot(mix_grp[g][...], wout_ref[:, :half])
        yield
        y1 = _dot(mix_grp[g][...], wout_ref[:, half:])
        yield
        ssq = jnp.sum(y0 * y0, axis=-1, keepdims=True) + jnp.sum(y1 * y1, axis=-1, keepdims=True)
        inv = lax.rsqrt(ssq * (1.0 / D_MODEL) + EPS)
        gate = mod_ref[2:3, :]
        o_ref[0, rws, :half] = x_ref[0, rws, :half] + gate[:, :half] * (y0 * inv * post_ref[:, :half])
        yield
        o_ref[0, rws, half:] = x_ref[0, rws, half:] + gate[:, half:] * (y1 * inv * post_ref[:, half:])

    gens = []
    for g in range(NGRP):
        for sb in range(g * PGRP, (g + 1) * PGRP):
            gens += [gen_gla(sb), gen_attn(sb), gen_ret(sb)]
        gens += [gen_out(g), gen_proj(g)] + [gen_norm(sb) for sb in range(g * PGRP, (g + 1) * PGRP)]
    _run_interleaved(gens, done)


def _const_spec(shape):
    nd = len(shape)
    return pl.BlockSpec(shape, lambda b, t, s, _n=nd: (0,) * _n)


def _layer(l, x, mod, pre_gain, post_gain, w_in_b, w_out_b, sinks, gate_w_p, gate_b, gng, tables, consts):
    bsz, seq, d = x.shape
    tb = NSUB * BLK
    tok_spec = pl.BlockSpec((1, tb, LANES), lambda b, t, s: (b, t, 0))
    stacked = [pre_gain, post_gain, w_in_b, w_out_b, gate_w_p, gate_b, gng]

    def layer_spec(a):
        return pl.BlockSpec((None,) + a.shape[1:], lambda b, t, s, _n=a.ndim - 1: (l,) + (0,) * _n)

    in_specs = ([pl.BlockSpec((1, tb, d), lambda b, t, s: (b, t, 0)),
                 pl.BlockSpec((None, None, 3, d), lambda b, t, s: (l, b, 0, 0))]
                + [layer_spec(a) for a in stacked]
                + [tok_spec] * 4
                + [_const_spec(a.shape) for a in consts])
    scratch = ([pltpu.VMEM((PGRP * BLK, d), jnp.bfloat16)] * NGRP
               + [pltpu.VMEM((PGRP * BLK, N_PROJ), jnp.float32)] * NGRP
               + [pltpu.VMEM((PGRP * BLK, d), jnp.bfloat16)] * NGRP
               + [pltpu.VMEM((2, BLK, LANES), jnp.bfloat16),
                  pltpu.VMEM((2, BLK, LANES), jnp.bfloat16),
                  pltpu.VMEM((2, LANES, LANES), jnp.float32),
                  pltpu.VMEM((GLA_W, LANES), jnp.float32)])
    grid_spec = pltpu.PrefetchScalarGridSpec(
        num_scalar_prefetch=1,
        grid=(bsz, seq // tb),
        in_specs=in_specs,
        out_specs=pl.BlockSpec((1, tb, d), lambda b, t, s: (b, t, 0)),
        scratch_shapes=scratch)
    return pl.pallas_call(
        _layer_kernel,
        out_shape=jax.ShapeDtypeStruct(x.shape, x.dtype),
        grid_spec=grid_spec,
        compiler_params=pltpu.CompilerParams(dimension_semantics=("arbitrary", "arbitrary"),
                                             vmem_limit_bytes=VMEM_LIMIT),
        name="hybrid_layer",
    )(sinks, x, mod, *stacked, *tables, *consts)


def kernel(x, c, positions, w_mod, b_mod, pre_norm_gain, post_norm_gain, w_in, attn_sinks, gla_gate_w,
           gla_gate_b, gla_norm_gain, w_out):
    depth = w_mod.shape[0]
    bsz = x.shape[0]
    f32, bf16 = jnp.float32, jnp.bfloat16

    mod = _modulation(c, w_mod, b_mod).reshape(depth, bsz, 3, D_MODEL)
    rope_freq = ROPE_THETA ** (-jnp.arange(0, HEAD_DIM, 2, dtype=f32) / HEAD_DIM)
    ret_freq = 1.0 / (10000.0 ** jnp.linspace(0.0, 1.0, RET_DK // 2, dtype=f32))
    freq = jnp.concatenate([rope_freq, ret_freq, rope_freq, ret_freq]).reshape(1, LANES)
    tables = _trig_tables(positions, freq)

    consts = [jnp.asarray(_C["bias"]), jnp.asarray(_C["vones"], bf16), jnp.asarray(_C["dmask"]),
              jnp.asarray(_C["qdec"]), jnp.asarray(_C["kdec"]), jnp.asarray(_C["cdm"]), jnp.asarray(_C["rbm"]),
              jnp.asarray(_C["mall"], bf16), jnp.asarray(_C["lmask"]), jnp.asarray(_C["smask"]),
              jnp.asarray(_C["e64"], bf16)]
    w_in_b = jnp.pad(w_in, ((0, 0), (0, 0), (0, N_PROJ - w_in.shape[-1]))).astype(bf16)
    w_out_b = w_out.astype(bf16)
    gate_w_p = jnp.pad(gla_gate_w, ((0, 0), (0, LANES - GLA_GATE_RANK), (0, 0))).astype(bf16)
    gng = jnp.tile(gla_norm_gain, (1, GLA_HEADS)).reshape(depth, 1, GLA_W)
    pre = pre_norm_gain.reshape(depth, 1, D_MODEL)
    post = post_norm_gain.reshape(depth, 1, D_MODEL)
    gate_b = gla_gate_b.reshape(depth, 1, LANES)

    for l in range(depth):
        x = _layer(l, x, mod, pre, post, w_in_b, w_out_b, attn_sinks[l], gate_w_p, gate_b, gng, tables, consts)
    return x
```

```python
import numpy as np
import jax
import jax.numpy as jnp
from jax import lax
from jax.experimental import pallas as pl
from jax.experimental.pallas import tpu as pltpu

D_MODEL = 1024
HEAD_DIM = 64
ATTN_HEADS = 8
ATTN_KV_HEADS = 2
WINDOW = 128
ROPE_THETA = 10000.0
RET_HEADS = 4
RET_DK = 64
RET_DV = 64
GLA_HEADS = 4
GLA_DK = 32
GLA_DV = 64
GLA_GATE_RANK = 16
GLA_GATE_NORMALIZER = 16.0
EPS = 1e-6

ATTN_W = ATTN_HEADS * HEAD_DIM
RET_W = RET_HEADS * RET_DV
GLA_W = GLA_HEADS * GLA_DV

LANES = 128
BLK = 128
NSUB = 4
PGRP = 2
NGRP = NSUB // PGRP
GLA_C = 64
GLA_LEVELS = 6
N_GLA_ROWS = (2 + GLA_LEVELS) * BLK
NEG = -1e30
VMEM_LIMIT = 56 * 1024 * 1024

O_AQ, O_AK, O_AV, O_AG = 0, 512, 640, 768
O_RQ, O_RK, O_RV, O_RG = 1280, 1536, 1792, 2048
O_GQ, O_GK, O_GV, O_GG, O_GA = 2304, 2432, 2560, 2816, 3072
N_PROJ = 3200
PROJ_SEGMENTS = (("gga", O_GG, N_PROJ), ("gqkv", O_GQ, O_GG), ("akv", O_AK, O_AG), ("aq", O_AQ, O_AK),
                 ("rqk", O_RQ, O_RV), ("rvg", O_RV, O_GQ), ("ag", O_AG, O_RQ))


def _build_constants():
    f32 = np.float32
    lane = np.arange(LANES)
    grp = lane // 64

    i = np.arange(BLK)[:, None]
    j = np.arange(2 * BLK)[None, :]
    rel = i + BLK - j
    ok = (rel >= 0) & (rel < WINDOW)
    bias = np.stack([np.where(ok & (j >= BLK), 0.0, NEG), np.where(ok, 0.0, NEG)]).astype(f32)
    vones = np.zeros((2 * 2 * BLK, LANES), f32)
    for g in range(2):
        vones[g * 2 * BLK:(g + 1) * 2 * BLK] = (lane // 64 == g)[None, :]

    kscale = RET_DK ** -0.5
    log_g = np.log(1.0 - 2.0 ** (-5.0 - np.arange(RET_HEADS, dtype=np.float64)))
    idx = np.arange(BLK, dtype=np.float64)
    diff = idx[:, None] - idx[None, :]
    dmask = np.zeros((2, BLK, 2 * BLK), f32)
    qdec = np.zeros((2, BLK, LANES), f32)
    kdec = np.zeros((2, BLK, LANES), f32)
    cdm = np.zeros((2, LANES, LANES), f32)
    rbm = ((np.arange(LANES)[:, None] // 64) == (np.arange(LANES)[None, :] // 64)).astype(f32)
    for p in range(2):
        for hh in range(2):
            lg = log_g[2 * p + hh]
            dmask[p, :, hh * BLK:(hh + 1) * BLK] = kscale * np.where(diff >= 0, np.exp(lg * np.maximum(diff, 0.0)), 0.0)
        lg_lane = log_g[2 * p + grp]
        qdec[p] = np.exp(lg_lane[None, :] * (idx[:, None] + 1.0))
        kdec[p] = kscale * np.exp(lg_lane[None, :] * (BLK - 1.0 - idx[:, None]))
        cdm[p] = np.exp(lg_lane * BLK)[:, None] * rbm

    t = np.arange(BLK)
    same_chunk = (t[:, None] // GLA_C) == (t[None, :] // GLA_C)
    mall = np.zeros((N_GLA_ROWS, BLK), f32)
    mall[0:BLK] = same_chunk & (t[None, :] <= t[:, None])
    mall[BLK:2 * BLK] = same_chunk & (t[None, :] > t[:, None])
    for lv in range(1, GLA_LEVELS + 1):
        G = 1 << lv
        mid = (t // G) * G + G // 2 - 1
        second = ((t >> (lv - 1)) & 1) == 1
        same_group = (t[:, None] // G) == (t[None, :] // G)
        as_q = second[:, None] & (t[None, :] > mid[:, None]) & (t[None, :] <= t[:, None])
        as_k = (~second)[:, None] & (t[None, :] > t[:, None]) & (t[None, :] <= mid[:, None])
        mall[(1 + lv) * BLK:(2 + lv) * BLK] = same_group & (as_q | as_k)
    ci = np.arange(GLA_C)[:, None]
    cj = np.arange(GLA_C)[None, :]
    x = ci ^ cj
    lvl = np.where(cj > ci, -1, np.where(x == 0, 0, np.floor(np.log2(np.maximum(x, 1))).astype(np.int64) + 1))
    lmask = np.stack([np.tile((lvl == lv).astype(f32), (1, GLA_HEADS)) for lv in range(GLA_LEVELS + 1)])
    smask = ((np.arange(GLA_W)[:, None] // GLA_DV) == (np.arange(LANES)[None, :] // GLA_DK)).astype(f32)
    e64 = ((np.arange(256)[:, None] // 64) == (np.arange(256)[None, :] // 64)).astype(f32) / 64.0

    return dict(bias=bias, vones=vones, dmask=dmask, qdec=qdec, kdec=kdec, cdm=cdm, rbm=rbm, mall=mall,
                lmask=lmask, smask=smask, e64=e64)


_C = _build_constants()


def _dot(a, b):
    return jnp.dot(a, b, preferred_element_type=jnp.float32)


def _dot_nt(a, b):
    return lax.dot_general(a, b, (((1,), (1,)), ((), ())), preferred_element_type=jnp.float32)


def _dot_tn(a, b):
    return lax.dot_general(a, b, (((0,), (0,)), ((), ())), preferred_element_type=jnp.float32)


def _bf(x):
    return x.astype(jnp.bfloat16)


def _split_hi_lo(x):
    hi = _bf(x)
    lo = _bf(x - hi.astype(jnp.float32))
    return hi, lo


def _silu(x):
    return x / (1.0 + jnp.exp(-x))


def _run_interleaved(gens, done):
    active = [[g, None] for g in gens]
    while active:
        progressed = False
        for item in list(active):
            if item[1] is not None and item[1] not in done:
                continue
            progressed = True
            try:
                need = next(item[0])
                while need is not None and need in done:
                    need = next(item[0])
                item[1] = need
            except StopIteration:
                active.remove(item)
        assert progressed, [item[1] for item in active]


MOD_TN = 512


def _mod_kernel(c_ref, w_ref, b_ref, o_ref):
    a = _silu(c_ref[...])
    a_hi, a_lo = _split_hi_lo(a)
    w_hi, w_lo = _split_hi_lo(w_ref[0])
    n = a.shape[0]
    both = _dot(jnp.concatenate([a_hi, a_lo], axis=0), w_hi)
    o_ref[0] = both[:n] + both[n:] + _dot(a_hi, w_lo) + b_ref[0]


def _modulation(c, w_mod, b_mod):
    depth, d, n = w_mod.shape
    bsz = c.shape[0]
    return pl.pallas_call(
        _mod_kernel,
        out_shape=jax.ShapeDtypeStruct((depth, bsz, n), jnp.float32),
        grid=(depth, n // MOD_TN),
        in_specs=[pl.BlockSpec((bsz, d), lambda l, j: (0, 0)),
                  pl.BlockSpec((1, d, MOD_TN), lambda l, j: (l, 0, j)),
                  pl.BlockSpec((1, 1, MOD_TN), lambda l, j: (l, 0, j))],
        out_specs=pl.BlockSpec((1, bsz, MOD_TN), lambda l, j: (l, 0, j)),
        compiler_params=pltpu.CompilerParams(dimension_semantics=("arbitrary", "arbitrary"),
                                             vmem_limit_bytes=VMEM_LIMIT),
        name="adaln_mod",
    )(c, w_mod, b_mod.reshape(depth, 1, n))


TRIG_T = 512


def _trig_kernel(pos_ref, freq_ref, ca_ref, sa_ref, cr_ref, sr_ref):
    ang = pos_ref[0].astype(jnp.float32) * freq_ref[0:1, :] - freq_ref[1:2, :]
    cc = jnp.cos(ang)
    lane = lax.broadcasted_iota(jnp.int32, ang.shape, 1)
    quarter = lane // 32
    sign = jnp.where((lane % 64) < 32, -1.0, 1.0)
    r32 = pltpu.roll(cc, 32, axis=1)
    r64 = pltpu.roll(cc, 64, axis=1)
    r96 = pltpu.roll(cc, 96, axis=1)

    def pick(q0, q1, q2, q3):
        return jnp.where(quarter == 0, q0, jnp.where(quarter == 1, q1, jnp.where(quarter == 2, q2, q3)))

    ca_ref[0] = pick(cc, r32, r64, r96)
    sa_ref[0] = pick(r64, r96, cc, r32) * sign
    cr_ref[0] = pick(r96, cc, r32, r64)
    sr_ref[0] = pick(r32, r64, r96, cc) * sign


def _trig_tables(positions, freq):
    bsz, seq = positions.shape
    out = jax.ShapeDtypeStruct((bsz, seq, LANES), jnp.float32)
    spec = pl.BlockSpec((1, TRIG_T, LANES), lambda b, t: (b, t, 0))
    return pl.pallas_call(
        _trig_kernel,
        out_shape=(out, out, out, out),
        grid=(bsz, seq // TRIG_T),
        in_specs=[pl.BlockSpec((1, TRIG_T, 1), lambda b, t: (b, t, 0)),
                  pl.BlockSpec((2, LANES), lambda b, t: (0, 0))],
        out_specs=(spec, spec, spec, spec),
        compiler_params=pltpu.CompilerParams(dimension_semantics=("arbitrary", "arbitrary"),
                                             vmem_limit_bytes=VMEM_LIMIT),
        name="rotary_tables",
    )(positions.reshape(bsz, seq, 1), freq)


def _seg_mean(x, e_ref):
    hi, lo = _split_hi_lo(x)
    e = e_ref[...]
    return _dot(hi, e) + _dot(lo, e)


def _rot(v, c, s, first_half):
    partner = jnp.where(first_half, pltpu.roll(v, 96, axis=1), pltpu.roll(v, 32, axis=1))
    return v * c + partner * s


def _layer_kernel(sinks_ref,
                  x_ref, mod_ref, pre_ref, post_ref, win_ref, wout_ref, gw_ref, gb_ref, gng_ref,
                  ca_ref, sa_ref, cr_ref, sr_ref,
                  bias_ref, vones_ref, dmask_ref, qdec_ref, kdec_ref, cdm_ref, rbm_ref,
                  mall_ref, lmask_ref, smask_ref, e64_ref,
                  o_ref, *scratch):
    hb_grp = scratch[0:NGRP]
    proj_grp = scratch[NGRP:2 * NGRP]
    mix_grp = scratch[2 * NGRP:3 * NGRP]
    kprev_ref, vprev_ref, rstate_ref, gstate_ref = scratch[3 * NGRP:]
    t = pl.program_id(1)

    def sub_view(refs, sb):
        return refs[sb // PGRP].at[pl.ds((sb % PGRP) * BLK, BLK)]

    hb_refs = [sub_view(hb_grp, sb) for sb in range(NSUB)]
    proj_refs = [sub_view(proj_grp, sb) for sb in range(NSUB)]
    mix_refs = [sub_view(mix_grp, sb) for sb in range(NSUB)]

    @pl.when(t == 0)
    def _():
        kprev_ref[...] = jnp.zeros_like(kprev_ref)
        vprev_ref[...] = jnp.zeros_like(vprev_ref)
        rstate_ref[...] = jnp.zeros_like(rstate_ref)
        gstate_ref[...] = jnp.zeros_like(gstate_ref)

    lane = lax.broadcasted_iota(jnp.int32, (1, LANES), 1)
    lane2 = lax.broadcasted_iota(jnp.int32, (1, GLA_W), 1)
    zero_bf = jnp.zeros((), jnp.bfloat16)
    first_half = (lane % 64) < 32
    lo_hi = [lane // 64 == g for g in range(2)]
    hgrp = [lane // GLA_DK == h for h in range(GLA_HEADS)]
    vhead = [lane2 // GLA_DV == h for h in range(GLA_HEADS)]

    def rows(sb):
        return slice(sb * BLK, (sb + 1) * BLK)

    done = set()

    def gen_norm(sb):
        if sb >= PGRP:
            yield f"projgrp{sb // PGRP - 1}.started"
        x = x_ref[0, rows(sb), :]
        inv = lax.rsqrt(jnp.mean(x * x, axis=-1, keepdims=True) + EPS)
        hb_refs[sb][...] = _bf((x * inv) * pre_ref[...] * (1.0 + mod_ref[1:2, :]) + mod_ref[0:1, :])
        done.add(f"norm{sb}")

    def gen_proj(g):
        subs = range(g * PGRP, (g + 1) * PGRP)
        for sb in subs:
            yield f"norm{sb}"
        if g > 0:
            yield f"projgrp{g - 1}.all"
        done.add(f"projgrp{g}.started")
        for name, a, b in PROJ_SEGMENTS:
            proj_grp[g][:, a:b] = _dot(hb_grp[g][...], win_ref[:, a:b])
            done.update(f"proj{sb}.{name}" for sb in subs)
            yield
        done.add(f"projgrp{g}.all")

    def gen_attn(sb):
        yield f"proj{sb}.akv"
        if sb > 0:
            yield f"attn{sb - 1}.kv"
        pr = proj_refs[sb]
        ca, sa = ca_ref[0, rows(sb), :], sa_ref[0, rows(sb), :]
        k_new = _rot(pr[:, O_AK:O_AK + 128], ca, sa, first_half) * (HEAD_DIM ** -0.5)
        v_new = pr[:, O_AV:O_AV + 128]
        k_sw = pltpu.roll(k_new, 64, axis=1)
        v_sw = pltpu.roll(v_new, 64, axis=1)
        k2, v2 = [], []
        for g in range(2):
            k_g = _bf(jnp.where(lo_hi[g], k_new, k_sw))
            v_g = _bf(jnp.where(lo_hi[g], v_new, v_sw))
            kcat = jnp.concatenate([kprev_ref[g], k_g], axis=0)
            vcat = jnp.concatenate([vprev_ref[g], v_g], axis=0)
            kprev_ref[g] = k_g
            vprev_ref[g] = v_g
            k2.append(jnp.concatenate([jnp.where(lo_hi[h], kcat, zero_bf) for h in range(2)], axis=0))
            v2_g = jnp.concatenate([jnp.where(lo_hi[h], vcat, zero_bf) for h in range(2)], axis=0)
            v2.append(jnp.concatenate([v2_g, vones_ref[...]], axis=1))
        done.add(f"attn{sb}.kv")
        if sb == 0:
            bias = bias_ref[jnp.minimum(t, 1)]
        else:
            bias = bias_ref[1]

        def scores(m):
            q_m = _rot(pr[:, O_AQ + m * 128:O_AQ + (m + 1) * 128], ca, sa, first_half)
            return _dot_nt(_bf(q_m), k2[m // 2])

        yield f"proj{sb}.aq"
        s_m = scores(0)
        yield
        for m in range(4):
            ps, mxs = [], []
            for h in range(2):
                s = s_m[:, h * 2 * BLK:(h + 1) * 2 * BLK] + bias
                mx = jnp.maximum(jnp.max(s, axis=-1, keepdims=True), sinks_ref[2 * m + h])
                ps.append(_bf(jnp.exp(s - mx)))
                mxs.append(mx)
            pv = _dot(jnp.concatenate(ps, axis=1), v2[m // 2])
            if m < 3:
                s_m = scores(m + 1)
            yield
            yield f"proj{sb}.ag"
            mx_l = jnp.where(lo_hi[0], mxs[0], mxs[1])
            sink_l = jnp.where(lo_hi[0], sinks_ref[2 * m], sinks_ref[2 * m + 1])
            den = pv[:, 128:] + jnp.exp(sink_l - mx_l)
            gate_m = _silu(pr[:, O_AG + m * 128:O_AG + (m + 1) * 128])
            mix_refs[sb][:, m * 128:(m + 1) * 128] = _bf(pv[:, :128] / den * gate_m)
        done.add(f"attn{sb}.done")

    def gen_ret(sb):
        yield f"proj{sb}.rqk"
        yield f"proj{sb}.rvg"
        if sb > 0:
            yield f"ret{sb - 1}.state"
        pr = proj_refs[sb]
        cr, sr = cr_ref[0, rows(sb), :], sr_ref[0, rows(sb), :]
        held = []
        for p in range(2):
            q_p = _rot(pr[:, O_RQ + p * 128:O_RQ + (p + 1) * 128], cr, sr, first_half)
            k_p = _rot(pr[:, O_RK + p * 128:O_RK + (p + 1) * 128], cr, sr, first_half)
            v_p = _bf(pr[:, O_RV + p * 128:O_RV + (p + 1) * 128])
            k_pb = _bf(k_p)
            kbd = jnp.concatenate([jnp.where(lo_hi[h], k_pb, zero_bf) for h in range(2)], axis=0)
            vbd = jnp.concatenate([jnp.where(lo_hi[h], v_p, zero_bf) for h in range(2)], axis=0)
            sc_raw = _dot_nt(_bf(q_p), kbd)
            st = rstate_ref[p]
            inter = _dot(_bf(q_p * qdec_ref[p]), _bf(st))
            upd = _dot_tn(_bf(k_p * kdec_ref[p]), v_p)
            held.append((sc_raw, vbd, st, inter, upd))
        yield
        intras = []
        for p in range(2):
            sc_raw, vbd, st, inter, upd = held[p]
            intras.append(_dot(_bf(sc_raw * dmask_ref[p]), vbd) + inter)
            rstate_ref[p] = st * cdm_ref[p] + upd * rbm_ref[...]
        done.add(f"ret{sb}.state")
        yield
        r = jnp.concatenate(intras, axis=1)
        ms = _seg_mean(r * r, e64_ref)
        yield
        mix_refs[sb][:, ATTN_W:ATTN_W + RET_W] = _bf(r * lax.rsqrt(ms + EPS) * _silu(pr[:, O_RG:O_RG + 256]))
        done.add(f"ret{sb}.done")

    def gen_gla(sb):
        yield f"proj{sb}.gga"
        pr = proj_refs[sb]
        logits = _dot(_bf(pr[:, O_GA:O_GA + 128]), gw_ref[...])
        yield
        logits = logits + gb_ref[...]
        log_a = (jnp.minimum(logits, 0.0) - jnp.log1p(jnp.exp(-jnp.abs(logits)))) * (1.0 / GLA_GATE_NORMALIZER)
        la_hi, la_lo = _split_hi_lo(log_a)
        xr = _dot(mall_ref[...], jnp.concatenate([la_hi, la_lo], axis=1))
        yield
        yield f"proj{sb}.gqkv"
        xs = xr[:, :128] + xr[:, 128:]
        gq = pr[:, O_GQ:O_GQ + 128] * (GLA_DK ** -0.5)
        gk = pr[:, O_GK:O_GK + 128]
        gv = _bf(pr[:, O_GV:O_GV + 256])
        nch = BLK // GLA_C
        saccs = [jnp.zeros((GLA_C, GLA_HEADS * GLA_C), jnp.float32) for _ in range(nch)]
        for lv in range(GLA_LEVELS + 1):
            raw = []
            for c in range(nch):
                r0 = c * GLA_C
                if lv == 0:
                    qt, kt = _bf(gq[r0:r0 + GLA_C]), _bf(gk[r0:r0 + GLA_C])
                else:
                    f = jnp.exp(xs[(1 + lv) * BLK + r0:(1 + lv) * BLK + r0 + GLA_C])
                    qt, kt = _bf(gq[r0:r0 + GLA_C] * f), _bf(gk[r0:r0 + GLA_C] * f)
                kbd = jnp.concatenate([jnp.where(hgrp[h], kt, zero_bf) for h in range(GLA_HEADS)], axis=0)
                raw.append(_dot_nt(qt, kbd))
            yield
            for c in range(nch):
                saccs[c] = saccs[c] + raw[c] * lmask_ref[lv]
        if sb > 0:
            yield f"gla{sb - 1}.state"
        g_parts = []
        st = gstate_ref[...]
        for c in range(nch):
            r0 = c * GLA_C
            v_c = gv[r0:r0 + GLA_C]
            vbd = jnp.concatenate([jnp.where(vhead[h], v_c, zero_bf) for h in range(GLA_HEADS)], axis=0)
            q_in = _bf(gq[r0:r0 + GLA_C] * jnp.exp(xs[r0:r0 + GLA_C]))
            o_c = _dot(_bf(saccs[c]), vbd) + _dot_nt(q_in, _bf(st))
            k_out = _bf(gk[r0:r0 + GLA_C] * jnp.exp(xs[BLK + r0:BLK + r0 + GLA_C]))
            upd = _dot_tn(v_c, k_out)
            yield
            g_parts.append(o_c)
            dec = jnp.exp(xs[r0 + GLA_C - 1:r0 + GLA_C])
            st = st * dec + upd * smask_ref[...]
        gstate_ref[...] = st
        done.add(f"gla{sb}.state")
        g = jnp.concatenate(g_parts, axis=0)
        ms = _seg_mean(g * g, e64_ref)
        yield
        gate_g = _silu(pr[:, O_GG:O_GG + 256])
        mix_refs[sb][:, ATTN_W + RET_W:] = _bf(g * lax.rsqrt(ms + EPS) * gng_ref[...] * gate_g)
        done.add(f"gla{sb}.done")

    def gen_out(g):
        for sb in range(g * PGRP, (g + 1) * PGRP):
            yield f"attn{sb}.done"
            yield f"ret{sb}.done"
            yield f"gla{sb}.done"
        rws = slice(g * PGRP * BLK, (g + 1) * PGRP * BLK)
        half = D_MODEL // 2
        y0 = _dot(mix_grp[g][...], wout_ref[:, :half])
        yield
        y1 = _dot(mix_grp[g][...], wout_ref[:, half:])
        yield
        ssq = jnp.sum(y0 * y0, axis=-1, keepdims=True) + jnp.sum(y1 * y1, axis=-1, keepdims=True)
        inv = lax.rsqrt(ssq * (1.0 / D_MODEL) + EPS)
        gate = mod_ref[2:3, :]
        o_ref[0, rws, :half] = x_ref[0, rws, :half] + gate[:, :half] * (y0 * inv * post_ref[:, :half])
        yield
        o_ref[0, rws, half:] = x_ref[0, rws, half:] + gate[:, half:] * (y1 * inv * post_ref[:, half:])

    gens = []
    for g in range(NGRP):
        for sb in range(g * PGRP, (g + 1) * PGRP):
            gens += [gen_gla(sb), gen_attn(sb), gen_ret(sb)]
        gens += [gen_out(g), gen_proj(g)] + [gen_norm(sb) for sb in range(g * PGRP, (g + 1) * PGRP)]
    _run_interleaved(gens, done)


def _const_spec(shape):
    nd = len(shape)
    return pl.BlockSpec(shape, lambda b, t, s, _n=nd: (0,) * _n)


def _layer(l, x, mod, pre_gain, post_gain, w_in_b, w_out_b, sinks, gate_w_p, gate_b, gng, tables, consts):
    bsz, seq, d = x.shape
    tb = NSUB * BLK
    tok_spec = pl.BlockSpec((1, tb, LANES), lambda b, t, s: (b, t, 0))
    stacked = [pre_gain, post_gain, w_in_b, w_out_b, gate_w_p, gate_b, gng]

    def layer_spec(a):
        return pl.BlockSpec((None,) + a.shape[1:], lambda b, t, s, _n=a.ndim - 1: (l,) + (0,) * _n)

    in_specs = ([pl.BlockSpec((1, tb, d), lambda b, t, s: (b, t, 0)),
                 pl.BlockSpec((None, None, 3, d), lambda b, t, s: (l, b, 0, 0))]
                + [layer_spec(a) for a in stacked]
                + [tok_spec] * 4
                + [_const_spec(a.shape) for a in consts])
    scratch = ([pltpu.VMEM((PGRP * BLK, d), jnp.bfloat16)] * NGRP
               + [pltpu.VMEM((PGRP * BLK, N_PROJ), jnp.float32)] * NGRP
               + [pltpu.VMEM((PGRP * BLK, d), jnp.bfloat16)] * NGRP
               + [pltpu.VMEM((2, BLK, LANES), jnp.bfloat16),
                  pltpu.VMEM((2, BLK, LANES), jnp.bfloat16),
                  pltpu.VMEM((2, LANES, LANES), jnp.float32),
                  pltpu.VMEM((GLA_W, LANES), jnp.float32)])
    grid_spec = pltpu.PrefetchScalarGridSpec(
        num_scalar_prefetch=1,
        grid=(bsz, seq // tb),
        in_specs=in_specs,
        out_specs=pl.BlockSpec((1, tb, d), lambda b, t, s: (b, t, 0)),
        scratch_shapes=scratch)
    return pl.pallas_call(
        _layer_kernel,
        out_shape=jax.ShapeDtypeStruct(x.shape, x.dtype),
        grid_spec=grid_spec,
        compiler_params=pltpu.CompilerParams(dimension_semantics=("arbitrary", "arbitrary"),
                                             vmem_limit_bytes=VMEM_LIMIT),
        name="hybrid_layer",
    )(sinks, x, mod, *stacked, *tables, *consts)


def kernel(x, c, positions, w_mod, b_mod, pre_norm_gain, post_norm_gain, w_in, attn_sinks, gla_gate_w,
           gla_gate_b, gla_norm_gain, w_out):
    depth = w_mod.shape[0]
    bsz = x.shape[0]
    f32, bf16 = jnp.float32, jnp.bfloat16

    mod = _modulation(c, w_mod, b_mod).reshape(depth, bsz, 3, D_MODEL)
    rope_freq = ROPE_THETA ** (-jnp.arange(0, HEAD_DIM, 2, dtype=f32) / HEAD_DIM)
    ret_freq = 1.0 / (10000.0 ** jnp.linspace(0.0, 1.0, RET_DK // 2, dtype=f32))
    freq = jnp.concatenate([rope_freq, ret_freq, rope_freq, ret_freq])
    phase = jnp.concatenate([jnp.zeros((LANES // 2,), f32), jnp.full((LANES // 2,), np.pi / 2, f32)])
    tables = _trig_tables(positions, jnp.stack([freq, phase]))

    consts = [jnp.asarray(_C["bias"]), jnp.asarray(_C["vones"], bf16), jnp.asarray(_C["dmask"]),
              jnp.asarray(_C["qdec"]), jnp.asarray(_C["kdec"]), jnp.asarray(_C["cdm"]), jnp.asarray(_C["rbm"]),
              jnp.asarray(_C["mall"], bf16), jnp.asarray(_C["lmask"]), jnp.asarray(_C["smask"]),
              jnp.asarray(_C["e64"], bf16)]
    w_in_b = jnp.pad(w_in.astype(bf16), ((0, 0), (0, 0), (0, N_PROJ - w_in.shape[-1])))
    w_out_b = w_out.astype(bf16)
    gate_w_p = jnp.pad(gla_gate_w, ((0, 0), (0, LANES - GLA_GATE_RANK), (0, 0))).astype(bf16)
    gng = jnp.tile(gla_norm_gain, (1, GLA_HEADS)).reshape(depth, 1, GLA_W)
    pre = pre_norm_gain.reshape(depth, 1, D_MODEL)
    post = post_norm_gain.reshape(depth, 1, D_MODEL)
    gate_b = gla_gate_b.reshape(depth, 1, LANES)

    for l in range(depth):
        x = _layer(l, x, mod, pre, post, w_in_b, w_out_b, attn_sinks[l], gate_w_p, gate_b, gng, tables, consts)
    return x
```

```python
import numpy as np
import jax
import jax.numpy as jnp
from jax import lax
from jax.experimental import pallas as pl
from jax.experimental.pallas import tpu as pltpu

D_MODEL = 1024
HEAD_DIM = 64
ATTN_HEADS = 8
ATTN_KV_HEADS = 2
WINDOW = 128
ROPE_THETA = 10000.0
RET_HEADS = 4
RET_DK = 64
RET_DV = 64
GLA_HEADS = 4
GLA_DK = 32
GLA_DV = 64
GLA_GATE_RANK = 16
GLA_GATE_NORMALIZER = 16.0
EPS = 1e-6

ATTN_W = ATTN_HEADS * HEAD_DIM
RET_W = RET_HEADS * RET_DV
GLA_W = GLA_HEADS * GLA_DV

LANES = 128
BLK = 128
NSUB = 4
PGRP = 2
NGRP = NSUB // PGRP
GLA_C = 64
GLA_LEVELS = 6
GLA_LEVELS_PER_STAGE = 7
N_GLA_ROWS = (2 + GLA_LEVELS) * BLK
NEG = -1e30
VMEM_LIMIT = 56 * 1024 * 1024
PROBE_NO_GLA, PROBE_NO_RET, PROBE_NO_ATTN = False, False, False

O_AQ, O_AK, O_AV, O_AG = 0, 512, 640, 768
O_RQ, O_RK, O_RV, O_RG = 1280, 1536, 1792, 2048
O_GQ, O_GK, O_GV, O_GG, O_GA = 2304, 2432, 2560, 2816, 3072
D_IN = O_GA + GLA_GATE_RANK
N_PROJ = 3200
PROJ_SEGMENTS = (("gga", O_GG, D_IN), ("gqkv", O_GQ, O_GG), ("akv", O_AK, O_AG), ("aq", O_AQ, O_AK),
                 ("rqk", O_RQ, O_RV), ("rvg", O_RV, O_GQ), ("ag", O_AG, O_RQ))


def _build_constants():
    f32 = np.float32
    lane = np.arange(LANES)
    grp = lane // 64

    i = np.arange(BLK)[:, None]
    j = np.arange(2 * BLK)[None, :]
    rel = i + BLK - j
    ok = (rel >= 0) & (rel < WINDOW)
    bias = np.stack([np.where(ok & (j >= BLK), 0.0, NEG), np.where(ok, 0.0, NEG)]).astype(f32)
    vones = np.zeros((2 * 2 * BLK, LANES), f32)
    for g in range(2):
        vones[g * 2 * BLK:(g + 1) * 2 * BLK] = (lane // 64 == g)[None, :]

    kscale = RET_DK ** -0.5
    log_g = np.log(1.0 - 2.0 ** (-5.0 - np.arange(RET_HEADS, dtype=np.float64)))
    idx = np.arange(BLK, dtype=np.float64)
    diff = idx[:, None] - idx[None, :]
    dmask = np.zeros((2, BLK, 2 * BLK), f32)
    qdec = np.zeros((2, BLK, LANES), f32)
    kdec = np.zeros((2, BLK, LANES), f32)
    cdm = np.zeros((2, LANES, LANES), f32)
    rbm = ((np.arange(LANES)[:, None] // 64) == (np.arange(LANES)[None, :] // 64)).astype(f32)
    for p in range(2):
        for hh in range(2):
            lg = log_g[2 * p + hh]
            dmask[p, :, hh * BLK:(hh + 1) * BLK] = kscale * np.where(diff >= 0, np.exp(lg * np.maximum(diff, 0.0)), 0.0)
        lg_lane = log_g[2 * p + grp]
        qdec[p] = np.exp(lg_lane[None, :] * (idx[:, None] + 1.0))
        kdec[p] = kscale * np.exp(lg_lane[None, :] * (BLK - 1.0 - idx[:, None]))
        cdm[p] = np.exp(lg_lane * BLK)[:, None] * rbm

    t = np.arange(BLK)
    same_chunk = (t[:, None] // GLA_C) == (t[None, :] // GLA_C)
    mall = np.zeros((N_GLA_ROWS, BLK), f32)
    mall[0:BLK] = same_chunk & (t[None, :] <= t[:, None])
    mall[BLK:2 * BLK] = same_chunk & (t[None, :] > t[:, None])
    for lv in range(1, GLA_LEVELS + 1):
        G = 1 << lv
        mid = (t // G) * G + G // 2 - 1
        second = ((t >> (lv - 1)) & 1) == 1
        same_group = (t[:, None] // G) == (t[None, :] // G)
        as_q = second[:, None] & (t[None, :] > mid[:, None]) & (t[None, :] <= t[:, None])
        as_k = (~second)[:, None] & (t[None, :] > t[:, None]) & (t[None, :] <= mid[:, None])
        mall[(1 + lv) * BLK:(2 + lv) * BLK] = same_group & (as_q | as_k)
    ci = np.arange(GLA_C)[:, None]
    cj = np.arange(GLA_C)[None, :]
    x = ci ^ cj
    lvl = np.where(cj > ci, -1, np.where(x == 0, 0, np.floor(np.log2(np.maximum(x, 1))).astype(np.int64) + 1))
    lmask = np.stack([np.tile((lvl == lv).astype(f32), (1, GLA_HEADS)) for lv in range(GLA_LEVELS + 1)])
    smask = ((np.arange(GLA_W)[:, None] // GLA_DV) == (np.arange(LANES)[None, :] // GLA_DK)).astype(f32)
    e64 = ((np.arange(256)[:, None] // 64) == (np.arange(256)[None, :] // 64)).astype(f32) / 64.0

    return dict(bias=bias, vones=vones, dmask=dmask, qdec=qdec, kdec=kdec, cdm=cdm, rbm=rbm, mall=mall,
                lmask=lmask, smask=smask, e64=e64)


_C = _build_constants()


def _dot(a, b):
    return jnp.dot(a, b, preferred_element_type=jnp.float32)


def _dot_nt(a, b):
    return lax.dot_general(a, b, (((1,), (1,)), ((), ())), preferred_element_type=jnp.float32)


def _dot_tn(a, b):
    return lax.dot_general(a, b, (((0,), (0,)), ((), ())), preferred_element_type=jnp.float32)


def _bf(x):
    return x.astype(jnp.bfloat16)


def _split_hi_lo(x):
    hi = _bf(x)
    lo = _bf(x - hi.astype(jnp.float32))
    return hi, lo


def _silu(x):
    return x / (1.0 + jnp.exp(-x))


def _run_interleaved(gens, done):
    active = [[g, None] for g in gens]
    while active:
        progressed = False
        for item in list(active):
            if item[1] is not None and item[1] not in done:
                continue
            progressed = True
            try:
                need = next(item[0])
                while need is not None and need in done:
                    need = next(item[0])
                item[1] = need
            except StopIteration:
                active.remove(item)
        assert progressed, [item[1] for item in active]


MOD_TK = 256


def _mod_kernel(c_ref, w_ref, b_ref, o_ref):
    @pl.when(pl.program_id(1) == 0)
    def _():
        o_ref[0] = jnp.broadcast_to(b_ref[0], o_ref.shape[1:])

    a = _silu(c_ref[...])
    a_hi, a_lo = _split_hi_lo(a)
    w_hi, w_lo = _split_hi_lo(w_ref[0])
    n = a.shape[0]
    both = _dot(jnp.concatenate([a_hi, a_lo], axis=0), w_hi)
    o_ref[0] += both[:n] + both[n:] + _dot(a_hi, w_lo)


def _modulation(c, w_mod, b_mod):
    depth, d, n = w_mod.shape
    bsz = c.shape[0]
    return pl.pallas_call(
        _mod_kernel,
        out_shape=jax.ShapeDtypeStruct((depth, bsz, n), jnp.float32),
        grid=(depth, d // MOD_TK),
        in_specs=[pl.BlockSpec((bsz, MOD_TK), lambda l, k: (0, k)),
                  pl.BlockSpec((1, MOD_TK, n), lambda l, k: (l, k, 0)),
                  pl.BlockSpec((1, 1, n), lambda l, k: (l, 0, 0))],
        out_specs=pl.BlockSpec((1, bsz, n), lambda l, k: (l, 0, 0)),
        compiler_params=pltpu.CompilerParams(dimension_semantics=("arbitrary", "arbitrary"),
                                             vmem_limit_bytes=VMEM_LIMIT),
        name="adaln_mod",
    )(c, w_mod, b_mod.reshape(depth, 1, n))


TRIG_T = 512


def _trig_kernel(posa_ref, posb_ref, freq_ref, ca_ref, sa_ref, cr_ref, sr_ref):
    half = TRIG_T // 2
    lane = lax.broadcasted_iota(jnp.int32, (half, LANES), 1)
    pos = jnp.where(lane < 64, posa_ref[0], posb_ref[0]).astype(jnp.float32)
    ang = pos * freq_ref[...]
    c = jnp.cos(ang)
    s = jnp.sin(ang)
    quarter = lane // 32
    sign = jnp.where((lane % 64) < 32, -1.0, 1.0)
    rc = [c] + [pltpu.roll(c, k, axis=1) for k in (32, 64, 96)]
    rs = [s] + [pltpu.roll(s, k, axis=1) for k in (32, 64, 96)]

    def spread(r, src):
        out = r[(0 - src) % 4]
        for q in (1, 2, 3):
            out = jnp.where(quarter == q, r[(q - src) % 4], out)
        return out

    for tok, rws in ((0, slice(0, half)), (1, slice(half, TRIG_T))):
        ca_ref[0, rws, :] = spread(rc, 2 * tok)
        sa_ref[0, rws, :] = spread(rs, 2 * tok) * sign
        cr_ref[0, rws, :] = spread(rc, 2 * tok + 1)
        sr_ref[0, rws, :] = spread(rs, 2 * tok + 1) * sign


def _trig_tables(positions, freq):
    bsz, seq = positions.shape
    half = TRIG_T // 2
    out = jax.ShapeDtypeStruct((bsz, seq, LANES), jnp.float32)
    spec = pl.BlockSpec((1, TRIG_T, LANES), lambda b, t: (b, t, 0))
    pos = positions.reshape(bsz, seq, 1)
    return pl.pallas_call(
        _trig_kernel,
        out_shape=(out, out, out, out),
        grid=(bsz, seq // TRIG_T),
        in_specs=[pl.BlockSpec((1, half, 1), lambda b, t: (b, 2 * t, 0)),
                  pl.BlockSpec((1, half, 1), lambda b, t: (b, 2 * t + 1, 0)),
                  pl.BlockSpec((1, LANES), lambda b, t: (0, 0))],
        out_specs=(spec, spec, spec, spec),
        compiler_params=pltpu.CompilerParams(dimension_semantics=("arbitrary", "arbitrary"),
                                             vmem_limit_bytes=VMEM_LIMIT),
        name="rotary_tables",
    )(pos, pos, freq)


def _seg_mean(x, e_ref):
    hi, lo = _split_hi_lo(x)
    both = _dot(jnp.concatenate([hi, lo], axis=0), e_ref[...])
    return both[:x.shape[0]] + both[x.shape[0]:]


def _rot(v, c, s, first_half):
    partner = jnp.where(first_half, pltpu.roll(v, 96, axis=1), pltpu.roll(v, 32, axis=1))
    return v * c + partner * s


def _layer_kernel(sinks_ref,
                  x_ref, mod_ref, pre_ref, post_ref, win_ref, wout_ref, gw_ref, gb_ref, gng_ref,
                  ca_ref, sa_ref, cr_ref, sr_ref,
                  bias_ref, vones_ref, dmask_ref, qdec_ref, kdec_ref, cdm_ref, rbm_ref,
                  mall_ref, lmask_ref, smask_ref, e64_ref,
                  o_ref, *scratch):
    hb_grp = scratch[0:NGRP]
    proj_grp = scratch[NGRP:2 * NGRP]
    mix_grp = scratch[2 * NGRP:3 * NGRP]
    kprev_ref, vprev_ref, rstate_ref, gstate_ref = scratch[3 * NGRP:]
    t = pl.program_id(1)

    def sub_view(refs, sb):
        return refs[sb // PGRP].at[pl.ds((sb % PGRP) * BLK, BLK)]

    hb_refs = [sub_view(hb_grp, sb) for sb in range(NSUB)]
    proj_refs = [sub_view(proj_grp, sb) for sb in range(NSUB)]
    mix_refs = [sub_view(mix_grp, sb) for sb in range(NSUB)]

    @pl.when(t == 0)
    def _():
        kprev_ref[...] = jnp.zeros_like(kprev_ref)
        vprev_ref[...] = jnp.zeros_like(vprev_ref)
        rstate_ref[...] = jnp.zeros_like(rstate_ref)
        gstate_ref[...] = jnp.zeros_like(gstate_ref)
        for g in range(NGRP):
            proj_grp[g][:, O_GA:] = jnp.zeros((PGRP * BLK, N_PROJ - O_GA), jnp.float32)

    lane = lax.broadcasted_iota(jnp.int32, (1, LANES), 1)
    lane2 = lax.broadcasted_iota(jnp.int32, (1, GLA_W), 1)
    zero_bf = jnp.zeros((), jnp.bfloat16)
    first_half = (lane % 64) < 32
    lo_hi = [lane // 64 == g for g in range(2)]
    hgrp = [lane // GLA_DK == h for h in range(GLA_HEADS)]
    vhead = [lane2 // GLA_DV == h for h in range(GLA_HEADS)]

    def rows(sb):
        return slice(sb * BLK, (sb + 1) * BLK)

    done = set()

    def gen_norm(sb):
        if sb >= PGRP:
            yield f"projgrp{sb // PGRP - 1}.started"
        x = x_ref[0, rows(sb), :]
        inv = lax.rsqrt(jnp.mean(x * x, axis=-1, keepdims=True) + EPS)
        hb_refs[sb][...] = _bf((x * inv) * pre_ref[...] * (1.0 + mod_ref[1:2, :]) + mod_ref[0:1, :])
        done.add(f"norm{sb}")

    def gen_proj(g):
        subs = range(g * PGRP, (g + 1) * PGRP)
        for sb in subs:
            yield f"norm{sb}"
        if g > 0:
            yield f"projgrp{g - 1}.all"
        done.add(f"projgrp{g}.started")
        for name, a, b in PROJ_SEGMENTS:
            proj_grp[g][:, a:b] = _dot(hb_grp[g][...], win_ref[:, a:b])
            done.update(f"proj{sb}.{name}" for sb in subs)
            yield
        done.add(f"projgrp{g}.all")

    def gen_attn(sb):
        yield f"proj{sb}.akv"
        if PROBE_NO_ATTN:
            mix_refs[sb][:, :ATTN_W] = jnp.zeros((BLK, ATTN_W), jnp.bfloat16)
            done.add(f"attn{sb}.kv")
            done.add(f"attn{sb}.done")
            return
        if sb > 0:
            yield f"attn{sb - 1}.kv"
        pr = proj_refs[sb]
        ca, sa = ca_ref[0, rows(sb), :], sa_ref[0, rows(sb), :]
        k_new = _rot(pr[:, O_AK:O_AK + 128], ca, sa, first_half) * (HEAD_DIM ** -0.5)
        v_new = pr[:, O_AV:O_AV + 128]
        k_sw = pltpu.roll(k_new, 64, axis=1)
        v_sw = pltpu.roll(v_new, 64, axis=1)
        k2, v2 = [], []
        for g in range(2):
            k_g = _bf(jnp.where(lo_hi[g], k_new, k_sw))
            v_g = _bf(jnp.where(lo_hi[g], v_new, v_sw))
            kcat = jnp.concatenate([kprev_ref[g], k_g], axis=0)
            vcat = jnp.concatenate([vprev_ref[g], v_g], axis=0)
            kprev_ref[g] = k_g
            vprev_ref[g] = v_g
            k2.append(jnp.concatenate([jnp.where(lo_hi[h], kcat, zero_bf) for h in range(2)], axis=0))
            v2_g = jnp.concatenate([jnp.where(lo_hi[h], vcat, zero_bf) for h in range(2)], axis=0)
            v2.append(jnp.concatenate([v2_g, vones_ref[...]], axis=1))
        done.add(f"attn{sb}.kv")
        if sb == 0:
            bias = bias_ref[jnp.minimum(t, 1)]
        else:
            bias = bias_ref[1]

        def scores(g):
            q = [_rot(pr[:, O_AQ + m * 128:O_AQ + (m + 1) * 128], ca, sa, first_half) for m in (2 * g, 2 * g + 1)]
            return _dot_nt(_bf(jnp.concatenate(q, axis=0)), k2[g])

        yield f"proj{sb}.aq"
        s_g = scores(0)
        yield
        for g in range(2):
            ps, mxs = [], []
            for j in range(2):
                row_p, row_mx = [], []
                for h in range(2):
                    s = s_g[j * BLK:(j + 1) * BLK, h * 2 * BLK:(h + 1) * 2 * BLK] + bias
                    mx = jnp.maximum(jnp.max(s, axis=-1, keepdims=True), sinks_ref[2 * (2 * g + j) + h])
                    row_p.append(_bf(jnp.exp(s - mx)))
                    row_mx.append(mx)
                ps.append(jnp.concatenate(row_p, axis=1))
                mxs.append(row_mx)
            pv = _dot(jnp.concatenate(ps, axis=0), v2[g])
            if g == 0:
                s_g = scores(1)
            yield
            yield f"proj{sb}.ag"
            for j in range(2):
                m = 2 * g + j
                pv_m = pv[j * BLK:(j + 1) * BLK]
                mx_l = jnp.where(lo_hi[0], mxs[j][0], mxs[j][1])
                sink_l = jnp.where(lo_hi[0], sinks_ref[2 * m], sinks_ref[2 * m + 1])
                den = pv_m[:, 128:] + jnp.exp(sink_l - mx_l)
                gate_m = _silu(pr[:, O_AG + m * 128:O_AG + (m + 1) * 128])
                mix_refs[sb][:, m * 128:(m + 1) * 128] = _bf(pv_m[:, :128] / den * gate_m)
        done.add(f"attn{sb}.done")

    def gen_ret(sb):
        yield f"proj{sb}.rqk"
        yield f"proj{sb}.rvg"
        if PROBE_NO_RET:
            mix_refs[sb][:, ATTN_W:ATTN_W + RET_W] = jnp.zeros((BLK, RET_W), jnp.bfloat16)
            done.add(f"ret{sb}.state")
            done.add(f"ret{sb}.done")
            return
        if sb > 0:
            yield f"ret{sb - 1}.state"
        pr = proj_refs[sb]
        cr, sr = cr_ref[0, rows(sb), :], sr_ref[0, rows(sb), :]
        held = []
        for p in range(2):
            q_p = _rot(pr[:, O_RQ + p * 128:O_RQ + (p + 1) * 128], cr, sr, first_half)
            k_p = _rot(pr[:, O_RK + p * 128:O_RK + (p + 1) * 128], cr, sr, first_half)
            v_p = _bf(pr[:, O_RV + p * 128:O_RV + (p + 1) * 128])
            k_pb = _bf(k_p)
            kbd = jnp.concatenate([jnp.where(lo_hi[h], k_pb, zero_bf) for h in range(2)], axis=0)
            vbd = jnp.concatenate([jnp.where(lo_hi[h], v_p, zero_bf) for h in range(2)], axis=0)
            sc_raw = _dot_nt(_bf(q_p), kbd)
            st = rstate_ref[p]
            inter = _dot(_bf(q_p * qdec_ref[p]), _bf(st))
            upd = _dot_tn(_bf(k_p * kdec_ref[p]), v_p)
            held.append((sc_raw, vbd, st, inter, upd))
        yield
        intras = []
        for p in range(2):
            sc_raw, vbd, st, inter, upd = held[p]
            intras.append(_dot(_bf(sc_raw * dmask_ref[p]), vbd) + inter)
            rstate_ref[p] = st * cdm_ref[p] + upd * rbm_ref[...]
        done.add(f"ret{sb}.state")
        yield
        r = jnp.concatenate(intras, axis=1)
        ms = _seg_mean(r * r, e64_ref)
        yield
        mix_refs[sb][:, ATTN_W:ATTN_W + RET_W] = _bf(r * lax.rsqrt(ms + EPS) * _silu(pr[:, O_RG:O_RG + 256]))
        done.add(f"ret{sb}.done")

    def gen_gla(sb):
        yield f"proj{sb}.gga"
        if PROBE_NO_GLA:
            mix_refs[sb][:, ATTN_W + RET_W:] = jnp.zeros((BLK, GLA_W), jnp.bfloat16)
            done.add(f"gla{sb}.state")
            done.add(f"gla{sb}.done")
            return
        pr = proj_refs[sb]
        logits = _dot(_bf(pr[:, O_GA:O_GA + 128]), gw_ref[...])
        yield
        logits = logits + gb_ref[...]
        log_a = (jnp.minimum(logits, 0.0) - jnp.log1p(jnp.exp(-jnp.abs(logits)))) * (1.0 / GLA_GATE_NORMALIZER)
        la_hi, la_lo = _split_hi_lo(log_a)
        xr = _dot(mall_ref[...], jnp.concatenate([la_hi, la_lo], axis=1))
        yield
        yield f"proj{sb}.gqkv"
        xs = xr[:, :128] + xr[:, 128:]
        gq = pr[:, O_GQ:O_GQ + 128] * (GLA_DK ** -0.5)
        gk = pr[:, O_GK:O_GK + 128]
        gv = _bf(pr[:, O_GV:O_GV + 256])
        nch = BLK // GLA_C
        saccs = [jnp.zeros((GLA_C, GLA_HEADS * GLA_C), jnp.float32) for _ in range(nch)]
        pending = []
        for lv in range(GLA_LEVELS + 1):
            raw = []
            for c in range(nch):
                r0 = c * GLA_C
                if lv == 0:
                    qt, kt = _bf(gq[r0:r0 + GLA_C]), _bf(gk[r0:r0 + GLA_C])
                else:
                    f = jnp.exp(xs[(1 + lv) * BLK + r0:(1 + lv) * BLK + r0 + GLA_C])
                    qt, kt = _bf(gq[r0:r0 + GLA_C] * f), _bf(gk[r0:r0 + GLA_C] * f)
                kbd = jnp.concatenate([jnp.where(hgrp[h], kt, zero_bf) for h in range(GLA_HEADS)], axis=0)
                raw.append(_dot_nt(qt, kbd))
            pending.append((lv, raw))
            if len(pending) == GLA_LEVELS_PER_STAGE or lv == GLA_LEVELS:
                yield
                for lv_p, raw_p in pending:
                    for c in range(nch):
                        saccs[c] = saccs[c] + raw_p[c] * lmask_ref[lv_p]
                pending = []
        if sb > 0:
            yield f"gla{sb - 1}.state"
        g_parts = []
        st = gstate_ref[...]
        for c in range(nch):
            r0 = c * GLA_C
            v_c = gv[r0:r0 + GLA_C]
            vbd = jnp.concatenate([jnp.where(vhead[h], v_c, zero_bf) for h in range(GLA_HEADS)], axis=0)
            q_in = _bf(gq[r0:r0 + GLA_C] * jnp.exp(xs[r0:r0 + GLA_C]))
            o_c = _dot(_bf(saccs[c]), vbd) + _dot_nt(q_in, _bf(st))
            k_out = _bf(gk[r0:r0 + GLA_C] * jnp.exp(xs[BLK + r0:BLK + r0 + GLA_C]))
            upd = _dot_tn(v_c, k_out)
            yield
            g_parts.append(o_c)
            dec = jnp.exp(xs[r0 + GLA_C - 1:r0 + GLA_C])
            st = st * dec + upd * smask_ref[...]
        gstate_ref[...] = st
        done.add(f"gla{sb}.state")
        g = jnp.concatenate(g_parts, axis=0)
        ms = _seg_mean(g * g, e64_ref)
        yield
        gate_g = _silu(pr[:, O_GG:O_GG + 256])
        mix_refs[sb][:, ATTN_W + RET_W:] = _bf(g * lax.rsqrt(ms + EPS) * gng_ref[...] * gate_g)
        done.add(f"gla{sb}.done")

    def gen_out(g):
        for sb in range(g * PGRP, (g + 1) * PGRP):
            yield f"attn{sb}.done"
            yield f"ret{sb}.done"
            yield f"gla{sb}.done"
        rws = slice(g * PGRP * BLK, (g + 1) * PGRP * BLK)
        half = D_MODEL // 2
        y0 = _dot(mix_grp[g][...], wout_ref[:, :half])
        yield
        y1 = _dot(mix_grp[g][...], wout_ref[:, half:])
        yield
        ssq = jnp.sum(y0 * y0, axis=-1, keepdims=True) + jnp.sum(y1 * y1, axis=-1, keepdims=True)
        inv = lax.rsqrt(ssq * (1.0 / D_MODEL) + EPS)
        gate = mod_ref[2:3, :]
        o_ref[0, rws, :half] = x_ref[0, rws, :half] + gate[:, :half] * (y0 * inv * post_ref[:, :half])
        yield
        o_ref[0, rws, half:] = x_ref[0, rws, half:] + gate[:, half:] * (y1 * inv * post_ref[:, half:])

    gens = []
    for g in range(NGRP):
        for sb in range(g * PGRP, (g + 1) * PGRP):
            gens += [gen_gla(sb), gen_attn(sb), gen_ret(sb)]
        gens += [gen_out(g), gen_proj(g)] + [gen_norm(sb) for sb in range(g * PGRP, (g + 1) * PGRP)]
    _run_interleaved(gens, done)


def _const_spec(shape):
    nd = len(shape)
    return pl.BlockSpec(shape, lambda b, t, s, _n=nd: (0,) * _n)


def _layer(l, x, mod, pre_gain, post_gain, w_in_b, w_out_b, sinks, gate_w_p, gate_b, gng, tables, consts):
    bsz, seq, d = x.shape
    tb = NSUB * BLK
    tok_spec = pl.BlockSpec((1, tb, LANES), lambda b, t, s: (b, t, 0))
    stacked = [pre_gain, post_gain, w_in_b, w_out_b, gate_w_p, gate_b, gng]

    def layer_spec(a):
        return pl.BlockSpec((None,) + a.shape[1:], lambda b, t, s, _n=a.ndim - 1: (l,) + (0,) * _n)

    in_specs = ([pl.BlockSpec((1, tb, d), lambda b, t, s: (b, t, 0)),
                 pl.BlockSpec((None, None, 3, d), lambda b, t, s: (l, b, 0, 0))]
                + [layer_spec(a) for a in stacked]
                + [tok_spec] * 4
                + [_const_spec(a.shape) for a in consts])
    scratch = ([pltpu.VMEM((PGRP * BLK, d), jnp.bfloat16)] * NGRP
               + [pltpu.VMEM((PGRP * BLK, N_PROJ), jnp.float32)] * NGRP
               + [pltpu.VMEM((PGRP * BLK, d), jnp.bfloat16)] * NGRP
               + [pltpu.VMEM((2, BLK, LANES), jnp.bfloat16),
                  pltpu.VMEM((2, BLK, LANES), jnp.bfloat16),
                  pltpu.VMEM((2, LANES, LANES), jnp.float32),
                  pltpu.VMEM((GLA_W, LANES), jnp.float32)])
    grid_spec = pltpu.PrefetchScalarGridSpec(
        num_scalar_prefetch=1,
        grid=(bsz, seq // tb),
        in_specs=in_specs,
        out_specs=pl.BlockSpec((1, tb, d), lambda b, t, s: (b, t, 0)),
        scratch_shapes=scratch)
    return pl.pallas_call(
        _layer_kernel,
        out_shape=jax.ShapeDtypeStruct(x.shape, x.dtype),
        grid_spec=grid_spec,
        compiler_params=pltpu.CompilerParams(dimension_semantics=("arbitrary", "arbitrary"),
                                             vmem_limit_bytes=VMEM_LIMIT),
        name="hybrid_layer",
    )(sinks, x, mod, *stacked, *tables, *consts)


def kernel(x, c, positions, w_mod, b_mod, pre_norm_gain, post_norm_gain, w_in, attn_sinks, gla_gate_w,
           gla_gate_b, gla_norm_gain, w_out):
    depth = w_mod.shape[0]
    bsz = x.shape[0]
    f32, bf16 = jnp.float32, jnp.bfloat16

    mod = _modulation(c, w_mod, b_mod).reshape(depth, bsz, 3, D_MODEL)
    rope_freq = ROPE_THETA ** (-jnp.arange(0, HEAD_DIM, 2, dtype=f32) / HEAD_DIM)
    ret_freq = 1.0 / (10000.0 ** jnp.linspace(0.0, 1.0, RET_DK // 2, dtype=f32))
    freq = jnp.concatenate([rope_freq, ret_freq, rope_freq, ret_freq]).reshape(1, LANES)
    tables = _trig_tables(positions, freq)

    consts = [jnp.asarray(_C["bias"]), jnp.asarray(_C["vones"], bf16), jnp.asarray(_C["dmask"]),
              jnp.asarray(_C["qdec"]), jnp.asarray(_C["kdec"]), jnp.asarray(_C["cdm"]), jnp.asarray(_C["rbm"]),
              jnp.asarray(_C["mall"], bf16), jnp.asarray(_C["lmask"]), jnp.asarray(_C["smask"]),
              jnp.asarray(_C["e64"], bf16)]
    w_in_b = w_in.astype(bf16)
    w_out_b = w_out.astype(bf16)
    gate_w_p = jnp.pad(gla_gate_w, ((0, 0), (0, LANES - GLA_GATE_RANK), (0, 0))).astype(bf16)
    gng = jnp.tile(gla_norm_gain, (1, GLA_HEADS)).reshape(depth, 1, GLA_W)
    pre = pre_norm_gain.reshape(depth, 1, D_MODEL)
    post = post_norm_gain.reshape(depth, 1, D_MODEL)
    gate_b = gla_gate_b.reshape(depth, 1, LANES)

    for l in range(depth):
        x = _layer(l, x, mod, pre, post, w_in_b, w_out_b, attn_sinks[l], gate_w_p, gate_b, gng, tables, consts)
    return x
```

```python
import numpy as np
import jax
import jax.numpy as jnp
from jax import lax
from jax.experimental import pallas as pl
from jax.experimental.pallas import tpu as pltpu

D_MODEL = 1024
HEAD_DIM = 64
ATTN_HEADS = 8
ATTN_KV_HEADS = 2
WINDOW = 128
ROPE_THETA = 10000.0
RET_HEADS = 4
RET_DK = 64
RET_DV = 64
GLA_HEADS = 4
GLA_DK = 32
GLA_DV = 64
GLA_GATE_RANK = 16
GLA_GATE_NORMALIZER = 16.0
EPS = 1e-6

ATTN_W = ATTN_HEADS * HEAD_DIM
RET_W = RET_HEADS * RET_DV
GLA_W = GLA_HEADS * GLA_DV

LANES = 128
BLK = 128
NSUB = 4
PGRP = 2
NGRP = NSUB // PGRP
GLA_C = 64
GLA_LEVELS = 6
GLA_LEVELS_PER_STAGE = 7
NEG = -1e30
VMEM_LIMIT = 56 * 1024 * 1024
PROBE_NO_GLA, PROBE_NO_RET, PROBE_NO_ATTN = False, False, False

O_AQ, O_AK, O_AV, O_AG = 0, 512, 640, 768
O_RQ, O_RK, O_RV, O_RG = 1280, 1536, 1792, 2048
O_GQ, O_GK, O_GV, O_GG, O_GA = 2304, 2432, 2560, 2816, 3072
D_IN = O_GA + GLA_GATE_RANK
N_PROJ = 3200
PROJ_SEGMENTS = (("gga", O_GG, D_IN), ("gqkv", O_GQ, O_GG), ("akv", O_AK, O_AG), ("aq", O_AQ, O_AK),
                 ("rqk", O_RQ, O_RV), ("rvg", O_RV, O_GQ), ("ag", O_AG, O_RQ))


def _build_constants():
    f32 = np.float32
    lane = np.arange(LANES)
    grp = lane // 64

    i = np.arange(BLK)[:, None]
    j = np.arange(2 * BLK)[None, :]
    rel = i + BLK - j
    ok = (rel >= 0) & (rel < WINDOW)
    bias = np.stack([np.where(ok & (j >= BLK), 0.0, NEG), np.where(ok, 0.0, NEG)]).astype(f32)
    vones = np.zeros((2 * 2 * BLK, LANES), f32)
    for g in range(2):
        vones[g * 2 * BLK:(g + 1) * 2 * BLK] = (lane // 64 == g)[None, :]

    kscale = RET_DK ** -0.5
    log_g = np.log(1.0 - 2.0 ** (-5.0 - np.arange(RET_HEADS, dtype=np.float64)))
    idx = np.arange(BLK, dtype=np.float64)
    diff = idx[:, None] - idx[None, :]
    dmask = np.zeros((2, BLK, 2 * BLK), f32)
    qdec = np.zeros((2, BLK, LANES), f32)
    kdec = np.zeros((2, BLK, LANES), f32)
    cdm = np.zeros((2, LANES, LANES), f32)
    rbm = ((np.arange(LANES)[:, None] // 64) == (np.arange(LANES)[None, :] // 64)).astype(f32)
    for p in range(2):
        for hh in range(2):
            lg = log_g[2 * p + hh]
            dmask[p, :, hh * BLK:(hh + 1) * BLK] = kscale * np.where(diff >= 0, np.exp(lg * np.maximum(diff, 0.0)), 0.0)
        lg_lane = log_g[2 * p + grp]
        qdec[p] = np.exp(lg_lane[None, :] * (idx[:, None] + 1.0))
        kdec[p] = kscale * np.exp(lg_lane[None, :] * (BLK - 1.0 - idx[:, None]))
        cdm[p] = np.exp(lg_lane * BLK)[:, None] * rbm

    t = np.arange(BLK)
    tri = (((t[:, None] // GLA_C) == (t[None, :] // GLA_C)) & (t[None, :] <= t[:, None])).astype(f32)
    ci = np.arange(GLA_C)[:, None]
    cj = np.arange(GLA_C)[None, :]
    x = ci ^ cj
    lvl = np.where(cj > ci, -1, np.where(x == 0, 0, np.floor(np.log2(np.maximum(x, 1))).astype(np.int64) + 1))
    lmask = np.stack([np.tile((lvl == lv).astype(f32), (1, GLA_HEADS)) for lv in range(GLA_LEVELS + 1)])
    smask = ((np.arange(GLA_W)[:, None] // GLA_DV) == (np.arange(LANES)[None, :] // GLA_DK)).astype(f32)
    e64 = ((np.arange(256)[:, None] // 64) == (np.arange(256)[None, :] // 64)).astype(f32) / 64.0

    return dict(bias=bias, vones=vones, dmask=dmask, qdec=qdec, kdec=kdec, cdm=cdm, rbm=rbm, tri=tri,
                lmask=lmask, smask=smask, e64=e64)


_C = _build_constants()


def _dot(a, b):
    return jnp.dot(a, b, preferred_element_type=jnp.float32)


def _dot_nt(a, b):
    return lax.dot_general(a, b, (((1,), (1,)), ((), ())), preferred_element_type=jnp.float32)


def _dot_tn(a, b):
    return lax.dot_general(a, b, (((0,), (0,)), ((), ())), preferred_element_type=jnp.float32)


def _bf(x):
    return x.astype(jnp.bfloat16)


def _split_hi_lo(x):
    hi = _bf(x)
    lo = _bf(x - hi.astype(jnp.float32))
    return hi, lo


def _silu(x):
    return x / (1.0 + jnp.exp(-x))


def _run_interleaved(gens, done):
    active = [[g, None] for g in gens]
    while active:
        progressed = False
        for item in list(active):
            if item[1] is not None and item[1] not in done:
                continue
            progressed = True
            try:
                need = next(item[0])
                while need is not None and need in done:
                    need = next(item[0])
                item[1] = need
            except StopIteration:
                active.remove(item)
        assert progressed, [item[1] for item in active]


MOD_TK = 256


def _mod_kernel(c_ref, w_ref, b_ref, o_ref):
    @pl.when(pl.program_id(1) == 0)
    def _():
        o_ref[0] = jnp.broadcast_to(b_ref[0], o_ref.shape[1:])

    a = _silu(c_ref[...])
    a_hi, a_lo = _split_hi_lo(a)
    w_hi, w_lo = _split_hi_lo(w_ref[0])
    n = a.shape[0]
    both = _dot(jnp.concatenate([a_hi, a_lo], axis=0), w_hi)
    o_ref[0] += both[:n] + both[n:] + _dot(a_hi, w_lo)


def _modulation(c, w_mod, b_mod):
    depth, d, n = w_mod.shape
    bsz = c.shape[0]
    return pl.pallas_call(
        _mod_kernel,
        out_shape=jax.ShapeDtypeStruct((depth, bsz, n), jnp.float32),
        grid=(depth, d // MOD_TK),
        in_specs=[pl.BlockSpec((bsz, MOD_TK), lambda l, k: (0, k)),
                  pl.BlockSpec((1, MOD_TK, n), lambda l, k: (l, k, 0)),
                  pl.BlockSpec((1, 1, n), lambda l, k: (l, 0, 0))],
        out_specs=pl.BlockSpec((1, bsz, n), lambda l, k: (l, 0, 0)),
        compiler_params=pltpu.CompilerParams(dimension_semantics=("arbitrary", "arbitrary"),
                                             vmem_limit_bytes=VMEM_LIMIT),
        name="adaln_mod",
    )(c, w_mod, b_mod.reshape(depth, 1, n))


TRIG_T = 512


def _trig_kernel(posa_ref, posb_ref, freq_ref, ca_ref, sa_ref, cr_ref, sr_ref):
    half = TRIG_T // 2
    lane = lax.broadcasted_iota(jnp.int32, (half, LANES), 1)
    pos = jnp.where(lane < 64, posa_ref[0], posb_ref[0]).astype(jnp.float32)
    ang = pos * freq_ref[...]
    c = jnp.cos(ang)
    s = jnp.sin(ang)
    quarter = lane // 32
    sign = jnp.where((lane % 64) < 32, -1.0, 1.0)
    rc = [c] + [pltpu.roll(c, k, axis=1) for k in (32, 64, 96)]
    rs = [s] + [pltpu.roll(s, k, axis=1) for k in (32, 64, 96)]

    def spread(r, src):
        out = r[(0 - src) % 4]
        for q in (1, 2, 3):
            out = jnp.where(quarter == q, r[(q - src) % 4], out)
        return out

    for tok, rws in ((0, slice(0, half)), (1, slice(half, TRIG_T))):
        ca_ref[0, rws, :] = spread(rc, 2 * tok)
        sa_ref[0, rws, :] = spread(rs, 2 * tok) * sign
        cr_ref[0, rws, :] = spread(rc, 2 * tok + 1)
        sr_ref[0, rws, :] = spread(rs, 2 * tok + 1) * sign


def _trig_tables(positions, freq):
    bsz, seq = positions.shape
    half = TRIG_T // 2
    out = jax.ShapeDtypeStruct((bsz, seq, LANES), jnp.float32)
    spec = pl.BlockSpec((1, TRIG_T, LANES), lambda b, t: (b, t, 0))
    pos = positions.reshape(bsz, seq, 1)
    return pl.pallas_call(
        _trig_kernel,
        out_shape=(out, out, out, out),
        grid=(bsz, seq // TRIG_T),
        in_specs=[pl.BlockSpec((1, half, 1), lambda b, t: (b, 2 * t, 0)),
                  pl.BlockSpec((1, half, 1), lambda b, t: (b, 2 * t + 1, 0)),
                  pl.BlockSpec((1, LANES), lambda b, t: (0, 0))],
        out_specs=(spec, spec, spec, spec),
        compiler_params=pltpu.CompilerParams(dimension_semantics=("arbitrary", "arbitrary"),
                                             vmem_limit_bytes=VMEM_LIMIT),
        name="rotary_tables",
    )(pos, pos, freq)


def _seg_mean(x, e_ref):
    return _dot(_bf(x), e_ref[...])


def _rot(v, c, s, first_half):
    partner = jnp.where(first_half, pltpu.roll(v, 96, axis=1), pltpu.roll(v, 32, axis=1))
    return v * c + partner * s


def _layer_kernel(sinks_ref,
                  x_ref, mod_ref, pre_ref, post_ref, win_ref, wout_ref, gw_ref, gb_ref, gng_ref,
                  ca_ref, sa_ref, cr_ref, sr_ref,
                  bias_ref, vones_ref, dmask_ref, qdec_ref, kdec_ref, cdm_ref, rbm_ref,
                  tri_ref, lmask_ref, smask_ref, e64_ref,
                  o_ref, *scratch):
    hb_grp = scratch[0:NGRP]
    proj_grp = scratch[NGRP:2 * NGRP]
    mix_grp = scratch[2 * NGRP:3 * NGRP]
    kprev_ref, vprev_ref, rstate_ref, gstate_ref, bscr_ref = scratch[3 * NGRP:]
    t = pl.program_id(1)

    def sub_view(refs, sb):
        return refs[sb // PGRP].at[pl.ds((sb % PGRP) * BLK, BLK)]

    hb_refs = [sub_view(hb_grp, sb) for sb in range(NSUB)]
    proj_refs = [sub_view(proj_grp, sb) for sb in range(NSUB)]
    mix_refs = [sub_view(mix_grp, sb) for sb in range(NSUB)]

    @pl.when(t == 0)
    def _():
        kprev_ref[...] = jnp.zeros_like(kprev_ref)
        vprev_ref[...] = jnp.zeros_like(vprev_ref)
        rstate_ref[...] = jnp.zeros_like(rstate_ref)
        gstate_ref[...] = jnp.zeros_like(gstate_ref)
        for g in range(NGRP):
            proj_grp[g][:, O_GA:] = jnp.zeros((PGRP * BLK, N_PROJ - O_GA), jnp.float32)

    lane = lax.broadcasted_iota(jnp.int32, (1, LANES), 1)
    lane2 = lax.broadcasted_iota(jnp.int32, (1, GLA_W), 1)
    zero_bf = jnp.zeros((), jnp.bfloat16)
    first_half = (lane % 64) < 32
    lo_hi = [lane // 64 == g for g in range(2)]
    hgrp = [lane // GLA_DK == h for h in range(GLA_HEADS)]
    vhead = [lane2 // GLA_DV == h for h in range(GLA_HEADS)]

    def rows(sb):
        return slice(sb * BLK, (sb + 1) * BLK)

    done = set()

    def gen_norm(sb):
        if sb >= PGRP:
            yield f"projgrp{sb // PGRP - 1}.started"
        x = x_ref[0, rows(sb), :]
        inv = lax.rsqrt(jnp.mean(x * x, axis=-1, keepdims=True) + EPS)
        hb_refs[sb][...] = _bf((x * inv) * (pre_ref[...] * (1.0 + mod_ref[1:2, :])) + mod_ref[0:1, :])
        done.add(f"norm{sb}")

    def gen_proj(g):
        subs = range(g * PGRP, (g + 1) * PGRP)
        for sb in subs:
            yield f"norm{sb}"
        if g > 0:
            yield f"projgrp{g - 1}.all"
        done.add(f"projgrp{g}.started")
        for name, a, b in PROJ_SEGMENTS:
            proj_grp[g][:, a:b] = _dot(hb_grp[g][...], win_ref[:, a:b])
            done.update(f"proj{sb}.{name}" for sb in subs)
            yield
        done.add(f"projgrp{g}.all")

    def gen_attn(sb):
        yield f"proj{sb}.akv"
        if PROBE_NO_ATTN:
            mix_refs[sb][:, :ATTN_W] = jnp.zeros((BLK, ATTN_W), jnp.bfloat16)
            done.add(f"attn{sb}.kv")
            done.add(f"attn{sb}.done")
            return
        if sb > 0:
            yield f"attn{sb - 1}.kv"
        pr = proj_refs[sb]
        ca, sa = ca_ref[0, rows(sb), :], sa_ref[0, rows(sb), :]
        k_new = _rot(pr[:, O_AK:O_AK + 128], ca, sa, first_half) * (HEAD_DIM ** -0.5)
        v_new = pr[:, O_AV:O_AV + 128]
        k_sw = pltpu.roll(k_new, 64, axis=1)
        v_sw = pltpu.roll(v_new, 64, axis=1)
        k2, v2 = [], []
        for g in range(2):
            k_g = _bf(jnp.where(lo_hi[g], k_new, k_sw))
            v_g = _bf(jnp.where(lo_hi[g], v_new, v_sw))
            kcat = jnp.concatenate([kprev_ref[g], k_g], axis=0)
            vcat = jnp.concatenate([vprev_ref[g], v_g], axis=0)
            kprev_ref[g] = k_g
            vprev_ref[g] = v_g
            k2.append(jnp.concatenate([jnp.where(lo_hi[h], kcat, zero_bf) for h in range(2)], axis=0))
            v2_g = jnp.concatenate([jnp.where(lo_hi[h], vcat, zero_bf) for h in range(2)], axis=0)
            v2.append(jnp.concatenate([v2_g, vones_ref[...]], axis=1))
        done.add(f"attn{sb}.kv")
        if sb == 0:
            bias = bias_ref[jnp.minimum(t, 1)]
        else:
            bias = bias_ref[1]

        def scores(g):
            q = [_rot(pr[:, O_AQ + m * 128:O_AQ + (m + 1) * 128], ca, sa, first_half) for m in (2 * g, 2 * g + 1)]
            return _dot_nt(_bf(jnp.concatenate(q, axis=0)), k2[g])

        yield f"proj{sb}.aq"
        s_g = scores(0)
        yield
        for g in range(2):
            ps, mxs = [], []
            for j in range(2):
                row_p, row_mx = [], []
                for h in range(2):
                    s = s_g[j * BLK:(j + 1) * BLK, h * 2 * BLK:(h + 1) * 2 * BLK] + bias
                    mx = jnp.maximum(jnp.max(s, axis=-1, keepdims=True), sinks_ref[2 * (2 * g + j) + h])
                    row_p.append(_bf(jnp.exp(s - mx)))
                    row_mx.append(mx)
                ps.append(jnp.concatenate(row_p, axis=1))
                mxs.append(row_mx)
            pv = _dot(jnp.concatenate(ps, axis=0), v2[g])
            if g == 0:
                s_g = scores(1)
            yield
            yield f"proj{sb}.ag"
            for j in range(2):
                m = 2 * g + j
                pv_m = pv[j * BLK:(j + 1) * BLK]
                mx_l = jnp.where(lo_hi[0], mxs[j][0], mxs[j][1])
                sink_l = jnp.where(lo_hi[0], sinks_ref[2 * m], sinks_ref[2 * m + 1])
                den = pv_m[:, 128:] + jnp.exp(sink_l - mx_l)
                gate_m = _silu(pr[:, O_AG + m * 128:O_AG + (m + 1) * 128])
                mix_refs[sb][:, m * 128:(m + 1) * 128] = _bf(pv_m[:, :128] / den * gate_m)
        done.add(f"attn{sb}.done")

    def gen_ret(sb):
        yield f"proj{sb}.rqk"
        yield f"proj{sb}.rvg"
        if PROBE_NO_RET:
            mix_refs[sb][:, ATTN_W:ATTN_W + RET_W] = jnp.zeros((BLK, RET_W), jnp.bfloat16)
            done.add(f"ret{sb}.state")
            done.add(f"ret{sb}.done")
            return
        if sb > 0:
            yield f"ret{sb - 1}.state"
        pr = proj_refs[sb]
        cr, sr = cr_ref[0, rows(sb), :], sr_ref[0, rows(sb), :]
        held = []
        for p in range(2):
            q_p = _rot(pr[:, O_RQ + p * 128:O_RQ + (p + 1) * 128], cr, sr, first_half)
            k_p = _rot(pr[:, O_RK + p * 128:O_RK + (p + 1) * 128], cr, sr, first_half)
            v_p = _bf(pr[:, O_RV + p * 128:O_RV + (p + 1) * 128])
            k_pb = _bf(k_p)
            kbd = jnp.concatenate([jnp.where(lo_hi[h], k_pb, zero_bf) for h in range(2)], axis=0)
            vbd = jnp.concatenate([jnp.where(lo_hi[h], v_p, zero_bf) for h in range(2)], axis=0)
            sc_raw = _dot_nt(_bf(q_p), kbd)
            st = rstate_ref[p]
            inter = _dot(_bf(q_p * qdec_ref[p]), _bf(st))
            upd = _dot_tn(_bf(k_p * kdec_ref[p]), v_p)
            held.append((sc_raw, vbd, st, inter, upd))
        yield
        intras = []
        for p in range(2):
            sc_raw, vbd, st, inter, upd = held[p]
            intras.append(_dot(_bf(sc_raw * dmask_ref[p]), vbd) + inter)
            rstate_ref[p] = st * cdm_ref[p] + upd * rbm_ref[...]
        done.add(f"ret{sb}.state")
        yield
        r = jnp.concatenate(intras, axis=1)
        ms = _seg_mean(r * r, e64_ref)
        yield
        mix_refs[sb][:, ATTN_W:ATTN_W + RET_W] = _bf(r * lax.rsqrt(ms + EPS) * _silu(pr[:, O_RG:O_RG + 256]))
        done.add(f"ret{sb}.done")

    def gen_gla(sb):
        yield f"proj{sb}.gga"
        if PROBE_NO_GLA:
            mix_refs[sb][:, ATTN_W + RET_W:] = jnp.zeros((BLK, GLA_W), jnp.bfloat16)
            done.add(f"gla{sb}.state")
            done.add(f"gla{sb}.done")
            return
        pr = proj_refs[sb]
        logits = _dot(_bf(pr[:, O_GA:O_GA + 128]), gw_ref[...])
        yield
        logits = logits + gb_ref[...]
        log_a = (jnp.minimum(logits, 0.0) - jnp.log1p(jnp.exp(-jnp.abs(logits)))) * (1.0 / GLA_GATE_NORMALIZER)
        la_hi, la_lo = _split_hi_lo(log_a)
        xr = _dot(tri_ref[...], jnp.concatenate([la_hi, la_lo], axis=1))
        yield
        yield f"proj{sb}.gqkv"
        b = xr[:, :128] + xr[:, 128:]
        bscr = bscr_ref.at[sb]
        bscr[...] = b
        gq = pr[:, O_GQ:O_GQ + 128] * (GLA_DK ** -0.5)
        gk = pr[:, O_GK:O_GK + 128]
        gv = _bf(pr[:, O_GV:O_GV + 256])
        nch = BLK // GLA_C
        row = lax.broadcasted_iota(jnp.int32, (GLA_C, LANES), 0)

        def level_factor(c, lv):
            r0 = c * GLA_C
            b_c = b[r0:r0 + GLA_C]
            grp = 1 << lv
            if grp >= 8:
                mids = [bscr[r0 + g0 + grp // 2 - 1:r0 + g0 + grp // 2, :] for g0 in range(0, GLA_C, grp)]
                ref = jnp.concatenate([jnp.broadcast_to(m, (grp, LANES)) for m in mids], axis=0)
            elif grp == 4:
                r = row % 4
                ref = jnp.where(r == 0, pltpu.roll(b_c, GLA_C - 1, axis=0),
                                jnp.where(r == 1, b_c,
                                          jnp.where(r == 2, pltpu.roll(b_c, 1, axis=0), pltpu.roll(b_c, 2, axis=0))))
            else:
                ref = jnp.where(row % 2 == 1, pltpu.roll(b_c, 1, axis=0), b_c)
            return jnp.exp(-jnp.abs(b_c - ref))

        saccs = [jnp.zeros((GLA_C, GLA_HEADS * GLA_C), jnp.float32) for _ in range(nch)]
        pending = []
        for lv in range(GLA_LEVELS + 1):
            raw = []
            for c in range(nch):
                r0 = c * GLA_C
                if lv == 0:
                    qt, kt = _bf(gq[r0:r0 + GLA_C]), _bf(gk[r0:r0 + GLA_C])
                else:
                    f = level_factor(c, lv)
                    qt, kt = _bf(gq[r0:r0 + GLA_C] * f), _bf(gk[r0:r0 + GLA_C] * f)
                kbd = jnp.concatenate([jnp.where(hgrp[h], kt, zero_bf) for h in range(GLA_HEADS)], axis=0)
                raw.append(_dot_nt(qt, kbd))
            pending.append((lv, raw))
            if len(pending) == GLA_LEVELS_PER_STAGE or lv == GLA_LEVELS:
                yield
                for lv_p, raw_p in pending:
                    for c in range(nch):
                        saccs[c] = saccs[c] + raw_p[c] * lmask_ref[lv_p]
                pending = []
        if sb > 0:
            yield f"gla{sb - 1}.state"
        g_parts = []
        st = gstate_ref[...]
        for c in range(nch):
            r0 = c * GLA_C
            v_c = gv[r0:r0 + GLA_C]
            vbd = jnp.concatenate([jnp.where(vhead[h], v_c, zero_bf) for h in range(GLA_HEADS)], axis=0)
            b_c = b[r0:r0 + GLA_C]
            b_last = bscr[r0 + GLA_C - 1:r0 + GLA_C, :]
            q_in = _bf(gq[r0:r0 + GLA_C] * jnp.exp(b_c))
            o_c = _dot(_bf(saccs[c]), vbd) + _dot_nt(q_in, _bf(st))
            k_out = _bf(gk[r0:r0 + GLA_C] * jnp.exp(b_last - b_c))
            upd = _dot_tn(v_c, k_out)
            yield
            g_parts.append(o_c)
            st = st * jnp.exp(b_last) + upd * smask_ref[...]
        gstate_ref[...] = st
        done.add(f"gla{sb}.state")
        g = jnp.concatenate(g_parts, axis=0)
        ms = _seg_mean(g * g, e64_ref)
        yield
        gate_g = _silu(pr[:, O_GG:O_GG + 256])
        mix_refs[sb][:, ATTN_W + RET_W:] = _bf(g * lax.rsqrt(ms + EPS) * gng_ref[...] * gate_g)
        done.add(f"gla{sb}.done")

    def gen_out(g):
        for sb in range(g * PGRP, (g + 1) * PGRP):
            yield f"attn{sb}.done"
            yield f"ret{sb}.done"
            yield f"gla{sb}.done"
        rws = slice(g * PGRP * BLK, (g + 1) * PGRP * BLK)
        half = D_MODEL // 2
        y0 = _dot(mix_grp[g][...], wout_ref[:, :half])
        yield
        y1 = _dot(mix_grp[g][...], wout_ref[:, half:])
        yield
        ssq = jnp.sum(y0 * y0, axis=-1, keepdims=True) + jnp.sum(y1 * y1, axis=-1, keepdims=True)
        inv = lax.rsqrt(ssq * (1.0 / D_MODEL) + EPS)
        gp = mod_ref[2:3, :] * post_ref[...]
        o_ref[0, rws, :half] = x_ref[0, rws, :half] + (y0 * inv) * gp[:, :half]
        yield
        o_ref[0, rws, half:] = x_ref[0, rws, half:] + (y1 * inv) * gp[:, half:]

    gens = []
    for g in range(NGRP):
        for sb in range(g * PGRP, (g + 1) * PGRP):
            gens += [gen_gla(sb), gen_attn(sb), gen_ret(sb)]
        gens += [gen_out(g), gen_proj(g)] + [gen_norm(sb) for sb in range(g * PGRP, (g + 1) * PGRP)]
    _run_interleaved(gens, done)


def _const_spec(shape):
    nd = len(shape)
    return pl.BlockSpec(shape, lambda b, t, s, _n=nd: (0,) * _n)


def _layer(l, x, mod, pre_gain, post_gain, w_in_b, w_out_b, sinks, gate_w_p, gate_b, gng, tables, consts):
    bsz, seq, d = x.shape
    tb = NSUB * BLK
    tok_spec = pl.BlockSpec((1, tb, LANES), lambda b, t, s: (b, t, 0))
    stacked = [pre_gain, post_gain, w_in_b, w_out_b, gate_w_p, gate_b, gng]

    def layer_spec(a):
        return pl.BlockSpec((None,) + a.shape[1:], lambda b, t, s, _n=a.ndim - 1: (l,) + (0,) * _n)

    in_specs = ([pl.BlockSpec((1, tb, d), lambda b, t, s: (b, t, 0)),
                 pl.BlockSpec((None, None, 3, d), lambda b, t, s: (l, b, 0, 0))]
                + [layer_spec(a) for a in stacked]
                + [tok_spec] * 4
                + [_const_spec(a.shape) for a in consts])
    scratch = ([pltpu.VMEM((PGRP * BLK, d), jnp.bfloat16)] * NGRP
               + [pltpu.VMEM((PGRP * BLK, N_PROJ), jnp.float32)] * NGRP
               + [pltpu.VMEM((PGRP * BLK, d), jnp.bfloat16)] * NGRP
               + [pltpu.VMEM((2, BLK, LANES), jnp.bfloat16),
                  pltpu.VMEM((2, BLK, LANES), jnp.bfloat16),
                  pltpu.VMEM((2, LANES, LANES), jnp.float32),
                  pltpu.VMEM((GLA_W, LANES), jnp.float32),
                  pltpu.VMEM((NSUB, BLK, LANES), jnp.float32)])
    grid_spec = pltpu.PrefetchScalarGridSpec(
        num_scalar_prefetch=1,
        grid=(bsz, seq // tb),
        in_specs=in_specs,
        out_specs=pl.BlockSpec((1, tb, d), lambda b, t, s: (b, t, 0)),
        scratch_shapes=scratch)
    return pl.pallas_call(
        _layer_kernel,
        out_shape=jax.ShapeDtypeStruct(x.shape, x.dtype),
        grid_spec=grid_spec,
        compiler_params=pltpu.CompilerParams(dimension_semantics=("arbitrary", "arbitrary"),
                                             vmem_limit_bytes=VMEM_LIMIT),
        name="hybrid_layer",
    )(sinks, x, mod, *stacked, *tables, *consts)


def kernel(x, c, positions, w_mod, b_mod, pre_norm_gain, post_norm_gain, w_in, attn_sinks, gla_gate_w,
           gla_gate_b, gla_norm_gain, w_out):
    depth = w_mod.shape[0]
    bsz = x.shape[0]
    f32, bf16 = jnp.float32, jnp.bfloat16

    mod = _modulation(c, w_mod, b_mod).reshape(depth, bsz, 3, D_MODEL)
    rope_freq = ROPE_THETA ** (-jnp.arange(0, HEAD_DIM, 2, dtype=f32) / HEAD_DIM)
    ret_freq = 1.0 / (10000.0 ** jnp.linspace(0.0, 1.0, RET_DK // 2, dtype=f32))
    freq = jnp.concatenate([rope_freq, ret_freq, rope_freq, ret_freq]).reshape(1, LANES)
    tables = _trig_tables(positions, freq)

    consts = [jnp.asarray(_C["bias"]), jnp.asarray(_C["vones"], bf16), jnp.asarray(_C["dmask"]),
              jnp.asarray(_C["qdec"]), jnp.asarray(_C["kdec"]), jnp.asarray(_C["cdm"]), jnp.asarray(_C["rbm"]),
              jnp.asarray(_C["tri"], bf16), jnp.asarray(_C["lmask"]), jnp.asarray(_C["smask"]),
              jnp.asarray(_C["e64"], bf16)]
    w_in_b = w_in.astype(bf16)
    w_out_b = w_out.astype(bf16)
    gate_w_p = jnp.pad(gla_gate_w, ((0, 0), (0, LANES - GLA_GATE_RANK), (0, 0))).astype(bf16)
    gng = jnp.tile(gla_norm_gain, (1, GLA_HEADS)).reshape(depth, 1, GLA_W)
    pre = pre_norm_gain.reshape(depth, 1, D_MODEL)
    post = post_norm_gain.reshape(depth, 1, D_MODEL)
    gate_b = gla_gate_b.reshape(depth, 1, LANES)

    for l in range(depth):
        x = _layer(l, x, mod, pre, post, w_in_b, w_out_b, attn_sinks[l], gate_w_p, gate_b, gng, tables, consts)
    return x
```

```python
import numpy as np
import jax
import jax.numpy as jnp
from jax import lax
from jax.experimental import pallas as pl
from jax.experimental.pallas import tpu as pltpu

D_MODEL = 1024
HEAD_DIM = 64
ATTN_HEADS = 8
ATTN_KV_HEADS = 2
WINDOW = 128
ROPE_THETA = 10000.0
RET_HEADS = 4
RET_DK = 64
RET_DV = 64
GLA_HEADS = 4
GLA_DK = 32
GLA_DV = 64
GLA_GATE_RANK = 16
GLA_GATE_NORMALIZER = 16.0
EPS = 1e-6

ATTN_W = ATTN_HEADS * HEAD_DIM
RET_W = RET_HEADS * RET_DV
GLA_W = GLA_HEADS * GLA_DV

LANES = 128
BLK = 128
NSUB = 8
PGRP = 2
NGRP = NSUB // PGRP
GLA_C = 64
GLA_LEVELS = 6
GLA_LEVELS_PER_STAGE = 7
NEG = -1e30
VMEM_LIMIT = 56 * 1024 * 1024
PROBE_NO_GLA, PROBE_NO_RET, PROBE_NO_ATTN = False, False, False

O_AQ, O_AK, O_AV, O_AG = 0, 512, 640, 768
O_RQ, O_RK, O_RV, O_RG = 1280, 1536, 1792, 2048
O_GQ, O_GK, O_GV, O_GG, O_GA = 2304, 2432, 2560, 2816, 3072
D_IN = O_GA + GLA_GATE_RANK
N_PROJ = 3200
PROJ_SEGMENTS = (("gga", O_GG, N_PROJ), ("gqkv", O_GQ, O_GG), ("akv", O_AK, O_AG), ("aq", O_AQ, O_AK),
                 ("rqk", O_RQ, O_RV), ("rvg", O_RV, O_GQ), ("ag", O_AG, O_RQ))


def _build_constants():
    f32 = np.float32
    lane = np.arange(LANES)
    grp = lane // 64

    i = np.arange(BLK)[:, None]
    j = np.arange(2 * BLK)[None, :]
    rel = i + BLK - j
    ok = (rel >= 0) & (rel < WINDOW)
    bias = np.stack([np.where(ok & (j >= BLK), 0.0, NEG), np.where(ok, 0.0, NEG)]).astype(f32)
    vones = np.zeros((2 * 2 * BLK, LANES), f32)
    for g in range(2):
        vones[g * 2 * BLK:(g + 1) * 2 * BLK] = (lane // 64 == g)[None, :]

    kscale = RET_DK ** -0.5
    log_g = np.log(1.0 - 2.0 ** (-5.0 - np.arange(RET_HEADS, dtype=np.float64)))
    idx = np.arange(BLK, dtype=np.float64)
    diff = idx[:, None] - idx[None, :]
    dmask = np.zeros((2, BLK, 2 * BLK), f32)
    qdec = np.zeros((2, BLK, LANES), f32)
    kdec = np.zeros((2, BLK, LANES), f32)
    cdm = np.zeros((2, LANES, LANES), f32)
    rbm = ((np.arange(LANES)[:, None] // 64) == (np.arange(LANES)[None, :] // 64)).astype(f32)
    for p in range(2):
        for hh in range(2):
            lg = log_g[2 * p + hh]
            dmask[p, :, hh * BLK:(hh + 1) * BLK] = kscale * np.where(diff >= 0, np.exp(lg * np.maximum(diff, 0.0)), 0.0)
        lg_lane = log_g[2 * p + grp]
        qdec[p] = np.exp(lg_lane[None, :] * (idx[:, None] + 1.0))
        kdec[p] = kscale * np.exp(lg_lane[None, :] * (BLK - 1.0 - idx[:, None]))
        cdm[p] = np.exp(lg_lane * BLK)[:, None] * rbm

    t = np.arange(BLK)
    tri = (((t[:, None] // GLA_C) == (t[None, :] // GLA_C)) & (t[None, :] <= t[:, None])).astype(f32)
    ci = np.arange(GLA_C)[:, None]
    cj = np.arange(GLA_C)[None, :]
    x = ci ^ cj
    lvl = np.where(cj > ci, -1, np.where(x == 0, 0, np.floor(np.log2(np.maximum(x, 1))).astype(np.int64) + 1))
    lmask = np.stack([np.tile((lvl == lv).astype(f32), (1, GLA_HEADS)) for lv in range(GLA_LEVELS + 1)])
    smask = ((np.arange(GLA_W)[:, None] // GLA_DV) == (np.arange(LANES)[None, :] // GLA_DK)).astype(f32)
    e64 = ((np.arange(256)[:, None] // 64) == (np.arange(256)[None, :] // 64)).astype(f32) / 64.0

    return dict(bias=bias, vones=vones, dmask=dmask, qdec=qdec, kdec=kdec, cdm=cdm, rbm=rbm, tri=tri,
                lmask=lmask, smask=smask, e64=e64)


_C = _build_constants()


def _dot(a, b):
    return jnp.dot(a, b, preferred_element_type=jnp.float32)


def _dot_nt(a, b):
    return lax.dot_general(a, b, (((1,), (1,)), ((), ())), preferred_element_type=jnp.float32)


def _dot_tn(a, b):
    return lax.dot_general(a, b, (((0,), (0,)), ((), ())), preferred_element_type=jnp.float32)


def _bf(x):
    return x.astype(jnp.bfloat16)


def _split_hi_lo(x):
    hi = _bf(x)
    lo = _bf(x - hi.astype(jnp.float32))
    return hi, lo


def _silu(x):
    return x / (1.0 + jnp.exp(-x))


def _run_interleaved(gens, done):
    active = [[g, None] for g in gens]
    while active:
        progressed = False
        for item in list(active):
            if item[1] is not None and item[1] not in done:
                continue
            progressed = True
            try:
                need = next(item[0])
                while need is not None and need in done:
                    need = next(item[0])
                item[1] = need
            except StopIteration:
                active.remove(item)
        assert progressed, [item[1] for item in active]


MOD_TK = 256


def _mod_kernel(c_ref, w_ref, b_ref, o_ref):
    @pl.when(pl.program_id(1) == 0)
    def _():
        o_ref[0] = jnp.broadcast_to(b_ref[0], o_ref.shape[1:])

    a = _silu(c_ref[...])
    a_hi, a_lo = _split_hi_lo(a)
    w_hi, w_lo = _split_hi_lo(w_ref[0])
    n = a.shape[0]
    both = _dot(jnp.concatenate([a_hi, a_lo], axis=0), w_hi)
    o_ref[0] += both[:n] + both[n:] + _dot(a_hi, w_lo)


def _modulation(c, w_mod, b_mod):
    depth, d, n = w_mod.shape
    bsz = c.shape[0]
    return pl.pallas_call(
        _mod_kernel,
        out_shape=jax.ShapeDtypeStruct((depth, bsz, n), jnp.float32),
        grid=(depth, d // MOD_TK),
        in_specs=[pl.BlockSpec((bsz, MOD_TK), lambda l, k: (0, k)),
                  pl.BlockSpec((1, MOD_TK, n), lambda l, k: (l, k, 0)),
                  pl.BlockSpec((1, 1, n), lambda l, k: (l, 0, 0))],
        out_specs=pl.BlockSpec((1, bsz, n), lambda l, k: (l, 0, 0)),
        compiler_params=pltpu.CompilerParams(dimension_semantics=("arbitrary", "arbitrary"),
                                             vmem_limit_bytes=VMEM_LIMIT),
        name="adaln_mod",
    )(c, w_mod, b_mod.reshape(depth, 1, n))


CAST_TR = 256


def _cast_kernel(w_ref, o_ref):
    w = w_ref[0]
    o_ref[0, :, :O_GA] = _bf(w[:, :O_GA])
    tail = jnp.concatenate([w[:, O_GA:], jnp.zeros((w.shape[0], N_PROJ - D_IN), w.dtype)], axis=1)
    o_ref[0, :, O_GA:] = _bf(tail)


def _cast_w_in(w_in):
    depth, d, n = w_in.shape
    return pl.pallas_call(
        _cast_kernel,
        out_shape=jax.ShapeDtypeStruct((depth, d, N_PROJ), jnp.bfloat16),
        grid=(depth, d // CAST_TR),
        in_specs=[pl.BlockSpec((1, CAST_TR, n), lambda l, r: (l, r, 0))],
        out_specs=pl.BlockSpec((1, CAST_TR, N_PROJ), lambda l, r: (l, r, 0)),
        compiler_params=pltpu.CompilerParams(dimension_semantics=("arbitrary", "arbitrary"),
                                             vmem_limit_bytes=VMEM_LIMIT),
        name="cast_w_in",
    )(w_in)


TRIG_T = 512


def _trig_kernel(posa_ref, posb_ref, freq_ref, ca_ref, sa_ref, cr_ref, sr_ref):
    half = TRIG_T // 2
    lane = lax.broadcasted_iota(jnp.int32, (half, LANES), 1)
    pos = jnp.where(lane < 64, posa_ref[0], posb_ref[0]).astype(jnp.float32)
    ang = pos * freq_ref[...]
    c = jnp.cos(ang)
    s = jnp.sin(ang)
    quarter = lane // 32
    sign = jnp.where((lane % 64) < 32, -1.0, 1.0)
    rc = [c] + [pltpu.roll(c, k, axis=1) for k in (32, 64, 96)]
    rs = [s] + [pltpu.roll(s, k, axis=1) for k in (32, 64, 96)]

    def spread(r, src):
        out = r[(0 - src) % 4]
        for q in (1, 2, 3):
            out = jnp.where(quarter == q, r[(q - src) % 4], out)
        return out

    for tok, rws in ((0, slice(0, half)), (1, slice(half, TRIG_T))):
        ca_ref[0, rws, :] = spread(rc, 2 * tok)
        sa_ref[0, rws, :] = spread(rs, 2 * tok) * sign
        cr_ref[0, rws, :] = spread(rc, 2 * tok + 1)
        sr_ref[0, rws, :] = spread(rs, 2 * tok + 1) * sign


def _trig_tables(positions, freq):
    bsz, seq = positions.shape
    half = TRIG_T // 2
    out = jax.ShapeDtypeStruct((bsz, seq, LANES), jnp.float32)
    spec = pl.BlockSpec((1, TRIG_T, LANES), lambda b, t: (b, t, 0))
    pos = positions.reshape(bsz, seq, 1)
    return pl.pallas_call(
        _trig_kernel,
        out_shape=(out, out, out, out),
        grid=(bsz, seq // TRIG_T),
        in_specs=[pl.BlockSpec((1, half, 1), lambda b, t: (b, 2 * t, 0)),
                  pl.BlockSpec((1, half, 1), lambda b, t: (b, 2 * t + 1, 0)),
                  pl.BlockSpec((1, LANES), lambda b, t: (0, 0))],
        out_specs=(spec, spec, spec, spec),
        compiler_params=pltpu.CompilerParams(dimension_semantics=("arbitrary", "arbitrary"),
                                             vmem_limit_bytes=VMEM_LIMIT),
        name="rotary_tables",
    )(pos, pos, freq)


def _seg_mean(x, e_ref):
    return _dot(_bf(x), e_ref[...])


def _rot(v, c, s, first_half):
    partner = jnp.where(first_half, pltpu.roll(v, 96, axis=1), pltpu.roll(v, 32, axis=1))
    return v * c + partner * s


def _layer_kernel(sinks_ref,
                  x_ref, mod_ref, pre_ref, post_ref, win_ref, wout_ref, gw_ref, gb_ref, gng_ref,
                  ca_ref, sa_ref, cr_ref, sr_ref,
                  bias_ref, vones_ref, dmask_ref, qdec_ref, kdec_ref, cdm_ref, rbm_ref,
                  tri_ref, lmask_ref, smask_ref, e64_ref,
                  o_ref, *scratch):
    hb_grp = scratch[0:NGRP]
    proj_grp = scratch[NGRP:2 * NGRP]
    mix_grp = scratch[2 * NGRP:3 * NGRP]
    kprev_ref, vprev_ref, rstate_ref, gstate_ref, bscr_ref = scratch[3 * NGRP:]
    t = pl.program_id(1)

    def sub_view(refs, sb):
        return refs[sb // PGRP].at[pl.ds((sb % PGRP) * BLK, BLK)]

    hb_refs = [sub_view(hb_grp, sb) for sb in range(NSUB)]
    proj_refs = [sub_view(proj_grp, sb) for sb in range(NSUB)]
    mix_refs = [sub_view(mix_grp, sb) for sb in range(NSUB)]

    @pl.when(t == 0)
    def _():
        kprev_ref[...] = jnp.zeros_like(kprev_ref)
        vprev_ref[...] = jnp.zeros_like(vprev_ref)
        rstate_ref[...] = jnp.zeros_like(rstate_ref)
        gstate_ref[...] = jnp.zeros_like(gstate_ref)

    lane = lax.broadcasted_iota(jnp.int32, (1, LANES), 1)
    lane2 = lax.broadcasted_iota(jnp.int32, (1, GLA_W), 1)
    zero_bf = jnp.zeros((), jnp.bfloat16)
    first_half = (lane % 64) < 32
    lo_hi = [lane // 64 == g for g in range(2)]
    hgrp = [lane // GLA_DK == h for h in range(GLA_HEADS)]
    vhead = [lane2 // GLA_DV == h for h in range(GLA_HEADS)]

    def rows(sb):
        return slice(sb * BLK, (sb + 1) * BLK)

    done = set()

    def gen_norm(sb):
        if sb >= PGRP:
            yield f"projgrp{sb // PGRP - 1}.started"
        x = x_ref[0, rows(sb), :]
        inv = lax.rsqrt(jnp.mean(x * x, axis=-1, keepdims=True) + EPS)
        hb_refs[sb][...] = _bf((x * inv) * (pre_ref[...] * (1.0 + mod_ref[1:2, :])) + mod_ref[0:1, :])
        done.add(f"norm{sb}")

    def gen_proj(g):
        subs = range(g * PGRP, (g + 1) * PGRP)
        for sb in subs:
            yield f"norm{sb}"
        if g > 0:
            yield f"projgrp{g - 1}.all"
        done.add(f"projgrp{g}.started")
        for name, a, b in PROJ_SEGMENTS:
            proj_grp[g][:, a:b] = _dot(hb_grp[g][...], win_ref[:, a:b])
            done.update(f"proj{sb}.{name}" for sb in subs)
            yield
        done.add(f"projgrp{g}.all")

    def gen_attn(sb):
        yield f"proj{sb}.akv"
        if PROBE_NO_ATTN:
            mix_refs[sb][:, :ATTN_W] = jnp.zeros((BLK, ATTN_W), jnp.bfloat16)
            done.add(f"attn{sb}.kv")
            done.add(f"attn{sb}.done")
            return
        if sb > 0:
            yield f"attn{sb - 1}.kv"
        pr = proj_refs[sb]
        ca, sa = ca_ref[0, rows(sb), :], sa_ref[0, rows(sb), :]
        k_new = _rot(pr[:, O_AK:O_AK + 128], ca, sa, first_half) * (HEAD_DIM ** -0.5)
        v_new = pr[:, O_AV:O_AV + 128]
        k_sw = pltpu.roll(k_new, 64, axis=1)
        v_sw = pltpu.roll(v_new, 64, axis=1)
        k2, v2 = [], []
        for g in range(2):
            k_g = _bf(jnp.where(lo_hi[g], k_new, k_sw))
            v_g = _bf(jnp.where(lo_hi[g], v_new, v_sw))
            kcat = jnp.concatenate([kprev_ref[g], k_g], axis=0)
            vcat = jnp.concatenate([vprev_ref[g], v_g], axis=0)
            kprev_ref[g] = k_g
            vprev_ref[g] = v_g
            k2.append(jnp.concatenate([jnp.where(lo_hi[h], kcat, zero_bf) for h in range(2)], axis=0))
            v2_g = jnp.concatenate([jnp.where(lo_hi[h], vcat, zero_bf) for h in range(2)], axis=0)
            v2.append(jnp.concatenate([v2_g, vones_ref[...]], axis=1))
        done.add(f"attn{sb}.kv")
        if sb == 0:
            bias = bias_ref[jnp.minimum(t, 1)]
        else:
            bias = bias_ref[1]

        def scores(g):
            q = [_rot(pr[:, O_AQ + m * 128:O_AQ + (m + 1) * 128], ca, sa, first_half) for m in (2 * g, 2 * g + 1)]
            return _dot_nt(_bf(jnp.concatenate(q, axis=0)), k2[g])

        yield f"proj{sb}.aq"
        s_g = scores(0)
        yield
        for g in range(2):
            ps, mxs = [], []
            for j in range(2):
                row_p, row_mx = [], []
                for h in range(2):
                    s = s_g[j * BLK:(j + 1) * BLK, h * 2 * BLK:(h + 1) * 2 * BLK] + bias
                    mx = jnp.maximum(jnp.max(s, axis=-1, keepdims=True), sinks_ref[2 * (2 * g + j) + h])
                    row_p.append(_bf(jnp.exp(s - mx)))
                    row_mx.append(mx)
                ps.append(jnp.concatenate(row_p, axis=1))
                mxs.append(row_mx)
            pv = _dot(jnp.concatenate(ps, axis=0), v2[g])
            if g == 0:
                s_g = scores(1)
            yield
            yield f"proj{sb}.ag"
            for j in range(2):
                m = 2 * g + j
                pv_m = pv[j * BLK:(j + 1) * BLK]
                mx_l = jnp.where(lo_hi[0], mxs[j][0], mxs[j][1])
                sink_l = jnp.where(lo_hi[0], sinks_ref[2 * m], sinks_ref[2 * m + 1])
                den = pv_m[:, 128:] + jnp.exp(sink_l - mx_l)
                gate_m = _silu(pr[:, O_AG + m * 128:O_AG + (m + 1) * 128])
                mix_refs[sb][:, m * 128:(m + 1) * 128] = _bf(pv_m[:, :128] / den * gate_m)
        done.add(f"attn{sb}.done")

    def gen_ret(sb):
        yield f"proj{sb}.rqk"
        yield f"proj{sb}.rvg"
        if PROBE_NO_RET:
            mix_refs[sb][:, ATTN_W:ATTN_W + RET_W] = jnp.zeros((BLK, RET_W), jnp.bfloat16)
            done.add(f"ret{sb}.state")
            done.add(f"ret{sb}.done")
            return
        if sb > 0:
            yield f"ret{sb - 1}.state"
        pr = proj_refs[sb]
        cr, sr = cr_ref[0, rows(sb), :], sr_ref[0, rows(sb), :]
        held = []
        for p in range(2):
            q_p = _rot(pr[:, O_RQ + p * 128:O_RQ + (p + 1) * 128], cr, sr, first_half)
            k_p = _rot(pr[:, O_RK + p * 128:O_RK + (p + 1) * 128], cr, sr, first_half)
            v_p = _bf(pr[:, O_RV + p * 128:O_RV + (p + 1) * 128])
            k_pb = _bf(k_p)
            kbd = jnp.concatenate([jnp.where(lo_hi[h], k_pb, zero_bf) for h in range(2)], axis=0)
            vbd = jnp.concatenate([jnp.where(lo_hi[h], v_p, zero_bf) for h in range(2)], axis=0)
            sc_raw = _dot_nt(_bf(q_p), kbd)
            st = rstate_ref[p]
            inter = _dot(_bf(q_p * qdec_ref[p]), _bf(st))
            upd = _dot_tn(_bf(k_p * kdec_ref[p]), v_p)
            held.append((sc_raw, vbd, st, inter, upd))
        yield
        intras = []
        for p in range(2):
            sc_raw, vbd, st, inter, upd = held[p]
            intras.append(_dot(_bf(sc_raw * dmask_ref[p]), vbd) + inter)
            rstate_ref[p] = st * cdm_ref[p] + upd * rbm_ref[...]
        done.add(f"ret{sb}.state")
        yield
        r = jnp.concatenate(intras, axis=1)
        ms = _seg_mean(r * r, e64_ref)
        yield
        mix_refs[sb][:, ATTN_W:ATTN_W + RET_W] = _bf(r * lax.rsqrt(ms + EPS) * _silu(pr[:, O_RG:O_RG + 256]))
        done.add(f"ret{sb}.done")

    def gen_gla(sb):
        yield f"proj{sb}.gga"
        if PROBE_NO_GLA:
            mix_refs[sb][:, ATTN_W + RET_W:] = jnp.zeros((BLK, GLA_W), jnp.bfloat16)
            done.add(f"gla{sb}.state")
            done.add(f"gla{sb}.done")
            return
        pr = proj_refs[sb]
        logits = _dot(_bf(pr[:, O_GA:O_GA + 128]), gw_ref[...])
        yield
        logits = logits + gb_ref[...]
        log_a = (jnp.minimum(logits, 0.0) - jnp.log1p(jnp.exp(-jnp.abs(logits)))) * (1.0 / GLA_GATE_NORMALIZER)
        la_hi, la_lo = _split_hi_lo(log_a)
        xr = _dot(tri_ref[...], jnp.concatenate([la_hi, la_lo], axis=1))
        yield
        yield f"proj{sb}.gqkv"
        b = xr[:, :128] + xr[:, 128:]
        bscr = bscr_ref.at[sb]
        bscr[...] = b
        gq = pr[:, O_GQ:O_GQ + 128] * (GLA_DK ** -0.5)
        gk = pr[:, O_GK:O_GK + 128]
        gv = _bf(pr[:, O_GV:O_GV + 256])
        nch = BLK // GLA_C
        row = lax.broadcasted_iota(jnp.int32, (GLA_C, LANES), 0)

        def level_factor(c, lv):
            r0 = c * GLA_C
            b_c = b[r0:r0 + GLA_C]
            grp = 1 << lv
            if grp >= 8:
                mids = [bscr[r0 + g0 + grp // 2 - 1:r0 + g0 + grp // 2, :] for g0 in range(0, GLA_C, grp)]
                ref = jnp.concatenate([jnp.broadcast_to(m, (grp, LANES)) for m in mids], axis=0)
            elif grp == 4:
                r = row % 4
                ref = jnp.where(r == 0, pltpu.roll(b_c, GLA_C - 1, axis=0),
                                jnp.where(r == 1, b_c,
                                          jnp.where(r == 2, pltpu.roll(b_c, 1, axis=0), pltpu.roll(b_c, 2, axis=0))))
            else:
                ref = jnp.where(row % 2 == 1, pltpu.roll(b_c, 1, axis=0), b_c)
            return jnp.exp(-jnp.abs(b_c - ref))

        saccs = [jnp.zeros((GLA_C, GLA_HEADS * GLA_C), jnp.float32) for _ in range(nch)]
        pending = []
        for lv in range(GLA_LEVELS + 1):
            raw = []
            for c in range(nch):
                r0 = c * GLA_C
                if lv == 0:
                    qt, kt = _bf(gq[r0:r0 + GLA_C]), _bf(gk[r0:r0 + GLA_C])
                else:
                    f = level_factor(c, lv)
                    qt, kt = _bf(gq[r0:r0 + GLA_C] * f), _bf(gk[r0:r0 + GLA_C] * f)
                kbd = jnp.concatenate([jnp.where(hgrp[h], kt, zero_bf) for h in range(GLA_HEADS)], axis=0)
                raw.append(_dot_nt(qt, kbd))
            pending.append((lv, raw))
            if len(pending) == GLA_LEVELS_PER_STAGE or lv == GLA_LEVELS:
                yield
                for lv_p, raw_p in pending:
                    for c in range(nch):
                        saccs[c] = saccs[c] + raw_p[c] * lmask_ref[lv_p]
                pending = []
        if sb > 0:
            yield f"gla{sb - 1}.state"
        g_parts = []
        st = gstate_ref[...]
        for c in range(nch):
            r0 = c * GLA_C
            v_c = gv[r0:r0 + GLA_C]
            vbd = jnp.concatenate([jnp.where(vhead[h], v_c, zero_bf) for h in range(GLA_HEADS)], axis=0)
            b_c = b[r0:r0 + GLA_C]
            b_last = bscr[r0 + GLA_C - 1:r0 + GLA_C, :]
            q_in = _bf(gq[r0:r0 + GLA_C] * jnp.exp(b_c))
            o_c = _dot(_bf(saccs[c]), vbd) + _dot_nt(q_in, _bf(st))
            k_out = _bf(gk[r0:r0 + GLA_C] * jnp.exp(b_last - b_c))
            upd = _dot_tn(v_c, k_out)
            yield
            g_parts.append(o_c)
            st = st * jnp.exp(b_last) + upd * smask_ref[...]
        gstate_ref[...] = st
        done.add(f"gla{sb}.state")
        g = jnp.concatenate(g_parts, axis=0)
        ms = _seg_mean(g * g, e64_ref)
        yield
        gate_g = _silu(pr[:, O_GG:O_GG + 256])
        mix_refs[sb][:, ATTN_W + RET_W:] = _bf(g * lax.rsqrt(ms + EPS) * gng_ref[...] * gate_g)
        done.add(f"gla{sb}.done")

    def gen_out(g):
        for sb in range(g * PGRP, (g + 1) * PGRP):
            yield f"attn{sb}.done"
            yield f"ret{sb}.done"
            yield f"gla{sb}.done"
        rws = slice(g * PGRP * BLK, (g + 1) * PGRP * BLK)
        half = D_MODEL // 2
        y0 = _dot(mix_grp[g][...], wout_ref[:, :half])
        yield
        y1 = _dot(mix_grp[g][...], wout_ref[:, half:])
        yield
        ssq = jnp.sum(y0 * y0, axis=-1, keepdims=True) + jnp.sum(y1 * y1, axis=-1, keepdims=True)
        inv = lax.rsqrt(ssq * (1.0 / D_MODEL) + EPS)
        gp = mod_ref[2:3, :] * post_ref[...]
        o_ref[0, rws, :half] = x_ref[0, rws, :half] + (y0 * inv) * gp[:, :half]
        yield
        o_ref[0, rws, half:] = x_ref[0, rws, half:] + (y1 * inv) * gp[:, half:]

    gens = []
    for g in range(NGRP):
        for sb in range(g * PGRP, (g + 1) * PGRP):
            gens += [gen_gla(sb), gen_attn(sb), gen_ret(sb)]
        gens += [gen_out(g), gen_proj(g)] + [gen_norm(sb) for sb in range(g * PGRP, (g + 1) * PGRP)]
    _run_interleaved(gens, done)


def _const_spec(shape):
    nd = len(shape)
    return pl.BlockSpec(shape, lambda b, t, s, _n=nd: (0,) * _n)


def _layer(l, x, mod, pre_gain, post_gain, w_in_b, w_out_b, sinks, gate_w_p, gate_b, gng, tables, consts):
    bsz, seq, d = x.shape
    tb = NSUB * BLK
    tok_spec = pl.BlockSpec((1, tb, LANES), lambda b, t, s: (b, t, 0))
    stacked = [pre_gain, post_gain, w_in_b, w_out_b, gate_w_p, gate_b, gng]

    def layer_spec(a):
        return pl.BlockSpec((None,) + a.shape[1:], lambda b, t, s, _n=a.ndim - 1: (l,) + (0,) * _n)

    in_specs = ([pl.BlockSpec((1, tb, d), lambda b, t, s: (b, t, 0)),
                 pl.BlockSpec((None, None, 3, d), lambda b, t, s: (l, b, 0, 0))]
                + [layer_spec(a) for a in stacked]
                + [tok_spec] * 4
                + [_const_spec(a.shape) for a in consts])
    scratch = ([pltpu.VMEM((PGRP * BLK, d), jnp.bfloat16)] * NGRP
               + [pltpu.VMEM((PGRP * BLK, N_PROJ), jnp.float32)] * NGRP
               + [pltpu.VMEM((PGRP * BLK, d), jnp.bfloat16)] * NGRP
               + [pltpu.VMEM((2, BLK, LANES), jnp.bfloat16),
                  pltpu.VMEM((2, BLK, LANES), jnp.bfloat16),
                  pltpu.VMEM((2, LANES, LANES), jnp.float32),
                  pltpu.VMEM((GLA_W, LANES), jnp.float32),
                  pltpu.VMEM((NSUB, BLK, LANES), jnp.float32)])
    grid_spec = pltpu.PrefetchScalarGridSpec(
        num_scalar_prefetch=1,
        grid=(bsz, seq // tb),
        in_specs=in_specs,
        out_specs=pl.BlockSpec((1, tb, d), lambda b, t, s: (b, t, 0)),
        scratch_shapes=scratch)
    return pl.pallas_call(
        _layer_kernel,
        out_shape=jax.ShapeDtypeStruct(x.shape, x.dtype),
        grid_spec=grid_spec,
        compiler_params=pltpu.CompilerParams(dimension_semantics=("arbitrary", "arbitrary"),
                                             vmem_limit_bytes=VMEM_LIMIT),
        name="hybrid_layer",
    )(sinks, x, mod, *stacked, *tables, *consts)


def kernel(x, c, positions, w_mod, b_mod, pre_norm_gain, post_norm_gain, w_in, attn_sinks, gla_gate_w,
           gla_gate_b, gla_norm_gain, w_out):
    depth = w_mod.shape[0]
    bsz = x.shape[0]
    f32, bf16 = jnp.float32, jnp.bfloat16

    mod = _modulation(c, w_mod, b_mod).reshape(depth, bsz, 3, D_MODEL)
    rope_freq = ROPE_THETA ** (-jnp.arange(0, HEAD_DIM, 2, dtype=f32) / HEAD_DIM)
    ret_freq = 1.0 / (10000.0 ** jnp.linspace(0.0, 1.0, RET_DK // 2, dtype=f32))
    freq = jnp.concatenate([rope_freq, ret_freq, rope_freq, ret_freq]).reshape(1, LANES)
    tables = _trig_tables(positions, freq)

    consts = [jnp.asarray(_C["bias"]), jnp.asarray(_C["vones"], bf16), jnp.asarray(_C["dmask"]),
              jnp.asarray(_C["qdec"]), jnp.asarray(_C["kdec"]), jnp.asarray(_C["cdm"]), jnp.asarray(_C["rbm"]),
              jnp.asarray(_C["tri"], bf16), jnp.asarray(_C["lmask"]), jnp.asarray(_C["smask"]),
              jnp.asarray(_C["e64"], bf16)]
    w_in_b = _cast_w_in(w_in)
    w_out_b = w_out.astype(bf16)
    gate_w_p = jnp.pad(gla_gate_w, ((0, 0), (0, LANES - GLA_GATE_RANK), (0, 0))).astype(bf16)
    gng = jnp.tile(gla_norm_gain, (1, GLA_HEADS)).reshape(depth, 1, GLA_W)
    pre = pre_norm_gain.reshape(depth, 1, D_MODEL)
    post = post_norm_gain.reshape(depth, 1, D_MODEL)
    gate_b = gla_gate_b.reshape(depth, 1, LANES)

    for l in range(depth):
        x = _layer(l, x, mod, pre, post, w_in_b, w_out_b, attn_sinks[l], gate_w_p, gate_b, gng, tables, consts)
    return x
```

```python
import numpy as np
import jax
import jax.numpy as jnp
from jax import lax
from jax.experimental import pallas as pl
from jax.experimental.pallas import tpu as pltpu

D_MODEL = 1024
HEAD_DIM = 64
ATTN_HEADS = 8
ATTN_KV_HEADS = 2
WINDOW = 128
ROPE_THETA = 10000.0
RET_HEADS = 4
RET_DK = 64
RET_DV = 64
GLA_HEADS = 4
GLA_DK = 32
GLA_DV = 64
GLA_GATE_RANK = 16
GLA_GATE_NORMALIZER = 16.0
EPS = 1e-6

ATTN_W = ATTN_HEADS * HEAD_DIM
RET_W = RET_HEADS * RET_DV
GLA_W = GLA_HEADS * GLA_DV

LANES = 128
BLK = 128
NSUB = 4
PGRP = 2
NGRP = NSUB // PGRP
GLA_C = 64
GLA_LEVELS = 6
GLA_LEVELS_PER_STAGE = 7
NEG = -1e30
VMEM_LIMIT = 56 * 1024 * 1024
PROBE_NO_GLA, PROBE_NO_RET, PROBE_NO_ATTN = False, False, False

O_AQ, O_AK, O_AV, O_AG = 0, 512, 640, 768
O_RQ, O_RK, O_RV, O_RG = 1280, 1536, 1792, 2048
O_GQ, O_GK, O_GV, O_GG, O_GA = 2304, 2432, 2560, 2816, 3072
N_PROJ = 3200
PROJ_SEGMENTS = (("gga", O_GG, N_PROJ), ("gqkv", O_GQ, O_GG), ("akv", O_AK, O_AG), ("aq", O_AQ, O_AK),
                 ("rqk", O_RQ, O_RV), ("rvg", O_RV, O_GQ), ("ag", O_AG, O_RQ))


def _build_constants():
    f32 = np.float32
    lane = np.arange(LANES)
    grp = lane // 64

    i = np.arange(BLK)[:, None]
    j = np.arange(2 * BLK)[None, :]
    rel = i + BLK - j
    ok = (rel >= 0) & (rel < WINDOW)
    bias = np.stack([np.where(ok & (j >= BLK), 0.0, NEG), np.where(ok, 0.0, NEG)]).astype(f32)
    vones = np.zeros((2 * 2 * BLK, LANES), f32)
    for g in range(2):
        vones[g * 2 * BLK:(g + 1) * 2 * BLK] = (lane // 64 == g)[None, :]

    kscale = RET_DK ** -0.5
    log_g = np.log(1.0 - 2.0 ** (-5.0 - np.arange(RET_HEADS, dtype=np.float64)))
    idx = np.arange(BLK, dtype=np.float64)
    diff = idx[:, None] - idx[None, :]
    dmask = np.zeros((2, BLK, 2 * BLK), f32)
    qdec = np.zeros((2, BLK, LANES), f32)
    kdec = np.zeros((2, BLK, LANES), f32)
    cdm = np.zeros((2, LANES, LANES), f32)
    rbm = ((np.arange(LANES)[:, None] // 64) == (np.arange(LANES)[None, :] // 64)).astype(f32)
    for p in range(2):
        for hh in range(2):
            lg = log_g[2 * p + hh]
            dmask[p, :, hh * BLK:(hh + 1) * BLK] = kscale * np.where(diff >= 0, np.exp(lg * np.maximum(diff, 0.0)), 0.0)
        lg_lane = log_g[2 * p + grp]
        qdec[p] = np.exp(lg_lane[None, :] * (idx[:, None] + 1.0))
        kdec[p] = kscale * np.exp(lg_lane[None, :] * (BLK - 1.0 - idx[:, None]))
        cdm[p] = np.exp(lg_lane * BLK)[:, None] * rbm

    t = np.arange(BLK)
    tri = (((t[:, None] // GLA_C) == (t[None, :] // GLA_C)) & (t[None, :] <= t[:, None])).astype(f32)
    ci = np.arange(GLA_C)[:, None]
    cj = np.arange(GLA_C)[None, :]
    x = ci ^ cj
    lvl = np.where(cj > ci, -1, np.where(x == 0, 0, np.floor(np.log2(np.maximum(x, 1))).astype(np.int64) + 1))
    lmask = np.stack([np.tile((lvl == lv).astype(f32), (1, GLA_HEADS)) for lv in range(GLA_LEVELS + 1)])
    smask = ((np.arange(GLA_W)[:, None] // GLA_DV) == (np.arange(LANES)[None, :] // GLA_DK)).astype(f32)
    e64 = ((np.arange(256)[:, None] // 64) == (np.arange(256)[None, :] // 64)).astype(f32) / 64.0

    return dict(bias=bias, vones=vones, dmask=dmask, qdec=qdec, kdec=kdec, cdm=cdm, rbm=rbm, tri=tri,
                lmask=lmask, smask=smask, e64=e64)


_C = _build_constants()


def _dot(a, b):
    return jnp.dot(a, b, preferred_element_type=jnp.float32)


def _dot_nt(a, b):
    return lax.dot_general(a, b, (((1,), (1,)), ((), ())), preferred_element_type=jnp.float32)


def _dot_tn(a, b):
    return lax.dot_general(a, b, (((0,), (0,)), ((), ())), preferred_element_type=jnp.float32)


def _bf(x):
    return x.astype(jnp.bfloat16)


def _split_hi_lo(x):
    hi = _bf(x)
    lo = _bf(x - hi.astype(jnp.float32))
    return hi, lo


def _silu(x):
    return x / (1.0 + jnp.exp(-x))


def _run_interleaved(gens, done):
    active = [[g, None] for g in gens]
    while active:
        progressed = False
        for item in list(active):
            if item[1] is not None and item[1] not in done:
                continue
            progressed = True
            try:
                need = next(item[0])
                while need is not None and need in done:
                    need = next(item[0])
                item[1] = need
            except StopIteration:
                active.remove(item)
        assert progressed, [item[1] for item in active]


MOD_TK = 256


def _mod_kernel(c_ref, w_ref, b_ref, o_ref):
    @pl.when(pl.program_id(1) == 0)
    def _():
        o_ref[0] = jnp.broadcast_to(b_ref[0], o_ref.shape[1:])

    a = _silu(c_ref[...])
    a_hi, a_lo = _split_hi_lo(a)
    w_hi, w_lo = _split_hi_lo(w_ref[0])
    n = a.shape[0]
    both = _dot(jnp.concatenate([a_hi, a_lo], axis=0), w_hi)
    o_ref[0] += both[:n] + both[n:] + _dot(a_hi, w_lo)


def _modulation(c, w_mod, b_mod):
    depth, d, n = w_mod.shape
    bsz = c.shape[0]
    return pl.pallas_call(
        _mod_kernel,
        out_shape=jax.ShapeDtypeStruct((depth, bsz, n), jnp.float32),
        grid=(depth, d // MOD_TK),
        in_specs=[pl.BlockSpec((bsz, MOD_TK), lambda l, k: (0, k)),
                  pl.BlockSpec((1, MOD_TK, n), lambda l, k: (l, k, 0)),
                  pl.BlockSpec((1, 1, n), lambda l, k: (l, 0, 0))],
        out_specs=pl.BlockSpec((1, bsz, n), lambda l, k: (l, 0, 0)),
        compiler_params=pltpu.CompilerParams(dimension_semantics=("arbitrary", "arbitrary"),
                                             vmem_limit_bytes=VMEM_LIMIT),
        name="adaln_mod",
    )(c, w_mod, b_mod.reshape(depth, 1, n))


TRIG_T = 512


def _trig_kernel(posa_ref, posb_ref, freq_ref, ca_ref, sa_ref, cr_ref, sr_ref):
    half = TRIG_T // 2
    lane = lax.broadcasted_iota(jnp.int32, (half, LANES), 1)
    pos = jnp.where(lane < 64, posa_ref[0], posb_ref[0]).astype(jnp.float32)
    ang = pos * freq_ref[...]
    c = jnp.cos(ang)
    s = jnp.sin(ang)
    quarter = lane // 32
    sign = jnp.where((lane % 64) < 32, -1.0, 1.0)
    rc = [c] + [pltpu.roll(c, k, axis=1) for k in (32, 64, 96)]
    rs = [s] + [pltpu.roll(s, k, axis=1) for k in (32, 64, 96)]

    def spread(r, src):
        out = r[(0 - src) % 4]
        for q in (1, 2, 3):
            out = jnp.where(quarter == q, r[(q - src) % 4], out)
        return out

    for tok, rws in ((0, slice(0, half)), (1, slice(half, TRIG_T))):
        ca_ref[0, rws, :] = spread(rc, 2 * tok)
        sa_ref[0, rws, :] = spread(rs, 2 * tok) * sign
        cr_ref[0, rws, :] = spread(rc, 2 * tok + 1)
        sr_ref[0, rws, :] = spread(rs, 2 * tok + 1) * sign


def _trig_tables(positions, freq):
    bsz, seq = positions.shape
    half = TRIG_T // 2
    out = jax.ShapeDtypeStruct((bsz, seq, LANES), jnp.float32)
    spec = pl.BlockSpec((1, TRIG_T, LANES), lambda b, t: (b, t, 0))
    pos = positions.reshape(bsz, seq, 1)
    return pl.pallas_call(
        _trig_kernel,
        out_shape=(out, out, out, out),
        grid=(bsz, seq // TRIG_T),
        in_specs=[pl.BlockSpec((1, half, 1), lambda b, t: (b, 2 * t, 0)),
                  pl.BlockSpec((1, half, 1), lambda b, t: (b, 2 * t + 1, 0)),
                  pl.BlockSpec((1, LANES), lambda b, t: (0, 0))],
        out_specs=(spec, spec, spec, spec),
        compiler_params=pltpu.CompilerParams(dimension_semantics=("arbitrary", "arbitrary"),
                                             vmem_limit_bytes=VMEM_LIMIT),
        name="rotary_tables",
    )(pos, pos, freq)


def _seg_mean(x, e_ref):
    return _dot(_bf(x), e_ref[...])


def _rot(v, c, s, first_half):
    partner = jnp.where(first_half, pltpu.roll(v, 96, axis=1), pltpu.roll(v, 32, axis=1))
    return v * c + partner * s


def _layer_kernel(sinks_ref,
                  x_ref, mod_ref, pre_ref, post_ref, win_ref, wout_ref, gw_ref, gb_ref, gng_ref,
                  ca_ref, sa_ref, cr_ref, sr_ref,
                  bias_ref, vones_ref, dmask_ref, qdec_ref, kdec_ref, cdm_ref, rbm_ref,
                  tri_ref, lmask_ref, smask_ref, e64_ref,
                  o_ref, *scratch):
    hb_grp = scratch[0:NGRP]
    proj_grp = scratch[NGRP:2 * NGRP]
    mix_grp = scratch[2 * NGRP:3 * NGRP]
    kprev_ref, vprev_ref, rstate_ref, gstate_ref, bscr_ref = scratch[3 * NGRP:]
    t = pl.program_id(1)

    def sub_view(refs, sb):
        return refs[sb // PGRP].at[pl.ds((sb % PGRP) * BLK, BLK)]

    hb_refs = [sub_view(hb_grp, sb) for sb in range(NSUB)]
    proj_refs = [sub_view(proj_grp, sb) for sb in range(NSUB)]
    mix_refs = [sub_view(mix_grp, sb) for sb in range(NSUB)]

    @pl.when(t == 0)
    def _():
        kprev_ref[...] = jnp.zeros_like(kprev_ref)
        vprev_ref[...] = jnp.zeros_like(vprev_ref)
        rstate_ref[...] = jnp.zeros_like(rstate_ref)
        gstate_ref[...] = jnp.zeros_like(gstate_ref)

    lane = lax.broadcasted_iota(jnp.int32, (1, LANES), 1)
    lane2 = lax.broadcasted_iota(jnp.int32, (1, GLA_W), 1)
    zero_bf = jnp.zeros((), jnp.bfloat16)
    first_half = (lane % 64) < 32
    lo_hi = [lane // 64 == g for g in range(2)]
    hgrp = [lane // GLA_DK == h for h in range(GLA_HEADS)]
    vhead = [lane2 // GLA_DV == h for h in range(GLA_HEADS)]

    def rows(sb):
        return slice(sb * BLK, (sb + 1) * BLK)

    done = set()

    def gen_norm(sb):
        if sb >= PGRP:
            yield f"projgrp{sb // PGRP - 1}.started"
        x = x_ref[0, rows(sb), :]
        inv = lax.rsqrt(jnp.mean(x * x, axis=-1, keepdims=True) + EPS)
        hb_refs[sb][...] = _bf((x * inv) * (pre_ref[...] * (1.0 + mod_ref[1:2, :])) + mod_ref[0:1, :])
        done.add(f"norm{sb}")

    def gen_proj(g):
        subs = range(g * PGRP, (g + 1) * PGRP)
        for sb in subs:
            yield f"norm{sb}"
        if g > 0:
            yield f"projgrp{g - 1}.all"
        done.add(f"projgrp{g}.started")
        for name, a, b in PROJ_SEGMENTS:
            proj_grp[g][:, a:b] = _dot(hb_grp[g][...], win_ref[:, a:b])
            done.update(f"proj{sb}.{name}" for sb in subs)
            yield
        done.add(f"projgrp{g}.all")

    def gen_attn(sb):
        yield f"proj{sb}.akv"
        if PROBE_NO_ATTN:
            mix_refs[sb][:, :ATTN_W] = jnp.zeros((BLK, ATTN_W), jnp.bfloat16)
            done.add(f"attn{sb}.kv")
            done.add(f"attn{sb}.done")
            return
        if sb > 0:
            yield f"attn{sb - 1}.kv"
        pr = proj_refs[sb]
        ca, sa = ca_ref[0, rows(sb), :], sa_ref[0, rows(sb), :]
        k_new = _rot(pr[:, O_AK:O_AK + 128], ca, sa, first_half) * (HEAD_DIM ** -0.5)
        v_new = pr[:, O_AV:O_AV + 128]
        k_sw = pltpu.roll(k_new, 64, axis=1)
        v_sw = pltpu.roll(v_new, 64, axis=1)
        k2, v2 = [], []
        for g in range(2):
            k_g = _bf(jnp.where(lo_hi[g], k_new, k_sw))
            v_g = _bf(jnp.where(lo_hi[g], v_new, v_sw))
            kcat = jnp.concatenate([kprev_ref[g], k_g], axis=0)
            vcat = jnp.concatenate([vprev_ref[g], v_g], axis=0)
            kprev_ref[g] = k_g
            vprev_ref[g] = v_g
            k2.append(jnp.concatenate([jnp.where(lo_hi[h], kcat, zero_bf) for h in range(2)], axis=0))
            v2_g = jnp.concatenate([jnp.where(lo_hi[h], vcat, zero_bf) for h in range(2)], axis=0)
            v2.append(jnp.concatenate([v2_g, vones_ref[...]], axis=1))
        done.add(f"attn{sb}.kv")
        if sb == 0:
            bias = bias_ref[jnp.minimum(t, 1)]
        else:
            bias = bias_ref[1]

        def scores(g):
            q = [_rot(pr[:, O_AQ + m * 128:O_AQ + (m + 1) * 128], ca, sa, first_half) for m in (2 * g, 2 * g + 1)]
            return _dot_nt(_bf(jnp.concatenate(q, axis=0)), k2[g])

        yield f"proj{sb}.aq"
        s_g = scores(0)
        yield
        for g in range(2):
            ps, mxs = [], []
            for j in range(2):
                row_p, row_mx = [], []
                for h in range(2):
                    s = s_g[j * BLK:(j + 1) * BLK, h * 2 * BLK:(h + 1) * 2 * BLK] + bias
                    mx = jnp.maximum(jnp.max(s, axis=-1, keepdims=True), sinks_ref[2 * (2 * g + j) + h])
                    row_p.append(_bf(jnp.exp(s - mx)))
                    row_mx.append(mx)
                ps.append(jnp.concatenate(row_p, axis=1))
                mxs.append(row_mx)
            pv = _dot(jnp.concatenate(ps, axis=0), v2[g])
            if g == 0:
                s_g = scores(1)
            yield
            yield f"proj{sb}.ag"
            for j in range(2):
                m = 2 * g + j
                pv_m = pv[j * BLK:(j + 1) * BLK]
                mx_l = jnp.where(lo_hi[0], mxs[j][0], mxs[j][1])
                sink_l = jnp.where(lo_hi[0], sinks_ref[2 * m], sinks_ref[2 * m + 1])
                den = pv_m[:, 128:] + jnp.exp(sink_l - mx_l)
                gate_m = _silu(pr[:, O_AG + m * 128:O_AG + (m + 1) * 128])
                mix_refs[sb][:, m * 128:(m + 1) * 128] = _bf(pv_m[:, :128] / den * gate_m)
        done.add(f"attn{sb}.done")

    def gen_ret(sb):
        yield f"proj{sb}.rqk"
        yield f"proj{sb}.rvg"
        if PROBE_NO_RET:
            mix_refs[sb][:, ATTN_W:ATTN_W + RET_W] = jnp.zeros((BLK, RET_W), jnp.bfloat16)
            done.add(f"ret{sb}.state")
            done.add(f"ret{sb}.done")
            return
        if sb > 0:
            yield f"ret{sb - 1}.state"
        pr = proj_refs[sb]
        cr, sr = cr_ref[0, rows(sb), :], sr_ref[0, rows(sb), :]
        held = []
        for p in range(2):
            q_p = _rot(pr[:, O_RQ + p * 128:O_RQ + (p + 1) * 128], cr, sr, first_half)
            k_p = _rot(pr[:, O_RK + p * 128:O_RK + (p + 1) * 128], cr, sr, first_half)
            v_p = _bf(pr[:, O_RV + p * 128:O_RV + (p + 1) * 128])
            k_pb = _bf(k_p)
            kbd = jnp.concatenate([jnp.where(lo_hi[h], k_pb, zero_bf) for h in range(2)], axis=0)
            vbd = jnp.concatenate([jnp.where(lo_hi[h], v_p, zero_bf) for h in range(2)], axis=0)
            sc_raw = _dot_nt(_bf(q_p), kbd)
            st = rstate_ref[p]
            inter = _dot(_bf(q_p * qdec_ref[p]), _bf(st))
            upd = _dot_tn(_bf(k_p * kdec_ref[p]), v_p)
            held.append((sc_raw, vbd, st, inter, upd))
        yield
        intras = []
        for p in range(2):
            sc_raw, vbd, st, inter, upd = held[p]
            intras.append(_dot(_bf(sc_raw * dmask_ref[p]), vbd) + inter)
            rstate_ref[p] = st * cdm_ref[p] + upd * rbm_ref[...]
        done.add(f"ret{sb}.state")
        yield
        r = jnp.concatenate(intras, axis=1)
        ms = _seg_mean(r * r, e64_ref)
        yield
        mix_refs[sb][:, ATTN_W:ATTN_W + RET_W] = _bf(r * lax.rsqrt(ms + EPS) * _silu(pr[:, O_RG:O_RG + 256]))
        done.add(f"ret{sb}.done")

    def gen_gla(sb):
        yield f"proj{sb}.gga"
        if PROBE_NO_GLA:
            mix_refs[sb][:, ATTN_W + RET_W:] = jnp.zeros((BLK, GLA_W), jnp.bfloat16)
            done.add(f"gla{sb}.state")
            done.add(f"gla{sb}.done")
            return
        pr = proj_refs[sb]
        logits = _dot(_bf(pr[:, O_GA:O_GA + 128]), gw_ref[...])
        yield
        logits = logits + gb_ref[...]
        log_a = (jnp.minimum(logits, 0.0) - jnp.log1p(jnp.exp(-jnp.abs(logits)))) * (1.0 / GLA_GATE_NORMALIZER)
        la_hi, la_lo = _split_hi_lo(log_a)
        xr = _dot(tri_ref[...], jnp.concatenate([la_hi, la_lo], axis=1))
        yield
        yield f"proj{sb}.gqkv"
        b = xr[:, :128] + xr[:, 128:]
        bscr = bscr_ref.at[sb]
        bscr[...] = b
        gq = pr[:, O_GQ:O_GQ + 128] * (GLA_DK ** -0.5)
        gk = pr[:, O_GK:O_GK + 128]
        gv = _bf(pr[:, O_GV:O_GV + 256])
        nch = BLK // GLA_C
        row = lax.broadcasted_iota(jnp.int32, (GLA_C, LANES), 0)

        def level_factor(c, lv):
            r0 = c * GLA_C
            b_c = b[r0:r0 + GLA_C]
            grp = 1 << lv
            if grp >= 8:
                mids = [bscr[r0 + g0 + grp // 2 - 1:r0 + g0 + grp // 2, :] for g0 in range(0, GLA_C, grp)]
                ref = jnp.concatenate([jnp.broadcast_to(m, (grp, LANES)) for m in mids], axis=0)
            elif grp == 4:
                r = row % 4
                ref = jnp.where(r == 0, pltpu.roll(b_c, GLA_C - 1, axis=0),
                                jnp.where(r == 1, b_c,
                                          jnp.where(r == 2, pltpu.roll(b_c, 1, axis=0), pltpu.roll(b_c, 2, axis=0))))
            else:
                ref = jnp.where(row % 2 == 1, pltpu.roll(b_c, 1, axis=0), b_c)
            return jnp.exp(-jnp.abs(b_c - ref))

        saccs = [jnp.zeros((GLA_C, GLA_HEADS * GLA_C), jnp.float32) for _ in range(nch)]
        pending = []
        for lv in range(GLA_LEVELS + 1):
            raw = []
            for c in range(nch):
                r0 = c * GLA_C
                if lv == 0:
                    qt, kt = _bf(gq[r0:r0 + GLA_C]), _bf(gk[r0:r0 + GLA_C])
                else:
                    f = level_factor(c, lv)
                    qt, kt = _bf(gq[r0:r0 + GLA_C] * f), _bf(gk[r0:r0 + GLA_C] * f)
                kbd = jnp.concatenate([jnp.where(hgrp[h], kt, zero_bf) for h in range(GLA_HEADS)], axis=0)
                raw.append(_dot_nt(qt, kbd))
            pending.append((lv, raw))
            if len(pending) == GLA_LEVELS_PER_STAGE or lv == GLA_LEVELS:
                yield
                for lv_p, raw_p in pending:
                    for c in range(nch):
                        saccs[c] = saccs[c] + raw_p[c] * lmask_ref[lv_p]
                pending = []
        if sb > 0:
            yield f"gla{sb - 1}.state"
        g_parts = []
        st = gstate_ref[...]
        for c in range(nch):
            r0 = c * GLA_C
            v_c = gv[r0:r0 + GLA_C]
            vbd = jnp.concatenate([jnp.where(vhead[h], v_c, zero_bf) for h in range(GLA_HEADS)], axis=0)
            b_c = b[r0:r0 + GLA_C]
            b_last = bscr[r0 + GLA_C - 1:r0 + GLA_C, :]
            q_in = _bf(gq[r0:r0 + GLA_C] * jnp.exp(b_c))
            o_c = _dot(_bf(saccs[c]), vbd) + _dot_nt(q_in, _bf(st))
            k_out = _bf(gk[r0:r0 + GLA_C] * jnp.exp(b_last - b_c))
            upd = _dot_tn(v_c, k_out)
            yield
            g_parts.append(o_c)
            st = st * jnp.exp(b_last) + upd * smask_ref[...]
        gstate_ref[...] = st
        done.add(f"gla{sb}.state")
        g = jnp.concatenate(g_parts, axis=0)
        ms = _seg_mean(g * g, e64_ref)
        yield
        gate_g = _silu(pr[:, O_GG:O_GG + 256])
        mix_refs[sb][:, ATTN_W + RET_W:] = _bf(g * lax.rsqrt(ms + EPS) * gng_ref[...] * gate_g)
        done.add(f"gla{sb}.done")

    def gen_out(g):
        for sb in range(g * PGRP, (g + 1) * PGRP):
            yield f"attn{sb}.done"
            yield f"ret{sb}.done"
            yield f"gla{sb}.done"
        rws = slice(g * PGRP * BLK, (g + 1) * PGRP * BLK)
        half = D_MODEL // 2
        y0 = _dot(mix_grp[g][...], wout_ref[:, :half])
        yield
        y1 = _dot(mix_grp[g][...], wout_ref[:, half:])
        yield
        ssq = jnp.sum(y0 * y0, axis=-1, keepdims=True) + jnp.sum(y1 * y1, axis=-1, keepdims=True)
        inv = lax.rsqrt(ssq * (1.0 / D_MODEL) + EPS)
        gp = mod_ref[2:3, :] * post_ref[...]
        o_ref[0, rws, :half] = x_ref[0, rws, :half] + (y0 * inv) * gp[:, :half]
        yield
        o_ref[0, rws, half:] = x_ref[0, rws, half:] + (y1 * inv) * gp[:, half:]

    gens = []
    for g in range(NGRP):
        for sb in range(g * PGRP, (g + 1) * PGRP):
            gens += [gen_gla(sb), gen_attn(sb), gen_ret(sb)]
        gens += [gen_out(g), gen_proj(g)] + [gen_norm(sb) for sb in range(g * PGRP, (g + 1) * PGRP)]
    _run_interleaved(gens, done)


def _const_spec(shape):
    nd = len(shape)
    return pl.BlockSpec(shape, lambda b, t, s, _n=nd: (0,) * _n)


def _layer(l, x, mod, pre_gain, post_gain, w_in_b, w_out_b, sinks, gate_w_p, gate_b, gng, tables, consts):
    bsz, seq, d = x.shape
    tb = NSUB * BLK
    tok_spec = pl.BlockSpec((1, tb, LANES), lambda b, t, s: (b, t, 0))
    stacked = [pre_gain, post_gain, w_in_b, w_out_b, gate_w_p, gate_b, gng]

    def layer_spec(a):
        return pl.BlockSpec((None,) + a.shape[1:], lambda b, t, s, _n=a.ndim - 1: (l,) + (0,) * _n)

    in_specs = ([pl.BlockSpec((1, tb, d), lambda b, t, s: (b, t, 0)),
                 pl.BlockSpec((None, None, 3, d), lambda b, t, s: (l, b, 0, 0))]
                + [layer_spec(a) for a in stacked]
                + [tok_spec] * 4
                + [_const_spec(a.shape) for a in consts])
    scratch = ([pltpu.VMEM((PGRP * BLK, d), jnp.bfloat16)] * NGRP
               + [pltpu.VMEM((PGRP * BLK, N_PROJ), jnp.float32)] * NGRP
               + [pltpu.VMEM((PGRP * BLK, d), jnp.bfloat16)] * NGRP
               + [pltpu.VMEM((2, BLK, LANES), jnp.bfloat16),
                  pltpu.VMEM((2, BLK, LANES), jnp.bfloat16),
                  pltpu.VMEM((2, LANES, LANES), jnp.float32),
                  pltpu.VMEM((GLA_W, LANES), jnp.float32),
                  pltpu.VMEM((NSUB, BLK, LANES), jnp.float32)])
    grid_spec = pltpu.PrefetchScalarGridSpec(
        num_scalar_prefetch=1,
        grid=(bsz, seq // tb),
        in_specs=in_specs,
        out_specs=pl.BlockSpec((1, tb, d), lambda b, t, s: (b, t, 0)),
        scratch_shapes=scratch)
    return pl.pallas_call(
        _layer_kernel,
        out_shape=jax.ShapeDtypeStruct(x.shape, x.dtype),
        grid_spec=grid_spec,
        compiler_params=pltpu.CompilerParams(dimension_semantics=("arbitrary", "arbitrary"),
                                             vmem_limit_bytes=VMEM_LIMIT),
        name="hybrid_layer",
    )(sinks, x, mod, *stacked, *tables, *consts)


def kernel(x, c, positions, w_mod, b_mod, pre_norm_gain, post_norm_gain, w_in, attn_sinks, gla_gate_w,
           gla_gate_b, gla_norm_gain, w_out):
    depth = w_mod.shape[0]
    bsz = x.shape[0]
    f32, bf16 = jnp.float32, jnp.bfloat16

    mod = _modulation(c, w_mod, b_mod).reshape(depth, bsz, 3, D_MODEL)
    rope_freq = ROPE_THETA ** (-jnp.arange(0, HEAD_DIM, 2, dtype=f32) / HEAD_DIM)
    ret_freq = 1.0 / (10000.0 ** jnp.linspace(0.0, 1.0, RET_DK // 2, dtype=f32))
    freq = jnp.concatenate([rope_freq, ret_freq, rope_freq, ret_freq]).reshape(1, LANES)
    tables = _trig_tables(positions, freq)

    consts = [jnp.asarray(_C["bias"]), jnp.asarray(_C["vones"], bf16), jnp.asarray(_C["dmask"]),
              jnp.asarray(_C["qdec"]), jnp.asarray(_C["kdec"]), jnp.asarray(_C["cdm"]), jnp.asarray(_C["rbm"]),
              jnp.asarray(_C["tri"], bf16), jnp.asarray(_C["lmask"]), jnp.asarray(_C["smask"]),
              jnp.asarray(_C["e64"], bf16)]
    w_ga = jnp.pad(w_in[:, :, O_GA:], ((0, 0), (0, 0), (0, N_PROJ - w_in.shape[-1])))
    w_in_b = jnp.concatenate([w_in[:, :, :O_GA].astype(bf16), w_ga.astype(bf16)], axis=-1)
    w_out_b = w_out.astype(bf16)
    gate_w_p = jnp.pad(gla_gate_w, ((0, 0), (0, LANES - GLA_GATE_RANK), (0, 0))).astype(bf16)
    gng = jnp.tile(gla_norm_gain, (1, GLA_HEADS)).reshape(depth, 1, GLA_W)
    pre = pre_norm_gain.reshape(depth, 1, D_MODEL)
    post = post_norm_gain.reshape(depth, 1, D_MODEL)
    gate_b = gla_gate_b.reshape(depth, 1, LANES)

    for l in range(depth):
        x = _layer(l, x, mod, pre, post, w_in_b, w_out_b, attn_sinks[l], gate_w_p, gate_b, gng, tables, consts)
    return x
```

```python
import numpy as np
import jax
import jax.numpy as jnp
from jax import lax
from jax.experimental import pallas as pl
from jax.experimental.pallas import tpu as pltpu

D_MODEL = 1024
HEAD_DIM = 64
ATTN_HEADS = 8
ATTN_KV_HEADS = 2
WINDOW = 128
ROPE_THETA = 10000.0
RET_HEADS = 4
RET_DK = 64
RET_DV = 64
GLA_HEADS = 4
GLA_DK = 32
GLA_DV = 64
GLA_GATE_RANK = 16
GLA_GATE_NORMALIZER = 16.0
EPS = 1e-6

ATTN_W = ATTN_HEADS * HEAD_DIM
RET_W = RET_HEADS * RET_DV
GLA_W = GLA_HEADS * GLA_DV

LANES = 128
BLK = 128
NSUB = 4
PGRP = 2
NGRP = NSUB // PGRP
GLA_C = 64
GLA_LEVELS = 6
GLA_LEVELS_PER_STAGE = 7
NEG = -1e30
VMEM_LIMIT = 56 * 1024 * 1024
LOG2E = 1.4426950408889634

O_AQ, O_AK, O_AV, O_AG = 0, 512, 640, 768
O_RQ, O_RK, O_RV, O_RG = 1280, 1536, 1792, 2048
O_GQ, O_GK, O_GV, O_GG, O_GA = 2304, 2432, 2560, 2816, 3072
D_IN = O_GA + GLA_GATE_RANK
N_PROJ = 3200
PROJ_SEGMENTS = (("gga", O_GG, D_IN), ("gqkv", O_GQ, O_GG), ("akv", O_AK, O_AG), ("aq", O_AQ, O_AK),
                 ("rqk", O_RQ, O_RV), ("rvg", O_RV, O_GQ), ("ag", O_AG, O_RQ))


def _build_constants():
    f32 = np.float32
    lane = np.arange(LANES)
    grp = lane // 64

    i = np.arange(BLK)[:, None]
    j = np.arange(2 * BLK)[None, :]
    rel = i + BLK - j
    ok = (rel >= 0) & (rel < WINDOW)
    bias = np.stack([np.where(ok & (j >= BLK), 0.0, NEG), np.where(ok, 0.0, NEG)]).astype(f32)
    vones = np.zeros((2 * 2 * BLK, LANES), f32)
    for g in range(2):
        vones[g * 2 * BLK:(g + 1) * 2 * BLK] = (lane // 64 == g)[None, :]

    kscale = RET_DK ** -0.5
    log_g = np.log(1.0 - 2.0 ** (-5.0 - np.arange(RET_HEADS, dtype=np.float64)))
    idx = np.arange(BLK, dtype=np.float64)
    diff = idx[:, None] - idx[None, :]
    dmask = np.zeros((2, BLK, 2 * BLK), f32)
    qdec = np.zeros((2, BLK, LANES), f32)
    kdec = np.zeros((2, BLK, LANES), f32)
    cdm = np.zeros((2, LANES, LANES), f32)
    rbm = ((np.arange(LANES)[:, None] // 64) == (np.arange(LANES)[None, :] // 64)).astype(f32)
    for p in range(2):
        for hh in range(2):
            lg = log_g[2 * p + hh]
            dmask[p, :, hh * BLK:(hh + 1) * BLK] = kscale * np.where(diff >= 0, np.exp(lg * np.maximum(diff, 0.0)), 0.0)
        lg_lane = log_g[2 * p + grp]
        qdec[p] = np.exp(lg_lane[None, :] * (idx[:, None] + 1.0))
        kdec[p] = kscale * np.exp(lg_lane[None, :] * (BLK - 1.0 - idx[:, None]))
        cdm[p] = np.exp(lg_lane * BLK)[:, None] * rbm

    t = np.arange(BLK)
    tri = (((t[:, None] // GLA_C) == (t[None, :] // GLA_C)) & (t[None, :] <= t[:, None])).astype(f32)
    ci = np.arange(GLA_C)[:, None]
    cj = np.arange(GLA_C)[None, :]
    x = ci ^ cj
    lvl = np.where(cj > ci, -1, np.where(x == 0, 0, np.floor(np.log2(np.maximum(x, 1))).astype(np.int64) + 1))
    lmask = np.stack([np.tile((lvl == lv).astype(f32), (1, GLA_HEADS)) for lv in range(GLA_LEVELS + 1)])
    smask = ((np.arange(GLA_W)[:, None] // GLA_DV) == (np.arange(LANES)[None, :] // GLA_DK)).astype(f32)
    e64 = ((np.arange(256)[:, None] // 64) == (np.arange(256)[None, :] // 64)).astype(f32) / 64.0

    return dict(bias=bias, vones=vones, dmask=dmask, qdec=qdec, kdec=kdec, cdm=cdm, rbm=rbm, tri=tri,
                lmask=lmask, smask=smask, e64=e64)


_C = _build_constants()


def _dot(a, b):
    return jnp.dot(a, b, preferred_element_type=jnp.float32)


def _dot_nt(a, b):
    return lax.dot_general(a, b, (((1,), (1,)), ((), ())), preferred_element_type=jnp.float32)


def _dot_tn(a, b):
    return lax.dot_general(a, b, (((0,), (0,)), ((), ())), preferred_element_type=jnp.float32)


def _bf(x):
    return x.astype(jnp.bfloat16)


def _split_hi_lo(x):
    hi = _bf(x)
    lo = _bf(x - hi.astype(jnp.float32))
    return hi, lo


def _silu(x):
    return x / (1.0 + jnp.exp(-x))


def _run_interleaved(gens, done):
    active = [[g, None] for g in gens]
    while active:
        progressed = False
        for item in list(active):
            if item[1] is not None and item[1] not in done:
                continue
            progressed = True
            try:
                need = next(item[0])
                while need is not None and need in done:
                    need = next(item[0])
                item[1] = need
            except StopIteration:
                active.remove(item)
        assert progressed, [item[1] for item in active]


MOD_TK = 256


def _mod_kernel(c_ref, w_ref, b_ref, o_ref):
    @pl.when(pl.program_id(1) == 0)
    def _():
        o_ref[0] = jnp.broadcast_to(b_ref[0], o_ref.shape[1:])

    a = _silu(c_ref[...])
    a_hi, a_lo = _split_hi_lo(a)
    w_hi, w_lo = _split_hi_lo(w_ref[0])
    n = a.shape[0]
    both = _dot(jnp.concatenate([a_hi, a_lo], axis=0), w_hi)
    o_ref[0] += both[:n] + both[n:] + _dot(a_hi, w_lo)


def _modulation(c, w_mod, b_mod):
    depth, d, n = w_mod.shape
    bsz = c.shape[0]
    return pl.pallas_call(
        _mod_kernel,
        out_shape=jax.ShapeDtypeStruct((depth, bsz, n), jnp.float32),
        grid=(depth, d // MOD_TK),
        in_specs=[pl.BlockSpec((bsz, MOD_TK), lambda l, k: (0, k)),
                  pl.BlockSpec((1, MOD_TK, n), lambda l, k: (l, k, 0)),
                  pl.BlockSpec((1, 1, n), lambda l, k: (l, 0, 0))],
        out_specs=pl.BlockSpec((1, bsz, n), lambda l, k: (l, 0, 0)),
        compiler_params=pltpu.CompilerParams(dimension_semantics=("arbitrary", "arbitrary"),
                                             vmem_limit_bytes=VMEM_LIMIT),
        name="adaln_mod",
    )(c, w_mod, b_mod.reshape(depth, 1, n))


TRIG_T = 512


def _trig_kernel(posa_ref, posb_ref, freq_ref, ca_ref, sa_ref, cr_ref, sr_ref):
    half = TRIG_T // 2
    lane = lax.broadcasted_iota(jnp.int32, (half, LANES), 1)
    pos = jnp.where(lane < 64, posa_ref[0], posb_ref[0]).astype(jnp.float32)
    ang = pos * freq_ref[...]
    c = jnp.cos(ang)
    s = jnp.sin(ang)
    quarter = lane // 32
    sign = jnp.where((lane % 64) < 32, -1.0, 1.0)
    rc = [c] + [pltpu.roll(c, k, axis=1) for k in (32, 64, 96)]
    rs = [s] + [pltpu.roll(s, k, axis=1) for k in (32, 64, 96)]

    def spread(r, src):
        out = r[(0 - src) % 4]
        for q in (1, 2, 3):
            out = jnp.where(quarter == q, r[(q - src) % 4], out)
        return out

    for tok, rws in ((0, slice(0, half)), (1, slice(half, TRIG_T))):
        ca_ref[0, rws, :] = spread(rc, 2 * tok)
        sa_ref[0, rws, :] = spread(rs, 2 * tok) * sign
        cr_ref[0, rws, :] = spread(rc, 2 * tok + 1)
        sr_ref[0, rws, :] = spread(rs, 2 * tok + 1) * sign


def _trig_tables(positions, freq):
    bsz, seq = positions.shape
    half = TRIG_T // 2
    out = jax.ShapeDtypeStruct((bsz, seq, LANES), jnp.float32)
    spec = pl.BlockSpec((1, TRIG_T, LANES), lambda b, t: (b, t, 0))
    pos = positions.reshape(bsz, seq, 1)
    return pl.pallas_call(
        _trig_kernel,
        out_shape=(out, out, out, out),
        grid=(bsz, seq // TRIG_T),
        in_specs=[pl.BlockSpec((1, half, 1), lambda b, t: (b, 2 * t, 0)),
                  pl.BlockSpec((1, half, 1), lambda b, t: (b, 2 * t + 1, 0)),
                  pl.BlockSpec((1, LANES), lambda b, t: (0, 0))],
        out_specs=(spec, spec, spec, spec),
        compiler_params=pltpu.CompilerParams(dimension_semantics=("arbitrary", "arbitrary"),
                                             vmem_limit_bytes=VMEM_LIMIT),
        name="rotary_tables",
    )(pos, pos, freq)


def _seg_mean(x, e_ref):
    return _dot(_bf(x), e_ref[...])


def _rot(v, c, s, first_half):
    partner = jnp.where(first_half, pltpu.roll(v, 96, axis=1), pltpu.roll(v, 32, axis=1))
    return v * c + partner * s


def _layer_kernel(sinks_ref,
                  x_ref, mod_ref, pre_ref, post_ref, win_ref, wout_ref, gw_ref, gb_ref, gng_ref,
                  ca_ref, sa_ref, cr_ref, sr_ref,
                  bias_ref, vones_ref, dmask_ref, qdec_ref, kdec_ref, cdm_ref, rbm_ref,
                  tri_ref, lmask_ref, smask_ref, e64_ref,
                  o_ref, *scratch):
    hb_grp = scratch[0:NGRP]
    proj_grp = scratch[NGRP:2 * NGRP]
    mix_grp = scratch[2 * NGRP:3 * NGRP]
    kprev_ref, vprev_ref, rstate_ref, gstate_ref, bscr_ref = scratch[3 * NGRP:]
    t = pl.program_id(1)

    def sub_view(refs, sb):
        return refs[sb // PGRP].at[pl.ds((sb % PGRP) * BLK, BLK)]

    hb_refs = [sub_view(hb_grp, sb) for sb in range(NSUB)]
    proj_refs = [sub_view(proj_grp, sb) for sb in range(NSUB)]
    mix_refs = [sub_view(mix_grp, sb) for sb in range(NSUB)]

    @pl.when(t == 0)
    def _():
        kprev_ref[...] = jnp.zeros_like(kprev_ref)
        vprev_ref[...] = jnp.zeros_like(vprev_ref)
        rstate_ref[...] = jnp.zeros_like(rstate_ref)
        gstate_ref[...] = jnp.zeros_like(gstate_ref)
        for g in range(NGRP):
            proj_grp[g][:, O_GA:] = jnp.zeros((PGRP * BLK, N_PROJ - O_GA), jnp.float32)

    lane = lax.broadcasted_iota(jnp.int32, (1, LANES), 1)
    lane2 = lax.broadcasted_iota(jnp.int32, (1, GLA_W), 1)
    zero_bf = jnp.zeros((), jnp.bfloat16)
    first_half = (lane % 64) < 32
    lo_hi = [lane // 64 == g for g in range(2)]
    hgrp = [lane // GLA_DK == h for h in range(GLA_HEADS)]
    vhead = [lane2 // GLA_DV == h for h in range(GLA_HEADS)]

    def rows(sb):
        return slice(sb * BLK, (sb + 1) * BLK)

    done = set()

    def gen_norm(sb):
        if sb >= PGRP:
            yield f"projgrp{sb // PGRP - 1}.started"
        x = x_ref[0, rows(sb), :]
        inv = lax.rsqrt(jnp.mean(x * x, axis=-1, keepdims=True) + EPS)
        hb_refs[sb][...] = _bf((x * inv) * (pre_ref[...] * (1.0 + mod_ref[1:2, :])) + mod_ref[0:1, :])
        done.add(f"norm{sb}")

    def gen_proj(g):
        subs = range(g * PGRP, (g + 1) * PGRP)
        for sb in subs:
            yield f"norm{sb}"
        if g > 0:
            yield f"projgrp{g - 1}.all"
        done.add(f"projgrp{g}.started")
        for name, a, b in PROJ_SEGMENTS:
            proj_grp[g][:, a:b] = _dot(hb_grp[g][...], win_ref[:, a:b])
            done.update(f"proj{sb}.{name}" for sb in subs)
            yield
        done.add(f"projgrp{g}.all")

    def gen_attn(sb):
        yield f"proj{sb}.akv"
        if sb > 0:
            yield f"attn{sb - 1}.kv"
        pr = proj_refs[sb]
        ca, sa = ca_ref[0, rows(sb), :], sa_ref[0, rows(sb), :]
        k_new = _rot(pr[:, O_AK:O_AK + 128], ca, sa, first_half) * (HEAD_DIM ** -0.5 * LOG2E)
        v_new = pr[:, O_AV:O_AV + 128]
        k_sw = pltpu.roll(k_new, 64, axis=1)
        v_sw = pltpu.roll(v_new, 64, axis=1)
        k2, v2 = [], []
        for g in range(2):
            k_g = _bf(jnp.where(lo_hi[g], k_new, k_sw))
            v_g = _bf(jnp.where(lo_hi[g], v_new, v_sw))
            kcat = jnp.concatenate([kprev_ref[g], k_g], axis=0)
            vcat = jnp.concatenate([vprev_ref[g], v_g], axis=0)
            kprev_ref[g] = k_g
            vprev_ref[g] = v_g
            k2.append(jnp.concatenate([jnp.where(lo_hi[h], kcat, zero_bf) for h in range(2)], axis=0))
            v2_g = jnp.concatenate([jnp.where(lo_hi[h], vcat, zero_bf) for h in range(2)], axis=0)
            v2.append(jnp.concatenate([v2_g, vones_ref[...]], axis=1))
        done.add(f"attn{sb}.kv")
        if sb == 0:
            bias = bias_ref[jnp.minimum(t, 1)]
        else:
            bias = bias_ref[1]

        def scores(g):
            q = [_rot(pr[:, O_AQ + m * 128:O_AQ + (m + 1) * 128], ca, sa, first_half) for m in (2 * g, 2 * g + 1)]
            return _dot_nt(_bf(jnp.concatenate(q, axis=0)), k2[g])

        yield f"proj{sb}.aq"
        s_g = scores(0)
        yield
        for g in range(2):
            ps, mxs = [], []
            for j in range(2):
                row_p, row_mx = [], []
                for h in range(2):
                    s = s_g[j * BLK:(j + 1) * BLK, h * 2 * BLK:(h + 1) * 2 * BLK] + bias
                    mx = jnp.maximum(jnp.max(s, axis=-1, keepdims=True), sinks_ref[2 * (2 * g + j) + h] * LOG2E)
                    row_p.append(_bf(jnp.exp2(s - mx)))
                    row_mx.append(mx)
                ps.append(jnp.concatenate(row_p, axis=1))
                mxs.append(row_mx)
            pv = _dot(jnp.concatenate(ps, axis=0), v2[g])
            if g == 0:
                s_g = scores(1)
            yield
            yield f"proj{sb}.ag"
            for j in range(2):
                m = 2 * g + j
                pv_m = pv[j * BLK:(j + 1) * BLK]
                mx_l = jnp.where(lo_hi[0], mxs[j][0], mxs[j][1])
                sink_l = jnp.where(lo_hi[0], sinks_ref[2 * m], sinks_ref[2 * m + 1]) * LOG2E
                den = pv_m[:, 128:] + jnp.exp2(sink_l - mx_l)
                gate_m = _silu(pr[:, O_AG + m * 128:O_AG + (m + 1) * 128])
                mix_refs[sb][:, m * 128:(m + 1) * 128] = _bf(pv_m[:, :128] / den * gate_m)
        done.add(f"attn{sb}.done")

    def gen_ret(sb):
        yield f"proj{sb}.rqk"
        yield f"proj{sb}.rvg"
        if sb > 0:
            yield f"ret{sb - 1}.state"
        pr = proj_refs[sb]
        cr, sr = cr_ref[0, rows(sb), :], sr_ref[0, rows(sb), :]
        held = []
        for p in range(2):
            q_p = _rot(pr[:, O_RQ + p * 128:O_RQ + (p + 1) * 128], cr, sr, first_half)
            k_p = _rot(pr[:, O_RK + p * 128:O_RK + (p + 1) * 128], cr, sr, first_half)
            v_p = _bf(pr[:, O_RV + p * 128:O_RV + (p + 1) * 128])
            k_pb = _bf(k_p)
            kbd = jnp.concatenate([jnp.where(lo_hi[h], k_pb, zero_bf) for h in range(2)], axis=0)
            vbd = jnp.concatenate([jnp.where(lo_hi[h], v_p, zero_bf) for h in range(2)], axis=0)
            sc_raw = _dot_nt(_bf(q_p), kbd)
            st = rstate_ref[p]
            inter = _dot(_bf(q_p * qdec_ref[p]), _bf(st))
            upd = _dot_tn(_bf(k_p * kdec_ref[p]), v_p)
            held.append((sc_raw, vbd, st, inter, upd))
        yield
        intras = []
        for p in range(2):
            sc_raw, vbd, st, inter, upd = held[p]
            intras.append(_dot(_bf(sc_raw * dmask_ref[p]), vbd) + inter)
            rstate_ref[p] = st * cdm_ref[p] + upd * rbm_ref[...]
        done.add(f"ret{sb}.state")
        yield
        r = jnp.concatenate(intras, axis=1)
        ms = _seg_mean(r * r, e64_ref)
        yield
        mix_refs[sb][:, ATTN_W:ATTN_W + RET_W] = _bf(r * lax.rsqrt(ms + EPS) * _silu(pr[:, O_RG:O_RG + 256]))
        done.add(f"ret{sb}.done")

    def gen_gla(sb):
        yield f"proj{sb}.gga"
        pr = proj_refs[sb]
        logits = _dot(_bf(pr[:, O_GA:O_GA + 128]), gw_ref[...])
        yield
        logits = logits + gb_ref[...]
        log_a = (jnp.minimum(logits, 0.0) - jnp.log1p(jnp.exp(-jnp.abs(logits)))) * (LOG2E / GLA_GATE_NORMALIZER)
        la_hi, la_lo = _split_hi_lo(log_a)
        xr = _dot(tri_ref[...], jnp.concatenate([la_hi, la_lo], axis=1))
        yield
        yield f"proj{sb}.gqkv"
        b = xr[:, :128] + xr[:, 128:]
        bscr = bscr_ref.at[sb]
        bscr[...] = b
        gq = pr[:, O_GQ:O_GQ + 128] * (GLA_DK ** -0.5)
        gk = pr[:, O_GK:O_GK + 128]
        gv = _bf(pr[:, O_GV:O_GV + 256])
        nch = BLK // GLA_C
        row = lax.broadcasted_iota(jnp.int32, (GLA_C, LANES), 0)

        def level_factor(c, lv):
            r0 = c * GLA_C
            b_c = b[r0:r0 + GLA_C]
            grp = 1 << lv
            if grp >= 8:
                mids = [bscr[r0 + g0 + grp // 2 - 1:r0 + g0 + grp // 2, :] for g0 in range(0, GLA_C, grp)]
                ref = jnp.concatenate([jnp.broadcast_to(m, (grp, LANES)) for m in mids], axis=0)
            elif grp == 4:
                r = row % 4
                ref = jnp.where(r == 0, pltpu.roll(b_c, GLA_C - 1, axis=0),
                                jnp.where(r == 1, b_c,
                                          jnp.where(r == 2, pltpu.roll(b_c, 1, axis=0), pltpu.roll(b_c, 2, axis=0))))
            else:
                ref = jnp.where(row % 2 == 1, pltpu.roll(b_c, 1, axis=0), b_c)
            return jnp.exp2(-jnp.abs(b_c - ref))

        saccs = [jnp.zeros((GLA_C, GLA_HEADS * GLA_C), jnp.float32) for _ in range(nch)]
        pending = []
        for lv in range(GLA_LEVELS + 1):
            raw = []
            for c in range(nch):
                r0 = c * GLA_C
                if lv == 0:
                    qt, kt = _bf(gq[r0:r0 + GLA_C]), _bf(gk[r0:r0 + GLA_C])
                else:
                    f = level_factor(c, lv)
                    qt, kt = _bf(gq[r0:r0 + GLA_C] * f), _bf(gk[r0:r0 + GLA_C] * f)
                kbd = jnp.concatenate([jnp.where(hgrp[h], kt, zero_bf) for h in range(GLA_HEADS)], axis=0)
                raw.append(_dot_nt(qt, kbd))
            pending.append((lv, raw))
            if len(pending) == GLA_LEVELS_PER_STAGE or lv == GLA_LEVELS:
                yield
                for lv_p, raw_p in pending:
                    for c in range(nch):
                        saccs[c] = saccs[c] + raw_p[c] * lmask_ref[lv_p]
                pending = []
        if sb > 0:
            yield f"gla{sb - 1}.state"
        g_parts = []
        st = gstate_ref[...]
        for c in range(nch):
            r0 = c * GLA_C
            v_c = gv[r0:r0 + GLA_C]
            vbd = jnp.concatenate([jnp.where(vhead[h], v_c, zero_bf) for h in range(GLA_HEADS)], axis=0)
            b_c = b[r0:r0 + GLA_C]
            b_last = bscr[r0 + GLA_C - 1:r0 + GLA_C, :]
            q_in = _bf(gq[r0:r0 + GLA_C] * jnp.exp2(b_c))
            o_c = _dot(_bf(saccs[c]), vbd) + _dot_nt(q_in, _bf(st))
            k_out = _bf(gk[r0:r0 + GLA_C] * jnp.exp2(b_last - b_c))
            upd = _dot_tn(v_c, k_out)
            yield
            g_parts.append(o_c)
            st = st * jnp.exp2(b_last) + upd * smask_ref[...]
        gstate_ref[...] = st
        done.add(f"gla{sb}.state")
        g = jnp.concatenate(g_parts, axis=0)
        ms = _seg_mean(g * g, e64_ref)
        yield
        gate_g = _silu(pr[:, O_GG:O_GG + 256])
        mix_refs[sb][:, ATTN_W + RET_W:] = _bf(g * lax.rsqrt(ms + EPS) * gng_ref[...] * gate_g)
        done.add(f"gla{sb}.done")

    def gen_out(g):
        for sb in range(g * PGRP, (g + 1) * PGRP):
            yield f"attn{sb}.done"
            yield f"ret{sb}.done"
            yield f"gla{sb}.done"
        rws = slice(g * PGRP * BLK, (g + 1) * PGRP * BLK)
        half = D_MODEL // 2
        y0 = _dot(mix_grp[g][...], wout_ref[:, :half])
        yield
        y1 = _dot(mix_grp[g][...], wout_ref[:, half:])
        yield
        ssq = jnp.sum(y0 * y0, axis=-1, keepdims=True) + jnp.sum(y1 * y1, axis=-1, keepdims=True)
        inv = lax.rsqrt(ssq * (1.0 / D_MODEL) + EPS)
        gp = mod_ref[2:3, :] * post_ref[...]
        o_ref[0, rws, :half] = x_ref[0, rws, :half] + (y0 * inv) * gp[:, :half]
        yield
        o_ref[0, rws, half:] = x_ref[0, rws, half:] + (y1 * inv) * gp[:, half:]

    gens = []
    for g in range(NGRP):
        for sb in range(g * PGRP, (g + 1) * PGRP):
            gens += [gen_gla(sb), gen_attn(sb), gen_ret(sb)]
        gens += [gen_out(g), gen_proj(g)] + [gen_norm(sb) for sb in range(g * PGRP, (g + 1) * PGRP)]
    _run_interleaved(gens, done)


def _const_spec(shape):
    nd = len(shape)
    return pl.BlockSpec(shape, lambda b, t, s, _n=nd: (0,) * _n)


def _layer(l, x, mod, pre_gain, post_gain, w_in_b, w_out_b, sinks, gate_w_p, gate_b, gng, tables, consts):
    bsz, seq, d = x.shape
    tb = NSUB * BLK
    tok_spec = pl.BlockSpec((1, tb, LANES), lambda b, t, s: (b, t, 0))
    stacked = [pre_gain, post_gain, w_in_b, w_out_b, gate_w_p, gate_b, gng]

    def layer_spec(a):
        return pl.BlockSpec((None,) + a.shape[1:], lambda b, t, s, _n=a.ndim - 1: (l,) + (0,) * _n)

    in_specs = ([pl.BlockSpec((1, tb, d), lambda b, t, s: (b, t, 0)),
                 pl.BlockSpec((None, None, 3, d), lambda b, t, s: (l, b, 0, 0))]
                + [layer_spec(a) for a in stacked]
                + [tok_spec] * 4
                + [_const_spec(a.shape) for a in consts])
    scratch = ([pltpu.VMEM((PGRP * BLK, d), jnp.bfloat16)] * NGRP
               + [pltpu.VMEM((PGRP * BLK, N_PROJ), jnp.float32)] * NGRP
               + [pltpu.VMEM((PGRP * BLK, d), jnp.bfloat16)] * NGRP
               + [pltpu.VMEM((2, BLK, LANES), jnp.bfloat16),
                  pltpu.VMEM((2, BLK, LANES), jnp.bfloat16),
                  pltpu.VMEM((2, LANES, LANES), jnp.float32),
                  pltpu.VMEM((GLA_W, LANES), jnp.float32),
                  pltpu.VMEM((NSUB, BLK, LANES), jnp.float32)])
    grid_spec = pltpu.PrefetchScalarGridSpec(
        num_scalar_prefetch=1,
        grid=(bsz, seq // tb),
        in_specs=in_specs,
        out_specs=pl.BlockSpec((1, tb, d), lambda b, t, s: (b, t, 0)),
        scratch_shapes=scratch)
    return pl.pallas_call(
        _layer_kernel,
        out_shape=jax.ShapeDtypeStruct(x.shape, x.dtype),
        grid_spec=grid_spec,
        compiler_params=pltpu.CompilerParams(dimension_semantics=("arbitrary", "arbitrary"),
                                             vmem_limit_bytes=VMEM_LIMIT),
        name="hybrid_layer",
    )(sinks, x, mod, *stacked, *tables, *consts)


def kernel(x, c, positions, w_mod, b_mod, pre_norm_gain, post_norm_gain, w_in, attn_sinks, gla_gate_w,
           gla_gate_b, gla_norm_gain, w_out):
    depth = w_mod.shape[0]
    bsz = x.shape[0]
    f32, bf16 = jnp.float32, jnp.bfloat16

    mod = _modulation(c, w_mod, b_mod).reshape(depth, bsz, 3, D_MODEL)
    rope_freq = ROPE_THETA ** (-jnp.arange(0, HEAD_DIM, 2, dtype=f32) / HEAD_DIM)
    ret_freq = 1.0 / (10000.0 ** jnp.linspace(0.0, 1.0, RET_DK // 2, dtype=f32))
    freq = jnp.concatenate([rope_freq, ret_freq, rope_freq, ret_freq]).reshape(1, LANES)
    tables = _trig_tables(positions, freq)

    consts = [jnp.asarray(_C["bias"]), jnp.asarray(_C["vones"], bf16), jnp.asarray(_C["dmask"]),
              jnp.asarray(_C["qdec"]), jnp.asarray(_C["kdec"]), jnp.asarray(_C["cdm"]), jnp.asarray(_C["rbm"]),
              jnp.asarray(_C["tri"], bf16), jnp.asarray(_C["lmask"]), jnp.asarray(_C["smask"]),
              jnp.asarray(_C["e64"], bf16)]
    w_in_b = w_in.astype(bf16)
    w_out_b = w_out.astype(bf16)
    gate_w_p = jnp.pad(gla_gate_w, ((0, 0), (0, LANES - GLA_GATE_RANK), (0, 0))).astype(bf16)
    gng = jnp.tile(gla_norm_gain, (1, GLA_HEADS)).reshape(depth, 1, GLA_W)
    pre = pre_norm_gain.reshape(depth, 1, D_MODEL)
    post = post_norm_gain.reshape(depth, 1, D_MODEL)
    gate_b = gla_gate_b.reshape(depth, 1, LANES)

    for l in range(depth):
        x = _layer(l, x, mod, pre, post, w_in_b, w_out_b, attn_sinks[l], gate_w_p, gate_b, gng, tables, consts)
    return x
```

```python
import numpy as np
import jax
import jax.numpy as jnp
from jax import lax
from jax.experimental import pallas as pl
from jax.experimental.pallas import tpu as pltpu

D_MODEL = 1024
HEAD_DIM = 64
ATTN_HEADS = 8
ATTN_KV_HEADS = 2
WINDOW = 128
ROPE_THETA = 10000.0
RET_HEADS = 4
RET_DK = 64
RET_DV = 64
GLA_HEADS = 4
GLA_DK = 32
GLA_DV = 64
GLA_GATE_RANK = 16
GLA_GATE_NORMALIZER = 16.0
EPS = 1e-6

ATTN_W = ATTN_HEADS * HEAD_DIM
RET_W = RET_HEADS * RET_DV
GLA_W = GLA_HEADS * GLA_DV

LANES = 128
BLK = 128
NSUB = 4
PGRP = 2
NGRP = NSUB // PGRP
GLA_C = 64
GLA_LEVELS = 6
GLA_LEVELS_PER_STAGE = 7
NEG = -1e30
VMEM_LIMIT = 56 * 1024 * 1024
LOG2E = 1.4426950408889634

O_AQ, O_AK, O_AV, O_AG = 0, 512, 640, 768
O_RQ, O_RK, O_RV, O_RG = 1280, 1536, 1792, 2048
O_GQ, O_GK, O_GV, O_GG, O_GA = 2304, 2432, 2560, 2816, 3072
D_IN = O_GA + GLA_GATE_RANK
N_PROJ = 3200
PROJ_SEGMENTS = (("gga", O_GG, D_IN), ("gqkv", O_GQ, O_GG), ("akv", O_AK, O_AG), ("aq", O_AQ, O_AK),
                 ("rqk", O_RQ, O_RV), ("rvg", O_RV, O_GQ), ("ag", O_AG, O_RQ))


def _build_constants():
    f32 = np.float32
    lane = np.arange(LANES)
    grp = lane // 64

    i = np.arange(BLK)[:, None]
    j = np.arange(2 * BLK)[None, :]
    rel = i + BLK - j
    ok = (rel >= 0) & (rel < WINDOW)
    bias = np.stack([np.where(ok & (j >= BLK), 0.0, NEG), np.where(ok, 0.0, NEG)]).astype(f32)
    vones = np.zeros((2 * 2 * BLK, LANES), f32)
    for g in range(2):
        vones[g * 2 * BLK:(g + 1) * 2 * BLK] = (lane // 64 == g)[None, :]

    kscale = RET_DK ** -0.5
    log_g = np.log(1.0 - 2.0 ** (-5.0 - np.arange(RET_HEADS, dtype=np.float64)))
    idx = np.arange(BLK, dtype=np.float64)
    diff = idx[:, None] - idx[None, :]
    dmask = np.zeros((2, BLK, 2 * BLK), f32)
    qdec = np.zeros((2, BLK, LANES), f32)
    kdec = np.zeros((2, BLK, LANES), f32)
    cdm = np.zeros((2, LANES, LANES), f32)
    rbm = ((np.arange(LANES)[:, None] // 64) == (np.arange(LANES)[None, :] // 64)).astype(f32)
    for p in range(2):
        for hh in range(2):
            lg = log_g[2 * p + hh]
            dmask[p, :, hh * BLK:(hh + 1) * BLK] = kscale * np.where(diff >= 0, np.exp(lg * np.maximum(diff, 0.0)), 0.0)
        lg_lane = log_g[2 * p + grp]
        qdec[p] = np.exp(lg_lane[None, :] * (idx[:, None] + 1.0))
        kdec[p] = kscale * np.exp(lg_lane[None, :] * (BLK - 1.0 - idx[:, None]))
        cdm[p] = np.exp(lg_lane * BLK)[:, None] * rbm

    t = np.arange(BLK)
    tri = (((t[:, None] // GLA_C) == (t[None, :] // GLA_C)) & (t[None, :] <= t[:, None])).astype(f32)
    ci = np.arange(GLA_C)[:, None]
    cj = np.arange(GLA_C)[None, :]
    x = ci ^ cj
    lvl = np.where(cj > ci, -1, np.where(x == 0, 0, np.floor(np.log2(np.maximum(x, 1))).astype(np.int64) + 1))
    lmask = np.stack([np.tile((lvl == lv).astype(f32), (1, GLA_HEADS)) for lv in range(GLA_LEVELS + 1)])
    smask = ((np.arange(GLA_W)[:, None] // GLA_DV) == (np.arange(LANES)[None, :] // GLA_DK)).astype(f32)
    e64 = ((np.arange(256)[:, None] // 64) == (np.arange(256)[None, :] // 64)).astype(f32) / 64.0

    return dict(bias=bias, vones=vones, dmask=dmask, qdec=qdec, kdec=kdec, cdm=cdm, rbm=rbm, tri=tri,
                lmask=lmask, smask=smask, e64=e64)


_C = _build_constants()


def _dot(a, b):
    return jnp.dot(a, b, preferred_element_type=jnp.float32)


def _dot_nt(a, b):
    return lax.dot_general(a, b, (((1,), (1,)), ((), ())), preferred_element_type=jnp.float32)


def _dot_tn(a, b):
    return lax.dot_general(a, b, (((0,), (0,)), ((), ())), preferred_element_type=jnp.float32)


def _bf(x):
    return x.astype(jnp.bfloat16)


def _split_hi_lo(x):
    hi = _bf(x)
    lo = _bf(x - hi.astype(jnp.float32))
    return hi, lo


def _silu(x):
    half = 0.5 * x
    return half + half * jnp.tanh(half)


def _run_interleaved(gens, done):
    active = [[g, None] for g in gens]
    while active:
        progressed = False
        for item in list(active):
            if item[1] is not None and item[1] not in done:
                continue
            progressed = True
            try:
                need = next(item[0])
                while need is not None and need in done:
                    need = next(item[0])
                item[1] = need
            except StopIteration:
                active.remove(item)
        assert progressed, [item[1] for item in active]


MOD_TK = 256


def _mod_kernel(c_ref, w_ref, b_ref, o_ref):
    @pl.when(pl.program_id(1) == 0)
    def _():
        o_ref[0] = jnp.broadcast_to(b_ref[0], o_ref.shape[1:])

    a = _silu(c_ref[...])
    a_hi, a_lo = _split_hi_lo(a)
    w_hi, w_lo = _split_hi_lo(w_ref[0])
    n = a.shape[0]
    both = _dot(jnp.concatenate([a_hi, a_lo], axis=0), w_hi)
    o_ref[0] += both[:n] + both[n:] + _dot(a_hi, w_lo)


def _modulation(c, w_mod, b_mod):
    depth, d, n = w_mod.shape
    bsz = c.shape[0]
    return pl.pallas_call(
        _mod_kernel,
        out_shape=jax.ShapeDtypeStruct((depth, bsz, n), jnp.float32),
        grid=(depth, d // MOD_TK),
        in_specs=[pl.BlockSpec((bsz, MOD_TK), lambda l, k: (0, k)),
                  pl.BlockSpec((1, MOD_TK, n), lambda l, k: (l, k, 0)),
                  pl.BlockSpec((1, 1, n), lambda l, k: (l, 0, 0))],
        out_specs=pl.BlockSpec((1, bsz, n), lambda l, k: (l, 0, 0)),
        compiler_params=pltpu.CompilerParams(dimension_semantics=("arbitrary", "arbitrary"),
                                             vmem_limit_bytes=VMEM_LIMIT),
        name="adaln_mod",
    )(c, w_mod, b_mod.reshape(depth, 1, n))


TRIG_T = 512


def _trig_kernel(posa_ref, posb_ref, freq_ref, ca_ref, sa_ref, cr_ref, sr_ref):
    half = TRIG_T // 2
    lane = lax.broadcasted_iota(jnp.int32, (half, LANES), 1)
    pos = jnp.where(lane < 64, posa_ref[0], posb_ref[0]).astype(jnp.float32)
    ang = pos * freq_ref[...]
    c = jnp.cos(ang)
    s = jnp.sin(ang)
    quarter = lane // 32
    sign = jnp.where((lane % 64) < 32, -1.0, 1.0)
    rc = [c] + [pltpu.roll(c, k, axis=1) for k in (32, 64, 96)]
    rs = [s] + [pltpu.roll(s, k, axis=1) for k in (32, 64, 96)]

    def spread(r, src):
        out = r[(0 - src) % 4]
        for q in (1, 2, 3):
            out = jnp.where(quarter == q, r[(q - src) % 4], out)
        return out

    for tok, rws in ((0, slice(0, half)), (1, slice(half, TRIG_T))):
        ca_ref[0, rws, :] = spread(rc, 2 * tok)
        sa_ref[0, rws, :] = spread(rs, 2 * tok) * sign
        cr_ref[0, rws, :] = spread(rc, 2 * tok + 1)
        sr_ref[0, rws, :] = spread(rs, 2 * tok + 1) * sign


def _trig_tables(positions, freq):
    bsz, seq = positions.shape
    half = TRIG_T // 2
    out = jax.ShapeDtypeStruct((bsz, seq, LANES), jnp.float32)
    spec = pl.BlockSpec((1, TRIG_T, LANES), lambda b, t: (b, t, 0))
    pos = positions.reshape(bsz, seq, 1)
    return pl.pallas_call(
        _trig_kernel,
        out_shape=(out, out, out, out),
        grid=(bsz, seq // TRIG_T),
        in_specs=[pl.BlockSpec((1, half, 1), lambda b, t: (b, 2 * t, 0)),
                  pl.BlockSpec((1, half, 1), lambda b, t: (b, 2 * t + 1, 0)),
                  pl.BlockSpec((1, LANES), lambda b, t: (0, 0))],
        out_specs=(spec, spec, spec, spec),
        compiler_params=pltpu.CompilerParams(dimension_semantics=("arbitrary", "arbitrary"),
                                             vmem_limit_bytes=VMEM_LIMIT),
        name="rotary_tables",
    )(pos, pos, freq)


def _seg_mean(x, e_ref):
    return _dot(_bf(x), e_ref[...])


def _rot(v, c, s, first_half):
    partner = jnp.where(first_half, pltpu.roll(v, 96, axis=1), pltpu.roll(v, 32, axis=1))
    return v * c + partner * s


def _layer_kernel(sinks_ref,
                  x_ref, mod_ref, pre_ref, post_ref, win_ref, wout_ref, gw_ref, gb_ref, gng_ref,
                  ca_ref, sa_ref, cr_ref, sr_ref,
                  bias_ref, vones_ref, dmask_ref, qdec_ref, kdec_ref, cdm_ref, rbm_ref,
                  tri_ref, lmask_ref, smask_ref, e64_ref,
                  o_ref, *scratch):
    hb_grp = scratch[0:NGRP]
    proj_grp = scratch[NGRP:2 * NGRP]
    mix_grp = scratch[2 * NGRP:3 * NGRP]
    kprev_ref, vprev_ref, rstate_ref, gstate_ref, bscr_ref = scratch[3 * NGRP:]
    t = pl.program_id(1)

    def sub_view(refs, sb):
        return refs[sb // PGRP].at[pl.ds((sb % PGRP) * BLK, BLK)]

    hb_refs = [sub_view(hb_grp, sb) for sb in range(NSUB)]
    proj_refs = [sub_view(proj_grp, sb) for sb in range(NSUB)]
    mix_refs = [sub_view(mix_grp, sb) for sb in range(NSUB)]

    @pl.when(t == 0)
    def _():
        kprev_ref[...] = jnp.zeros_like(kprev_ref)
        vprev_ref[...] = jnp.zeros_like(vprev_ref)
        rstate_ref[...] = jnp.zeros_like(rstate_ref)
        gstate_ref[...] = jnp.zeros_like(gstate_ref)
        for g in range(NGRP):
            proj_grp[g][:, O_GA:] = jnp.zeros((PGRP * BLK, N_PROJ - O_GA), jnp.float32)

    lane = lax.broadcasted_iota(jnp.int32, (1, LANES), 1)
    lane2 = lax.broadcasted_iota(jnp.int32, (1, GLA_W), 1)
    zero_bf = jnp.zeros((), jnp.bfloat16)
    first_half = (lane % 64) < 32
    lo_hi = [lane // 64 == g for g in range(2)]
    hgrp = [lane // GLA_DK == h for h in range(GLA_HEADS)]
    vhead = [lane2 // GLA_DV == h for h in range(GLA_HEADS)]

    def rows(sb):
        return slice(sb * BLK, (sb + 1) * BLK)

    done = set()

    def gen_norm(sb):
        if sb >= PGRP:
            yield f"projgrp{sb // PGRP - 1}.started"
        x = x_ref[0, rows(sb), :]
        inv = lax.rsqrt(jnp.mean(x * x, axis=-1, keepdims=True) + EPS)
        hb_refs[sb][...] = _bf((x * inv) * (pre_ref[...] * (1.0 + mod_ref[1:2, :])) + mod_ref[0:1, :])
        done.add(f"norm{sb}")

    def gen_proj(g):
        subs = range(g * PGRP, (g + 1) * PGRP)
        for sb in subs:
            yield f"norm{sb}"
        if g > 0:
            yield f"projgrp{g - 1}.all"
        done.add(f"projgrp{g}.started")
        for name, a, b in PROJ_SEGMENTS:
            proj_grp[g][:, a:b] = _dot(hb_grp[g][...], win_ref[:, a:b])
            done.update(f"proj{sb}.{name}" for sb in subs)
            yield
        done.add(f"projgrp{g}.all")

    def gen_attn(sb):
        yield f"proj{sb}.akv"
        if sb > 0:
            yield f"attn{sb - 1}.kv"
        pr = proj_refs[sb]
        ca, sa = ca_ref[0, rows(sb), :], sa_ref[0, rows(sb), :]
        k_new = _rot(pr[:, O_AK:O_AK + 128], ca, sa, first_half) * (HEAD_DIM ** -0.5 * LOG2E)
        v_new = pr[:, O_AV:O_AV + 128]
        k_sw = pltpu.roll(k_new, 64, axis=1)
        v_sw = pltpu.roll(v_new, 64, axis=1)
        k2, v2 = [], []
        for g in range(2):
            k_g = _bf(jnp.where(lo_hi[g], k_new, k_sw))
            v_g = _bf(jnp.where(lo_hi[g], v_new, v_sw))
            kcat = jnp.concatenate([kprev_ref[g], k_g], axis=0)
            vcat = jnp.concatenate([vprev_ref[g], v_g], axis=0)
            kprev_ref[g] = k_g
            vprev_ref[g] = v_g
            k2.append(jnp.concatenate([jnp.where(lo_hi[h], kcat, zero_bf) for h in range(2)], axis=0))
            v2_g = jnp.concatenate([jnp.where(lo_hi[h], vcat, zero_bf) for h in range(2)], axis=0)
            v2.append(jnp.concatenate([v2_g, vones_ref[...]], axis=1))
        done.add(f"attn{sb}.kv")
        if sb == 0:
            bias = bias_ref[jnp.minimum(t, 1)]
        else:
            bias = bias_ref[1]

        def scores(g):
            q = [_rot(pr[:, O_AQ + m * 128:O_AQ + (m + 1) * 128], ca, sa, first_half) for m in (2 * g, 2 * g + 1)]
            return _dot_nt(_bf(jnp.concatenate(q, axis=0)), k2[g])

        yield f"proj{sb}.aq"
        s_g = scores(0)
        yield
        for g in range(2):
            ps, mxs = [], []
            for j in range(2):
                row_p, row_mx = [], []
                for h in range(2):
                    s = s_g[j * BLK:(j + 1) * BLK, h * 2 * BLK:(h + 1) * 2 * BLK] + bias
                    mx = jnp.maximum(jnp.max(s, axis=-1, keepdims=True), sinks_ref[2 * (2 * g + j) + h] * LOG2E)
                    row_p.append(_bf(jnp.exp2(s - mx)))
                    row_mx.append(mx)
                ps.append(jnp.concatenate(row_p, axis=1))
                mxs.append(row_mx)
            pv = _dot(jnp.concatenate(ps, axis=0), v2[g])
            if g == 0:
                s_g = scores(1)
            yield
            yield f"proj{sb}.ag"
            for j in range(2):
                m = 2 * g + j
                pv_m = pv[j * BLK:(j + 1) * BLK]
                mx_l = jnp.where(lo_hi[0], mxs[j][0], mxs[j][1])
                sink_l = jnp.where(lo_hi[0], sinks_ref[2 * m], sinks_ref[2 * m + 1]) * LOG2E
                den = pv_m[:, 128:] + jnp.exp2(sink_l - mx_l)
                gate_m = _silu(pr[:, O_AG + m * 128:O_AG + (m + 1) * 128])
                mix_refs[sb][:, m * 128:(m + 1) * 128] = _bf(pv_m[:, :128] / den * gate_m)
        done.add(f"attn{sb}.done")

    def gen_ret(sb):
        yield f"proj{sb}.rqk"
        yield f"proj{sb}.rvg"
        if sb > 0:
            yield f"ret{sb - 1}.state"
        pr = proj_refs[sb]
        cr, sr = cr_ref[0, rows(sb), :], sr_ref[0, rows(sb), :]
        held = []
        for p in range(2):
            q_p = _rot(pr[:, O_RQ + p * 128:O_RQ + (p + 1) * 128], cr, sr, first_half)
            k_p = _rot(pr[:, O_RK + p * 128:O_RK + (p + 1) * 128], cr, sr, first_half)
            v_p = _bf(pr[:, O_RV + p * 128:O_RV + (p + 1) * 128])
            k_pb = _bf(k_p)
            kbd = jnp.concatenate([jnp.where(lo_hi[h], k_pb, zero_bf) for h in range(2)], axis=0)
            vbd = jnp.concatenate([jnp.where(lo_hi[h], v_p, zero_bf) for h in range(2)], axis=0)
            sc_raw = _dot_nt(_bf(q_p), kbd)
            st = rstate_ref[p]
            inter = _dot(_bf(q_p * qdec_ref[p]), _bf(st))
            upd = _dot_tn(_bf(k_p * kdec_ref[p]), v_p)
            held.append((sc_raw, vbd, st, inter, upd))
        yield
        intras = []
        for p in range(2):
            sc_raw, vbd, st, inter, upd = held[p]
            intras.append(_dot(_bf(sc_raw * dmask_ref[p]), vbd) + inter)
            rstate_ref[p] = st * cdm_ref[p] + upd * rbm_ref[...]
        done.add(f"ret{sb}.state")
        yield
        r = jnp.concatenate(intras, axis=1)
        ms = _seg_mean(r * r, e64_ref)
        yield
        mix_refs[sb][:, ATTN_W:ATTN_W + RET_W] = _bf(r * lax.rsqrt(ms + EPS) * _silu(pr[:, O_RG:O_RG + 256]))
        done.add(f"ret{sb}.done")

    def gen_gla(sb):
        yield f"proj{sb}.gga"
        pr = proj_refs[sb]
        logits = _dot(_bf(pr[:, O_GA:O_GA + 128]), gw_ref[...])
        yield
        logits = logits + gb_ref[...]
        log_a = (jnp.minimum(logits, 0.0) - jnp.log(1.0 + jnp.exp(-jnp.abs(logits)))) * (LOG2E / GLA_GATE_NORMALIZER)
        la_hi, la_lo = _split_hi_lo(log_a)
        xr = _dot(tri_ref[...], jnp.concatenate([la_hi, la_lo], axis=1))
        yield
        yield f"proj{sb}.gqkv"
        bscr = bscr_ref.at[sb]
        bscr[...] = xr[:, :128] + xr[:, 128:]
        gq = pr[:, O_GQ:O_GQ + 128] * (GLA_DK ** -0.5)
        gk = pr[:, O_GK:O_GK + 128]
        gv = _bf(pr[:, O_GV:O_GV + 256])
        nch = BLK // GLA_C
        row = lax.broadcasted_iota(jnp.int32, (GLA_C, LANES), 0)

        def level_factor(c, lv):
            r0 = c * GLA_C
            b_c = bscr[r0:r0 + GLA_C, :]
            grp = 1 << lv
            if grp >= 8:
                mids = [bscr[r0 + g0 + grp // 2 - 1:r0 + g0 + grp // 2, :] for g0 in range(0, GLA_C, grp)]
                ref = jnp.concatenate([jnp.broadcast_to(m, (grp, LANES)) for m in mids], axis=0)
            elif grp == 4:
                r = row % 4
                ref = jnp.where(r == 0, pltpu.roll(b_c, GLA_C - 1, axis=0),
                                jnp.where(r == 1, b_c,
                                          jnp.where(r == 2, pltpu.roll(b_c, 1, axis=0), pltpu.roll(b_c, 2, axis=0))))
            else:
                ref = jnp.where(row % 2 == 1, pltpu.roll(b_c, 1, axis=0), b_c)
            return jnp.exp2(-jnp.abs(b_c - ref))

        saccs = [jnp.zeros((GLA_C, GLA_HEADS * GLA_C), jnp.float32) for _ in range(nch)]
        pending = []
        for lv in range(GLA_LEVELS + 1):
            raw = []
            for c in range(nch):
                r0 = c * GLA_C
                if lv == 0:
                    qt, kt = _bf(gq[r0:r0 + GLA_C]), _bf(gk[r0:r0 + GLA_C])
                else:
                    f = level_factor(c, lv)
                    qt, kt = _bf(gq[r0:r0 + GLA_C] * f), _bf(gk[r0:r0 + GLA_C] * f)
                kbd = jnp.concatenate([jnp.where(hgrp[h], kt, zero_bf) for h in range(GLA_HEADS)], axis=0)
                raw.append(_dot_nt(qt, kbd))
            pending.append((lv, raw))
            if len(pending) == GLA_LEVELS_PER_STAGE or lv == GLA_LEVELS:
                yield
                for lv_p, raw_p in pending:
                    for c in range(nch):
                        saccs[c] = saccs[c] + raw_p[c] * lmask_ref[lv_p]
                pending = []
        if sb > 0:
            yield f"gla{sb - 1}.state"
        g_parts = []
        st = gstate_ref[...]
        for c in range(nch):
            r0 = c * GLA_C
            v_c = gv[r0:r0 + GLA_C]
            vbd = jnp.concatenate([jnp.where(vhead[h], v_c, zero_bf) for h in range(GLA_HEADS)], axis=0)
            b_c = bscr[r0:r0 + GLA_C, :]
            b_last = bscr[r0 + GLA_C - 1:r0 + GLA_C, :]
            q_in = _bf(gq[r0:r0 + GLA_C] * jnp.exp2(b_c))
            o_c = _dot(_bf(saccs[c]), vbd) + _dot_nt(q_in, _bf(st))
            k_out = _bf(gk[r0:r0 + GLA_C] * jnp.exp2(b_last - b_c))
            upd = _dot_tn(v_c, k_out)
            yield
            g_parts.append(o_c)
            st = st * jnp.exp2(b_last) + upd * smask_ref[...]
        gstate_ref[...] = st
        done.add(f"gla{sb}.state")
        g = jnp.concatenate(g_parts, axis=0)
        ms = _seg_mean(g * g, e64_ref)
        yield
        gate_g = _silu(pr[:, O_GG:O_GG + 256])
        mix_refs[sb][:, ATTN_W + RET_W:] = _bf(g * lax.rsqrt(ms + EPS) * gng_ref[...] * gate_g)
        done.add(f"gla{sb}.done")

    def gen_out(g):
        for sb in range(g * PGRP, (g + 1) * PGRP):
            yield f"attn{sb}.done"
            yield f"ret{sb}.done"
            yield f"gla{sb}.done"
        rws = slice(g * PGRP * BLK, (g + 1) * PGRP * BLK)
        half = D_MODEL // 2
        y0 = _dot(mix_grp[g][...], wout_ref[:, :half])
        yield
        y1 = _dot(mix_grp[g][...], wout_ref[:, half:])
        yield
        ssq = jnp.sum(y0 * y0, axis=-1, keepdims=True) + jnp.sum(y1 * y1, axis=-1, keepdims=True)
        inv = lax.rsqrt(ssq * (1.0 / D_MODEL) + EPS)
        gp = mod_ref[2:3, :] * post_ref[...]
        o_ref[0, rws, :half] = x_ref[0, rws, :half] + (y0 * inv) * gp[:, :half]
        yield
        o_ref[0, rws, half:] = x_ref[0, rws, half:] + (y1 * inv) * gp[:, half:]

    gens = []
    for g in range(NGRP):
        for sb in range(g * PGRP, (g + 1) * PGRP):
            gens += [gen_gla(sb), gen_attn(sb), gen_ret(sb)]
        gens += [gen_out(g), gen_proj(g)] + [gen_norm(sb) for sb in range(g * PGRP, (g + 1) * PGRP)]
    _run_interleaved(gens, done)


def _const_spec(shape):
    nd = len(shape)
    return pl.BlockSpec(shape, lambda b, t, s, _n=nd: (0,) * _n)


def _layer(l, x, mod, pre_gain, post_gain, w_in_b, w_out_b, sinks, gate_w_p, gate_b, gng, tables, consts):
    bsz, seq, d = x.shape
    tb = NSUB * BLK
    tok_spec = pl.BlockSpec((1, tb, LANES), lambda b, t, s: (b, t, 0))
    stacked = [pre_gain, post_gain, w_in_b, w_out_b, gate_w_p, gate_b, gng]

    def layer_spec(a):
        return pl.BlockSpec((None,) + a.shape[1:], lambda b, t, s, _n=a.ndim - 1: (l,) + (0,) * _n)

    in_specs = ([pl.BlockSpec((1, tb, d), lambda b, t, s: (b, t, 0)),
                 pl.BlockSpec((None, None, 3, d), lambda b, t, s: (l, b, 0, 0))]
                + [layer_spec(a) for a in stacked]
                + [tok_spec] * 4
                + [_const_spec(a.shape) for a in consts])
    scratch = ([pltpu.VMEM((PGRP * BLK, d), jnp.bfloat16)] * NGRP
               + [pltpu.VMEM((PGRP * BLK, N_PROJ), jnp.float32)] * NGRP
               + [pltpu.VMEM((PGRP * BLK, d), jnp.bfloat16)] * NGRP
               + [pltpu.VMEM((2, BLK, LANES), jnp.bfloat16),
                  pltpu.VMEM((2, BLK, LANES), jnp.bfloat16),
                  pltpu.VMEM((2, LANES, LANES), jnp.float32),
                  pltpu.VMEM((GLA_W, LANES), jnp.float32),
                  pltpu.VMEM((NSUB, BLK, LANES), jnp.float32)])
    grid_spec = pltpu.PrefetchScalarGridSpec(
        num_scalar_prefetch=1,
        grid=(bsz, seq // tb),
        in_specs=in_specs,
        out_specs=pl.BlockSpec((1, tb, d), lambda b, t, s: (b, t, 0)),
        scratch_shapes=scratch)
    return pl.pallas_call(
        _layer_kernel,
        out_shape=jax.ShapeDtypeStruct(x.shape, x.dtype),
        grid_spec=grid_spec,
        compiler_params=pltpu.CompilerParams(dimension_semantics=("arbitrary", "arbitrary"),
                                             vmem_limit_bytes=VMEM_LIMIT),
        name="hybrid_layer",
    )(sinks, x, mod, *stacked, *tables, *consts)


def kernel(x, c, positions, w_mod, b_mod, pre_norm_gain, post_norm_gain, w_in, attn_sinks, gla_gate_w,
           gla_gate_b, gla_norm_gain, w_out):
    depth = w_mod.shape[0]
    bsz = x.shape[0]
    f32, bf16 = jnp.float32, jnp.bfloat16

    mod = _modulation(c, w_mod, b_mod).reshape(depth, bsz, 3, D_MODEL)
    rope_freq = ROPE_THETA ** (-jnp.arange(0, HEAD_DIM, 2, dtype=f32) / HEAD_DIM)
    ret_freq = 1.0 / (10000.0 ** jnp.linspace(0.0, 1.0, RET_DK // 2, dtype=f32))
    freq = jnp.concatenate([rope_freq, ret_freq, rope_freq, ret_freq]).reshape(1, LANES)
    tables = _trig_tables(positions, freq)

    consts = [jnp.asarray(_C["bias"]), jnp.asarray(_C["vones"], bf16), jnp.asarray(_C["dmask"]),
              jnp.asarray(_C["qdec"]), jnp.asarray(_C["kdec"]), jnp.asarray(_C["cdm"]), jnp.asarray(_C["rbm"]),
              jnp.asarray(_C["tri"], bf16), jnp.asarray(_C["lmask"]), jnp.asarray(_C["smask"]),
              jnp.asarray(_C["e64"], bf16)]
    w_in_b = w_in.astype(bf16)
    w_out_b = w_out.astype(bf16)
    gate_w_p = jnp.pad(gla_gate_w, ((0, 0), (0, LANES - GLA_GATE_RANK), (0, 0))).astype(bf16)
    gng = jnp.tile(gla_norm_gain, (1, GLA_HEADS)).reshape(depth, 1, GLA_W)
    pre = pre_norm_gain.reshape(depth, 1, D_MODEL)
    post = post_norm_gain.reshape(depth, 1, D_MODEL)
    gate_b = gla_gate_b.reshape(depth, 1, LANES)

    for l in range(depth):
        x = _layer(l, x, mod, pre, post, w_in_b, w_out_b, attn_sinks[l], gate_w_p, gate_b, gng, tables, consts)
    return x
```

```python
import numpy as np
import jax
import jax.numpy as jnp
from jax import lax
from jax.experimental import pallas as pl
from jax.experimental.pallas import tpu as pltpu

D_MODEL = 1024
HEAD_DIM = 64
ATTN_HEADS = 8
ATTN_KV_HEADS = 2
WINDOW = 128
ROPE_THETA = 10000.0
RET_HEADS = 4
RET_DK = 64
RET_DV = 64
GLA_HEADS = 4
GLA_DK = 32
GLA_DV = 64
GLA_GATE_RANK = 16
GLA_GATE_NORMALIZER = 16.0
EPS = 1e-6

ATTN_W = ATTN_HEADS * HEAD_DIM
RET_W = RET_HEADS * RET_DV
GLA_W = GLA_HEADS * GLA_DV

LANES = 128
BLK = 128
NSUB = 4
PGRP = 2
NGRP = NSUB // PGRP
GLA_C = 64
GLA_LEVELS = 6
GLA_LEVELS_PER_STAGE = 4
NEG = -1e30
VMEM_LIMIT = 56 * 1024 * 1024
LOG2E = 1.4426950408889634

O_AQ, O_AK, O_AV, O_AG = 0, 512, 640, 768
O_RQ, O_RK, O_RV, O_RG = 1280, 1536, 1792, 2048
O_GQ, O_GK, O_GV, O_GG, O_GA = 2304, 2432, 2560, 2816, 3072
D_IN = O_GA + GLA_GATE_RANK
N_PROJ = 3200
PROJ_SEGMENTS = (("gga", O_GG, D_IN), ("gqkv", O_GQ, O_GG), ("akv", O_AK, O_AG), ("aq", O_AQ, O_AK),
                 ("rqk", O_RQ, O_RV), ("rvg", O_RV, O_GQ), ("ag", O_AG, O_RQ))


def _build_constants():
    f32 = np.float32
    lane = np.arange(LANES)
    grp = lane // 64

    i = np.arange(BLK)[:, None]
    j = np.arange(2 * BLK)[None, :]
    rel = i + BLK - j
    ok = (rel >= 0) & (rel < WINDOW)
    bias = np.stack([np.where(ok & (j >= BLK), 0.0, NEG), np.where(ok, 0.0, NEG)]).astype(f32)
    vones = np.zeros((2 * 2 * BLK, LANES), f32)
    for g in range(2):
        vones[g * 2 * BLK:(g + 1) * 2 * BLK] = (lane // 64 == g)[None, :]

    kscale = RET_DK ** -0.5
    log_g = np.log(1.0 - 2.0 ** (-5.0 - np.arange(RET_HEADS, dtype=np.float64)))
    idx = np.arange(BLK, dtype=np.float64)
    diff = idx[:, None] - idx[None, :]
    dmask = np.zeros((2, BLK, 2 * BLK), f32)
    qdec = np.zeros((2, BLK, LANES), f32)
    kdec = np.zeros((2, BLK, LANES), f32)
    cdm = np.zeros((2, LANES, LANES), f32)
    rbm = ((np.arange(LANES)[:, None] // 64) == (np.arange(LANES)[None, :] // 64)).astype(f32)
    for p in range(2):
        for hh in range(2):
            lg = log_g[2 * p + hh]
            dmask[p, :, hh * BLK:(hh + 1) * BLK] = kscale * np.where(diff >= 0, np.exp(lg * np.maximum(diff, 0.0)), 0.0)
        lg_lane = log_g[2 * p + grp]
        qdec[p] = np.exp(lg_lane[None, :] * (idx[:, None] + 1.0))
        kdec[p] = kscale * np.exp(lg_lane[None, :] * (BLK - 1.0 - idx[:, None]))
        cdm[p] = np.exp(lg_lane * BLK)[:, None] * rbm

    t = np.arange(BLK)
    tri = (((t[:, None] // GLA_C) == (t[None, :] // GLA_C)) & (t[None, :] <= t[:, None])).astype(f32)
    ci = np.arange(GLA_C)[:, None]
    cj = np.arange(GLA_C)[None, :]
    x = ci ^ cj
    lvl = np.where(cj > ci, -1, np.where(x == 0, 0, np.floor(np.log2(np.maximum(x, 1))).astype(np.int64) + 1))
    lmask = np.stack([np.tile((lvl == lv).astype(f32), (1, GLA_HEADS)) for lv in range(GLA_LEVELS + 1)])
    smask = ((np.arange(GLA_W)[:, None] // GLA_DV) == (np.arange(LANES)[None, :] // GLA_DK)).astype(f32)
    e64 = ((np.arange(256)[:, None] // 64) == (np.arange(256)[None, :] // 64)).astype(f32) / 64.0

    return dict(bias=bias, vones=vones, dmask=dmask, qdec=qdec, kdec=kdec, cdm=cdm, rbm=rbm, tri=tri,
                lmask=lmask, smask=smask, e64=e64)


_C = _build_constants()


def _dot(a, b):
    return jnp.dot(a, b, preferred_element_type=jnp.float32)


def _dot_nt(a, b):
    return lax.dot_general(a, b, (((1,), (1,)), ((), ())), preferred_element_type=jnp.float32)


def _dot_tn(a, b):
    return lax.dot_general(a, b, (((0,), (0,)), ((), ())), preferred_element_type=jnp.float32)


def _bf(x):
    return x.astype(jnp.bfloat16)


def _split_hi_lo(x):
    hi = _bf(x)
    lo = _bf(x - hi.astype(jnp.float32))
    return hi, lo


def _silu(x):
    half = 0.5 * x
    return half + half * jnp.tanh(half)


def _run_interleaved(gens, done):
    active = [[g, None] for g in gens]
    while active:
        progressed = False
        for item in list(active):
            if item[1] is not None and item[1] not in done:
                continue
            progressed = True
            try:
                need = next(item[0])
                while need is not None and need in done:
                    need = next(item[0])
                item[1] = need
            except StopIteration:
                active.remove(item)
        assert progressed, [item[1] for item in active]


MOD_TK = 256


def _mod_kernel(c_ref, w_ref, b_ref, o_ref):
    @pl.when(pl.program_id(1) == 0)
    def _():
        o_ref[0] = jnp.broadcast_to(b_ref[0], o_ref.shape[1:])

    a = _silu(c_ref[...])
    a_hi, a_lo = _split_hi_lo(a)
    w_hi, w_lo = _split_hi_lo(w_ref[0])
    n = a.shape[0]
    both = _dot(jnp.concatenate([a_hi, a_lo], axis=0), w_hi)
    o_ref[0] += both[:n] + both[n:] + _dot(a_hi, w_lo)


def _modulation(c, w_mod, b_mod):
    depth, d, n = w_mod.shape
    bsz = c.shape[0]
    return pl.pallas_call(
        _mod_kernel,
        out_shape=jax.ShapeDtypeStruct((depth, bsz, n), jnp.float32),
        grid=(depth, d // MOD_TK),
        in_specs=[pl.BlockSpec((bsz, MOD_TK), lambda l, k: (0, k)),
                  pl.BlockSpec((1, MOD_TK, n), lambda l, k: (l, k, 0)),
                  pl.BlockSpec((1, 1, n), lambda l, k: (l, 0, 0))],
        out_specs=pl.BlockSpec((1, bsz, n), lambda l, k: (l, 0, 0)),
        compiler_params=pltpu.CompilerParams(dimension_semantics=("arbitrary", "arbitrary"),
                                             vmem_limit_bytes=VMEM_LIMIT),
        name="adaln_mod",
    )(c, w_mod, b_mod.reshape(depth, 1, n))


TRIG_T = 512


def _trig_kernel(posa_ref, posb_ref, freq_ref, ca_ref, sa_ref, cr_ref, sr_ref):
    half = TRIG_T // 2
    lane = lax.broadcasted_iota(jnp.int32, (half, LANES), 1)
    pos = jnp.where(lane < 64, posa_ref[0], posb_ref[0]).astype(jnp.float32)
    ang = pos * freq_ref[...]
    c = jnp.cos(ang)
    s = jnp.sin(ang)
    quarter = lane // 32
    sign = jnp.where((lane % 64) < 32, -1.0, 1.0)
    rc = [c] + [pltpu.roll(c, k, axis=1) for k in (32, 64, 96)]
    rs = [s] + [pltpu.roll(s, k, axis=1) for k in (32, 64, 96)]

    def spread(r, src):
        out = r[(0 - src) % 4]
        for q in (1, 2, 3):
            out = jnp.where(quarter == q, r[(q - src) % 4], out)
        return out

    for tok, rws in ((0, slice(0, half)), (1, slice(half, TRIG_T))):
        ca_ref[0, rws, :] = spread(rc, 2 * tok)
        sa_ref[0, rws, :] = spread(rs, 2 * tok) * sign
        cr_ref[0, rws, :] = spread(rc, 2 * tok + 1)
        sr_ref[0, rws, :] = spread(rs, 2 * tok + 1) * sign


def _trig_tables(positions, freq):
    bsz, seq = positions.shape
    half = TRIG_T // 2
    out = jax.ShapeDtypeStruct((bsz, seq, LANES), jnp.float32)
    spec = pl.BlockSpec((1, TRIG_T, LANES), lambda b, t: (b, t, 0))
    pos = positions.reshape(bsz, seq, 1)
    return pl.pallas_call(
        _trig_kernel,
        out_shape=(out, out, out, out),
        grid=(bsz, seq // TRIG_T),
        in_specs=[pl.BlockSpec((1, half, 1), lambda b, t: (b, 2 * t, 0)),
                  pl.BlockSpec((1, half, 1), lambda b, t: (b, 2 * t + 1, 0)),
                  pl.BlockSpec((1, LANES), lambda b, t: (0, 0))],
        out_specs=(spec, spec, spec, spec),
        compiler_params=pltpu.CompilerParams(dimension_semantics=("arbitrary", "arbitrary"),
                                             vmem_limit_bytes=VMEM_LIMIT),
        name="rotary_tables",
    )(pos, pos, freq)


def _seg_mean(x, e_ref):
    return _dot(_bf(x), e_ref[...])


def _rot(v, c, s, first_half):
    partner = jnp.where(first_half, pltpu.roll(v, 96, axis=1), pltpu.roll(v, 32, axis=1))
    return v * c + partner * s


def _layer_kernel(sinks_ref,
                  x_ref, mod_ref, pre_ref, post_ref, win_ref, wout_ref, gw_ref, gb_ref, gng_ref,
                  ca_ref, sa_ref, cr_ref, sr_ref,
                  bias_ref, vones_ref, dmask_ref, qdec_ref, kdec_ref, cdm_ref, rbm_ref,
                  tri_ref, lmask_ref, smask_ref, e64_ref,
                  o_ref, *scratch):
    hb_grp = scratch[0:NGRP]
    proj_grp = scratch[NGRP:2 * NGRP]
    mix_grp = scratch[2 * NGRP:3 * NGRP]
    kprev_ref, vprev_ref, rstate_ref, gstate_ref, bscr_ref = scratch[3 * NGRP:]
    t = pl.program_id(1)

    def sub_view(refs, sb):
        return refs[sb // PGRP].at[pl.ds((sb % PGRP) * BLK, BLK)]

    hb_refs = [sub_view(hb_grp, sb) for sb in range(NSUB)]
    proj_refs = [sub_view(proj_grp, sb) for sb in range(NSUB)]
    mix_refs = [sub_view(mix_grp, sb) for sb in range(NSUB)]

    @pl.when(t == 0)
    def _():
        kprev_ref[...] = jnp.zeros_like(kprev_ref)
        vprev_ref[...] = jnp.zeros_like(vprev_ref)
        rstate_ref[...] = jnp.zeros_like(rstate_ref)
        gstate_ref[...] = jnp.zeros_like(gstate_ref)
        for g in range(NGRP):
            proj_grp[g][:, O_GA:] = jnp.zeros((PGRP * BLK, N_PROJ - O_GA), jnp.float32)

    lane = lax.broadcasted_iota(jnp.int32, (1, LANES), 1)
    lane2 = lax.broadcasted_iota(jnp.int32, (1, GLA_W), 1)
    zero_bf = jnp.zeros((), jnp.bfloat16)
    first_half = (lane % 64) < 32
    lo_hi = [lane // 64 == g for g in range(2)]
    hgrp = [lane // GLA_DK == h for h in range(GLA_HEADS)]
    vhead = [lane2 // GLA_DV == h for h in range(GLA_HEADS)]

    def rows(sb):
        return slice(sb * BLK, (sb + 1) * BLK)

    done = set()

    def gen_norm(sb):
        if sb >= PGRP:
            yield f"projgrp{sb // PGRP - 1}.started"
        x = x_ref[0, rows(sb), :]
        inv = lax.rsqrt(jnp.mean(x * x, axis=-1, keepdims=True) + EPS)
        hb_refs[sb][...] = _bf((x * inv) * (pre_ref[...] * (1.0 + mod_ref[1:2, :])) + mod_ref[0:1, :])
        done.add(f"norm{sb}")

    def gen_proj(g):
        subs = range(g * PGRP, (g + 1) * PGRP)
        for sb in subs:
            yield f"norm{sb}"
        if g > 0:
            yield f"projgrp{g - 1}.all"
        done.add(f"projgrp{g}.started")
        for name, a, b in PROJ_SEGMENTS:
            proj_grp[g][:, a:b] = _dot(hb_grp[g][...], win_ref[:, a:b])
            done.update(f"proj{sb}.{name}" for sb in subs)
            yield
        done.add(f"projgrp{g}.all")

    def gen_attn(sb):
        yield f"proj{sb}.akv"
        if sb > 0:
            yield f"attn{sb - 1}.kv"
        pr = proj_refs[sb]
        ca, sa = ca_ref[0, rows(sb), :], sa_ref[0, rows(sb), :]
        k_new = _rot(pr[:, O_AK:O_AK + 128], ca, sa, first_half) * (HEAD_DIM ** -0.5 * LOG2E)
        v_new = pr[:, O_AV:O_AV + 128]
        k_sw = pltpu.roll(k_new, 64, axis=1)
        v_sw = pltpu.roll(v_new, 64, axis=1)
        k2, v2 = [], []
        for g in range(2):
            k_g = _bf(jnp.where(lo_hi[g], k_new, k_sw))
            v_g = _bf(jnp.where(lo_hi[g], v_new, v_sw))
            kcat = jnp.concatenate([kprev_ref[g], k_g], axis=0)
            vcat = jnp.concatenate([vprev_ref[g], v_g], axis=0)
            kprev_ref[g] = k_g
            vprev_ref[g] = v_g
            k2.append(jnp.concatenate([jnp.where(lo_hi[h], kcat, zero_bf) for h in range(2)], axis=0))
            v2_g = jnp.concatenate([jnp.where(lo_hi[h], vcat, zero_bf) for h in range(2)], axis=0)
            v2.append(jnp.concatenate([v2_g, vones_ref[...]], axis=1))
        done.add(f"attn{sb}.kv")
        if sb == 0:
            bias = bias_ref[jnp.minimum(t, 1)]
        else:
            bias = bias_ref[1]

        def scores(g):
            q = [_rot(pr[:, O_AQ + m * 128:O_AQ + (m + 1) * 128], ca, sa, first_half) for m in (2 * g, 2 * g + 1)]
            return _dot_nt(_bf(jnp.concatenate(q, axis=0)), k2[g])

        yield f"proj{sb}.aq"
        s_g = scores(0)
        yield
        for g in range(2):
            ps, mxs = [], []
            for j in range(2):
                row_p, row_mx = [], []
                for h in range(2):
                    s = s_g[j * BLK:(j + 1) * BLK, h * 2 * BLK:(h + 1) * 2 * BLK] + bias
                    mx = jnp.maximum(jnp.max(s, axis=-1, keepdims=True), sinks_ref[2 * (2 * g + j) + h] * LOG2E)
                    row_p.append(_bf(jnp.exp2(s - mx)))
                    row_mx.append(mx)
                ps.append(jnp.concatenate(row_p, axis=1))
                mxs.append(row_mx)
            pv = _dot(jnp.concatenate(ps, axis=0), v2[g])
            if g == 0:
                s_g = scores(1)
            yield
            yield f"proj{sb}.ag"
            for j in range(2):
                m = 2 * g + j
                pv_m = pv[j * BLK:(j + 1) * BLK]
                mx_l = jnp.where(lo_hi[0], mxs[j][0], mxs[j][1])
                sink_l = jnp.where(lo_hi[0], sinks_ref[2 * m], sinks_ref[2 * m + 1]) * LOG2E
                den = pv_m[:, 128:] + jnp.exp2(sink_l - mx_l)
                gate_m = _silu(pr[:, O_AG + m * 128:O_AG + (m + 1) * 128])
                mix_refs[sb][:, m * 128:(m + 1) * 128] = _bf(pv_m[:, :128] / den * gate_m)
        done.add(f"attn{sb}.done")

    def gen_ret(sb):
        yield f"proj{sb}.rqk"
        yield f"proj{sb}.rvg"
        if sb > 0:
            yield f"ret{sb - 1}.state"
        pr = proj_refs[sb]
        cr, sr = cr_ref[0, rows(sb), :], sr_ref[0, rows(sb), :]
        held = []
        for p in range(2):
            q_p = _rot(pr[:, O_RQ + p * 128:O_RQ + (p + 1) * 128], cr, sr, first_half)
            k_p = _rot(pr[:, O_RK + p * 128:O_RK + (p + 1) * 128], cr, sr, first_half)
            v_p = _bf(pr[:, O_RV + p * 128:O_RV + (p + 1) * 128])
            k_pb = _bf(k_p)
            kbd = jnp.concatenate([jnp.where(lo_hi[h], k_pb, zero_bf) for h in range(2)], axis=0)
            vbd = jnp.concatenate([jnp.where(lo_hi[h], v_p, zero_bf) for h in range(2)], axis=0)
            sc_raw = _dot_nt(_bf(q_p), kbd)
            st = rstate_ref[p]
            inter = _dot(_bf(q_p * qdec_ref[p]), _bf(st))
            upd = _dot_tn(_bf(k_p * kdec_ref[p]), v_p)
            held.append((sc_raw, vbd, st, inter, upd))
        yield
        intras = []
        for p in range(2):
            sc_raw, vbd, st, inter, upd = held[p]
            intras.append(_dot(_bf(sc_raw * dmask_ref[p]), vbd) + inter)
            rstate_ref[p] = st * cdm_ref[p] + upd * rbm_ref[...]
        done.add(f"ret{sb}.state")
        yield
        r = jnp.concatenate(intras, axis=1)
        ms = _seg_mean(r * r, e64_ref)
        yield
        mix_refs[sb][:, ATTN_W:ATTN_W + RET_W] = _bf(r * lax.rsqrt(ms + EPS) * _silu(pr[:, O_RG:O_RG + 256]))
        done.add(f"ret{sb}.done")

    def gen_gla(sb):
        yield f"proj{sb}.gga"
        pr = proj_refs[sb]
        logits = _dot(_bf(pr[:, O_GA:O_GA + 128]), gw_ref[...])
        yield
        logits = logits + gb_ref[...]
        log_a = (jnp.minimum(logits, 0.0) - jnp.log(1.0 + jnp.exp(-jnp.abs(logits)))) * (LOG2E / GLA_GATE_NORMALIZER)
        la_hi, la_lo = _split_hi_lo(log_a)
        xr = _dot(tri_ref[...], jnp.concatenate([la_hi, la_lo], axis=1))
        yield
        yield f"proj{sb}.gqkv"
        bscr = bscr_ref.at[sb]
        bscr[...] = xr[:, :128] + xr[:, 128:]
        gq = pr[:, O_GQ:O_GQ + 128] * (GLA_DK ** -0.5)
        gk = pr[:, O_GK:O_GK + 128]
        gv = _bf(pr[:, O_GV:O_GV + 256])
        nch = BLK // GLA_C
        row = lax.broadcasted_iota(jnp.int32, (GLA_C, LANES), 0)

        def level_factor(c, lv):
            r0 = c * GLA_C
            b_c = bscr[r0:r0 + GLA_C, :]
            grp = 1 << lv
            if grp >= 8:
                mids = [bscr[r0 + g0 + grp // 2 - 1:r0 + g0 + grp // 2, :] for g0 in range(0, GLA_C, grp)]
                ref = jnp.concatenate([jnp.broadcast_to(m, (grp, LANES)) for m in mids], axis=0)
            elif grp == 4:
                r = row % 4
                ref = jnp.where(r == 0, pltpu.roll(b_c, GLA_C - 1, axis=0),
                                jnp.where(r == 1, b_c,
                                          jnp.where(r == 2, pltpu.roll(b_c, 1, axis=0), pltpu.roll(b_c, 2, axis=0))))
            else:
                ref = jnp.where(row % 2 == 1, pltpu.roll(b_c, 1, axis=0), b_c)
            return jnp.exp2(-jnp.abs(b_c - ref))

        saccs = [jnp.zeros((GLA_C, GLA_HEADS * GLA_C), jnp.float32) for _ in range(nch)]
        pending = []
        for lv in range(GLA_LEVELS + 1):
            raw = []
            for c in range(nch):
                r0 = c * GLA_C
                if lv == 0:
                    qt, kt = _bf(gq[r0:r0 + GLA_C]), _bf(gk[r0:r0 + GLA_C])
                else:
                    f = level_factor(c, lv)
                    qt, kt = _bf(gq[r0:r0 + GLA_C] * f), _bf(gk[r0:r0 + GLA_C] * f)
                kbd = jnp.concatenate([jnp.where(hgrp[h], kt, zero_bf) for h in range(GLA_HEADS)], axis=0)
                raw.append(_dot_nt(qt, kbd))
            pending.append((lv, raw))
            if len(pending) == GLA_LEVELS_PER_STAGE or lv == GLA_LEVELS:
                yield
                for lv_p, raw_p in pending:
                    for c in range(nch):
                        saccs[c] = saccs[c] + raw_p[c] * lmask_ref[lv_p]
                pending = []
        if sb > 0:
            yield f"gla{sb - 1}.state"
        g_parts = []
        st = gstate_ref[...]
        for c in range(nch):
            r0 = c * GLA_C
            v_c = gv[r0:r0 + GLA_C]
            vbd = jnp.concatenate([jnp.where(vhead[h], v_c, zero_bf) for h in range(GLA_HEADS)], axis=0)
            b_c = bscr[r0:r0 + GLA_C, :]
            b_last = bscr[r0 + GLA_C - 1:r0 + GLA_C, :]
            q_in = _bf(gq[r0:r0 + GLA_C] * jnp.exp2(b_c))
            o_c = _dot(_bf(saccs[c]), vbd) + _dot_nt(q_in, _bf(st))
            k_out = _bf(gk[r0:r0 + GLA_C] * jnp.exp2(b_last - b_c))
            upd = _dot_tn(v_c, k_out)
            yield
            g_parts.append(o_c)
            st = st * jnp.exp2(b_last) + upd * smask_ref[...]
        gstate_ref[...] = st
        done.add(f"gla{sb}.state")
        g = jnp.concatenate(g_parts, axis=0)
        ms = _seg_mean(g * g, e64_ref)
        yield
        gate_g = _silu(pr[:, O_GG:O_GG + 256])
        mix_refs[sb][:, ATTN_W + RET_W:] = _bf(g * lax.rsqrt(ms + EPS) * gng_ref[...] * gate_g)
        done.add(f"gla{sb}.done")

    def gen_out(g):
        for sb in range(g * PGRP, (g + 1) * PGRP):
            yield f"attn{sb}.done"
            yield f"ret{sb}.done"
            yield f"gla{sb}.done"
        rws = slice(g * PGRP * BLK, (g + 1) * PGRP * BLK)
        half = D_MODEL // 2
        y0 = _dot(mix_grp[g][...], wout_ref[:, :half])
        yield
        y1 = _dot(mix_grp[g][...], wout_ref[:, half:])
        yield
        ssq = jnp.sum(y0 * y0, axis=-1, keepdims=True) + jnp.sum(y1 * y1, axis=-1, keepdims=True)
        inv = lax.rsqrt(ssq * (1.0 / D_MODEL) + EPS)
        gp = mod_ref[2:3, :] * post_ref[...]
        o_ref[0, rws, :half] = x_ref[0, rws, :half] + (y0 * inv) * gp[:, :half]
        yield
        o_ref[0, rws, half:] = x_ref[0, rws, half:] + (y1 * inv) * gp[:, half:]

    gens = []
    for g in range(NGRP):
        for sb in range(g * PGRP, (g + 1) * PGRP):
            gens += [gen_gla(sb), gen_attn(sb), gen_ret(sb)]
        gens += [gen_out(g), gen_proj(g)] + [gen_norm(sb) for sb in range(g * PGRP, (g + 1) * PGRP)]
    _run_interleaved(gens, done)


def _const_spec(shape):
    nd = len(shape)
    return pl.BlockSpec(shape, lambda b, t, s, _n=nd: (0,) * _n)


def _layer(l, x, mod, pre_gain, post_gain, w_in_b, w_out_b, sinks, gate_w_p, gate_b, gng, tables, consts):
    bsz, seq, d = x.shape
    tb = NSUB * BLK
    tok_spec = pl.BlockSpec((1, tb, LANES), lambda b, t, s: (b, t, 0))
    stacked = [pre_gain, post_gain, w_in_b, w_out_b, gate_w_p, gate_b, gng]

    def layer_spec(a):
        return pl.BlockSpec((None,) + a.shape[1:], lambda b, t, s, _n=a.ndim - 1: (l,) + (0,) * _n)

    in_specs = ([pl.BlockSpec((1, tb, d), lambda b, t, s: (b, t, 0)),
                 pl.BlockSpec((None, None, 3, d), lambda b, t, s: (l, b, 0, 0))]
                + [layer_spec(a) for a in stacked]
                + [tok_spec] * 4
                + [_const_spec(a.shape) for a in consts])
    scratch = ([pltpu.VMEM((PGRP * BLK, d), jnp.bfloat16)] * NGRP
               + [pltpu.VMEM((PGRP * BLK, N_PROJ), jnp.float32)] * NGRP
               + [pltpu.VMEM((PGRP * BLK, d), jnp.bfloat16)] * NGRP
               + [pltpu.VMEM((2, BLK, LANES), jnp.bfloat16),
                  pltpu.VMEM((2, BLK, LANES), jnp.bfloat16),
                  pltpu.VMEM((2, LANES, LANES), jnp.float32),
                  pltpu.VMEM((GLA_W, LANES), jnp.float32),
                  pltpu.VMEM((NSUB, BLK, LANES), jnp.float32)])
    grid_spec = pltpu.PrefetchScalarGridSpec(
        num_scalar_prefetch=1,
        grid=(bsz, seq // tb),
        in_specs=in_specs,
        out_specs=pl.BlockSpec((1, tb, d), lambda b, t, s: (b, t, 0)),
        scratch_shapes=scratch)
    return pl.pallas_call(
        _layer_kernel,
        out_shape=jax.ShapeDtypeStruct(x.shape, x.dtype),
        grid_spec=grid_spec,
        compiler_params=pltpu.CompilerParams(dimension_semantics=("arbitrary", "arbitrary"),
                                             vmem_limit_bytes=VMEM_LIMIT),
        name="hybrid_layer",
    )(sinks, x, mod, *stacked, *tables, *consts)


def kernel(x, c, positions, w_mod, b_mod, pre_norm_gain, post_norm_gain, w_in, attn_sinks, gla_gate_w,
           gla_gate_b, gla_norm_gain, w_out):
    depth = w_mod.shape[0]
    bsz = x.shape[0]
    f32, bf16 = jnp.float32, jnp.bfloat16

    mod = _modulation(c, w_mod, b_mod).reshape(depth, bsz, 3, D_MODEL)
    rope_freq = ROPE_THETA ** (-jnp.arange(0, HEAD_DIM, 2, dtype=f32) / HEAD_DIM)
    ret_freq = 1.0 / (10000.0 ** jnp.linspace(0.0, 1.0, RET_DK // 2, dtype=f32))
    freq = jnp.concatenate([rope_freq, ret_freq, rope_freq, ret_freq]).reshape(1, LANES)
    tables = _trig_tables(positions, freq)

    consts = [jnp.asarray(_C["bias"]), jnp.asarray(_C["vones"], bf16), jnp.asarray(_C["dmask"]),
              jnp.asarray(_C["qdec"]), jnp.asarray(_C["kdec"]), jnp.asarray(_C["cdm"]), jnp.asarray(_C["rbm"]),
              jnp.asarray(_C["tri"], bf16), jnp.asarray(_C["lmask"]), jnp.asarray(_C["smask"]),
              jnp.asarray(_C["e64"], bf16)]
    w_in_b = w_in.astype(bf16)
    w_out_b = w_out.astype(bf16)
    gate_w_p = jnp.pad(gla_gate_w, ((0, 0), (0, LANES - GLA_GATE_RANK), (0, 0))).astype(bf16)
    gng = jnp.tile(gla_norm_gain, (1, GLA_HEADS)).reshape(depth, 1, GLA_W)
    pre = pre_norm_gain.reshape(depth, 1, D_MODEL)
    post = post_norm_gain.reshape(depth, 1, D_MODEL)
    gate_b = gla_gate_b.reshape(depth, 1, LANES)

    for l in range(depth):
        x = _layer(l, x, mod, pre, post, w_in_b, w_out_b, attn_sinks[l], gate_w_p, gate_b, gng, tables, consts)
    return x
```

```python
import functools

import numpy as np
import jax
import jax.numpy as jnp
from jax import lax
from jax.experimental import pallas as pl
from jax.experimental.pallas import tpu as pltpu

D_MODEL = 1024
HEAD_DIM = 64
ATTN_HEADS = 8
ATTN_KV_HEADS = 2
WINDOW = 128
ROPE_THETA = 10000.0
RET_HEADS = 4
RET_DK = 64
RET_DV = 64
GLA_HEADS = 4
GLA_DK = 32
GLA_DV = 64
GLA_GATE_RANK = 16
GLA_GATE_NORMALIZER = 16.0
EPS = 1e-6

ATTN_W = ATTN_HEADS * HEAD_DIM
RET_W = RET_HEADS * RET_DV
GLA_W = GLA_HEADS * GLA_DV

LANES = 128
BLK = 128
NSUB = 4
PGRP = 2
NGRP = NSUB // PGRP
GLA_C = 64
GLA_LEVELS = 6
GLA_LEVELS_PER_STAGE = 7
NEG = -1e30
VMEM_LIMIT = 56 * 1024 * 1024
LOG2E = 1.4426950408889634

O_AQ, O_AK, O_AV, O_AG = 0, 512, 640, 768
O_RQ, O_RK, O_RV, O_RG = 1280, 1536, 1792, 2048
O_GQ, O_GK, O_GV, O_GG, O_GA = 2304, 2432, 2560, 2816, 3072
D_IN = O_GA + GLA_GATE_RANK
N_PROJ = 3200
PROJ_SEGMENTS = (("gga", O_GG, D_IN), ("gqkv", O_GQ, O_GG), ("akv", O_AK, O_AG), ("aq", O_AQ, O_AK),
                 ("rqk", O_RQ, O_RV), ("rvg", O_RV, O_GQ), ("ag", O_AG, O_RQ))


def _build_constants():
    f32 = np.float32
    lane = np.arange(LANES)
    grp = lane // 64

    i = np.arange(BLK)[:, None]
    j = np.arange(2 * BLK)[None, :]
    rel = i + BLK - j
    ok = (rel >= 0) & (rel < WINDOW)
    bias = np.stack([np.where(ok & (j >= BLK), 0.0, NEG), np.where(ok, 0.0, NEG)]).astype(f32)
    vones = np.zeros((2 * 2 * BLK, LANES), f32)
    for g in range(2):
        vones[g * 2 * BLK:(g + 1) * 2 * BLK] = (lane // 64 == g)[None, :]

    kscale = RET_DK ** -0.5
    log_g = np.log(1.0 - 2.0 ** (-5.0 - np.arange(RET_HEADS, dtype=np.float64)))
    idx = np.arange(BLK, dtype=np.float64)
    diff = idx[:, None] - idx[None, :]
    dmask = np.zeros((2, BLK, 2 * BLK), f32)
    qdec = np.zeros((2, BLK, LANES), f32)
    kdec = np.zeros((2, BLK, LANES), f32)
    cdm = np.zeros((2, LANES, LANES), f32)
    rbm = ((np.arange(LANES)[:, None] // 64) == (np.arange(LANES)[None, :] // 64)).astype(f32)
    for p in range(2):
        for hh in range(2):
            lg = log_g[2 * p + hh]
            dmask[p, :, hh * BLK:(hh + 1) * BLK] = kscale * np.where(diff >= 0, np.exp(lg * np.maximum(diff, 0.0)), 0.0)
        lg_lane = log_g[2 * p + grp]
        qdec[p] = np.exp(lg_lane[None, :] * (idx[:, None] + 1.0))
        kdec[p] = kscale * np.exp(lg_lane[None, :] * (BLK - 1.0 - idx[:, None]))
        cdm[p] = np.exp(lg_lane * BLK)[:, None] * rbm

    t = np.arange(BLK)
    tri = (((t[:, None] // GLA_C) == (t[None, :] // GLA_C)) & (t[None, :] <= t[:, None])).astype(f32)
    ci = np.arange(GLA_C)[:, None]
    cj = np.arange(GLA_C)[None, :]
    x = ci ^ cj
    lvl = np.where(cj > ci, -1, np.where(x == 0, 0, np.floor(np.log2(np.maximum(x, 1))).astype(np.int64) + 1))
    lmask = np.stack([np.tile((lvl == lv).astype(f32), (1, GLA_HEADS)) for lv in range(GLA_LEVELS + 1)])
    smask = ((np.arange(GLA_W)[:, None] // GLA_DV) == (np.arange(LANES)[None, :] // GLA_DK)).astype(f32)
    e64 = ((np.arange(256)[:, None] // 64) == (np.arange(256)[None, :] // 64)).astype(f32) / 64.0

    return dict(bias=bias, vones=vones, dmask=dmask, qdec=qdec, kdec=kdec, cdm=cdm, rbm=rbm, tri=tri,
                lmask=lmask, smask=smask, e64=e64)


_C = _build_constants()


def _dot(a, b):
    return jnp.dot(a, b, preferred_element_type=jnp.float32)


def _dot_nt(a, b):
    return lax.dot_general(a, b, (((1,), (1,)), ((), ())), preferred_element_type=jnp.float32)


def _dot_tn(a, b):
    return lax.dot_general(a, b, (((0,), (0,)), ((), ())), preferred_element_type=jnp.float32)


def _bf(x):
    return x.astype(jnp.bfloat16)


def _split_hi_lo(x):
    hi = _bf(x)
    lo = _bf(x - hi.astype(jnp.float32))
    return hi, lo


def _silu(x):
    half = 0.5 * x
    return half + half * jnp.tanh(half)


def _run_interleaved(gens, done):
    active = [[g, None] for g in gens]
    while active:
        progressed = False
        for item in list(active):
            if item[1] is not None and item[1] not in done:
                continue
            progressed = True
            try:
                need = next(item[0])
                while need is not None and need in done:
                    need = next(item[0])
                item[1] = need
            except StopIteration:
                active.remove(item)
        assert progressed, [item[1] for item in active]


MOD_TK = 256
TRIG_T = 512


def _mod_step(c_ref, w_ref, b_ref, o_ref, first):
    @pl.when(first)
    def _():
        o_ref[0] = jnp.broadcast_to(b_ref[0], o_ref.shape[1:])

    a = _silu(c_ref[...])
    a_hi, a_lo = _split_hi_lo(a)
    w_hi, w_lo = _split_hi_lo(w_ref[0])
    n = a.shape[0]
    both = _dot(jnp.concatenate([a_hi, a_lo], axis=0), w_hi)
    o_ref[0] += both[:n] + both[n:] + _dot(a_hi, w_lo)


def _trig_tile(pos_ref, freq_ref, out_refs, row0):
    ca_ref, sa_ref, cr_ref, sr_ref = out_refs
    half = TRIG_T // 2
    lane = lax.broadcasted_iota(jnp.int32, (half, LANES), 1)
    pos_a = pos_ref[row0:row0 + half, :]
    pos_b = pos_ref[row0 + half:row0 + TRIG_T, :]
    ang = jnp.where(lane < 64, pos_a, pos_b).astype(jnp.float32) * freq_ref[...]
    c = jnp.cos(ang)
    s = jnp.sin(ang)
    quarter = lane // 32
    sign = jnp.where((lane % 64) < 32, -1.0, 1.0)
    rc = [c] + [pltpu.roll(c, k, axis=1) for k in (32, 64, 96)]
    rs = [s] + [pltpu.roll(s, k, axis=1) for k in (32, 64, 96)]

    def spread(r, src):
        out = r[(0 - src) % 4]
        for q in (1, 2, 3):
            out = jnp.where(quarter == q, r[(q - src) % 4], out)
        return out

    for tok, rws in ((0, slice(row0, row0 + half)), (1, slice(row0 + half, row0 + TRIG_T))):
        ca_ref[rws, :] = spread(rc, 2 * tok)
        sa_ref[rws, :] = spread(rs, 2 * tok) * sign
        cr_ref[rws, :] = spread(rc, 2 * tok + 1)
        sr_ref[rws, :] = spread(rs, 2 * tok + 1) * sign


def _prologue_kernel(c_ref, w_ref, b_ref, pos_ref, freq_ref, mod_ref, *table_refs, k_tiles, tiles):
    _mod_step(c_ref, w_ref, b_ref, mod_ref, pl.program_id(0) % k_tiles == 0)
    for j in range(tiles):
        _trig_tile(pos_ref, freq_ref, table_refs, j * TRIG_T)


def _prologue(c, w_mod, b_mod, positions, freq):
    depth, d, n = w_mod.shape
    bsz, seq = positions.shape
    k_tiles = d // MOD_TK
    steps = depth * k_tiles
    tiles = (bsz * seq) // (TRIG_T * steps)
    rows = tiles * TRIG_T
    assert rows * steps == bsz * seq
    table = jax.ShapeDtypeStruct((bsz * seq, LANES), jnp.float32)
    tspec = pl.BlockSpec((rows, LANES), lambda i: (i, 0))
    outs = pl.pallas_call(
        functools.partial(_prologue_kernel, k_tiles=k_tiles, tiles=tiles),
        out_shape=(jax.ShapeDtypeStruct((depth, bsz, n), jnp.float32), table, table, table, table),
        grid=(steps,),
        in_specs=[pl.BlockSpec((bsz, MOD_TK), lambda i: (0, i % k_tiles)),
                  pl.BlockSpec((1, MOD_TK, n), lambda i: (i // k_tiles, i % k_tiles, 0)),
                  pl.BlockSpec((1, 1, n), lambda i: (i // k_tiles, 0, 0)),
                  pl.BlockSpec((rows, 1), lambda i: (i, 0)),
                  pl.BlockSpec((1, LANES), lambda i: (0, 0))],
        out_specs=(pl.BlockSpec((1, bsz, n), lambda i: (i // k_tiles, 0, 0)), tspec, tspec, tspec, tspec),
        compiler_params=pltpu.CompilerParams(dimension_semantics=("arbitrary",), vmem_limit_bytes=VMEM_LIMIT),
        name="prologue",
    )(c, w_mod, b_mod.reshape(depth, 1, n), positions.reshape(bsz * seq, 1), freq)
    return outs[0], tuple(t.reshape(bsz, seq, LANES) for t in outs[1:])


def _seg_mean(x, e_ref):
    return _dot(_bf(x), e_ref[...])


def _rot(v, c, s, first_half):
    partner = jnp.where(first_half, pltpu.roll(v, 96, axis=1), pltpu.roll(v, 32, axis=1))
    return v * c + partner * s


def _layer_kernel(sinks_ref,
                  x_ref, mod_ref, pre_ref, post_ref, win_ref, wout_ref, gw_ref, gb_ref, gng_ref,
                  ca_ref, sa_ref, cr_ref, sr_ref,
                  bias_ref, vones_ref, dmask_ref, qdec_ref, kdec_ref, cdm_ref, rbm_ref,
                  tri_ref, lmask_ref, smask_ref, e64_ref,
                  o_ref, *scratch):
    hb_grp = scratch[0:NGRP]
    proj_grp = scratch[NGRP:2 * NGRP]
    mix_grp = scratch[2 * NGRP:3 * NGRP]
    kprev_ref, vprev_ref, rstate_ref, gstate_ref, bscr_ref = scratch[3 * NGRP:]
    t = pl.program_id(1)

    def sub_view(refs, sb):
        return refs[sb // PGRP].at[pl.ds((sb % PGRP) * BLK, BLK)]

    hb_refs = [sub_view(hb_grp, sb) for sb in range(NSUB)]
    proj_refs = [sub_view(proj_grp, sb) for sb in range(NSUB)]
    mix_refs = [sub_view(mix_grp, sb) for sb in range(NSUB)]

    @pl.when(t == 0)
    def _():
        kprev_ref[...] = jnp.zeros_like(kprev_ref)
        vprev_ref[...] = jnp.zeros_like(vprev_ref)
        rstate_ref[...] = jnp.zeros_like(rstate_ref)
        gstate_ref[...] = jnp.zeros_like(gstate_ref)
        for g in range(NGRP):
            proj_grp[g][:, O_GA:] = jnp.zeros((PGRP * BLK, N_PROJ - O_GA), jnp.float32)

    lane = lax.broadcasted_iota(jnp.int32, (1, LANES), 1)
    lane2 = lax.broadcasted_iota(jnp.int32, (1, GLA_W), 1)
    zero_bf = jnp.zeros((), jnp.bfloat16)
    first_half = (lane % 64) < 32
    lo_hi = [lane // 64 == g for g in range(2)]
    hgrp = [lane // GLA_DK == h for h in range(GLA_HEADS)]
    vhead = [lane2 // GLA_DV == h for h in range(GLA_HEADS)]

    def rows(sb):
        return slice(sb * BLK, (sb + 1) * BLK)

    done = set()

    def gen_norm(sb):
        if sb >= PGRP:
            yield f"projgrp{sb // PGRP - 1}.started"
        x = x_ref[0, rows(sb), :]
        inv = lax.rsqrt(jnp.mean(x * x, axis=-1, keepdims=True) + EPS)
        hb_refs[sb][...] = _bf((x * inv) * (pre_ref[...] * (1.0 + mod_ref[1:2, :])) + mod_ref[0:1, :])
        done.add(f"norm{sb}")

    def gen_proj(g):
        subs = range(g * PGRP, (g + 1) * PGRP)
        for sb in subs:
            yield f"norm{sb}"
        if g > 0:
            yield f"projgrp{g - 1}.all"
        done.add(f"projgrp{g}.started")
        for name, a, b in PROJ_SEGMENTS:
            proj_grp[g][:, a:b] = _dot(hb_grp[g][...], win_ref[:, a:b])
            done.update(f"proj{sb}.{name}" for sb in subs)
            yield
        done.add(f"projgrp{g}.all")

    def gen_attn(sb):
        yield f"proj{sb}.akv"
        if sb > 0:
            yield f"attn{sb - 1}.kv"
        pr = proj_refs[sb]
        ca, sa = ca_ref[0, rows(sb), :], sa_ref[0, rows(sb), :]
        k_new = _rot(pr[:, O_AK:O_AK + 128], ca, sa, first_half) * (HEAD_DIM ** -0.5 * LOG2E)
        v_new = pr[:, O_AV:O_AV + 128]
        k_sw = pltpu.roll(k_new, 64, axis=1)
        v_sw = pltpu.roll(v_new, 64, axis=1)
        k2, v2 = [], []
        for g in range(2):
            k_g = _bf(jnp.where(lo_hi[g], k_new, k_sw))
            v_g = _bf(jnp.where(lo_hi[g], v_new, v_sw))
            kcat = jnp.concatenate([kprev_ref[g], k_g], axis=0)
            vcat = jnp.concatenate([vprev_ref[g], v_g], axis=0)
            kprev_ref[g] = k_g
            vprev_ref[g] = v_g
            k2.append(jnp.concatenate([jnp.where(lo_hi[h], kcat, zero_bf) for h in range(2)], axis=0))
            v2_g = jnp.concatenate([jnp.where(lo_hi[h], vcat, zero_bf) for h in range(2)], axis=0)
            v2.append(jnp.concatenate([v2_g, vones_ref[...]], axis=1))
        done.add(f"attn{sb}.kv")
        if sb == 0:
            bias = bias_ref[jnp.minimum(t, 1)]
        else:
            bias = bias_ref[1]

        def scores(g):
            q = [_rot(pr[:, O_AQ + m * 128:O_AQ + (m + 1) * 128], ca, sa, first_half) for m in (2 * g, 2 * g + 1)]
            return _dot_nt(_bf(jnp.concatenate(q, axis=0)), k2[g])

        yield f"proj{sb}.aq"
        s_g = scores(0)
        yield
        for g in range(2):
            ps, mxs = [], []
            for j in range(2):
                row_p, row_mx = [], []
                for h in range(2):
                    s = s_g[j * BLK:(j + 1) * BLK, h * 2 * BLK:(h + 1) * 2 * BLK] + bias
                    mx = jnp.maximum(jnp.max(s, axis=-1, keepdims=True), sinks_ref[2 * (2 * g + j) + h] * LOG2E)
                    row_p.append(_bf(jnp.exp2(s - mx)))
                    row_mx.append(mx)
                ps.append(jnp.concatenate(row_p, axis=1))
                mxs.append(row_mx)
            pv = _dot(jnp.concatenate(ps, axis=0), v2[g])
            if g == 0:
                s_g = scores(1)
            yield
            yield f"proj{sb}.ag"
            for j in range(2):
                m = 2 * g + j
                pv_m = pv[j * BLK:(j + 1) * BLK]
                mx_l = jnp.where(lo_hi[0], mxs[j][0], mxs[j][1])
                sink_l = jnp.where(lo_hi[0], sinks_ref[2 * m], sinks_ref[2 * m + 1]) * LOG2E
                den = pv_m[:, 128:] + jnp.exp2(sink_l - mx_l)
                gate_m = _silu(pr[:, O_AG + m * 128:O_AG + (m + 1) * 128])
                mix_refs[sb][:, m * 128:(m + 1) * 128] = _bf(pv_m[:, :128] / den * gate_m)
        done.add(f"attn{sb}.done")

    def gen_ret(sb):
        yield f"proj{sb}.rqk"
        yield f"proj{sb}.rvg"
        if sb > 0:
            yield f"ret{sb - 1}.state"
        pr = proj_refs[sb]
        cr, sr = cr_ref[0, rows(sb), :], sr_ref[0, rows(sb), :]
        held = []
        for p in range(2):
            q_p = _rot(pr[:, O_RQ + p * 128:O_RQ + (p + 1) * 128], cr, sr, first_half)
            k_p = _rot(pr[:, O_RK + p * 128:O_RK + (p + 1) * 128], cr, sr, first_half)
            v_p = _bf(pr[:, O_RV + p * 128:O_RV + (p + 1) * 128])
            k_pb = _bf(k_p)
            kbd = jnp.concatenate([jnp.where(lo_hi[h], k_pb, zero_bf) for h in range(2)], axis=0)
            vbd = jnp.concatenate([jnp.where(lo_hi[h], v_p, zero_bf) for h in range(2)], axis=0)
            sc_raw = _dot_nt(_bf(q_p), kbd)
            st = rstate_ref[p]
            inter = _dot(_bf(q_p * qdec_ref[p]), _bf(st))
            upd = _dot_tn(_bf(k_p * kdec_ref[p]), v_p)
            held.append((sc_raw, vbd, st, inter, upd))
        yield
        intras = []
        for p in range(2):
            sc_raw, vbd, st, inter, upd = held[p]
            intras.append(_dot(_bf(sc_raw * dmask_ref[p]), vbd) + inter)
            rstate_ref[p] = st * cdm_ref[p] + upd * rbm_ref[...]
        done.add(f"ret{sb}.state")
        yield
        r = jnp.concatenate(intras, axis=1)
        ms = _seg_mean(r * r, e64_ref)
        yield
        mix_refs[sb][:, ATTN_W:ATTN_W + RET_W] = _bf(r * lax.rsqrt(ms + EPS) * _silu(pr[:, O_RG:O_RG + 256]))
        done.add(f"ret{sb}.done")

    def gen_gla(sb):
        yield f"proj{sb}.gga"
        pr = proj_refs[sb]
        logits = _dot(_bf(pr[:, O_GA:O_GA + 128]), gw_ref[...])
        yield
        logits = logits + gb_ref[...]
        log_a = (jnp.minimum(logits, 0.0) - jnp.log(1.0 + jnp.exp(-jnp.abs(logits)))) * (LOG2E / GLA_GATE_NORMALIZER)
        la_hi, la_lo = _split_hi_lo(log_a)
        xr = _dot(tri_ref[...], jnp.concatenate([la_hi, la_lo], axis=1))
        yield
        yield f"proj{sb}.gqkv"
        bscr = bscr_ref.at[sb]
        bscr[...] = xr[:, :128] + xr[:, 128:]
        gq = pr[:, O_GQ:O_GQ + 128] * (GLA_DK ** -0.5)
        gk = pr[:, O_GK:O_GK + 128]
        gv = _bf(pr[:, O_GV:O_GV + 256])
        nch = BLK // GLA_C
        row = lax.broadcasted_iota(jnp.int32, (GLA_C, LANES), 0)

        def level_factor(c, lv):
            r0 = c * GLA_C
            b_c = bscr[r0:r0 + GLA_C, :]
            grp = 1 << lv
            if grp >= 8:
                mids = [bscr[r0 + g0 + grp // 2 - 1:r0 + g0 + grp // 2, :] for g0 in range(0, GLA_C, grp)]
                ref = jnp.concatenate([jnp.broadcast_to(m, (grp, LANES)) for m in mids], axis=0)
            elif grp == 4:
                r = row % 4
                ref = jnp.where(r == 0, pltpu.roll(b_c, GLA_C - 1, axis=0),
                                jnp.where(r == 1, b_c,
                                          jnp.where(r == 2, pltpu.roll(b_c, 1, axis=0), pltpu.roll(b_c, 2, axis=0))))
            else:
                ref = jnp.where(row % 2 == 1, pltpu.roll(b_c, 1, axis=0), b_c)
            return jnp.exp2(-jnp.abs(b_c - ref))

        saccs = [jnp.zeros((GLA_C, GLA_HEADS * GLA_C), jnp.float32) for _ in range(nch)]
        pending = []
        for lv in range(GLA_LEVELS + 1):
            raw = []
            for c in range(nch):
                r0 = c * GLA_C
                if lv == 0:
                    qt, kt = _bf(gq[r0:r0 + GLA_C]), _bf(gk[r0:r0 + GLA_C])
                else:
                    f = level_factor(c, lv)
                    qt, kt = _bf(gq[r0:r0 + GLA_C] * f), _bf(gk[r0:r0 + GLA_C] * f)
                kbd = jnp.concatenate([jnp.where(hgrp[h], kt, zero_bf) for h in range(GLA_HEADS)], axis=0)
                raw.append(_dot_nt(qt, kbd))
            pending.append((lv, raw))
            if len(pending) == GLA_LEVELS_PER_STAGE or lv == GLA_LEVELS:
                yield
                for lv_p, raw_p in pending:
                    for c in range(nch):
                        saccs[c] = saccs[c] + raw_p[c] * lmask_ref[lv_p]
                pending = []
        if sb > 0:
            yield f"gla{sb - 1}.state"
        g_parts = []
        st = gstate_ref[...]
        for c in range(nch):
            r0 = c * GLA_C
            v_c = gv[r0:r0 + GLA_C]
            vbd = jnp.concatenate([jnp.where(vhead[h], v_c, zero_bf) for h in range(GLA_HEADS)], axis=0)
            b_c = bscr[r0:r0 + GLA_C, :]
            b_last = bscr[r0 + GLA_C - 1:r0 + GLA_C, :]
            q_in = _bf(gq[r0:r0 + GLA_C] * jnp.exp2(b_c))
            o_c = _dot(_bf(saccs[c]), vbd) + _dot_nt(q_in, _bf(st))
            k_out = _bf(gk[r0:r0 + GLA_C] * jnp.exp2(b_last - b_c))
            upd = _dot_tn(v_c, k_out)
            yield
            g_parts.append(o_c)
            st = st * jnp.exp2(b_last) + upd * smask_ref[...]
        gstate_ref[...] = st
        done.add(f"gla{sb}.state")
        g = jnp.concatenate(g_parts, axis=0)
        ms = _seg_mean(g * g, e64_ref)
        yield
        gate_g = _silu(pr[:, O_GG:O_GG + 256])
        mix_refs[sb][:, ATTN_W + RET_W:] = _bf(g * lax.rsqrt(ms + EPS) * gng_ref[...] * gate_g)
        done.add(f"gla{sb}.done")

    def gen_out(g):
        for sb in range(g * PGRP, (g + 1) * PGRP):
            yield f"attn{sb}.done"
            yield f"ret{sb}.done"
            yield f"gla{sb}.done"
        rws = slice(g * PGRP * BLK, (g + 1) * PGRP * BLK)
        half = D_MODEL // 2
        y0 = _dot(mix_grp[g][...], wout_ref[:, :half])
        yield
        y1 = _dot(mix_grp[g][...], wout_ref[:, half:])
        yield
        ssq = jnp.sum(y0 * y0, axis=-1, keepdims=True) + jnp.sum(y1 * y1, axis=-1, keepdims=True)
        inv = lax.rsqrt(ssq * (1.0 / D_MODEL) + EPS)
        gp = mod_ref[2:3, :] * post_ref[...]
        o_ref[0, rws, :half] = x_ref[0, rws, :half] + (y0 * inv) * gp[:, :half]
        yield
        o_ref[0, rws, half:] = x_ref[0, rws, half:] + (y1 * inv) * gp[:, half:]

    gens = []
    for g in range(NGRP):
        for sb in range(g * PGRP, (g + 1) * PGRP):
            gens += [gen_gla(sb), gen_attn(sb), gen_ret(sb)]
        gens += [gen_out(g), gen_proj(g)] + [gen_norm(sb) for sb in range(g * PGRP, (g + 1) * PGRP)]
    _run_interleaved(gens, done)


def _const_spec(shape):
    nd = len(shape)
    return pl.BlockSpec(shape, lambda b, t, s, _n=nd: (0,) * _n)


def _layer(l, x, mod, pre_gain, post_gain, w_in_b, w_out_b, sinks, gate_w_p, gate_b, gng, tables, consts):
    bsz, seq, d = x.shape
    tb = NSUB * BLK
    tok_spec = pl.BlockSpec((1, tb, LANES), lambda b, t, s: (b, t, 0))
    stacked = [pre_gain, post_gain, w_in_b, w_out_b, gate_w_p, gate_b, gng]

    def layer_spec(a):
        return pl.BlockSpec((None,) + a.shape[1:], lambda b, t, s, _n=a.ndim - 1: (l,) + (0,) * _n)

    in_specs = ([pl.BlockSpec((1, tb, d), lambda b, t, s: (b, t, 0)),
                 pl.BlockSpec((None, None, 3, d), lambda b, t, s: (l, b, 0, 0))]
                + [layer_spec(a) for a in stacked]
                + [tok_spec] * 4
                + [_const_spec(a.shape) for a in consts])
    scratch = ([pltpu.VMEM((PGRP * BLK, d), jnp.bfloat16)] * NGRP
               + [pltpu.VMEM((PGRP * BLK, N_PROJ), jnp.float32)] * NGRP
               + [pltpu.VMEM((PGRP * BLK, d), jnp.bfloat16)] * NGRP
               + [pltpu.VMEM((2, BLK, LANES), jnp.bfloat16),
                  pltpu.VMEM((2, BLK, LANES), jnp.bfloat16),
                  pltpu.VMEM((2, LANES, LANES), jnp.float32),
                  pltpu.VMEM((GLA_W, LANES), jnp.float32),
                  pltpu.VMEM((NSUB, BLK, LANES), jnp.float32)])
    grid_spec = pltpu.PrefetchScalarGridSpec(
        num_scalar_prefetch=1,
        grid=(bsz, seq // tb),
        in_specs=in_specs,
        out_specs=pl.BlockSpec((1, tb, d), lambda b, t, s: (b, t, 0)),
        scratch_shapes=scratch)
    return pl.pallas_call(
        _layer_kernel,
        out_shape=jax.ShapeDtypeStruct(x.shape, x.dtype),
        grid_spec=grid_spec,
        compiler_params=pltpu.CompilerParams(dimension_semantics=("arbitrary", "arbitrary"),
                                             vmem_limit_bytes=VMEM_LIMIT),
        name="hybrid_layer",
    )(sinks, x, mod, *stacked, *tables, *consts)


def kernel(x, c, positions, w_mod, b_mod, pre_norm_gain, post_norm_gain, w_in, attn_sinks, gla_gate_w,
           gla_gate_b, gla_norm_gain, w_out):
    depth = w_mod.shape[0]
    bsz = x.shape[0]
    f32, bf16 = jnp.float32, jnp.bfloat16

    rope_freq = ROPE_THETA ** (-jnp.arange(0, HEAD_DIM, 2, dtype=f32) / HEAD_DIM)
    ret_freq = 1.0 / (10000.0 ** jnp.linspace(0.0, 1.0, RET_DK // 2, dtype=f32))
    freq = jnp.concatenate([rope_freq, ret_freq, rope_freq, ret_freq]).reshape(1, LANES)
    mod, tables = _prologue(c, w_mod, b_mod, positions, freq)
    mod = mod.reshape(depth, bsz, 3, D_MODEL)

    consts = [jnp.asarray(_C["bias"]), jnp.asarray(_C["vones"], bf16), jnp.asarray(_C["dmask"]),
              jnp.asarray(_C["qdec"]), jnp.asarray(_C["kdec"]), jnp.asarray(_C["cdm"]), jnp.asarray(_C["rbm"]),
              jnp.asarray(_C["tri"], bf16), jnp.asarray(_C["lmask"]), jnp.asarray(_C["smask"]),
              jnp.asarray(_C["e64"], bf16)]
    w_in_b = w_in.astype(bf16)
    w_out_b = w_out.astype(bf16)
    gate_w_p = jnp.pad(gla_gate_w, ((0, 0), (0, LANES - GLA_GATE_RANK), (0, 0))).astype(bf16)
    gng = jnp.tile(gla_norm_gain, (1, GLA_HEADS)).reshape(depth, 1, GLA_W)
    pre = pre_norm_gain.reshape(depth, 1, D_MODEL)
    post = post_norm_gain.reshape(depth, 1, D_MODEL)
    gate_b = gla_gate_b.reshape(depth, 1, LANES)

    for l in range(depth):
        x = _layer(l, x, mod, pre, post, w_in_b, w_out_b, attn_sinks[l], gate_w_p, gate_b, gng, tables, consts)
    return x
```

```python
import functools

import numpy as np
import jax
import jax.numpy as jnp
from jax import lax
from jax.experimental import pallas as pl
from jax.experimental.pallas import tpu as pltpu

D_MODEL = 1024
HEAD_DIM = 64
ATTN_HEADS = 8
ATTN_KV_HEADS = 2
WINDOW = 128
ROPE_THETA = 10000.0
RET_HEADS = 4
RET_DK = 64
RET_DV = 64
GLA_HEADS = 4
GLA_DK = 32
GLA_DV = 64
GLA_GATE_RANK = 16
GLA_GATE_NORMALIZER = 16.0
EPS = 1e-6

ATTN_W = ATTN_HEADS * HEAD_DIM
RET_W = RET_HEADS * RET_DV
GLA_W = GLA_HEADS * GLA_DV

LANES = 128
BLK = 128
NSUB = 4
PGRP = 2
NGRP = NSUB // PGRP
GLA_C = 64
GLA_LEVELS = 6
GLA_LEVELS_PER_STAGE = 7
NEG = -1e30
VMEM_LIMIT = 56 * 1024 * 1024
LOG2E = 1.4426950408889634

O_AQ, O_AK, O_AV, O_AG = 0, 512, 640, 768
O_RQ, O_RK, O_RV, O_RG = 1280, 1536, 1792, 2048
O_GQ, O_GK, O_GV, O_GG, O_GA = 2304, 2432, 2560, 2816, 3072
D_IN = O_GA + GLA_GATE_RANK
N_PROJ = 3200
PROJ_SEGMENTS = (("gga", O_GG, D_IN), ("gqkv", O_GQ, O_GG), ("akv", O_AK, O_AG), ("aq", O_AQ, O_AK),
                 ("rqk", O_RQ, O_RV), ("rvg", O_RV, O_GQ), ("ag", O_AG, O_RQ))


def _build_constants():
    f32 = np.float32
    lane = np.arange(LANES)
    grp = lane // 64

    i = np.arange(BLK)[:, None]
    j = np.arange(2 * BLK)[None, :]
    rel = i + BLK - j
    ok = (rel >= 0) & (rel < WINDOW)
    bias = np.stack([np.where(ok & (j >= BLK), 0.0, NEG), np.where(ok, 0.0, NEG)]).astype(f32)
    vones = np.zeros((2 * 2 * BLK, LANES), f32)
    for g in range(2):
        vones[g * 2 * BLK:(g + 1) * 2 * BLK] = (lane // 64 == g)[None, :]

    kscale = RET_DK ** -0.5
    log_g = np.log(1.0 - 2.0 ** (-5.0 - np.arange(RET_HEADS, dtype=np.float64)))
    idx = np.arange(BLK, dtype=np.float64)
    diff = idx[:, None] - idx[None, :]
    dmask = np.zeros((2, BLK, 2 * BLK), f32)
    qdec = np.zeros((2, BLK, LANES), f32)
    kdec = np.zeros((2, BLK, LANES), f32)
    cdm = np.zeros((2, LANES, LANES), f32)
    rbm = ((np.arange(LANES)[:, None] // 64) == (np.arange(LANES)[None, :] // 64)).astype(f32)
    for p in range(2):
        for hh in range(2):
            lg = log_g[2 * p + hh]
            dmask[p, :, hh * BLK:(hh + 1) * BLK] = kscale * np.where(diff >= 0, np.exp(lg * np.maximum(diff, 0.0)), 0.0)
        lg_lane = log_g[2 * p + grp]
        qdec[p] = np.exp(lg_lane[None, :] * (idx[:, None] + 1.0))
        kdec[p] = kscale * np.exp(lg_lane[None, :] * (BLK - 1.0 - idx[:, None]))
        cdm[p] = np.exp(lg_lane * BLK)[:, None] * rbm

    t = np.arange(BLK)
    tri = (((t[:, None] // GLA_C) == (t[None, :] // GLA_C)) & (t[None, :] <= t[:, None])).astype(f32)
    ci = np.arange(GLA_C)[:, None]
    cj = np.arange(GLA_C)[None, :]
    x = ci ^ cj
    lvl = np.where(cj > ci, -1, np.where(x == 0, 0, np.floor(np.log2(np.maximum(x, 1))).astype(np.int64) + 1))
    lmask = np.stack([np.tile((lvl == lv).astype(f32), (1, GLA_HEADS)) for lv in range(GLA_LEVELS + 1)])
    smask = ((np.arange(GLA_W)[:, None] // GLA_DV) == (np.arange(LANES)[None, :] // GLA_DK)).astype(f32)
    e64 = ((np.arange(256)[:, None] // 64) == (np.arange(256)[None, :] // 64)).astype(f32) / 64.0

    return dict(bias=bias, vones=vones, dmask=dmask, qdec=qdec, kdec=kdec, cdm=cdm, rbm=rbm, tri=tri,
                lmask=lmask, smask=smask, e64=e64)


_C = _build_constants()


def _dot(a, b):
    return jnp.dot(a, b, preferred_element_type=jnp.float32)


def _dot_nt(a, b):
    return lax.dot_general(a, b, (((1,), (1,)), ((), ())), preferred_element_type=jnp.float32)


def _dot_tn(a, b):
    return lax.dot_general(a, b, (((0,), (0,)), ((), ())), preferred_element_type=jnp.float32)


def _bf(x):
    return x.astype(jnp.bfloat16)


def _split_hi_lo(x):
    hi = _bf(x)
    lo = _bf(x - hi.astype(jnp.float32))
    return hi, lo


def _silu(x):
    half = 0.5 * x
    return half + half * jnp.tanh(half)


def _run_interleaved(gens, done):
    active = [[g, None] for g in gens]
    while active:
        progressed = False
        for item in list(active):
            if item[1] is not None and item[1] not in done:
                continue
            progressed = True
            try:
                need = next(item[0])
                while need is not None and need in done:
                    need = next(item[0])
                item[1] = need
            except StopIteration:
                active.remove(item)
        assert progressed, [item[1] for item in active]


MOD_TK = 256
TRIG_T = 512


def _mod_step(c_ref, w_ref, b_ref, o_ref, first):
    @pl.when(first)
    def _():
        o_ref[0] = jnp.broadcast_to(b_ref[0], o_ref.shape[1:])

    a = _silu(c_ref[...])
    a_hi, a_lo = _split_hi_lo(a)
    w_hi, w_lo = _split_hi_lo(w_ref[0])
    n = a.shape[0]
    both = _dot(jnp.concatenate([a_hi, a_lo], axis=0), w_hi)
    o_ref[0] += both[:n] + both[n:] + _dot(a_hi, w_lo)


def _trig_tile(pos_ref, freq_ref, out_refs, row0):
    ca_ref, sa_ref, cr_ref, sr_ref = out_refs
    half = TRIG_T // 2
    lane = lax.broadcasted_iota(jnp.int32, (half, LANES), 1)
    pos_a = pos_ref[row0:row0 + half, :]
    pos_b = pos_ref[row0 + half:row0 + TRIG_T, :]
    ang = jnp.where(lane < 64, pos_a, pos_b).astype(jnp.float32) * freq_ref[...]
    c = jnp.cos(ang)
    s = jnp.sin(ang)
    quarter = lane // 32
    sign = jnp.where((lane % 64) < 32, -1.0, 1.0)
    rc = [c] + [pltpu.roll(c, k, axis=1) for k in (32, 64, 96)]
    rs = [s] + [pltpu.roll(s, k, axis=1) for k in (32, 64, 96)]

    def spread(r, src):
        out = r[(0 - src) % 4]
        for q in (1, 2, 3):
            out = jnp.where(quarter == q, r[(q - src) % 4], out)
        return out

    for tok, rws in ((0, slice(row0, row0 + half)), (1, slice(row0 + half, row0 + TRIG_T))):
        ca_ref[rws, :] = spread(rc, 2 * tok)
        sa_ref[rws, :] = spread(rs, 2 * tok) * sign
        cr_ref[rws, :] = spread(rc, 2 * tok + 1)
        sr_ref[rws, :] = spread(rs, 2 * tok + 1) * sign


def _prologue_kernel(c_ref, w_ref, b_ref, pos_ref, freq_ref, wo_ref, mod_ref, wob_ref, *table_refs, k_tiles, tiles):
    _mod_step(c_ref, w_ref, b_ref, mod_ref, pl.program_id(0) % k_tiles == 0)
    wob_ref[...] = _bf(wo_ref[...])
    for j in range(tiles):
        _trig_tile(pos_ref, freq_ref, table_refs, j * TRIG_T)


def _prologue(c, w_mod, b_mod, positions, freq, w_out):
    depth, d, n = w_mod.shape
    bsz, seq = positions.shape
    k_tiles = d // MOD_TK
    steps = depth * k_tiles
    tiles = (bsz * seq) // (TRIG_T * steps)
    rows = tiles * TRIG_T
    assert rows * steps == bsz * seq
    table = jax.ShapeDtypeStruct((bsz * seq, LANES), jnp.float32)
    tspec = pl.BlockSpec((rows, LANES), lambda i: (i, 0))
    wo_spec = pl.BlockSpec((1, w_out.shape[1] // k_tiles, w_out.shape[2]), lambda i: (i // k_tiles, i % k_tiles, 0))
    outs = pl.pallas_call(
        functools.partial(_prologue_kernel, k_tiles=k_tiles, tiles=tiles),
        out_shape=(jax.ShapeDtypeStruct((depth, bsz, n), jnp.float32),
                   jax.ShapeDtypeStruct(w_out.shape, jnp.bfloat16), table, table, table, table),
        grid=(steps,),
        in_specs=[pl.BlockSpec((bsz, MOD_TK), lambda i: (0, i % k_tiles)),
                  pl.BlockSpec((1, MOD_TK, n), lambda i: (i // k_tiles, i % k_tiles, 0)),
                  pl.BlockSpec((1, 1, n), lambda i: (i // k_tiles, 0, 0)),
                  pl.BlockSpec((rows, 1), lambda i: (i, 0)),
                  pl.BlockSpec((1, LANES), lambda i: (0, 0)),
                  wo_spec],
        out_specs=(pl.BlockSpec((1, bsz, n), lambda i: (i // k_tiles, 0, 0)), wo_spec, tspec, tspec, tspec, tspec),
        compiler_params=pltpu.CompilerParams(dimension_semantics=("arbitrary",), vmem_limit_bytes=VMEM_LIMIT),
        name="prologue",
    )(c, w_mod, b_mod.reshape(depth, 1, n), positions.reshape(bsz * seq, 1), freq, w_out)
    return outs[0], outs[1], tuple(t.reshape(bsz, seq, LANES) for t in outs[2:])


def _seg_mean(x, e_ref):
    return _dot(_bf(x), e_ref[...])


def _rot(v, c, s, first_half):
    partner = jnp.where(first_half, pltpu.roll(v, 96, axis=1), pltpu.roll(v, 32, axis=1))
    return v * c + partner * s


def _layer_kernel(sinks_ref,
                  x_ref, mod_ref, pre_ref, post_ref, win_ref, wout_ref, gw_ref, gb_ref, gng_ref,
                  ca_ref, sa_ref, cr_ref, sr_ref,
                  bias_ref, vones_ref, dmask_ref, qdec_ref, kdec_ref, cdm_ref, rbm_ref,
                  tri_ref, lmask_ref, smask_ref, e64_ref,
                  o_ref, *scratch):
    hb_grp = scratch[0:NGRP]
    proj_grp = scratch[NGRP:2 * NGRP]
    mix_grp = scratch[2 * NGRP:3 * NGRP]
    kprev_ref, vprev_ref, rstate_ref, gstate_ref, bscr_ref = scratch[3 * NGRP:]
    t = pl.program_id(1)

    def sub_view(refs, sb):
        return refs[sb // PGRP].at[pl.ds((sb % PGRP) * BLK, BLK)]

    hb_refs = [sub_view(hb_grp, sb) for sb in range(NSUB)]
    proj_refs = [sub_view(proj_grp, sb) for sb in range(NSUB)]
    mix_refs = [sub_view(mix_grp, sb) for sb in range(NSUB)]

    @pl.when(t == 0)
    def _():
        kprev_ref[...] = jnp.zeros_like(kprev_ref)
        vprev_ref[...] = jnp.zeros_like(vprev_ref)
        rstate_ref[...] = jnp.zeros_like(rstate_ref)
        gstate_ref[...] = jnp.zeros_like(gstate_ref)
        for g in range(NGRP):
            proj_grp[g][:, O_GA:] = jnp.zeros((PGRP * BLK, N_PROJ - O_GA), jnp.float32)

    lane = lax.broadcasted_iota(jnp.int32, (1, LANES), 1)
    lane2 = lax.broadcasted_iota(jnp.int32, (1, GLA_W), 1)
    zero_bf = jnp.zeros((), jnp.bfloat16)
    first_half = (lane % 64) < 32
    lo_hi = [lane // 64 == g for g in range(2)]
    hgrp = [lane // GLA_DK == h for h in range(GLA_HEADS)]
    vhead = [lane2 // GLA_DV == h for h in range(GLA_HEADS)]

    def rows(sb):
        return slice(sb * BLK, (sb + 1) * BLK)

    done = set()

    def gen_norm(sb):
        if sb >= PGRP:
            yield f"projgrp{sb // PGRP - 1}.started"
        x = x_ref[0, rows(sb), :]
        inv = lax.rsqrt(jnp.mean(x * x, axis=-1, keepdims=True) + EPS)
        hb_refs[sb][...] = _bf((x * inv) * (pre_ref[...] * (1.0 + mod_ref[1:2, :])) + mod_ref[0:1, :])
        done.add(f"norm{sb}")

    def gen_proj(g):
        subs = range(g * PGRP, (g + 1) * PGRP)
        for sb in subs:
            yield f"norm{sb}"
        if g > 0:
            yield f"projgrp{g - 1}.all"
        done.add(f"projgrp{g}.started")
        for name, a, b in PROJ_SEGMENTS:
            proj_grp[g][:, a:b] = _dot(hb_grp[g][...], win_ref[:, a:b])
            done.update(f"proj{sb}.{name}" for sb in subs)
            yield
        done.add(f"projgrp{g}.all")

    def gen_attn(sb):
        yield f"proj{sb}.akv"
        if sb > 0:
            yield f"attn{sb - 1}.kv"
        pr = proj_refs[sb]
        ca, sa = ca_ref[0, rows(sb), :], sa_ref[0, rows(sb), :]
        k_new = _rot(pr[:, O_AK:O_AK + 128], ca, sa, first_half) * (HEAD_DIM ** -0.5 * LOG2E)
        v_new = pr[:, O_AV:O_AV + 128]
        k_sw = pltpu.roll(k_new, 64, axis=1)
        v_sw = pltpu.roll(v_new, 64, axis=1)
        k2, v2 = [], []
        for g in range(2):
            k_g = _bf(jnp.where(lo_hi[g], k_new, k_sw))
            v_g = _bf(jnp.where(lo_hi[g], v_new, v_sw))
            kcat = jnp.concatenate([kprev_ref[g], k_g], axis=0)
            vcat = jnp.concatenate([vprev_ref[g], v_g], axis=0)
            kprev_ref[g] = k_g
            vprev_ref[g] = v_g
            k2.append(jnp.concatenate([jnp.where(lo_hi[h], kcat, zero_bf) for h in range(2)], axis=0))
            v2_g = jnp.concatenate([jnp.where(lo_hi[h], vcat, zero_bf) for h in range(2)], axis=0)
            v2.append(jnp.concatenate([v2_g, vones_ref[...]], axis=1))
        done.add(f"attn{sb}.kv")
        if sb == 0:
            bias = bias_ref[jnp.minimum(t, 1)]
        else:
            bias = bias_ref[1]

        def scores(g):
            q = [_rot(pr[:, O_AQ + m * 128:O_AQ + (m + 1) * 128], ca, sa, first_half) for m in (2 * g, 2 * g + 1)]
            return _dot_nt(_bf(jnp.concatenate(q, axis=0)), k2[g])

        yield f"proj{sb}.aq"
        s_g = scores(0)
        yield
        for g in range(2):
            ps, mxs = [], []
            for j in range(2):
                row_p, row_mx = [], []
                for h in range(2):
                    s = s_g[j * BLK:(j + 1) * BLK, h * 2 * BLK:(h + 1) * 2 * BLK] + bias
                    mx = jnp.maximum(jnp.max(s, axis=-1, keepdims=True), sinks_ref[2 * (2 * g + j) + h] * LOG2E)
                    row_p.append(_bf(jnp.exp2(s - mx)))
                    row_mx.append(mx)
                ps.append(jnp.concatenate(row_p, axis=1))
                mxs.append(row_mx)
            pv = _dot(jnp.concatenate(ps, axis=0), v2[g])
            if g == 0:
                s_g = scores(1)
            yield
            yield f"proj{sb}.ag"
            for j in range(2):
                m = 2 * g + j
                pv_m = pv[j * BLK:(j + 1) * BLK]
                mx_l = jnp.where(lo_hi[0], mxs[j][0], mxs[j][1])
                sink_l = jnp.where(lo_hi[0], sinks_ref[2 * m], sinks_ref[2 * m + 1]) * LOG2E
                den = pv_m[:, 128:] + jnp.exp2(sink_l - mx_l)
                gate_m = _silu(pr[:, O_AG + m * 128:O_AG + (m + 1) * 128])
                mix_refs[sb][:, m * 128:(m + 1) * 128] = _bf(pv_m[:, :128] / den * gate_m)
        done.add(f"attn{sb}.done")

    def gen_ret(sb):
        yield f"proj{sb}.rqk"
        yield f"proj{sb}.rvg"
        if sb > 0:
            yield f"ret{sb - 1}.state"
        pr = proj_refs[sb]
        cr, sr = cr_ref[0, rows(sb), :], sr_ref[0, rows(sb), :]
        held = []
        for p in range(2):
            q_p = _rot(pr[:, O_RQ + p * 128:O_RQ + (p + 1) * 128], cr, sr, first_half)
            k_p = _rot(pr[:, O_RK + p * 128:O_RK + (p + 1) * 128], cr, sr, first_half)
            v_p = _bf(pr[:, O_RV + p * 128:O_RV + (p + 1) * 128])
            k_pb = _bf(k_p)
            kbd = jnp.concatenate([jnp.where(lo_hi[h], k_pb, zero_bf) for h in range(2)], axis=0)
            vbd = jnp.concatenate([jnp.where(lo_hi[h], v_p, zero_bf) for h in range(2)], axis=0)
            sc_raw = _dot_nt(_bf(q_p), kbd)
            st = rstate_ref[p]
            inter = _dot(_bf(q_p * qdec_ref[p]), _bf(st))
            upd = _dot_tn(_bf(k_p * kdec_ref[p]), v_p)
            held.append((sc_raw, vbd, st, inter, upd))
        yield
        intras = []
        for p in range(2):
            sc_raw, vbd, st, inter, upd = held[p]
            intras.append(_dot(_bf(sc_raw * dmask_ref[p]), vbd) + inter)
            rstate_ref[p] = st * cdm_ref[p] + upd * rbm_ref[...]
        done.add(f"ret{sb}.state")
        yield
        r = jnp.concatenate(intras, axis=1)
        ms = _seg_mean(r * r, e64_ref)
        yield
        mix_refs[sb][:, ATTN_W:ATTN_W + RET_W] = _bf(r * lax.rsqrt(ms + EPS) * _silu(pr[:, O_RG:O_RG + 256]))
        done.add(f"ret{sb}.done")

    def gen_gla(sb):
        yield f"proj{sb}.gga"
        pr = proj_refs[sb]
        logits = _dot(_bf(pr[:, O_GA:O_GA + 128]), gw_ref[...])
        yield
        logits = logits + gb_ref[...]
        log_a = (jnp.minimum(logits, 0.0) - jnp.log(1.0 + jnp.exp(-jnp.abs(logits)))) * (LOG2E / GLA_GATE_NORMALIZER)
        la_hi, la_lo = _split_hi_lo(log_a)
        xr = _dot(tri_ref[...], jnp.concatenate([la_hi, la_lo], axis=1))
        yield
        yield f"proj{sb}.gqkv"
        bscr = bscr_ref.at[sb]
        bscr[...] = xr[:, :128] + xr[:, 128:]
        gq = pr[:, O_GQ:O_GQ + 128] * (GLA_DK ** -0.5)
        gk = pr[:, O_GK:O_GK + 128]
        gv = _bf(pr[:, O_GV:O_GV + 256])
        nch = BLK // GLA_C
        row = lax.broadcasted_iota(jnp.int32, (GLA_C, LANES), 0)

        def level_factor(c, lv):
            r0 = c * GLA_C
            b_c = bscr[r0:r0 + GLA_C, :]
            grp = 1 << lv
            if grp >= 8:
                mids = [bscr[r0 + g0 + grp // 2 - 1:r0 + g0 + grp // 2, :] for g0 in range(0, GLA_C, grp)]
                ref = jnp.concatenate([jnp.broadcast_to(m, (grp, LANES)) for m in mids], axis=0)
            elif grp == 4:
                r = row % 4
                ref = jnp.where(r == 0, pltpu.roll(b_c, GLA_C - 1, axis=0),
                                jnp.where(r == 1, b_c,
                                          jnp.where(r == 2, pltpu.roll(b_c, 1, axis=0), pltpu.roll(b_c, 2, axis=0))))
            else:
                ref = jnp.where(row % 2 == 1, pltpu.roll(b_c, 1, axis=0), b_c)
            return jnp.exp2(-jnp.abs(b_c - ref))

        saccs = [jnp.zeros((GLA_C, GLA_HEADS * GLA_C), jnp.float32) for _ in range(nch)]
        pending = []
        for lv in range(GLA_LEVELS + 1):
            raw = []
            for c in range(nch):
                r0 = c * GLA_C
                if lv == 0:
                    qt, kt = _bf(gq[r0:r0 + GLA_C]), _bf(gk[r0:r0 + GLA_C])
                else:
                    f = level_factor(c, lv)
                    qt, kt = _bf(gq[r0:r0 + GLA_C] * f), _bf(gk[r0:r0 + GLA_C] * f)
                kbd = jnp.concatenate([jnp.where(hgrp[h], kt, zero_bf) for h in range(GLA_HEADS)], axis=0)
                raw.append(_dot_nt(qt, kbd))
            pending.append((lv, raw))
            if len(pending) == GLA_LEVELS_PER_STAGE or lv == GLA_LEVELS:
                yield
                for lv_p, raw_p in pending:
                    for c in range(nch):
                        saccs[c] = saccs[c] + raw_p[c] * lmask_ref[lv_p]
                pending = []
        if sb > 0:
            yield f"gla{sb - 1}.state"
        g_parts = []
        st = gstate_ref[...]
        for c in range(nch):
            r0 = c * GLA_C
            v_c = gv[r0:r0 + GLA_C]
            vbd = jnp.concatenate([jnp.where(vhead[h], v_c, zero_bf) for h in range(GLA_HEADS)], axis=0)
            b_c = bscr[r0:r0 + GLA_C, :]
            b_last = bscr[r0 + GLA_C - 1:r0 + GLA_C, :]
            q_in = _bf(gq[r0:r0 + GLA_C] * jnp.exp2(b_c))
            o_c = _dot(_bf(saccs[c]), vbd) + _dot_nt(q_in, _bf(st))
            k_out = _bf(gk[r0:r0 + GLA_C] * jnp.exp2(b_last - b_c))
            upd = _dot_tn(v_c, k_out)
            yield
            g_parts.append(o_c)
            st = st * jnp.exp2(b_last) + upd * smask_ref[...]
        gstate_ref[...] = st
        done.add(f"gla{sb}.state")
        g = jnp.concatenate(g_parts, axis=0)
        ms = _seg_mean(g * g, e64_ref)
        yield
        gate_g = _silu(pr[:, O_GG:O_GG + 256])
        mix_refs[sb][:, ATTN_W + RET_W:] = _bf(g * lax.rsqrt(ms + EPS) * gng_ref[...] * gate_g)
        done.add(f"gla{sb}.done")

    def gen_out(g):
        for sb in range(g * PGRP, (g + 1) * PGRP):
            yield f"attn{sb}.done"
            yield f"ret{sb}.done"
            yield f"gla{sb}.done"
        rws = slice(g * PGRP * BLK, (g + 1) * PGRP * BLK)
        half = D_MODEL // 2
        y0 = _dot(mix_grp[g][...], wout_ref[:, :half])
        yield
        y1 = _dot(mix_grp[g][...], wout_ref[:, half:])
        yield
        ssq = jnp.sum(y0 * y0, axis=-1, keepdims=True) + jnp.sum(y1 * y1, axis=-1, keepdims=True)
        inv = lax.rsqrt(ssq * (1.0 / D_MODEL) + EPS)
        gp = mod_ref[2:3, :] * post_ref[...]
        o_ref[0, rws, :half] = x_ref[0, rws, :half] + (y0 * inv) * gp[:, :half]
        yield
        o_ref[0, rws, half:] = x_ref[0, rws, half:] + (y1 * inv) * gp[:, half:]

    gens = []
    for g in range(NGRP):
        for sb in range(g * PGRP, (g + 1) * PGRP):
            gens += [gen_gla(sb), gen_attn(sb), gen_ret(sb)]
        gens += [gen_out(g), gen_proj(g)] + [gen_norm(sb) for sb in range(g * PGRP, (g + 1) * PGRP)]
    _run_interleaved(gens, done)


def _const_spec(shape):
    nd = len(shape)
    return pl.BlockSpec(shape, lambda b, t, s, _n=nd: (0,) * _n)


def _layer(l, x, mod, pre_gain, post_gain, w_in_b, w_out_b, sinks, gate_w_p, gate_b, gng, tables, consts):
    bsz, seq, d = x.shape
    tb = NSUB * BLK
    tok_spec = pl.BlockSpec((1, tb, LANES), lambda b, t, s: (b, t, 0))
    stacked = [pre_gain, post_gain, w_in_b, w_out_b, gate_w_p, gate_b, gng]

    def layer_spec(a):
        return pl.BlockSpec((None,) + a.shape[1:], lambda b, t, s, _n=a.ndim - 1: (l,) + (0,) * _n)

    in_specs = ([pl.BlockSpec((1, tb, d), lambda b, t, s: (b, t, 0)),
                 pl.BlockSpec((None, None, 3, d), lambda b, t, s: (l, b, 0, 0))]
                + [layer_spec(a) for a in stacked]
                + [tok_spec] * 4
                + [_const_spec(a.shape) for a in consts])
    scratch = ([pltpu.VMEM((PGRP * BLK, d), jnp.bfloat16)] * NGRP
               + [pltpu.VMEM((PGRP * BLK, N_PROJ), jnp.float32)] * NGRP
               + [pltpu.VMEM((PGRP * BLK, d), jnp.bfloat16)] * NGRP
               + [pltpu.VMEM((2, BLK, LANES), jnp.bfloat16),
                  pltpu.VMEM((2, BLK, LANES), jnp.bfloat16),
                  pltpu.VMEM((2, LANES, LANES), jnp.float32),
                  pltpu.VMEM((GLA_W, LANES), jnp.float32),
                  pltpu.VMEM((NSUB, BLK, LANES), jnp.float32)])
    grid_spec = pltpu.PrefetchScalarGridSpec(
        num_scalar_prefetch=1,
        grid=(bsz, seq // tb),
        in_specs=in_specs,
        out_specs=pl.BlockSpec((1, tb, d), lambda b, t, s: (b, t, 0)),
        scratch_shapes=scratch)
    return pl.pallas_call(
        _layer_kernel,
        out_shape=jax.ShapeDtypeStruct(x.shape, x.dtype),
        grid_spec=grid_spec,
        compiler_params=pltpu.CompilerParams(dimension_semantics=("arbitrary", "arbitrary"),
                                             vmem_limit_bytes=VMEM_LIMIT),
        name="hybrid_layer",
    )(sinks, x, mod, *stacked, *tables, *consts)


def kernel(x, c, positions, w_mod, b_mod, pre_norm_gain, post_norm_gain, w_in, attn_sinks, gla_gate_w,
           gla_gate_b, gla_norm_gain, w_out):
    depth = w_mod.shape[0]
    bsz = x.shape[0]
    f32, bf16 = jnp.float32, jnp.bfloat16

    rope_freq = ROPE_THETA ** (-jnp.arange(0, HEAD_DIM, 2, dtype=f32) / HEAD_DIM)
    ret_freq = 1.0 / (10000.0 ** jnp.linspace(0.0, 1.0, RET_DK // 2, dtype=f32))
    freq = jnp.concatenate([rope_freq, ret_freq, rope_freq, ret_freq]).reshape(1, LANES)
    mod, w_out_b, tables = _prologue(c, w_mod, b_mod, positions, freq, w_out)
    mod = mod.reshape(depth, bsz, 3, D_MODEL)

    consts = [jnp.asarray(_C["bias"]), jnp.asarray(_C["vones"], bf16), jnp.asarray(_C["dmask"]),
              jnp.asarray(_C["qdec"]), jnp.asarray(_C["kdec"]), jnp.asarray(_C["cdm"]), jnp.asarray(_C["rbm"]),
              jnp.asarray(_C["tri"], bf16), jnp.asarray(_C["lmask"]), jnp.asarray(_C["smask"]),
              jnp.asarray(_C["e64"], bf16)]
    w_in_b = w_in.astype(bf16)
    gate_w_p = jnp.pad(gla_gate_w, ((0, 0), (0, LANES - GLA_GATE_RANK), (0, 0))).astype(bf16)
    gng = jnp.tile(gla_norm_gain, (1, GLA_HEADS)).reshape(depth, 1, GLA_W)
    pre = pre_norm_gain.reshape(depth, 1, D_MODEL)
    post = post_norm_gain.reshape(depth, 1, D_MODEL)
    gate_b = gla_gate_b.reshape(depth, 1, LANES)

    for l in range(depth):
        x = _layer(l, x, mod, pre, post, w_in_b, w_out_b, attn_sinks[l], gate_w_p, gate_b, gng, tables, consts)
    return x
```

```python
import functools

import numpy as np
import jax
import jax.numpy as jnp
from jax import lax
from jax.experimental import pallas as pl
from jax.experimental.pallas import tpu as pltpu

D_MODEL = 1024
HEAD_DIM = 64
ATTN_HEADS = 8
ATTN_KV_HEADS = 2
WINDOW = 128
ROPE_THETA = 10000.0
RET_HEADS = 4
RET_DK = 64
RET_DV = 64
GLA_HEADS = 4
GLA_DK = 32
GLA_DV = 64
GLA_GATE_RANK = 16
GLA_GATE_NORMALIZER = 16.0
EPS = 1e-6

ATTN_W = ATTN_HEADS * HEAD_DIM
RET_W = RET_HEADS * RET_DV
GLA_W = GLA_HEADS * GLA_DV

LANES = 128
HALF = HEAD_DIM // 2
BLK = 128
NSUB = 4
PGRP = 2
NGRP = NSUB // PGRP
GLA_C = 64
GLA_LEVELS = 6
GLA_LEVELS_PER_STAGE = 7
NEG = -1e30
VMEM_LIMIT = 56 * 1024 * 1024
LOG2E = 1.4426950408889634

O_AQ, O_AK, O_AV, O_AG = 0, 512, 640, 768
O_RQ, O_RK, O_RV, O_RG = 1280, 1536, 1792, 2048
O_GQ, O_GK, O_GV, O_GG, O_GA = 2304, 2432, 2560, 2816, 3072
D_IN = O_GA + GLA_GATE_RANK
N_PROJ = 3200
PROJ_SEGMENTS = (("gga", O_GG, D_IN), ("gqkv", O_GQ, O_GG), ("akv", O_AK, O_AG), ("aq", O_AQ, O_AK),
                 ("rqk", O_RQ, O_RV), ("rvg", O_RV, O_GQ), ("ag", O_AG, O_RQ))


def _build_constants():
    f32 = np.float32
    lane = np.arange(LANES)
    grp = lane // HEAD_DIM

    i = np.arange(BLK)[:, None]
    j = np.arange(2 * BLK)[None, :]
    rel = i + BLK - j
    ok = (rel >= 0) & (rel < WINDOW)
    bias = np.stack([np.where(ok & (j >= BLK), 0.0, NEG), np.where(ok, 0.0, NEG)]).astype(f32)
    vones = np.zeros((2 * 2 * BLK, LANES), f32)
    for g in range(2):
        vones[g * 2 * BLK:(g + 1) * 2 * BLK] = (lane // HEAD_DIM == g)[None, :]

    kscale = RET_DK ** -0.5
    log_g = np.log(1.0 - 2.0 ** (-5.0 - np.arange(RET_HEADS, dtype=np.float64)))
    idx = np.arange(BLK, dtype=np.float64)
    diff = idx[:, None] - idx[None, :]
    dmask = np.zeros((2, BLK, 2 * BLK), f32)
    qdec = np.zeros((2, BLK, LANES), f32)
    kdec = np.zeros((2, BLK, LANES), f32)
    cdm = np.zeros((2, LANES, LANES), f32)
    rbm = ((np.arange(LANES)[:, None] // RET_DK) == (np.arange(LANES)[None, :] // RET_DV)).astype(f32)
    for p in range(2):
        for hh in range(2):
            lg = log_g[2 * p + hh]
            dmask[p, :, hh * BLK:(hh + 1) * BLK] = kscale * np.where(diff >= 0, np.exp(lg * np.maximum(diff, 0.0)), 0.0)
        lg_lane = log_g[2 * p + grp]
        qdec[p] = np.exp(lg_lane[None, :] * (idx[:, None] + 1.0))
        kdec[p] = kscale * np.exp(lg_lane[None, :] * (BLK - 1.0 - idx[:, None]))
        cdm[p] = np.exp(lg_lane * BLK)[:, None] * rbm

    t = np.arange(BLK)
    tri = (((t[:, None] // GLA_C) == (t[None, :] // GLA_C)) & (t[None, :] <= t[:, None])).astype(f32)
    ci = np.arange(GLA_C)[:, None]
    cj = np.arange(GLA_C)[None, :]
    x = ci ^ cj
    lvl = np.where(cj > ci, -1, np.where(x == 0, 0, np.floor(np.log2(np.maximum(x, 1))).astype(np.int64) + 1))
    lmask = np.stack([np.tile((lvl == lv).astype(f32), (1, GLA_HEADS)) for lv in range(GLA_LEVELS + 1)])
    smask = ((np.arange(GLA_W)[:, None] // GLA_DV) == (np.arange(LANES)[None, :] // GLA_DK)).astype(f32)
    e64 = ((np.arange(RET_W)[:, None] // RET_DV) == (np.arange(RET_W)[None, :] // RET_DV)).astype(f32) / RET_DV

    return dict(bias=bias, vones=vones, dmask=dmask, qdec=qdec, kdec=kdec, cdm=cdm, rbm=rbm, tri=tri,
                lmask=lmask, smask=smask, e64=e64)


_C = _build_constants()


def _dot(a, b):
    return jnp.dot(a, b, preferred_element_type=jnp.float32)


def _dot_nt(a, b):
    return lax.dot_general(a, b, (((1,), (1,)), ((), ())), preferred_element_type=jnp.float32)


def _dot_tn(a, b):
    return lax.dot_general(a, b, (((0,), (0,)), ((), ())), preferred_element_type=jnp.float32)


def _bf(x):
    return x.astype(jnp.bfloat16)


def _split_hi_lo(x):
    hi = _bf(x)
    lo = _bf(x - hi.astype(jnp.float32))
    return hi, lo


def _silu(x):
    half = 0.5 * x
    return half + half * jnp.tanh(half)


def _run_interleaved(gens, done):
    active = [[g, None] for g in gens]
    while active:
        progressed = False
        for item in list(active):
            if item[1] is not None and item[1] not in done:
                continue
            progressed = True
            try:
                need = next(item[0])
                while need is not None and need in done:
                    need = next(item[0])
                item[1] = need
            except StopIteration:
                active.remove(item)
        assert progressed, [item[1] for item in active]


MOD_TK = 256
TRIG_T = 512


def _mod_step(c_ref, w_ref, b_ref, o_ref, first):
    @pl.when(first)
    def _():
        o_ref[0] = jnp.broadcast_to(b_ref[0], o_ref.shape[1:])

    a = _silu(c_ref[...])
    a_hi, a_lo = _split_hi_lo(a)
    w_hi, w_lo = _split_hi_lo(w_ref[0])
    n = a.shape[0]
    both = _dot(jnp.concatenate([a_hi, a_lo], axis=0), w_hi)
    o_ref[0] += both[:n] + both[n:] + _dot(a_hi, w_lo)


def _trig_tile(pos_ref, freq_ref, out_refs, row0):
    ca_ref, sa_ref, cr_ref, sr_ref = out_refs
    half = TRIG_T // 2
    lane = lax.broadcasted_iota(jnp.int32, (half, LANES), 1)
    pos_a = pos_ref[row0:row0 + half, :]
    pos_b = pos_ref[row0 + half:row0 + TRIG_T, :]
    ang = jnp.where(lane < 2 * HALF, pos_a, pos_b).astype(jnp.float32) * freq_ref[...]
    c = jnp.cos(ang)
    s = jnp.sin(ang)
    quarter = lane // HALF
    sign = jnp.where((lane % HEAD_DIM) < HALF, -1.0, 1.0)
    rc = [c] + [pltpu.roll(c, k * HALF, axis=1) for k in (1, 2, 3)]
    rs = [s] + [pltpu.roll(s, k * HALF, axis=1) for k in (1, 2, 3)]

    def spread(r, src):
        out = r[(0 - src) % 4]
        for q in (1, 2, 3):
            out = jnp.where(quarter == q, r[(q - src) % 4], out)
        return out

    for tok, rws in ((0, slice(row0, row0 + half)), (1, slice(row0 + half, row0 + TRIG_T))):
        ca_ref[rws, :] = spread(rc, 2 * tok)
        sa_ref[rws, :] = spread(rs, 2 * tok) * sign
        cr_ref[rws, :] = spread(rc, 2 * tok + 1)
        sr_ref[rws, :] = spread(rs, 2 * tok + 1) * sign


def _prologue_kernel(c_ref, w_ref, b_ref, pos_ref, freq_ref, wo_ref, mod_ref, wob_ref, *table_refs, k_tiles, tiles):
    _mod_step(c_ref, w_ref, b_ref, mod_ref, pl.program_id(0) % k_tiles == 0)
    wob_ref[...] = _bf(wo_ref[...])
    for j in range(tiles):
        _trig_tile(pos_ref, freq_ref, table_refs, j * TRIG_T)


def _prologue(c, w_mod, b_mod, positions, freq, w_out):
    depth, d, n = w_mod.shape
    bsz, seq = positions.shape
    k_tiles = d // MOD_TK
    steps = depth * k_tiles
    tiles = (bsz * seq) // (TRIG_T * steps)
    rows = tiles * TRIG_T
    assert rows * steps == bsz * seq
    table = jax.ShapeDtypeStruct((bsz * seq, LANES), jnp.float32)
    tspec = pl.BlockSpec((rows, LANES), lambda i: (i, 0))
    wo_spec = pl.BlockSpec((1, w_out.shape[1] // k_tiles, w_out.shape[2]), lambda i: (i // k_tiles, i % k_tiles, 0))
    outs = pl.pallas_call(
        functools.partial(_prologue_kernel, k_tiles=k_tiles, tiles=tiles),
        out_shape=(jax.ShapeDtypeStruct((depth, bsz, n), jnp.float32),
                   jax.ShapeDtypeStruct(w_out.shape, jnp.bfloat16), table, table, table, table),
        grid=(steps,),
        in_specs=[pl.BlockSpec((bsz, MOD_TK), lambda i: (0, i % k_tiles)),
                  pl.BlockSpec((1, MOD_TK, n), lambda i: (i // k_tiles, i % k_tiles, 0)),
                  pl.BlockSpec((1, 1, n), lambda i: (i // k_tiles, 0, 0)),
                  pl.BlockSpec((rows, 1), lambda i: (i, 0)),
                  pl.BlockSpec((1, LANES), lambda i: (0, 0)),
                  wo_spec],
        out_specs=(pl.BlockSpec((1, bsz, n), lambda i: (i // k_tiles, 0, 0)), wo_spec, tspec, tspec, tspec, tspec),
        compiler_params=pltpu.CompilerParams(dimension_semantics=("arbitrary",), vmem_limit_bytes=VMEM_LIMIT),
        name="prologue",
    )(c, w_mod, b_mod.reshape(depth, 1, n), positions.reshape(bsz * seq, 1), freq, w_out)
    return outs[0], outs[1], tuple(t.reshape(bsz, seq, LANES) for t in outs[2:])


def _seg_mean(x, e_ref):
    return _dot(_bf(x), e_ref[...])


def _rot(v, c, s, first_half):
    partner = jnp.where(first_half, pltpu.roll(v, LANES - HALF, axis=1), pltpu.roll(v, HALF, axis=1))
    return v * c + partner * s


def _layer_kernel(sinks_ref,
                  x_ref, mod_ref, pre_ref, post_ref, win_ref, wout_ref, gw_ref, gb_ref, gng_ref,
                  ca_ref, sa_ref, cr_ref, sr_ref,
                  bias_ref, vones_ref, dmask_ref, qdec_ref, kdec_ref, cdm_ref, rbm_ref,
                  tri_ref, lmask_ref, smask_ref, e64_ref,
                  o_ref, *scratch):
    hb_grp = scratch[0:NGRP]
    proj_grp = scratch[NGRP:2 * NGRP]
    mix_grp = scratch[2 * NGRP:3 * NGRP]
    kprev_ref, vprev_ref, rstate_ref, gstate_ref, bscr_ref = scratch[3 * NGRP:]
    t = pl.program_id(1)

    def sub_view(refs, sb):
        return refs[sb // PGRP].at[pl.ds((sb % PGRP) * BLK, BLK)]

    hb_refs = [sub_view(hb_grp, sb) for sb in range(NSUB)]
    proj_refs = [sub_view(proj_grp, sb) for sb in range(NSUB)]
    mix_refs = [sub_view(mix_grp, sb) for sb in range(NSUB)]

    @pl.when(t == 0)
    def _():
        kprev_ref[...] = jnp.zeros_like(kprev_ref)
        vprev_ref[...] = jnp.zeros_like(vprev_ref)
        rstate_ref[...] = jnp.zeros_like(rstate_ref)
        gstate_ref[...] = jnp.zeros_like(gstate_ref)
        for g in range(NGRP):
            proj_grp[g][:, O_GA:] = jnp.zeros((PGRP * BLK, N_PROJ - O_GA), jnp.float32)

    lane = lax.broadcasted_iota(jnp.int32, (1, LANES), 1)
    lane2 = lax.broadcasted_iota(jnp.int32, (1, GLA_W), 1)
    zero_bf = jnp.zeros((), jnp.bfloat16)
    first_half = (lane % HEAD_DIM) < HALF
    lo_hi = [lane // HEAD_DIM == g for g in range(2)]
    hgrp = [lane // GLA_DK == h for h in range(GLA_HEADS)]
    vhead = [lane2 // GLA_DV == h for h in range(GLA_HEADS)]

    def rows(sb):
        return slice(sb * BLK, (sb + 1) * BLK)

    done = set()

    def gen_norm(sb):
        if sb >= PGRP:
            yield f"projgrp{sb // PGRP - 1}.started"
        x = x_ref[0, rows(sb), :]
        inv = lax.rsqrt(jnp.mean(x * x, axis=-1, keepdims=True) + EPS)
        hb_refs[sb][...] = _bf((x * inv) * (pre_ref[...] * (1.0 + mod_ref[1:2, :])) + mod_ref[0:1, :])
        done.add(f"norm{sb}")

    def gen_proj(g):
        subs = range(g * PGRP, (g + 1) * PGRP)
        for sb in subs:
            yield f"norm{sb}"
        if g > 0:
            yield f"projgrp{g - 1}.all"
        done.add(f"projgrp{g}.started")
        for name, a, b in PROJ_SEGMENTS:
            proj_grp[g][:, a:b] = _dot(hb_grp[g][...], win_ref[:, a:b])
            done.update(f"proj{sb}.{name}" for sb in subs)
            yield
        done.add(f"projgrp{g}.all")

    def gen_attn(sb):
        yield f"proj{sb}.akv"
        if sb > 0:
            yield f"attn{sb - 1}.kv"
        pr = proj_refs[sb]
        ca, sa = ca_ref[0, rows(sb), :], sa_ref[0, rows(sb), :]
        k_new = _rot(pr[:, O_AK:O_AK + 128], ca, sa, first_half) * (HEAD_DIM ** -0.5 * LOG2E)
        v_new = pr[:, O_AV:O_AV + 128]
        k_sw = pltpu.roll(k_new, HEAD_DIM, axis=1)
        v_sw = pltpu.roll(v_new, HEAD_DIM, axis=1)
        k2, v2 = [], []
        for g in range(2):
            k_g = _bf(jnp.where(lo_hi[g], k_new, k_sw))
            v_g = _bf(jnp.where(lo_hi[g], v_new, v_sw))
            kcat = jnp.concatenate([kprev_ref[g], k_g], axis=0)
            vcat = jnp.concatenate([vprev_ref[g], v_g], axis=0)
            kprev_ref[g] = k_g
            vprev_ref[g] = v_g
            k2.append(jnp.concatenate([jnp.where(lo_hi[h], kcat, zero_bf) for h in range(2)], axis=0))
            v2_g = jnp.concatenate([jnp.where(lo_hi[h], vcat, zero_bf) for h in range(2)], axis=0)
            v2.append(jnp.concatenate([v2_g, vones_ref[...]], axis=1))
        done.add(f"attn{sb}.kv")
        if sb == 0:
            bias = bias_ref[jnp.minimum(t, 1)]
        else:
            bias = bias_ref[1]

        def scores(g):
            q = [_rot(pr[:, O_AQ + m * 128:O_AQ + (m + 1) * 128], ca, sa, first_half) for m in (2 * g, 2 * g + 1)]
            return _dot_nt(_bf(jnp.concatenate(q, axis=0)), k2[g])

        yield f"proj{sb}.aq"
        s_g = scores(0)
        yield
        for g in range(2):
            ps, mxs = [], []
            for j in range(2):
                row_p, row_mx = [], []
                for h in range(2):
                    s = s_g[j * BLK:(j + 1) * BLK, h * 2 * BLK:(h + 1) * 2 * BLK] + bias
                    mx = jnp.maximum(jnp.max(s, axis=-1, keepdims=True), sinks_ref[2 * (2 * g + j) + h] * LOG2E)
                    row_p.append(_bf(jnp.exp2(s - mx)))
                    row_mx.append(mx)
                ps.append(jnp.concatenate(row_p, axis=1))
                mxs.append(row_mx)
            pv = _dot(jnp.concatenate(ps, axis=0), v2[g])
            if g == 0:
                s_g = scores(1)
            yield
            yield f"proj{sb}.ag"
            for j in range(2):
                m = 2 * g + j
                pv_m = pv[j * BLK:(j + 1) * BLK]
                mx_l = jnp.where(lo_hi[0], mxs[j][0], mxs[j][1])
                sink_l = jnp.where(lo_hi[0], sinks_ref[2 * m], sinks_ref[2 * m + 1]) * LOG2E
                den = pv_m[:, 128:] + jnp.exp2(sink_l - mx_l)
                gate_m = _silu(pr[:, O_AG + m * 128:O_AG + (m + 1) * 128])
                mix_refs[sb][:, m * 128:(m + 1) * 128] = _bf(pv_m[:, :128] / den * gate_m)
        done.add(f"attn{sb}.done")

    def gen_ret(sb):
        yield f"proj{sb}.rqk"
        yield f"proj{sb}.rvg"
        if sb > 0:
            yield f"ret{sb - 1}.state"
        pr = proj_refs[sb]
        cr, sr = cr_ref[0, rows(sb), :], sr_ref[0, rows(sb), :]
        held = []
        for p in range(2):
            q_p = _rot(pr[:, O_RQ + p * 128:O_RQ + (p + 1) * 128], cr, sr, first_half)
            k_p = _rot(pr[:, O_RK + p * 128:O_RK + (p + 1) * 128], cr, sr, first_half)
            v_p = _bf(pr[:, O_RV + p * 128:O_RV + (p + 1) * 128])
            k_pb = _bf(k_p)
            kbd = jnp.concatenate([jnp.where(lo_hi[h], k_pb, zero_bf) for h in range(2)], axis=0)
            vbd = jnp.concatenate([jnp.where(lo_hi[h], v_p, zero_bf) for h in range(2)], axis=0)
            sc_raw = _dot_nt(_bf(q_p), kbd)
            st = rstate_ref[p]
            inter = _dot(_bf(q_p * qdec_ref[p]), _bf(st))
            upd = _dot_tn(_bf(k_p * kdec_ref[p]), v_p)
            held.append((sc_raw, vbd, st, inter, upd))
        yield
        intras = []
        for p in range(2):
            sc_raw, vbd, st, inter, upd = held[p]
            intras.append(_dot(_bf(sc_raw * dmask_ref[p]), vbd) + inter)
            rstate_ref[p] = st * cdm_ref[p] + upd * rbm_ref[...]
        done.add(f"ret{sb}.state")
        yield
        r = jnp.concatenate(intras, axis=1)
        ms = _seg_mean(r * r, e64_ref)
        yield
        mix_refs[sb][:, ATTN_W:ATTN_W + RET_W] = _bf(r * lax.rsqrt(ms + EPS) * _silu(pr[:, O_RG:O_RG + 256]))
        done.add(f"ret{sb}.done")

    def gen_gla(sb):
        yield f"proj{sb}.gga"
        pr = proj_refs[sb]
        logits = _dot(_bf(pr[:, O_GA:O_GA + 128]), gw_ref[...])
        yield
        logits = logits + gb_ref[...]
        log_a = (jnp.minimum(logits, 0.0) - jnp.log(1.0 + jnp.exp(-jnp.abs(logits)))) * (LOG2E / GLA_GATE_NORMALIZER)
        la_hi, la_lo = _split_hi_lo(log_a)
        xr = _dot(tri_ref[...], jnp.concatenate([la_hi, la_lo], axis=1))
        yield
        yield f"proj{sb}.gqkv"
        bscr = bscr_ref.at[sb]
        bscr[...] = xr[:, :128] + xr[:, 128:]
        gq = pr[:, O_GQ:O_GQ + 128] * (GLA_DK ** -0.5)
        gk = pr[:, O_GK:O_GK + 128]
        gv = _bf(pr[:, O_GV:O_GV + 256])
        nch = BLK // GLA_C
        row = lax.broadcasted_iota(jnp.int32, (GLA_C, LANES), 0)

        def level_factor(c, lv):
            r0 = c * GLA_C
            b_c = bscr[r0:r0 + GLA_C, :]
            grp = 1 << lv
            if grp >= 8:
                mids = [bscr[r0 + g0 + grp // 2 - 1:r0 + g0 + grp // 2, :] for g0 in range(0, GLA_C, grp)]
                ref = jnp.concatenate([jnp.broadcast_to(m, (grp, LANES)) for m in mids], axis=0)
            elif grp == 4:
                r = row % 4
                ref = jnp.where(r == 0, pltpu.roll(b_c, GLA_C - 1, axis=0),
                                jnp.where(r == 1, b_c,
                                          jnp.where(r == 2, pltpu.roll(b_c, 1, axis=0), pltpu.roll(b_c, 2, axis=0))))
            else:
                ref = jnp.where(row % 2 == 1, pltpu.roll(b_c, 1, axis=0), b_c)
            return jnp.exp2(-jnp.abs(b_c - ref))

        saccs = [jnp.zeros((GLA_C, GLA_HEADS * GLA_C), jnp.float32) for _ in range(nch)]
        pending = []
        for lv in range(GLA_LEVELS + 1):
            raw = []
            for c in range(nch):
                r0 = c * GLA_C
                if lv == 0:
                    qt, kt = _bf(gq[r0:r0 + GLA_C]), _bf(gk[r0:r0 + GLA_C])
                else:
                    f = level_factor(c, lv)
                    qt, kt = _bf(gq[r0:r0 + GLA_C] * f), _bf(gk[r0:r0 + GLA_C] * f)
                kbd = jnp.concatenate([jnp.where(hgrp[h], kt, zero_bf) for h in range(GLA_HEADS)], axis=0)
                raw.append(_dot_nt(qt, kbd))
            pending.append((lv, raw))
            if len(pending) == GLA_LEVELS_PER_STAGE or lv == GLA_LEVELS:
                yield
                for lv_p, raw_p in pending:
                    for c in range(nch):
                        saccs[c] = saccs[c] + raw_p[c] * lmask_ref[lv_p]
                pending = []
        if sb > 0:
            yield f"gla{sb - 1}.state"
        g_parts = []
        st = gstate_ref[...]
        for c in range(nch):
            r0 = c * GLA_C
            v_c = gv[r0:r0 + GLA_C]
            vbd = jnp.concatenate([jnp.where(vhead[h], v_c, zero_bf) for h in range(GLA_HEADS)], axis=0)
            b_c = bscr[r0:r0 + GLA_C, :]
            b_last = bscr[r0 + GLA_C - 1:r0 + GLA_C, :]
            q_in = _bf(gq[r0:r0 + GLA_C] * jnp.exp2(b_c))
            o_c = _dot(_bf(saccs[c]), vbd) + _dot_nt(q_in, _bf(st))
            k_out = _bf(gk[r0:r0 + GLA_C] * jnp.exp2(b_last - b_c))
            upd = _dot_tn(v_c, k_out)
            yield
            g_parts.append(o_c)
            st = st * jnp.exp2(b_last) + upd * smask_ref[...]
        gstate_ref[...] = st
        done.add(f"gla{sb}.state")
        g = jnp.concatenate(g_parts, axis=0)
        ms = _seg_mean(g * g, e64_ref)
        yield
        gate_g = _silu(pr[:, O_GG:O_GG + 256])
        mix_refs[sb][:, ATTN_W + RET_W:] = _bf(g * lax.rsqrt(ms + EPS) * gng_ref[...] * gate_g)
        done.add(f"gla{sb}.done")

    def gen_out(g):
        for sb in range(g * PGRP, (g + 1) * PGRP):
            yield f"attn{sb}.done"
            yield f"ret{sb}.done"
            yield f"gla{sb}.done"
        rws = slice(g * PGRP * BLK, (g + 1) * PGRP * BLK)
        half = D_MODEL // 2
        y0 = _dot(mix_grp[g][...], wout_ref[:, :half])
        yield
        y1 = _dot(mix_grp[g][...], wout_ref[:, half:])
        yield
        ssq = jnp.sum(y0 * y0, axis=-1, keepdims=True) + jnp.sum(y1 * y1, axis=-1, keepdims=True)
        inv = lax.rsqrt(ssq * (1.0 / D_MODEL) + EPS)
        gp = mod_ref[2:3, :] * post_ref[...]
        o_ref[0, rws, :half] = x_ref[0, rws, :half] + (y0 * inv) * gp[:, :half]
        yield
        o_ref[0, rws, half:] = x_ref[0, rws, half:] + (y1 * inv) * gp[:, half:]

    gens = []
    for g in range(NGRP):
        for sb in range(g * PGRP, (g + 1) * PGRP):
            gens += [gen_gla(sb), gen_attn(sb), gen_ret(sb)]
        gens += [gen_out(g), gen_proj(g)] + [gen_norm(sb) for sb in range(g * PGRP, (g + 1) * PGRP)]
    _run_interleaved(gens, done)


def _const_spec(shape):
    nd = len(shape)
    return pl.BlockSpec(shape, lambda b, t, s, _n=nd: (0,) * _n)


def _layer(l, x, mod, pre_gain, post_gain, w_in_b, w_out_b, sinks, gate_w_p, gate_b, gng, tables, consts):
    bsz, seq, d = x.shape
    tb = NSUB * BLK
    tok_spec = pl.BlockSpec((1, tb, LANES), lambda b, t, s: (b, t, 0))
    stacked = [pre_gain, post_gain, w_in_b, w_out_b, gate_w_p, gate_b, gng]

    def layer_spec(a):
        return pl.BlockSpec((None,) + a.shape[1:], lambda b, t, s, _n=a.ndim - 1: (l,) + (0,) * _n)

    in_specs = ([pl.BlockSpec((1, tb, d), lambda b, t, s: (b, t, 0)),
                 pl.BlockSpec((None, None, 3, d), lambda b, t, s: (l, b, 0, 0))]
                + [layer_spec(a) for a in stacked]
                + [tok_spec] * 4
                + [_const_spec(a.shape) for a in consts])
    scratch = ([pltpu.VMEM((PGRP * BLK, d), jnp.bfloat16)] * NGRP
               + [pltpu.VMEM((PGRP * BLK, N_PROJ), jnp.float32)] * NGRP
               + [pltpu.VMEM((PGRP * BLK, d), jnp.bfloat16)] * NGRP
               + [pltpu.VMEM((2, BLK, LANES), jnp.bfloat16),
                  pltpu.VMEM((2, BLK, LANES), jnp.bfloat16),
                  pltpu.VMEM((2, LANES, LANES), jnp.float32),
                  pltpu.VMEM((GLA_W, LANES), jnp.float32),
                  pltpu.VMEM((NSUB, BLK, LANES), jnp.float32)])
    grid_spec = pltpu.PrefetchScalarGridSpec(
        num_scalar_prefetch=1,
        grid=(bsz, seq // tb),
        in_specs=in_specs,
        out_specs=pl.BlockSpec((1, tb, d), lambda b, t, s: (b, t, 0)),
        scratch_shapes=scratch)
    return pl.pallas_call(
        _layer_kernel,
        out_shape=jax.ShapeDtypeStruct(x.shape, x.dtype),
        grid_spec=grid_spec,
        compiler_params=pltpu.CompilerParams(dimension_semantics=("arbitrary", "arbitrary"),
                                             vmem_limit_bytes=VMEM_LIMIT),
        name="hybrid_layer",
    )(sinks, x, mod, *stacked, *tables, *consts)


def kernel(x, c, positions, w_mod, b_mod, pre_norm_gain, post_norm_gain, w_in, attn_sinks, gla_gate_w,
           gla_gate_b, gla_norm_gain, w_out):
    depth = w_mod.shape[0]
    bsz = x.shape[0]
    f32, bf16 = jnp.float32, jnp.bfloat16

    rope_freq = ROPE_THETA ** (-jnp.arange(0, HEAD_DIM, 2, dtype=f32) / HEAD_DIM)
    ret_freq = 1.0 / (10000.0 ** jnp.linspace(0.0, 1.0, RET_DK // 2, dtype=f32))
    freq = jnp.concatenate([rope_freq, ret_freq, rope_freq, ret_freq]).reshape(1, LANES)
    mod, w_out_b, tables = _prologue(c, w_mod, b_mod, positions, freq, w_out)
    mod = mod.reshape(depth, bsz, 3, D_MODEL)

    consts = [jnp.asarray(_C["bias"]), jnp.asarray(_C["vones"], bf16), jnp.asarray(_C["dmask"]),
              jnp.asarray(_C["qdec"]), jnp.asarray(_C["kdec"]), jnp.asarray(_C["cdm"]), jnp.asarray(_C["rbm"]),
              jnp.asarray(_C["tri"], bf16), jnp.asarray(_C["lmask"]), jnp.asarray(_C["smask"]),
              jnp.asarray(_C["e64"], bf16)]
    w_in_b = w_in.astype(bf16)
    gate_w_p = jnp.pad(gla_gate_w, ((0, 0), (0, LANES - GLA_GATE_RANK), (0, 0))).astype(bf16)
    gng = jnp.tile(gla_norm_gain, (1, GLA_HEADS)).reshape(depth, 1, GLA_W)
    pre = pre_norm_gain.reshape(depth, 1, D_MODEL)
    post = post_norm_gain.reshape(depth, 1, D_MODEL)
    gate_b = gla_gate_b.reshape(depth, 1, LANES)

    for l in range(depth):
        x = _layer(l, x, mod, pre, post, w_in_b, w_out_b, attn_sinks[l], gate_w_p, gate_b, gng, tables, consts)
    return x
```

```python
import functools

import numpy as np
import jax
import jax.numpy as jnp
from jax import lax
from jax.experimental import pallas as pl
from jax.experimental.pallas import tpu as pltpu

D_MODEL = 1024
HEAD_DIM = 64
ATTN_HEADS = 8
ATTN_KV_HEADS = 2
WINDOW = 128
ROPE_THETA = 10000.0
RET_HEADS = 4
RET_DK = 64
RET_DV = 64
GLA_HEADS = 4
GLA_DK = 32
GLA_DV = 64
GLA_GATE_RANK = 16
GLA_GATE_NORMALIZER = 16.0
EPS = 1e-6

ATTN_W = ATTN_HEADS * HEAD_DIM
RET_W = RET_HEADS * RET_DV
GLA_W = GLA_HEADS * GLA_DV

LANES = 128
HALF = HEAD_DIM // 2
BLK = 128
NSUB = 8
PGRP = 2
NGRP = NSUB // PGRP
GLA_C = 64
GLA_LEVELS = 6
GLA_LEVELS_PER_STAGE = 7
NEG = -1e30
VMEM_LIMIT = 56 * 1024 * 1024
LOG2E = 1.4426950408889634

O_AQ, O_AK, O_AV, O_AG = 0, 512, 640, 768
O_RQ, O_RK, O_RV, O_RG = 1280, 1536, 1792, 2048
O_GQ, O_GK, O_GV, O_GG, O_GA = 2304, 2432, 2560, 2816, 3072
D_IN = O_GA + GLA_GATE_RANK
N_PROJ = 3200
PROJ_SEGMENTS = (("gga", O_GG, D_IN), ("gqkv", O_GQ, O_GG), ("akv", O_AK, O_AG), ("aq", O_AQ, O_AK),
                 ("rqk", O_RQ, O_RV), ("rvg", O_RV, O_GQ), ("ag", O_AG, O_RQ))


def _build_constants():
    f32 = np.float32
    lane = np.arange(LANES)
    grp = lane // HEAD_DIM

    i = np.arange(BLK)[:, None]
    j = np.arange(2 * BLK)[None, :]
    rel = i + BLK - j
    ok = (rel >= 0) & (rel < WINDOW)
    bias = np.stack([np.where(ok & (j >= BLK), 0.0, NEG), np.where(ok, 0.0, NEG)]).astype(f32)
    vones = np.zeros((2 * 2 * BLK, LANES), f32)
    for g in range(2):
        vones[g * 2 * BLK:(g + 1) * 2 * BLK] = (lane // HEAD_DIM == g)[None, :]

    kscale = RET_DK ** -0.5
    log_g = np.log(1.0 - 2.0 ** (-5.0 - np.arange(RET_HEADS, dtype=np.float64)))
    idx = np.arange(BLK, dtype=np.float64)
    diff = idx[:, None] - idx[None, :]
    dmask = np.zeros((2, BLK, 2 * BLK), f32)
    qdec = np.zeros((2, BLK, LANES), f32)
    kdec = np.zeros((2, BLK, LANES), f32)
    cdm = np.zeros((2, LANES, LANES), f32)
    rbm = ((np.arange(LANES)[:, None] // RET_DK) == (np.arange(LANES)[None, :] // RET_DV)).astype(f32)
    for p in range(2):
        for hh in range(2):
            lg = log_g[2 * p + hh]
            dmask[p, :, hh * BLK:(hh + 1) * BLK] = kscale * np.where(diff >= 0, np.exp(lg * np.maximum(diff, 0.0)), 0.0)
        lg_lane = log_g[2 * p + grp]
        qdec[p] = np.exp(lg_lane[None, :] * (idx[:, None] + 1.0))
        kdec[p] = kscale * np.exp(lg_lane[None, :] * (BLK - 1.0 - idx[:, None]))
        cdm[p] = np.exp(lg_lane * BLK)[:, None] * rbm

    t = np.arange(BLK)
    tri = (((t[:, None] // GLA_C) == (t[None, :] // GLA_C)) & (t[None, :] <= t[:, None])).astype(f32)
    ci = np.arange(GLA_C)[:, None]
    cj = np.arange(GLA_C)[None, :]
    x = ci ^ cj
    lvl = np.where(cj > ci, -1, np.where(x == 0, 0, np.floor(np.log2(np.maximum(x, 1))).astype(np.int64) + 1))
    lmask = np.stack([np.tile((lvl == lv).astype(f32), (1, GLA_HEADS)) for lv in range(GLA_LEVELS + 1)])
    smask = ((np.arange(GLA_W)[:, None] // GLA_DV) == (np.arange(LANES)[None, :] // GLA_DK)).astype(f32)
    e64 = ((np.arange(RET_W)[:, None] // RET_DV) == (np.arange(RET_W)[None, :] // RET_DV)).astype(f32) / RET_DV

    return dict(bias=bias, vones=vones, dmask=dmask, qdec=qdec, kdec=kdec, cdm=cdm, rbm=rbm, tri=tri,
                lmask=lmask, smask=smask, e64=e64)


_C = _build_constants()


def _dot(a, b):
    return jnp.dot(a, b, preferred_element_type=jnp.float32)


def _dot_nt(a, b):
    return lax.dot_general(a, b, (((1,), (1,)), ((), ())), preferred_element_type=jnp.float32)


def _dot_tn(a, b):
    return lax.dot_general(a, b, (((0,), (0,)), ((), ())), preferred_element_type=jnp.float32)


def _bf(x):
    return x.astype(jnp.bfloat16)


def _split_hi_lo(x):
    hi = _bf(x)
    lo = _bf(x - hi.astype(jnp.float32))
    return hi, lo


def _silu(x):
    half = 0.5 * x
    return half + half * jnp.tanh(half)


def _run_interleaved(gens, done):
    active = [[g, None] for g in gens]
    while active:
        progressed = False
        for item in list(active):
            if item[1] is not None and item[1] not in done:
                continue
            progressed = True
            try:
                need = next(item[0])
                while need is not None and need in done:
                    need = next(item[0])
                item[1] = need
            except StopIteration:
                active.remove(item)
        assert progressed, [item[1] for item in active]


MOD_TK = 256
TRIG_T = 512


def _mod_step(c_ref, w_ref, b_ref, o_ref, first):
    @pl.when(first)
    def _():
        o_ref[0] = jnp.broadcast_to(b_ref[0], o_ref.shape[1:])

    a = _silu(c_ref[...])
    a_hi, a_lo = _split_hi_lo(a)
    w_hi, w_lo = _split_hi_lo(w_ref[0])
    n = a.shape[0]
    both = _dot(jnp.concatenate([a_hi, a_lo], axis=0), w_hi)
    o_ref[0] += both[:n] + both[n:] + _dot(a_hi, w_lo)


def _trig_tile(pos_ref, freq_ref, out_refs, row0):
    ca_ref, sa_ref, cr_ref, sr_ref = out_refs
    half = TRIG_T // 2
    lane = lax.broadcasted_iota(jnp.int32, (half, LANES), 1)
    pos_a = pos_ref[row0:row0 + half, :]
    pos_b = pos_ref[row0 + half:row0 + TRIG_T, :]
    ang = jnp.where(lane < 2 * HALF, pos_a, pos_b).astype(jnp.float32) * freq_ref[...]
    c = jnp.cos(ang)
    s = jnp.sin(ang)
    quarter = lane // HALF
    sign = jnp.where((lane % HEAD_DIM) < HALF, -1.0, 1.0)
    rc = [c] + [pltpu.roll(c, k * HALF, axis=1) for k in (1, 2, 3)]
    rs = [s] + [pltpu.roll(s, k * HALF, axis=1) for k in (1, 2, 3)]

    def spread(r, src):
        out = r[(0 - src) % 4]
        for q in (1, 2, 3):
            out = jnp.where(quarter == q, r[(q - src) % 4], out)
        return out

    for tok, rws in ((0, slice(row0, row0 + half)), (1, slice(row0 + half, row0 + TRIG_T))):
        ca_ref[rws, :] = spread(rc, 2 * tok)
        sa_ref[rws, :] = spread(rs, 2 * tok) * sign
        cr_ref[rws, :] = spread(rc, 2 * tok + 1)
        sr_ref[rws, :] = spread(rs, 2 * tok + 1) * sign


def _prologue_kernel(c_ref, w_ref, b_ref, pos_ref, freq_ref, wo_ref, mod_ref, wob_ref, *table_refs, k_tiles, tiles):
    _mod_step(c_ref, w_ref, b_ref, mod_ref, pl.program_id(0) % k_tiles == 0)
    wob_ref[...] = _bf(wo_ref[...])
    for j in range(tiles):
        _trig_tile(pos_ref, freq_ref, table_refs, j * TRIG_T)


def _prologue(c, w_mod, b_mod, positions, freq, w_out):
    depth, d, n = w_mod.shape
    bsz, seq = positions.shape
    k_tiles = d // MOD_TK
    steps = depth * k_tiles
    tiles = (bsz * seq) // (TRIG_T * steps)
    rows = tiles * TRIG_T
    assert rows * steps == bsz * seq
    table = jax.ShapeDtypeStruct((bsz * seq, LANES), jnp.float32)
    tspec = pl.BlockSpec((rows, LANES), lambda i: (i, 0))
    wo_spec = pl.BlockSpec((1, w_out.shape[1] // k_tiles, w_out.shape[2]), lambda i: (i // k_tiles, i % k_tiles, 0))
    outs = pl.pallas_call(
        functools.partial(_prologue_kernel, k_tiles=k_tiles, tiles=tiles),
        out_shape=(jax.ShapeDtypeStruct((depth, bsz, n), jnp.float32),
                   jax.ShapeDtypeStruct(w_out.shape, jnp.bfloat16), table, table, table, table),
        grid=(steps,),
        in_specs=[pl.BlockSpec((bsz, MOD_TK), lambda i: (0, i % k_tiles)),
                  pl.BlockSpec((1, MOD_TK, n), lambda i: (i // k_tiles, i % k_tiles, 0)),
                  pl.BlockSpec((1, 1, n), lambda i: (i // k_tiles, 0, 0)),
                  pl.BlockSpec((rows, 1), lambda i: (i, 0)),
                  pl.BlockSpec((1, LANES), lambda i: (0, 0)),
                  wo_spec],
        out_specs=(pl.BlockSpec((1, bsz, n), lambda i: (i // k_tiles, 0, 0)), wo_spec, tspec, tspec, tspec, tspec),
        compiler_params=pltpu.CompilerParams(dimension_semantics=("arbitrary",), vmem_limit_bytes=VMEM_LIMIT),
        name="prologue",
    )(c, w_mod, b_mod.reshape(depth, 1, n), positions.reshape(bsz * seq, 1), freq, w_out)
    return outs[0], outs[1], tuple(t.reshape(bsz, seq, LANES) for t in outs[2:])


def _seg_mean(x, e_ref):
    return _dot(_bf(x), e_ref[...])


def _rot(v, c, s, first_half):
    partner = jnp.where(first_half, pltpu.roll(v, LANES - HALF, axis=1), pltpu.roll(v, HALF, axis=1))
    return v * c + partner * s


def _layer_kernel(sinks_ref,
                  x_ref, mod_ref, pre_ref, post_ref, win_ref, wout_ref, gw_ref, gb_ref, gng_ref,
                  ca_ref, sa_ref, cr_ref, sr_ref,
                  bias_ref, vones_ref, dmask_ref, qdec_ref, kdec_ref, cdm_ref, rbm_ref,
                  tri_ref, lmask_ref, smask_ref, e64_ref,
                  o_ref, *scratch):
    hb_grp = scratch[0:NGRP]
    proj_grp = scratch[NGRP:2 * NGRP]
    mix_grp = scratch[2 * NGRP:3 * NGRP]
    kprev_ref, vprev_ref, rstate_ref, gstate_ref, bscr_ref = scratch[3 * NGRP:]
    t = pl.program_id(1)

    def sub_view(refs, sb):
        return refs[sb // PGRP].at[pl.ds((sb % PGRP) * BLK, BLK)]

    hb_refs = [sub_view(hb_grp, sb) for sb in range(NSUB)]
    proj_refs = [sub_view(proj_grp, sb) for sb in range(NSUB)]
    mix_refs = [sub_view(mix_grp, sb) for sb in range(NSUB)]

    @pl.when(t == 0)
    def _():
        kprev_ref[...] = jnp.zeros_like(kprev_ref)
        vprev_ref[...] = jnp.zeros_like(vprev_ref)
        rstate_ref[...] = jnp.zeros_like(rstate_ref)
        gstate_ref[...] = jnp.zeros_like(gstate_ref)
        for g in range(NGRP):
            proj_grp[g][:, O_GA:] = jnp.zeros((PGRP * BLK, N_PROJ - O_GA), jnp.float32)

    lane = lax.broadcasted_iota(jnp.int32, (1, LANES), 1)
    lane2 = lax.broadcasted_iota(jnp.int32, (1, GLA_W), 1)
    zero_bf = jnp.zeros((), jnp.bfloat16)
    first_half = (lane % HEAD_DIM) < HALF
    lo_hi = [lane // HEAD_DIM == g for g in range(2)]
    hgrp = [lane // GLA_DK == h for h in range(GLA_HEADS)]
    vhead = [lane2 // GLA_DV == h for h in range(GLA_HEADS)]

    def rows(sb):
        return slice(sb * BLK, (sb + 1) * BLK)

    done = set()

    def gen_norm(sb):
        if sb >= PGRP:
            yield f"projgrp{sb // PGRP - 1}.started"
        x = x_ref[0, rows(sb), :]
        inv = lax.rsqrt(jnp.mean(x * x, axis=-1, keepdims=True) + EPS)
        hb_refs[sb][...] = _bf((x * inv) * (pre_ref[...] * (1.0 + mod_ref[1:2, :])) + mod_ref[0:1, :])
        done.add(f"norm{sb}")

    def gen_proj(g):
        subs = range(g * PGRP, (g + 1) * PGRP)
        for sb in subs:
            yield f"norm{sb}"
        if g > 0:
            yield f"projgrp{g - 1}.all"
        done.add(f"projgrp{g}.started")
        for name, a, b in PROJ_SEGMENTS:
            proj_grp[g][:, a:b] = _dot(hb_grp[g][...], win_ref[:, a:b])
            done.update(f"proj{sb}.{name}" for sb in subs)
            yield
        done.add(f"projgrp{g}.all")

    def gen_attn(sb):
        yield f"proj{sb}.akv"
        if sb > 0:
            yield f"attn{sb - 1}.kv"
        pr = proj_refs[sb]
        ca, sa = ca_ref[0, rows(sb), :], sa_ref[0, rows(sb), :]
        k_new = _rot(pr[:, O_AK:O_AK + 128], ca, sa, first_half) * (HEAD_DIM ** -0.5 * LOG2E)
        v_new = pr[:, O_AV:O_AV + 128]
        k_sw = pltpu.roll(k_new, HEAD_DIM, axis=1)
        v_sw = pltpu.roll(v_new, HEAD_DIM, axis=1)
        k2, v2 = [], []
        for g in range(2):
            k_g = _bf(jnp.where(lo_hi[g], k_new, k_sw))
            v_g = _bf(jnp.where(lo_hi[g], v_new, v_sw))
            kcat = jnp.concatenate([kprev_ref[g], k_g], axis=0)
            vcat = jnp.concatenate([vprev_ref[g], v_g], axis=0)
            kprev_ref[g] = k_g
            vprev_ref[g] = v_g
            k2.append(jnp.concatenate([jnp.where(lo_hi[h], kcat, zero_bf) for h in range(2)], axis=0))
            v2_g = jnp.concatenate([jnp.where(lo_hi[h], vcat, zero_bf) for h in range(2)], axis=0)
            v2.append(jnp.concatenate([v2_g, vones_ref[...]], axis=1))
        done.add(f"attn{sb}.kv")
        if sb == 0:
            bias = bias_ref[jnp.minimum(t, 1)]
        else:
            bias = bias_ref[1]

        def scores(g):
            q = [_rot(pr[:, O_AQ + m * 128:O_AQ + (m + 1) * 128], ca, sa, first_half) for m in (2 * g, 2 * g + 1)]
            return _dot_nt(_bf(jnp.concatenate(q, axis=0)), k2[g])

        yield f"proj{sb}.aq"
        s_g = scores(0)
        yield
        for g in range(2):
            ps, mxs = [], []
            for j in range(2):
                row_p, row_mx = [], []
                for h in range(2):
                    s = s_g[j * BLK:(j + 1) * BLK, h * 2 * BLK:(h + 1) * 2 * BLK] + bias
                    mx = jnp.maximum(jnp.max(s, axis=-1, keepdims=True), sinks_ref[2 * (2 * g + j) + h] * LOG2E)
                    row_p.append(_bf(jnp.exp2(s - mx)))
                    row_mx.append(mx)
                ps.append(jnp.concatenate(row_p, axis=1))
                mxs.append(row_mx)
            pv = _dot(jnp.concatenate(ps, axis=0), v2[g])
            if g == 0:
                s_g = scores(1)
            yield
            yield f"proj{sb}.ag"
            for j in range(2):
                m = 2 * g + j
                pv_m = pv[j * BLK:(j + 1) * BLK]
                mx_l = jnp.where(lo_hi[0], mxs[j][0], mxs[j][1])
                sink_l = jnp.where(lo_hi[0], sinks_ref[2 * m], sinks_ref[2 * m + 1]) * LOG2E
                den = pv_m[:, 128:] + jnp.exp2(sink_l - mx_l)
                gate_m = _silu(pr[:, O_AG + m * 128:O_AG + (m + 1) * 128])
                mix_refs[sb][:, m * 128:(m + 1) * 128] = _bf(pv_m[:, :128] / den * gate_m)
        done.add(f"attn{sb}.done")

    def gen_ret(sb):
        yield f"proj{sb}.rqk"
        yield f"proj{sb}.rvg"
        if sb > 0:
            yield f"ret{sb - 1}.state"
        pr = proj_refs[sb]
        cr, sr = cr_ref[0, rows(sb), :], sr_ref[0, rows(sb), :]
        held = []
        for p in range(2):
            q_p = _rot(pr[:, O_RQ + p * 128:O_RQ + (p + 1) * 128], cr, sr, first_half)
            k_p = _rot(pr[:, O_RK + p * 128:O_RK + (p + 1) * 128], cr, sr, first_half)
            v_p = _bf(pr[:, O_RV + p * 128:O_RV + (p + 1) * 128])
            k_pb = _bf(k_p)
            kbd = jnp.concatenate([jnp.where(lo_hi[h], k_pb, zero_bf) for h in range(2)], axis=0)
            vbd = jnp.concatenate([jnp.where(lo_hi[h], v_p, zero_bf) for h in range(2)], axis=0)
            sc_raw = _dot_nt(_bf(q_p), kbd)
            st = rstate_ref[p]
            inter = _dot(_bf(q_p * qdec_ref[p]), _bf(st))
            upd = _dot_tn(_bf(k_p * kdec_ref[p]), v_p)
            held.append((sc_raw, vbd, st, inter, upd))
        yield
        intras = []
        for p in range(2):
            sc_raw, vbd, st, inter, upd = held[p]
            intras.append(_dot(_bf(sc_raw * dmask_ref[p]), vbd) + inter)
            rstate_ref[p] = st * cdm_ref[p] + upd * rbm_ref[...]
        done.add(f"ret{sb}.state")
        yield
        r = jnp.concatenate(intras, axis=1)
        ms = _seg_mean(r * r, e64_ref)
        yield
        mix_refs[sb][:, ATTN_W:ATTN_W + RET_W] = _bf(r * lax.rsqrt(ms + EPS) * _silu(pr[:, O_RG:O_RG + 256]))
        done.add(f"ret{sb}.done")

    def gen_gla(sb):
        yield f"proj{sb}.gga"
        pr = proj_refs[sb]
        logits = _dot(_bf(pr[:, O_GA:O_GA + 128]), gw_ref[...])
        yield
        logits = logits + gb_ref[...]
        log_a = (jnp.minimum(logits, 0.0) - jnp.log(1.0 + jnp.exp(-jnp.abs(logits)))) * (LOG2E / GLA_GATE_NORMALIZER)
        la_hi, la_lo = _split_hi_lo(log_a)
        xr = _dot(tri_ref[...], jnp.concatenate([la_hi, la_lo], axis=1))
        yield
        yield f"proj{sb}.gqkv"
        bscr = bscr_ref.at[sb]
        bscr[...] = xr[:, :128] + xr[:, 128:]
        gq = pr[:, O_GQ:O_GQ + 128] * (GLA_DK ** -0.5)
        gk = pr[:, O_GK:O_GK + 128]
        gv = _bf(pr[:, O_GV:O_GV + 256])
        nch = BLK // GLA_C
        row = lax.broadcasted_iota(jnp.int32, (GLA_C, LANES), 0)

        def level_factor(c, lv):
            r0 = c * GLA_C
            b_c = bscr[r0:r0 + GLA_C, :]
            grp = 1 << lv
            if grp >= 8:
                mids = [bscr[r0 + g0 + grp // 2 - 1:r0 + g0 + grp // 2, :] for g0 in range(0, GLA_C, grp)]
                ref = jnp.concatenate([jnp.broadcast_to(m, (grp, LANES)) for m in mids], axis=0)
            elif grp == 4:
                r = row % 4
                ref = jnp.where(r == 0, pltpu.roll(b_c, GLA_C - 1, axis=0),
                                jnp.where(r == 1, b_c,
                                          jnp.where(r == 2, pltpu.roll(b_c, 1, axis=0), pltpu.roll(b_c, 2, axis=0))))
            else:
                ref = jnp.where(row % 2 == 1, pltpu.roll(b_c, 1, axis=0), b_c)
            return jnp.exp2(-jnp.abs(b_c - ref))

        saccs = [jnp.zeros((GLA_C, GLA_HEADS * GLA_C), jnp.float32) for _ in range(nch)]
        pending = []
        for lv in range(GLA_LEVELS + 1):
            raw = []
            for c in range(nch):
                r0 = c * GLA_C
                if lv == 0:
                    qt, kt = _bf(gq[r0:r0 + GLA_C]), _bf(gk[r0:r0 + GLA_C])
                else:
                    f = level_factor(c, lv)
                    qt, kt = _bf(gq[r0:r0 + GLA_C] * f), _bf(gk[r0:r0 + GLA_C] * f)
                kbd = jnp.concatenate([jnp.where(hgrp[h], kt, zero_bf) for h in range(GLA_HEADS)], axis=0)
                raw.append(_dot_nt(qt, kbd))
            pending.append((lv, raw))
            if len(pending) == GLA_LEVELS_PER_STAGE or lv == GLA_LEVELS:
                yield
                for lv_p, raw_p in pending:
                    for c in range(nch):
                        saccs[c] = saccs[c] + raw_p[c] * lmask_ref[lv_p]
                pending = []
        if sb > 0:
            yield f"gla{sb - 1}.state"
        g_parts = []
        st = gstate_ref[...]
        for c in range(nch):
            r0 = c * GLA_C
            v_c = gv[r0:r0 + GLA_C]
            vbd = jnp.concatenate([jnp.where(vhead[h], v_c, zero_bf) for h in range(GLA_HEADS)], axis=0)
            b_c = bscr[r0:r0 + GLA_C, :]
            b_last = bscr[r0 + GLA_C - 1:r0 + GLA_C, :]
            q_in = _bf(gq[r0:r0 + GLA_C] * jnp.exp2(b_c))
            o_c = _dot(_bf(saccs[c]), vbd) + _dot_nt(q_in, _bf(st))
            k_out = _bf(gk[r0:r0 + GLA_C] * jnp.exp2(b_last - b_c))
            upd = _dot_tn(v_c, k_out)
            yield
            g_parts.append(o_c)
            st = st * jnp.exp2(b_last) + upd * smask_ref[...]
        gstate_ref[...] = st
        done.add(f"gla{sb}.state")
        g = jnp.concatenate(g_parts, axis=0)
        ms = _seg_mean(g * g, e64_ref)
        yield
        gate_g = _silu(pr[:, O_GG:O_GG + 256])
        mix_refs[sb][:, ATTN_W + RET_W:] = _bf(g * lax.rsqrt(ms + EPS) * gng_ref[...] * gate_g)
        done.add(f"gla{sb}.done")

    def gen_out(g):
        for sb in range(g * PGRP, (g + 1) * PGRP):
            yield f"attn{sb}.done"
            yield f"ret{sb}.done"
            yield f"gla{sb}.done"
        rws = slice(g * PGRP * BLK, (g + 1) * PGRP * BLK)
        half = D_MODEL // 2
        y0 = _dot(mix_grp[g][...], wout_ref[:, :half])
        yield
        y1 = _dot(mix_grp[g][...], wout_ref[:, half:])
        yield
        ssq = jnp.sum(y0 * y0, axis=-1, keepdims=True) + jnp.sum(y1 * y1, axis=-1, keepdims=True)
        inv = lax.rsqrt(ssq * (1.0 / D_MODEL) + EPS)
        gp = mod_ref[2:3, :] * post_ref[...]
        o_ref[0, rws, :half] = x_ref[0, rws, :half] + (y0 * inv) * gp[:, :half]
        yield
        o_ref[0, rws, half:] = x_ref[0, rws, half:] + (y1 * inv) * gp[:, half:]

    gens = []
    for g in range(NGRP):
        for sb in range(g * PGRP, (g + 1) * PGRP):
            gens += [gen_gla(sb), gen_attn(sb), gen_ret(sb)]
        gens += [gen_out(g), gen_proj(g)] + [gen_norm(sb) for sb in range(g * PGRP, (g + 1) * PGRP)]
    _run_interleaved(gens, done)


def _const_spec(shape):
    nd = len(shape)
    return pl.BlockSpec(shape, lambda b, t, s, _n=nd: (0,) * _n)


def _layer(l, x, mod, pre_gain, post_gain, w_in_b, w_out_b, sinks, gate_w_p, gate_b, gng, tables, consts):
    bsz, seq, d = x.shape
    tb = NSUB * BLK
    tok_spec = pl.BlockSpec((1, tb, LANES), lambda b, t, s: (b, t, 0))
    stacked = [pre_gain, post_gain, w_in_b, w_out_b, gate_w_p, gate_b, gng]

    def layer_spec(a):
        return pl.BlockSpec((None,) + a.shape[1:], lambda b, t, s, _n=a.ndim - 1: (l,) + (0,) * _n)

    in_specs = ([pl.BlockSpec((1, tb, d), lambda b, t, s: (b, t, 0)),
                 pl.BlockSpec((None, None, 3, d), lambda b, t, s: (l, b, 0, 0))]
                + [layer_spec(a) for a in stacked]
                + [tok_spec] * 4
                + [_const_spec(a.shape) for a in consts])
    scratch = ([pltpu.VMEM((PGRP * BLK, d), jnp.bfloat16)] * NGRP
               + [pltpu.VMEM((PGRP * BLK, N_PROJ), jnp.float32)] * NGRP
               + [pltpu.VMEM((PGRP * BLK, d), jnp.bfloat16)] * NGRP
               + [pltpu.VMEM((2, BLK, LANES), jnp.bfloat16),
                  pltpu.VMEM((2, BLK, LANES), jnp.bfloat16),
                  pltpu.VMEM((2, LANES, LANES), jnp.float32),
                  pltpu.VMEM((GLA_W, LANES), jnp.float32),
                  pltpu.VMEM((NSUB, BLK, LANES), jnp.float32)])
    grid_spec = pltpu.PrefetchScalarGridSpec(
        num_scalar_prefetch=1,
        grid=(bsz, seq // tb),
        in_specs=in_specs,
        out_specs=pl.BlockSpec((1, tb, d), lambda b, t, s: (b, t, 0)),
        scratch_shapes=scratch)
    return pl.pallas_call(
        _layer_kernel,
        out_shape=jax.ShapeDtypeStruct(x.shape, x.dtype),
        grid_spec=grid_spec,
        compiler_params=pltpu.CompilerParams(dimension_semantics=("arbitrary", "arbitrary"),
                                             vmem_limit_bytes=VMEM_LIMIT),
        name="hybrid_layer",
    )(sinks, x, mod, *stacked, *tables, *consts)


def kernel(x, c, positions, w_mod, b_mod, pre_norm_gain, post_norm_gain, w_in, attn_sinks, gla_gate_w,
           gla_gate_b, gla_norm_gain, w_out):
    depth = w_mod.shape[0]
    bsz = x.shape[0]
    f32, bf16 = jnp.float32, jnp.bfloat16

    rope_freq = ROPE_THETA ** (-jnp.arange(0, HEAD_DIM, 2, dtype=f32) / HEAD_DIM)
    ret_freq = 1.0 / (10000.0 ** jnp.linspace(0.0, 1.0, RET_DK // 2, dtype=f32))
    freq = jnp.concatenate([rope_freq, ret_freq, rope_freq, ret_freq]).reshape(1, LANES)
    mod, w_out_b, tables = _prologue(c, w_mod, b_mod, positions, freq, w_out)
    mod = mod.reshape(depth, bsz, 3, D_MODEL)

    consts = [jnp.asarray(_C["bias"]), jnp.asarray(_C["vones"], bf16), jnp.asarray(_C["dmask"]),
              jnp.asarray(_C["qdec"]), jnp.asarray(_C["kdec"]), jnp.asarray(_C["cdm"]), jnp.asarray(_C["rbm"]),
              jnp.asarray(_C["tri"], bf16), jnp.asarray(_C["lmask"]), jnp.asarray(_C["smask"]),
              jnp.asarray(_C["e64"], bf16)]
    w_in_b = w_in.astype(bf16)
    gate_w_p = jnp.pad(gla_gate_w, ((0, 0), (0, LANES - GLA_GATE_RANK), (0, 0))).astype(bf16)
    gng = jnp.tile(gla_norm_gain, (1, GLA_HEADS)).reshape(depth, 1, GLA_W)
    pre = pre_norm_gain.reshape(depth, 1, D_MODEL)
    post = post_norm_gain.reshape(depth, 1, D_MODEL)
    gate_b = gla_gate_b.reshape(depth, 1, LANES)

    for l in range(depth):
        x = _layer(l, x, mod, pre, post, w_in_b, w_out_b, attn_sinks[l], gate_w_p, gate_b, gng, tables, consts)
    return x
```

```python
import functools

import numpy as np
import jax
import jax.numpy as jnp
from jax import lax
from jax.experimental import pallas as pl
from jax.experimental.pallas import tpu as pltpu

D_MODEL = 1024
HEAD_DIM = 64
ATTN_HEADS = 8
ATTN_KV_HEADS = 2
WINDOW = 128
ROPE_THETA = 10000.0
RET_HEADS = 4
RET_DK = 64
RET_DV = 64
GLA_HEADS = 4
GLA_DK = 32
GLA_DV = 64
GLA_GATE_RANK = 16
GLA_GATE_NORMALIZER = 16.0
EPS = 1e-6

ATTN_W = ATTN_HEADS * HEAD_DIM
RET_W = RET_HEADS * RET_DV
GLA_W = GLA_HEADS * GLA_DV

LANES = 128
HALF = HEAD_DIM // 2
BLK = 128
NSUB = 8
PGRP = 4
NGRP = NSUB // PGRP
GLA_C = 64
GLA_LEVELS = 6
GLA_LEVELS_PER_STAGE = 7
NEG = -1e30
VMEM_LIMIT = 56 * 1024 * 1024
LOG2E = 1.4426950408889634

O_AQ, O_AK, O_AV, O_AG = 0, 512, 640, 768
O_RQ, O_RK, O_RV, O_RG = 1280, 1536, 1792, 2048
O_GQ, O_GK, O_GV, O_GG, O_GA = 2304, 2432, 2560, 2816, 3072
D_IN = O_GA + GLA_GATE_RANK
N_PROJ = 3200
PROJ_SEGMENTS = (("gga", O_GG, D_IN), ("gqkv", O_GQ, O_GG), ("akv", O_AK, O_AG), ("aq", O_AQ, O_AK),
                 ("rqk", O_RQ, O_RV), ("rvg", O_RV, O_GQ), ("ag", O_AG, O_RQ))


def _build_constants():
    f32 = np.float32
    lane = np.arange(LANES)
    grp = lane // HEAD_DIM

    i = np.arange(BLK)[:, None]
    j = np.arange(2 * BLK)[None, :]
    rel = i + BLK - j
    ok = (rel >= 0) & (rel < WINDOW)
    bias = np.stack([np.where(ok & (j >= BLK), 0.0, NEG), np.where(ok, 0.0, NEG)]).astype(f32)
    vones = np.zeros((2 * 2 * BLK, LANES), f32)
    for g in range(2):
        vones[g * 2 * BLK:(g + 1) * 2 * BLK] = (lane // HEAD_DIM == g)[None, :]

    kscale = RET_DK ** -0.5
    log_g = np.log(1.0 - 2.0 ** (-5.0 - np.arange(RET_HEADS, dtype=np.float64)))
    idx = np.arange(BLK, dtype=np.float64)
    diff = idx[:, None] - idx[None, :]
    dmask = np.zeros((2, BLK, 2 * BLK), f32)
    qdec = np.zeros((2, BLK, LANES), f32)
    kdec = np.zeros((2, BLK, LANES), f32)
    cdm = np.zeros((2, LANES, LANES), f32)
    rbm = ((np.arange(LANES)[:, None] // RET_DK) == (np.arange(LANES)[None, :] // RET_DV)).astype(f32)
    for p in range(2):
        for hh in range(2):
            lg = log_g[2 * p + hh]
            dmask[p, :, hh * BLK:(hh + 1) * BLK] = kscale * np.where(diff >= 0, np.exp(lg * np.maximum(diff, 0.0)), 0.0)
        lg_lane = log_g[2 * p + grp]
        qdec[p] = np.exp(lg_lane[None, :] * (idx[:, None] + 1.0))
        kdec[p] = kscale * np.exp(lg_lane[None, :] * (BLK - 1.0 - idx[:, None]))
        cdm[p] = np.exp(lg_lane * BLK)[:, None] * rbm

    t = np.arange(BLK)
    tri = (((t[:, None] // GLA_C) == (t[None, :] // GLA_C)) & (t[None, :] <= t[:, None])).astype(f32)
    ci = np.arange(GLA_C)[:, None]
    cj = np.arange(GLA_C)[None, :]
    x = ci ^ cj
    lvl = np.where(cj > ci, -1, np.where(x == 0, 0, np.floor(np.log2(np.maximum(x, 1))).astype(np.int64) + 1))
    lmask = np.stack([np.tile((lvl == lv).astype(f32), (1, GLA_HEADS)) for lv in range(GLA_LEVELS + 1)])
    smask = ((np.arange(GLA_W)[:, None] // GLA_DV) == (np.arange(LANES)[None, :] // GLA_DK)).astype(f32)
    e64 = ((np.arange(RET_W)[:, None] // RET_DV) == (np.arange(RET_W)[None, :] // RET_DV)).astype(f32) / RET_DV

    return dict(bias=bias, vones=vones, dmask=dmask, qdec=qdec, kdec=kdec, cdm=cdm, rbm=rbm, tri=tri,
                lmask=lmask, smask=smask, e64=e64)


_C = _build_constants()


def _dot(a, b):
    return jnp.dot(a, b, preferred_element_type=jnp.float32)


def _dot_nt(a, b):
    return lax.dot_general(a, b, (((1,), (1,)), ((), ())), preferred_element_type=jnp.float32)


def _dot_tn(a, b):
    return lax.dot_general(a, b, (((0,), (0,)), ((), ())), preferred_element_type=jnp.float32)


def _bf(x):
    return x.astype(jnp.bfloat16)


def _split_hi_lo(x):
    hi = _bf(x)
    lo = _bf(x - hi.astype(jnp.float32))
    return hi, lo


def _silu(x):
    half = 0.5 * x
    return half + half * jnp.tanh(half)


def _run_interleaved(gens, done):
    active = [[g, None] for g in gens]
    while active:
        progressed = False
        for item in list(active):
            if item[1] is not None and item[1] not in done:
                continue
            progressed = True
            try:
                need = next(item[0])
                while need is not None and need in done:
                    need = next(item[0])
                item[1] = need
            except StopIteration:
                active.remove(item)
        assert progressed, [item[1] for item in active]


MOD_TK = 256
TRIG_T = 512


def _mod_step(c_ref, w_ref, b_ref, o_ref, first):
    @pl.when(first)
    def _():
        o_ref[0] = jnp.broadcast_to(b_ref[0], o_ref.shape[1:])

    a = _silu(c_ref[...])
    a_hi, a_lo = _split_hi_lo(a)
    w_hi, w_lo = _split_hi_lo(w_ref[0])
    n = a.shape[0]
    both = _dot(jnp.concatenate([a_hi, a_lo], axis=0), w_hi)
    o_ref[0] += both[:n] + both[n:] + _dot(a_hi, w_lo)


def _trig_tile(pos_ref, freq_ref, out_refs, row0):
    ca_ref, sa_ref, cr_ref, sr_ref = out_refs
    half = TRIG_T // 2
    lane = lax.broadcasted_iota(jnp.int32, (half, LANES), 1)
    pos_a = pos_ref[row0:row0 + half, :]
    pos_b = pos_ref[row0 + half:row0 + TRIG_T, :]
    ang = jnp.where(lane < 2 * HALF, pos_a, pos_b).astype(jnp.float32) * freq_ref[...]
    c = jnp.cos(ang)
    s = jnp.sin(ang)
    quarter = lane // HALF
    sign = jnp.where((lane % HEAD_DIM) < HALF, -1.0, 1.0)
    rc = [c] + [pltpu.roll(c, k * HALF, axis=1) for k in (1, 2, 3)]
    rs = [s] + [pltpu.roll(s, k * HALF, axis=1) for k in (1, 2, 3)]

    def spread(r, src):
        out = r[(0 - src) % 4]
        for q in (1, 2, 3):
            out = jnp.where(quarter == q, r[(q - src) % 4], out)
        return out

    for tok, rws in ((0, slice(row0, row0 + half)), (1, slice(row0 + half, row0 + TRIG_T))):
        ca_ref[rws, :] = spread(rc, 2 * tok)
        sa_ref[rws, :] = spread(rs, 2 * tok) * sign
        cr_ref[rws, :] = spread(rc, 2 * tok + 1)
        sr_ref[rws, :] = spread(rs, 2 * tok + 1) * sign


def _prologue_kernel(c_ref, w_ref, b_ref, pos_ref, freq_ref, wo_ref, mod_ref, wob_ref, *table_refs, k_tiles, tiles):
    _mod_step(c_ref, w_ref, b_ref, mod_ref, pl.program_id(0) % k_tiles == 0)
    wob_ref[...] = _bf(wo_ref[...])
    for j in range(tiles):
        _trig_tile(pos_ref, freq_ref, table_refs, j * TRIG_T)


def _prologue(c, w_mod, b_mod, positions, freq, w_out):
    depth, d, n = w_mod.shape
    bsz, seq = positions.shape
    k_tiles = d // MOD_TK
    steps = depth * k_tiles
    tiles = (bsz * seq) // (TRIG_T * steps)
    rows = tiles * TRIG_T
    assert rows * steps == bsz * seq
    table = jax.ShapeDtypeStruct((bsz * seq, LANES), jnp.float32)
    tspec = pl.BlockSpec((rows, LANES), lambda i: (i, 0))
    wo_spec = pl.BlockSpec((1, w_out.shape[1] // k_tiles, w_out.shape[2]), lambda i: (i // k_tiles, i % k_tiles, 0))
    outs = pl.pallas_call(
        functools.partial(_prologue_kernel, k_tiles=k_tiles, tiles=tiles),
        out_shape=(jax.ShapeDtypeStruct((depth, bsz, n), jnp.float32),
                   jax.ShapeDtypeStruct(w_out.shape, jnp.bfloat16), table, table, table, table),
        grid=(steps,),
        in_specs=[pl.BlockSpec((bsz, MOD_TK), lambda i: (0, i % k_tiles)),
                  pl.BlockSpec((1, MOD_TK, n), lambda i: (i // k_tiles, i % k_tiles, 0)),
                  pl.BlockSpec((1, 1, n), lambda i: (i // k_tiles, 0, 0)),
                  pl.BlockSpec((rows, 1), lambda i: (i, 0)),
                  pl.BlockSpec((1, LANES), lambda i: (0, 0)),
                  wo_spec],
        out_specs=(pl.BlockSpec((1, bsz, n), lambda i: (i // k_tiles, 0, 0)), wo_spec, tspec, tspec, tspec, tspec),
        compiler_params=pltpu.CompilerParams(dimension_semantics=("arbitrary",), vmem_limit_bytes=VMEM_LIMIT),
        name="prologue",
    )(c, w_mod, b_mod.reshape(depth, 1, n), positions.reshape(bsz * seq, 1), freq, w_out)
    return outs[0], outs[1], tuple(t.reshape(bsz, seq, LANES) for t in outs[2:])


def _seg_mean(x, e_ref):
    return _dot(_bf(x), e_ref[...])


def _rot(v, c, s, first_half):
    partner = jnp.where(first_half, pltpu.roll(v, LANES - HALF, axis=1), pltpu.roll(v, HALF, axis=1))
    return v * c + partner * s


def _layer_kernel(sinks_ref,
                  x_ref, mod_ref, pre_ref, post_ref, win_ref, wout_ref, gw_ref, gb_ref, gng_ref,
                  ca_ref, sa_ref, cr_ref, sr_ref,
                  bias_ref, vones_ref, dmask_ref, qdec_ref, kdec_ref, cdm_ref, rbm_ref,
                  tri_ref, lmask_ref, smask_ref, e64_ref,
                  o_ref, *scratch):
    hb_grp = scratch[0:NGRP]
    proj_grp = scratch[NGRP:2 * NGRP]
    mix_grp = scratch[2 * NGRP:3 * NGRP]
    kprev_ref, vprev_ref, rstate_ref, gstate_ref, bscr_ref = scratch[3 * NGRP:]
    t = pl.program_id(1)

    def sub_view(refs, sb):
        return refs[sb // PGRP].at[pl.ds((sb % PGRP) * BLK, BLK)]

    hb_refs = [sub_view(hb_grp, sb) for sb in range(NSUB)]
    proj_refs = [sub_view(proj_grp, sb) for sb in range(NSUB)]
    mix_refs = [sub_view(mix_grp, sb) for sb in range(NSUB)]

    @pl.when(t == 0)
    def _():
        kprev_ref[...] = jnp.zeros_like(kprev_ref)
        vprev_ref[...] = jnp.zeros_like(vprev_ref)
        rstate_ref[...] = jnp.zeros_like(rstate_ref)
        gstate_ref[...] = jnp.zeros_like(gstate_ref)
        for g in range(NGRP):
            proj_grp[g][:, O_GA:] = jnp.zeros((PGRP * BLK, N_PROJ - O_GA), jnp.float32)

    lane = lax.broadcasted_iota(jnp.int32, (1, LANES), 1)
    lane2 = lax.broadcasted_iota(jnp.int32, (1, GLA_W), 1)
    zero_bf = jnp.zeros((), jnp.bfloat16)
    first_half = (lane % HEAD_DIM) < HALF
    lo_hi = [lane // HEAD_DIM == g for g in range(2)]
    hgrp = [lane // GLA_DK == h for h in range(GLA_HEADS)]
    vhead = [lane2 // GLA_DV == h for h in range(GLA_HEADS)]

    def rows(sb):
        return slice(sb * BLK, (sb + 1) * BLK)

    done = set()

    def gen_norm(sb):
        if sb >= PGRP:
            yield f"projgrp{sb // PGRP - 1}.started"
        x = x_ref[0, rows(sb), :]
        inv = lax.rsqrt(jnp.mean(x * x, axis=-1, keepdims=True) + EPS)
        hb_refs[sb][...] = _bf((x * inv) * (pre_ref[...] * (1.0 + mod_ref[1:2, :])) + mod_ref[0:1, :])
        done.add(f"norm{sb}")

    def gen_proj(g):
        subs = range(g * PGRP, (g + 1) * PGRP)
        for sb in subs:
            yield f"norm{sb}"
        if g > 0:
            yield f"projgrp{g - 1}.all"
        done.add(f"projgrp{g}.started")
        for name, a, b in PROJ_SEGMENTS:
            proj_grp[g][:, a:b] = _dot(hb_grp[g][...], win_ref[:, a:b])
            done.update(f"proj{sb}.{name}" for sb in subs)
            yield
        done.add(f"projgrp{g}.all")

    def gen_attn(sb):
        yield f"proj{sb}.akv"
        if sb > 0:
            yield f"attn{sb - 1}.kv"
        pr = proj_refs[sb]
        ca, sa = ca_ref[0, rows(sb), :], sa_ref[0, rows(sb), :]
        k_new = _rot(pr[:, O_AK:O_AK + 128], ca, sa, first_half) * (HEAD_DIM ** -0.5 * LOG2E)
        v_new = pr[:, O_AV:O_AV + 128]
        k_sw = pltpu.roll(k_new, HEAD_DIM, axis=1)
        v_sw = pltpu.roll(v_new, HEAD_DIM, axis=1)
        k2, v2 = [], []
        for g in range(2):
            k_g = _bf(jnp.where(lo_hi[g], k_new, k_sw))
            v_g = _bf(jnp.where(lo_hi[g], v_new, v_sw))
            kcat = jnp.concatenate([kprev_ref[g], k_g], axis=0)
            vcat = jnp.concatenate([vprev_ref[g], v_g], axis=0)
            kprev_ref[g] = k_g
            vprev_ref[g] = v_g
            k2.append(jnp.concatenate([jnp.where(lo_hi[h], kcat, zero_bf) for h in range(2)], axis=0))
            v2_g = jnp.concatenate([jnp.where(lo_hi[h], vcat, zero_bf) for h in range(2)], axis=0)
            v2.append(jnp.concatenate([v2_g, vones_ref[...]], axis=1))
        done.add(f"attn{sb}.kv")
        if sb == 0:
            bias = bias_ref[jnp.minimum(t, 1)]
        else:
            bias = bias_ref[1]

        def scores(g):
            q = [_rot(pr[:, O_AQ + m * 128:O_AQ + (m + 1) * 128], ca, sa, first_half) for m in (2 * g, 2 * g + 1)]
            return _dot_nt(_bf(jnp.concatenate(q, axis=0)), k2[g])

        yield f"proj{sb}.aq"
        s_g = scores(0)
        yield
        for g in range(2):
            ps, mxs = [], []
            for j in range(2):
                row_p, row_mx = [], []
                for h in range(2):
                    s = s_g[j * BLK:(j + 1) * BLK, h * 2 * BLK:(h + 1) * 2 * BLK] + bias
                    mx = jnp.maximum(jnp.max(s, axis=-1, keepdims=True), sinks_ref[2 * (2 * g + j) + h] * LOG2E)
                    row_p.append(_bf(jnp.exp2(s - mx)))
                    row_mx.append(mx)
                ps.append(jnp.concatenate(row_p, axis=1))
                mxs.append(row_mx)
            pv = _dot(jnp.concatenate(ps, axis=0), v2[g])
            if g == 0:
                s_g = scores(1)
            yield
            yield f"proj{sb}.ag"
            for j in range(2):
                m = 2 * g + j
                pv_m = pv[j * BLK:(j + 1) * BLK]
                mx_l = jnp.where(lo_hi[0], mxs[j][0], mxs[j][1])
                sink_l = jnp.where(lo_hi[0], sinks_ref[2 * m], sinks_ref[2 * m + 1]) * LOG2E
                den = pv_m[:, 128:] + jnp.exp2(sink_l - mx_l)
                gate_m = _silu(pr[:, O_AG + m * 128:O_AG + (m + 1) * 128])
                mix_refs[sb][:, m * 128:(m + 1) * 128] = _bf(pv_m[:, :128] / den * gate_m)
        done.add(f"attn{sb}.done")

    def gen_ret(sb):
        yield f"proj{sb}.rqk"
        yield f"proj{sb}.rvg"
        if sb > 0:
            yield f"ret{sb - 1}.state"
        pr = proj_refs[sb]
        cr, sr = cr_ref[0, rows(sb), :], sr_ref[0, rows(sb), :]
        held = []
        for p in range(2):
            q_p = _rot(pr[:, O_RQ + p * 128:O_RQ + (p + 1) * 128], cr, sr, first_half)
            k_p = _rot(pr[:, O_RK + p * 128:O_RK + (p + 1) * 128], cr, sr, first_half)
            v_p = _bf(pr[:, O_RV + p * 128:O_RV + (p + 1) * 128])
            k_pb = _bf(k_p)
            kbd = jnp.concatenate([jnp.where(lo_hi[h], k_pb, zero_bf) for h in range(2)], axis=0)
            vbd = jnp.concatenate([jnp.where(lo_hi[h], v_p, zero_bf) for h in range(2)], axis=0)
            sc_raw = _dot_nt(_bf(q_p), kbd)
            st = rstate_ref[p]
            inter = _dot(_bf(q_p * qdec_ref[p]), _bf(st))
            upd = _dot_tn(_bf(k_p * kdec_ref[p]), v_p)
            held.append((sc_raw, vbd, st, inter, upd))
        yield
        intras = []
        for p in range(2):
            sc_raw, vbd, st, inter, upd = held[p]
            intras.append(_dot(_bf(sc_raw * dmask_ref[p]), vbd) + inter)
            rstate_ref[p] = st * cdm_ref[p] + upd * rbm_ref[...]
        done.add(f"ret{sb}.state")
        yield
        r = jnp.concatenate(intras, axis=1)
        ms = _seg_mean(r * r, e64_ref)
        yield
        mix_refs[sb][:, ATTN_W:ATTN_W + RET_W] = _bf(r * lax.rsqrt(ms + EPS) * _silu(pr[:, O_RG:O_RG + 256]))
        done.add(f"ret{sb}.done")

    def gen_gla(sb):
        yield f"proj{sb}.gga"
        pr = proj_refs[sb]
        logits = _dot(_bf(pr[:, O_GA:O_GA + 128]), gw_ref[...])
        yield
        logits = logits + gb_ref[...]
        log_a = (jnp.minimum(logits, 0.0) - jnp.log(1.0 + jnp.exp(-jnp.abs(logits)))) * (LOG2E / GLA_GATE_NORMALIZER)
        la_hi, la_lo = _split_hi_lo(log_a)
        xr = _dot(tri_ref[...], jnp.concatenate([la_hi, la_lo], axis=1))
        yield
        yield f"proj{sb}.gqkv"
        bscr = bscr_ref.at[sb]
        bscr[...] = xr[:, :128] + xr[:, 128:]
        gq = pr[:, O_GQ:O_GQ + 128] * (GLA_DK ** -0.5)
        gk = pr[:, O_GK:O_GK + 128]
        gv = _bf(pr[:, O_GV:O_GV + 256])
        nch = BLK // GLA_C
        row = lax.broadcasted_iota(jnp.int32, (GLA_C, LANES), 0)

        def level_factor(c, lv):
            r0 = c * GLA_C
            b_c = bscr[r0:r0 + GLA_C, :]
            grp = 1 << lv
            if grp >= 8:
                mids = [bscr[r0 + g0 + grp // 2 - 1:r0 + g0 + grp // 2, :] for g0 in range(0, GLA_C, grp)]
                ref = jnp.concatenate([jnp.broadcast_to(m, (grp, LANES)) for m in mids], axis=0)
            elif grp == 4:
                r = row % 4
                ref = jnp.where(r == 0, pltpu.roll(b_c, GLA_C - 1, axis=0),
                                jnp.where(r == 1, b_c,
                                          jnp.where(r == 2, pltpu.roll(b_c, 1, axis=0), pltpu.roll(b_c, 2, axis=0))))
            else:
                ref = jnp.where(row % 2 == 1, pltpu.roll(b_c, 1, axis=0), b_c)
            return jnp.exp2(-jnp.abs(b_c - ref))

        saccs = [jnp.zeros((GLA_C, GLA_HEADS * GLA_C), jnp.float32) for _ in range(nch)]
        pending = []
        for lv in range(GLA_LEVELS + 1):
            raw = []
            for c in range(nch):
                r0 = c * GLA_C
                if lv == 0:
                    qt, kt = _bf(gq[r0:r0 + GLA_C]), _bf(gk[r0:r0 + GLA_C])
                else:
                    f = level_factor(c, lv)
                    qt, kt = _bf(gq[r0:r0 + GLA_C] * f), _bf(gk[r0:r0 + GLA_C] * f)
                kbd = jnp.concatenate([jnp.where(hgrp[h], kt, zero_bf) for h in range(GLA_HEADS)], axis=0)
                raw.append(_dot_nt(qt, kbd))
            pending.append((lv, raw))
            if len(pending) == GLA_LEVELS_PER_STAGE or lv == GLA_LEVELS:
                yield
                for lv_p, raw_p in pending:
                    for c in range(nch):
                        saccs[c] = saccs[c] + raw_p[c] * lmask_ref[lv_p]
                pending = []
        if sb > 0:
            yield f"gla{sb - 1}.state"
        g_parts = []
        st = gstate_ref[...]
        for c in range(nch):
            r0 = c * GLA_C
            v_c = gv[r0:r0 + GLA_C]
            vbd = jnp.concatenate([jnp.where(vhead[h], v_c, zero_bf) for h in range(GLA_HEADS)], axis=0)
            b_c = bscr[r0:r0 + GLA_C, :]
            b_last = bscr[r0 + GLA_C - 1:r0 + GLA_C, :]
            q_in = _bf(gq[r0:r0 + GLA_C] * jnp.exp2(b_c))
            o_c = _dot(_bf(saccs[c]), vbd) + _dot_nt(q_in, _bf(st))
            k_out = _bf(gk[r0:r0 + GLA_C] * jnp.exp2(b_last - b_c))
            upd = _dot_tn(v_c, k_out)
            yield
            g_parts.append(o_c)
            st = st * jnp.exp2(b_last) + upd * smask_ref[...]
        gstate_ref[...] = st
        done.add(f"gla{sb}.state")
        g = jnp.concatenate(g_parts, axis=0)
        ms = _seg_mean(g * g, e64_ref)
        yield
        gate_g = _silu(pr[:, O_GG:O_GG + 256])
        mix_refs[sb][:, ATTN_W + RET_W:] = _bf(g * lax.rsqrt(ms + EPS) * gng_ref[...] * gate_g)
        done.add(f"gla{sb}.done")

    def gen_out(g):
        for sb in range(g * PGRP, (g + 1) * PGRP):
            yield f"attn{sb}.done"
            yield f"ret{sb}.done"
            yield f"gla{sb}.done"
        rws = slice(g * PGRP * BLK, (g + 1) * PGRP * BLK)
        half = D_MODEL // 2
        y0 = _dot(mix_grp[g][...], wout_ref[:, :half])
        yield
        y1 = _dot(mix_grp[g][...], wout_ref[:, half:])
        yield
        ssq = jnp.sum(y0 * y0, axis=-1, keepdims=True) + jnp.sum(y1 * y1, axis=-1, keepdims=True)
        inv = lax.rsqrt(ssq * (1.0 / D_MODEL) + EPS)
        gp = mod_ref[2:3, :] * post_ref[...]
        o_ref[0, rws, :half] = x_ref[0, rws, :half] + (y0 * inv) * gp[:, :half]
        yield
        o_ref[0, rws, half:] = x_ref[0, rws, half:] + (y1 * inv) * gp[:, half:]

    gens = []
    for g in range(NGRP):
        for sb in range(g * PGRP, (g + 1) * PGRP):
            gens += [gen_gla(sb), gen_attn(sb), gen_ret(sb)]
        gens += [gen_out(g), gen_proj(g)] + [gen_norm(sb) for sb in range(g * PGRP, (g + 1) * PGRP)]
    _run_interleaved(gens, done)


def _const_spec(shape):
    nd = len(shape)
    return pl.BlockSpec(shape, lambda b, t, s, _n=nd: (0,) * _n)


def _layer(l, x, mod, pre_gain, post_gain, w_in_b, w_out_b, sinks, gate_w_p, gate_b, gng, tables, consts):
    bsz, seq, d = x.shape
    tb = NSUB * BLK
    tok_spec = pl.BlockSpec((1, tb, LANES), lambda b, t, s: (b, t, 0))
    stacked = [pre_gain, post_gain, w_in_b, w_out_b, gate_w_p, gate_b, gng]

    def layer_spec(a):
        return pl.BlockSpec((None,) + a.shape[1:], lambda b, t, s, _n=a.ndim - 1: (l,) + (0,) * _n)

    in_specs = ([pl.BlockSpec((1, tb, d), lambda b, t, s: (b, t, 0)),
                 pl.BlockSpec((None, None, 3, d), lambda b, t, s: (l, b, 0, 0))]
                + [layer_spec(a) for a in stacked]
                + [tok_spec] * 4
                + [_const_spec(a.shape) for a in consts])
    scratch = ([pltpu.VMEM((PGRP * BLK, d), jnp.bfloat16)] * NGRP
               + [pltpu.VMEM((PGRP * BLK, N_PROJ), jnp.float32)] * NGRP
               + [pltpu.VMEM((PGRP * BLK, d), jnp.bfloat16)] * NGRP
               + [pltpu.VMEM((2, BLK, LANES), jnp.bfloat16),
                  pltpu.VMEM((2, BLK, LANES), jnp.bfloat16),
                  pltpu.VMEM((2, LANES, LANES), jnp.float32),
                  pltpu.VMEM((GLA_W, LANES), jnp.float32),
                  pltpu.VMEM((NSUB, BLK, LANES), jnp.float32)])
    grid_spec = pltpu.PrefetchScalarGridSpec(
        num_scalar_prefetch=1,
        grid=(bsz, seq // tb),
        in_specs=in_specs,
        out_specs=pl.BlockSpec((1, tb, d), lambda b, t, s: (b, t, 0)),
        scratch_shapes=scratch)
    return pl.pallas_call(
        _layer_kernel,
        out_shape=jax.ShapeDtypeStruct(x.shape, x.dtype),
        grid_spec=grid_spec,
        compiler_params=pltpu.CompilerParams(dimension_semantics=("arbitrary", "arbitrary"),
                                             vmem_limit_bytes=VMEM_LIMIT),
        name="hybrid_layer",
    )(sinks, x, mod, *stacked, *tables, *consts)


def kernel(x, c, positions, w_mod, b_mod, pre_norm_gain, post_norm_gain, w_in, attn_sinks, gla_gate_w,
           gla_gate_b, gla_norm_gain, w_out):
    depth = w_mod.shape[0]
    bsz = x.shape[0]
    f32, bf16 = jnp.float32, jnp.bfloat16

    rope_freq = ROPE_THETA ** (-jnp.arange(0, HEAD_DIM, 2, dtype=f32) / HEAD_DIM)
    ret_freq = 1.0 / (10000.0 ** jnp.linspace(0.0, 1.0, RET_DK // 2, dtype=f32))
    freq = jnp.concatenate([rope_freq, ret_freq, rope_freq, ret_freq]).reshape(1, LANES)
    mod, w_out_b, tables = _prologue(c, w_mod, b_mod, positions, freq, w_out)
    mod = mod.reshape(depth, bsz, 3, D_MODEL)

    consts = [jnp.asarray(_C["bias"]), jnp.asarray(_C["vones"], bf16), jnp.asarray(_C["dmask"]),
              jnp.asarray(_C["qdec"]), jnp.asarray(_C["kdec"]), jnp.asarray(_C["cdm"]), jnp.asarray(_C["rbm"]),
              jnp.asarray(_C["tri"], bf16), jnp.asarray(_C["lmask"]), jnp.asarray(_C["smask"]),
              jnp.asarray(_C["e64"], bf16)]
    w_in_b = w_in.astype(bf16)
    gate_w_p = jnp.pad(gla_gate_w, ((0, 0), (0, LANES - GLA_GATE_RANK), (0, 0))).astype(bf16)
    gng = jnp.tile(gla_norm_gain, (1, GLA_HEADS)).reshape(depth, 1, GLA_W)
    pre = pre_norm_gain.reshape(depth, 1, D_MODEL)
    post = post_norm_gain.reshape(depth, 1, D_MODEL)
    gate_b = gla_gate_b.reshape(depth, 1, LANES)

    for l in range(depth):
        x = _layer(l, x, mod, pre, post, w_in_b, w_out_b, attn_sinks[l], gate_w_p, gate_b, gng, tables, consts)
    return x
```

```python
import functools

import numpy as np
import jax
import jax.numpy as jnp
from jax import lax
from jax.experimental import pallas as pl
from jax.experimental.pallas import tpu as pltpu

D_MODEL = 1024
HEAD_DIM = 64
ATTN_HEADS = 8
ATTN_KV_HEADS = 2
WINDOW = 128
ROPE_THETA = 10000.0
RET_HEADS = 4
RET_DK = 64
RET_DV = 64
GLA_HEADS = 4
GLA_DK = 32
GLA_DV = 64
GLA_GATE_RANK = 16
GLA_GATE_NORMALIZER = 16.0
EPS = 1e-6

ATTN_W = ATTN_HEADS * HEAD_DIM
RET_W = RET_HEADS * RET_DV
GLA_W = GLA_HEADS * GLA_DV

LANES = 128
HALF = HEAD_DIM // 2
BLK = 128
NSUB = 8
PGRP = 4
NGRP = NSUB // PGRP
GLA_C = 64
GLA_LEVELS = 6
GLA_LEVELS_PER_STAGE = 4
NEG = -1e30
VMEM_LIMIT = 56 * 1024 * 1024
LOG2E = 1.4426950408889634

O_AQ, O_AK, O_AV, O_AG = 0, 512, 640, 768
O_RQ, O_RK, O_RV, O_RG = 1280, 1536, 1792, 2048
O_GQ, O_GK, O_GV, O_GG, O_GA = 2304, 2432, 2560, 2816, 3072
D_IN = O_GA + GLA_GATE_RANK
N_PROJ = 3200
PROJ_SEGMENTS = (("gga", O_GG, D_IN), ("gqkv", O_GQ, O_GG), ("akv", O_AK, O_AG), ("aq", O_AQ, O_AK),
                 ("rqk", O_RQ, O_RV), ("rvg", O_RV, O_GQ), ("ag", O_AG, O_RQ))


def _build_constants():
    f32 = np.float32
    lane = np.arange(LANES)
    grp = lane // HEAD_DIM

    i = np.arange(BLK)[:, None]
    j = np.arange(2 * BLK)[None, :]
    rel = i + BLK - j
    ok = (rel >= 0) & (rel < WINDOW)
    bias = np.stack([np.where(ok & (j >= BLK), 0.0, NEG), np.where(ok, 0.0, NEG)]).astype(f32)
    vones = np.zeros((2 * 2 * BLK, LANES), f32)
    for g in range(2):
        vones[g * 2 * BLK:(g + 1) * 2 * BLK] = (lane // HEAD_DIM == g)[None, :]

    kscale = RET_DK ** -0.5
    log_g = np.log(1.0 - 2.0 ** (-5.0 - np.arange(RET_HEADS, dtype=np.float64)))
    idx = np.arange(BLK, dtype=np.float64)
    diff = idx[:, None] - idx[None, :]
    dmask = np.zeros((2, BLK, 2 * BLK), f32)
    qdec = np.zeros((2, BLK, LANES), f32)
    kdec = np.zeros((2, BLK, LANES), f32)
    cdm = np.zeros((2, LANES, LANES), f32)
    rbm = ((np.arange(LANES)[:, None] // RET_DK) == (np.arange(LANES)[None, :] // RET_DV)).astype(f32)
    for p in range(2):
        for hh in range(2):
            lg = log_g[2 * p + hh]
            dmask[p, :, hh * BLK:(hh + 1) * BLK] = kscale * np.where(diff >= 0, np.exp(lg * np.maximum(diff, 0.0)), 0.0)
        lg_lane = log_g[2 * p + grp]
        qdec[p] = np.exp(lg_lane[None, :] * (idx[:, None] + 1.0))
        kdec[p] = kscale * np.exp(lg_lane[None, :] * (BLK - 1.0 - idx[:, None]))
        cdm[p] = np.exp(lg_lane * BLK)[:, None] * rbm

    t = np.arange(BLK)
    tri = (((t[:, None] // GLA_C) == (t[None, :] // GLA_C)) & (t[None, :] <= t[:, None])).astype(f32)
    ci = np.arange(GLA_C)[:, None]
    cj = np.arange(GLA_C)[None, :]
    x = ci ^ cj
    lvl = np.where(cj > ci, -1, np.where(x == 0, 0, np.floor(np.log2(np.maximum(x, 1))).astype(np.int64) + 1))
    lmask = np.stack([np.tile((lvl == lv).astype(f32), (1, GLA_HEADS)) for lv in range(GLA_LEVELS + 1)])
    smask = ((np.arange(GLA_W)[:, None] // GLA_DV) == (np.arange(LANES)[None, :] // GLA_DK)).astype(f32)
    e64 = ((np.arange(RET_W)[:, None] // RET_DV) == (np.arange(RET_W)[None, :] // RET_DV)).astype(f32) / RET_DV

    return dict(bias=bias, vones=vones, dmask=dmask, qdec=qdec, kdec=kdec, cdm=cdm, rbm=rbm, tri=tri,
                lmask=lmask, smask=smask, e64=e64)


_C = _build_constants()


def _dot(a, b):
    return jnp.dot(a, b, preferred_element_type=jnp.float32)


def _dot_nt(a, b):
    return lax.dot_general(a, b, (((1,), (1,)), ((), ())), preferred_element_type=jnp.float32)


def _dot_tn(a, b):
    return lax.dot_general(a, b, (((0,), (0,)), ((), ())), preferred_element_type=jnp.float32)


def _bf(x):
    return x.astype(jnp.bfloat16)


def _split_hi_lo(x):
    hi = _bf(x)
    lo = _bf(x - hi.astype(jnp.float32))
    return hi, lo


def _silu(x):
    half = 0.5 * x
    return half + half * jnp.tanh(half)


def _run_interleaved(gens, done):
    active = [[g, None] for g in gens]
    while active:
        progressed = False
        for item in list(active):
            if item[1] is not None and item[1] not in done:
                continue
            progressed = True
            try:
                need = next(item[0])
                while need is not None and need in done:
                    need = next(item[0])
                item[1] = need
            except StopIteration:
                active.remove(item)
        assert progressed, [item[1] for item in active]


MOD_TK = 256
TRIG_T = 512


def _mod_step(c_ref, w_ref, b_ref, o_ref, first):
    @pl.when(first)
    def _():
        o_ref[0] = jnp.broadcast_to(b_ref[0], o_ref.shape[1:])

    a = _silu(c_ref[...])
    a_hi, a_lo = _split_hi_lo(a)
    w_hi, w_lo = _split_hi_lo(w_ref[0])
    n = a.shape[0]
    both = _dot(jnp.concatenate([a_hi, a_lo], axis=0), w_hi)
    o_ref[0] += both[:n] + both[n:] + _dot(a_hi, w_lo)


def _trig_tile(pos_ref, freq_ref, out_refs, row0):
    ca_ref, sa_ref, cr_ref, sr_ref = out_refs
    half = TRIG_T // 2
    lane = lax.broadcasted_iota(jnp.int32, (half, LANES), 1)
    pos_a = pos_ref[row0:row0 + half, :]
    pos_b = pos_ref[row0 + half:row0 + TRIG_T, :]
    ang = jnp.where(lane < 2 * HALF, pos_a, pos_b).astype(jnp.float32) * freq_ref[...]
    c = jnp.cos(ang)
    s = jnp.sin(ang)
    quarter = lane // HALF
    sign = jnp.where((lane % HEAD_DIM) < HALF, -1.0, 1.0)
    rc = [c] + [pltpu.roll(c, k * HALF, axis=1) for k in (1, 2, 3)]
    rs = [s] + [pltpu.roll(s, k * HALF, axis=1) for k in (1, 2, 3)]

    def spread(r, src):
        out = r[(0 - src) % 4]
        for q in (1, 2, 3):
            out = jnp.where(quarter == q, r[(q - src) % 4], out)
        return out

    for tok, rws in ((0, slice(row0, row0 + half)), (1, slice(row0 + half, row0 + TRIG_T))):
        ca_ref[rws, :] = spread(rc, 2 * tok)
        sa_ref[rws, :] = spread(rs, 2 * tok) * sign
        cr_ref[rws, :] = spread(rc, 2 * tok + 1)
        sr_ref[rws, :] = spread(rs, 2 * tok + 1) * sign


def _prologue_kernel(c_ref, w_ref, b_ref, pos_ref, freq_ref, wo_ref, mod_ref, wob_ref, *table_refs, k_tiles, tiles):
    _mod_step(c_ref, w_ref, b_ref, mod_ref, pl.program_id(0) % k_tiles == 0)
    wob_ref[...] = _bf(wo_ref[...])
    for j in range(tiles):
        _trig_tile(pos_ref, freq_ref, table_refs, j * TRIG_T)


def _prologue(c, w_mod, b_mod, positions, freq, w_out):
    depth, d, n = w_mod.shape
    bsz, seq = positions.shape
    k_tiles = d // MOD_TK
    steps = depth * k_tiles
    tiles = (bsz * seq) // (TRIG_T * steps)
    rows = tiles * TRIG_T
    assert rows * steps == bsz * seq
    table = jax.ShapeDtypeStruct((bsz * seq, LANES), jnp.float32)
    tspec = pl.BlockSpec((rows, LANES), lambda i: (i, 0))
    wo_spec = pl.BlockSpec((1, w_out.shape[1] // k_tiles, w_out.shape[2]), lambda i: (i // k_tiles, i % k_tiles, 0))
    outs = pl.pallas_call(
        functools.partial(_prologue_kernel, k_tiles=k_tiles, tiles=tiles),
        out_shape=(jax.ShapeDtypeStruct((depth, bsz, n), jnp.float32),
                   jax.ShapeDtypeStruct(w_out.shape, jnp.bfloat16), table, table, table, table),
        grid=(steps,),
        in_specs=[pl.BlockSpec((bsz, MOD_TK), lambda i: (0, i % k_tiles)),
                  pl.BlockSpec((1, MOD_TK, n), lambda i: (i // k_tiles, i % k_tiles, 0)),
                  pl.BlockSpec((1, 1, n), lambda i: (i // k_tiles, 0, 0)),
                  pl.BlockSpec((rows, 1), lambda i: (i, 0)),
                  pl.BlockSpec((1, LANES), lambda i: (0, 0)),
                  wo_spec],
        out_specs=(pl.BlockSpec((1, bsz, n), lambda i: (i // k_tiles, 0, 0)), wo_spec, tspec, tspec, tspec, tspec),
        compiler_params=pltpu.CompilerParams(dimension_semantics=("arbitrary",), vmem_limit_bytes=VMEM_LIMIT),
        name="prologue",
    )(c, w_mod, b_mod.reshape(depth, 1, n), positions.reshape(bsz * seq, 1), freq, w_out)
    return outs[0], outs[1], tuple(t.reshape(bsz, seq, LANES) for t in outs[2:])


def _seg_mean(x, e_ref):
    return _dot(_bf(x), e_ref[...])


def _rot(v, c, s, first_half):
    partner = jnp.where(first_half, pltpu.roll(v, LANES - HALF, axis=1), pltpu.roll(v, HALF, axis=1))
    return v * c + partner * s


def _layer_kernel(sinks_ref,
                  x_ref, mod_ref, pre_ref, post_ref, win_ref, wout_ref, gw_ref, gb_ref, gng_ref,
                  ca_ref, sa_ref, cr_ref, sr_ref,
                  bias_ref, vones_ref, dmask_ref, qdec_ref, kdec_ref, cdm_ref, rbm_ref,
                  tri_ref, lmask_ref, smask_ref, e64_ref,
                  o_ref, *scratch):
    hb_grp = scratch[0:NGRP]
    proj_grp = scratch[NGRP:2 * NGRP]
    mix_grp = scratch[2 * NGRP:3 * NGRP]
    kprev_ref, vprev_ref, rstate_ref, gstate_ref, bscr_ref = scratch[3 * NGRP:]
    t = pl.program_id(1)

    def sub_view(refs, sb):
        return refs[sb // PGRP].at[pl.ds((sb % PGRP) * BLK, BLK)]

    hb_refs = [sub_view(hb_grp, sb) for sb in range(NSUB)]
    proj_refs = [sub_view(proj_grp, sb) for sb in range(NSUB)]
    mix_refs = [sub_view(mix_grp, sb) for sb in range(NSUB)]

    @pl.when(t == 0)
    def _():
        kprev_ref[...] = jnp.zeros_like(kprev_ref)
        vprev_ref[...] = jnp.zeros_like(vprev_ref)
        rstate_ref[...] = jnp.zeros_like(rstate_ref)
        gstate_ref[...] = jnp.zeros_like(gstate_ref)
        for g in range(NGRP):
            proj_grp[g][:, O_GA:] = jnp.zeros((PGRP * BLK, N_PROJ - O_GA), jnp.float32)

    lane = lax.broadcasted_iota(jnp.int32, (1, LANES), 1)
    lane2 = lax.broadcasted_iota(jnp.int32, (1, GLA_W), 1)
    zero_bf = jnp.zeros((), jnp.bfloat16)
    first_half = (lane % HEAD_DIM) < HALF
    lo_hi = [lane // HEAD_DIM == g for g in range(2)]
    hgrp = [lane // GLA_DK == h for h in range(GLA_HEADS)]
    vhead = [lane2 // GLA_DV == h for h in range(GLA_HEADS)]

    def rows(sb):
        return slice(sb * BLK, (sb + 1) * BLK)

    done = set()

    def gen_norm(sb):
        if sb >= PGRP:
            yield f"projgrp{sb // PGRP - 1}.started"
        x = x_ref[0, rows(sb), :]
        inv = lax.rsqrt(jnp.mean(x * x, axis=-1, keepdims=True) + EPS)
        hb_refs[sb][...] = _bf((x * inv) * (pre_ref[...] * (1.0 + mod_ref[1:2, :])) + mod_ref[0:1, :])
        done.add(f"norm{sb}")

    def gen_proj(g):
        subs = range(g * PGRP, (g + 1) * PGRP)
        for sb in subs:
            yield f"norm{sb}"
        if g > 0:
            yield f"projgrp{g - 1}.all"
        done.add(f"projgrp{g}.started")
        for name, a, b in PROJ_SEGMENTS:
            proj_grp[g][:, a:b] = _dot(hb_grp[g][...], win_ref[:, a:b])
            done.update(f"proj{sb}.{name}" for sb in subs)
            yield
        done.add(f"projgrp{g}.all")

    def gen_attn(sb):
        yield f"proj{sb}.akv"
        if sb > 0:
            yield f"attn{sb - 1}.kv"
        pr = proj_refs[sb]
        ca, sa = ca_ref[0, rows(sb), :], sa_ref[0, rows(sb), :]
        k_new = _rot(pr[:, O_AK:O_AK + 128], ca, sa, first_half) * (HEAD_DIM ** -0.5 * LOG2E)
        v_new = pr[:, O_AV:O_AV + 128]
        k_sw = pltpu.roll(k_new, HEAD_DIM, axis=1)
        v_sw = pltpu.roll(v_new, HEAD_DIM, axis=1)
        k2, v2 = [], []
        for g in range(2):
            k_g = _bf(jnp.where(lo_hi[g], k_new, k_sw))
            v_g = _bf(jnp.where(lo_hi[g], v_new, v_sw))
            kcat = jnp.concatenate([kprev_ref[g], k_g], axis=0)
            vcat = jnp.concatenate([vprev_ref[g], v_g], axis=0)
            kprev_ref[g] = k_g
            vprev_ref[g] = v_g
            k2.append(jnp.concatenate([jnp.where(lo_hi[h], kcat, zero_bf) for h in range(2)], axis=0))
            v2_g = jnp.concatenate([jnp.where(lo_hi[h], vcat, zero_bf) for h in range(2)], axis=0)
            v2.append(jnp.concatenate([v2_g, vones_ref[...]], axis=1))
        done.add(f"attn{sb}.kv")
        if sb == 0:
            bias = bias_ref[jnp.minimum(t, 1)]
        else:
            bias = bias_ref[1]

        def scores(g):
            q = [_rot(pr[:, O_AQ + m * 128:O_AQ + (m + 1) * 128], ca, sa, first_half) for m in (2 * g, 2 * g + 1)]
            return _dot_nt(_bf(jnp.concatenate(q, axis=0)), k2[g])

        yield f"proj{sb}.aq"
        s_g = scores(0)
        yield
        for g in range(2):
            ps, mxs = [], []
            for j in range(2):
                row_p, row_mx = [], []
                for h in range(2):
                    s = s_g[j * BLK:(j + 1) * BLK, h * 2 * BLK:(h + 1) * 2 * BLK] + bias
                    mx = jnp.maximum(jnp.max(s, axis=-1, keepdims=True), sinks_ref[2 * (2 * g + j) + h] * LOG2E)
                    row_p.append(_bf(jnp.exp2(s - mx)))
                    row_mx.append(mx)
                ps.append(jnp.concatenate(row_p, axis=1))
                mxs.append(row_mx)
            pv = _dot(jnp.concatenate(ps, axis=0), v2[g])
            if g == 0:
                s_g = scores(1)
            yield
            yield f"proj{sb}.ag"
            for j in range(2):
                m = 2 * g + j
                pv_m = pv[j * BLK:(j + 1) * BLK]
                mx_l = jnp.where(lo_hi[0], mxs[j][0], mxs[j][1])
                sink_l = jnp.where(lo_hi[0], sinks_ref[2 * m], sinks_ref[2 * m + 1]) * LOG2E
                den = pv_m[:, 128:] + jnp.exp2(sink_l - mx_l)
                gate_m = _silu(pr[:, O_AG + m * 128:O_AG + (m + 1) * 128])
                mix_refs[sb][:, m * 128:(m + 1) * 128] = _bf(pv_m[:, :128] / den * gate_m)
        done.add(f"attn{sb}.done")

    def gen_ret(sb):
        yield f"proj{sb}.rqk"
        yield f"proj{sb}.rvg"
        if sb > 0:
            yield f"ret{sb - 1}.state"
        pr = proj_refs[sb]
        cr, sr = cr_ref[0, rows(sb), :], sr_ref[0, rows(sb), :]
        held = []
        for p in range(2):
            q_p = _rot(pr[:, O_RQ + p * 128:O_RQ + (p + 1) * 128], cr, sr, first_half)
            k_p = _rot(pr[:, O_RK + p * 128:O_RK + (p + 1) * 128], cr, sr, first_half)
            v_p = _bf(pr[:, O_RV + p * 128:O_RV + (p + 1) * 128])
            k_pb = _bf(k_p)
            kbd = jnp.concatenate([jnp.where(lo_hi[h], k_pb, zero_bf) for h in range(2)], axis=0)
            vbd = jnp.concatenate([jnp.where(lo_hi[h], v_p, zero_bf) for h in range(2)], axis=0)
            sc_raw = _dot_nt(_bf(q_p), kbd)
            st = rstate_ref[p]
            inter = _dot(_bf(q_p * qdec_ref[p]), _bf(st))
            upd = _dot_tn(_bf(k_p * kdec_ref[p]), v_p)
            held.append((sc_raw, vbd, st, inter, upd))
        yield
        intras = []
        for p in range(2):
            sc_raw, vbd, st, inter, upd = held[p]
            intras.append(_dot(_bf(sc_raw * dmask_ref[p]), vbd) + inter)
            rstate_ref[p] = st * cdm_ref[p] + upd * rbm_ref[...]
        done.add(f"ret{sb}.state")
        yield
        r = jnp.concatenate(intras, axis=1)
        ms = _seg_mean(r * r, e64_ref)
        yield
        mix_refs[sb][:, ATTN_W:ATTN_W + RET_W] = _bf(r * lax.rsqrt(ms + EPS) * _silu(pr[:, O_RG:O_RG + 256]))
        done.add(f"ret{sb}.done")

    def gen_gla(sb):
        yield f"proj{sb}.gga"
        pr = proj_refs[sb]
        logits = _dot(_bf(pr[:, O_GA:O_GA + 128]), gw_ref[...])
        yield
        logits = logits + gb_ref[...]
        log_a = (jnp.minimum(logits, 0.0) - jnp.log(1.0 + jnp.exp(-jnp.abs(logits)))) * (LOG2E / GLA_GATE_NORMALIZER)
        la_hi, la_lo = _split_hi_lo(log_a)
        xr = _dot(tri_ref[...], jnp.concatenate([la_hi, la_lo], axis=1))
        yield
        yield f"proj{sb}.gqkv"
        bscr = bscr_ref.at[sb]
        bscr[...] = xr[:, :128] + xr[:, 128:]
        gq = pr[:, O_GQ:O_GQ + 128] * (GLA_DK ** -0.5)
        gk = pr[:, O_GK:O_GK + 128]
        gv = _bf(pr[:, O_GV:O_GV + 256])
        nch = BLK // GLA_C
        row = lax.broadcasted_iota(jnp.int32, (GLA_C, LANES), 0)

        def level_factor(c, lv):
            r0 = c * GLA_C
            b_c = bscr[r0:r0 + GLA_C, :]
            grp = 1 << lv
            if grp >= 8:
                mids = [bscr[r0 + g0 + grp // 2 - 1:r0 + g0 + grp // 2, :] for g0 in range(0, GLA_C, grp)]
                ref = jnp.concatenate([jnp.broadcast_to(m, (grp, LANES)) for m in mids], axis=0)
            elif grp == 4:
                r = row % 4
                ref = jnp.where(r == 0, pltpu.roll(b_c, GLA_C - 1, axis=0),
                                jnp.where(r == 1, b_c,
                                          jnp.where(r == 2, pltpu.roll(b_c, 1, axis=0), pltpu.roll(b_c, 2, axis=0))))
            else:
                ref = jnp.where(row % 2 == 1, pltpu.roll(b_c, 1, axis=0), b_c)
            return jnp.exp2(-jnp.abs(b_c - ref))

        saccs = [jnp.zeros((GLA_C, GLA_HEADS * GLA_C), jnp.float32) for _ in range(nch)]
        pending = []
        for lv in range(GLA_LEVELS + 1):
            raw = []
            for c in range(nch):
                r0 = c * GLA_C
                if lv == 0:
                    qt, kt = _bf(gq[r0:r0 + GLA_C]), _bf(gk[r0:r0 + GLA_C])
                else:
                    f = level_factor(c, lv)
                    qt, kt = _bf(gq[r0:r0 + GLA_C] * f), _bf(gk[r0:r0 + GLA_C] * f)
                kbd = jnp.concatenate([jnp.where(hgrp[h], kt, zero_bf) for h in range(GLA_HEADS)], axis=0)
                raw.append(_dot_nt(qt, kbd))
            pending.append((lv, raw))
            if len(pending) == GLA_LEVELS_PER_STAGE or lv == GLA_LEVELS:
                yield
                for lv_p, raw_p in pending:
                    for c in range(nch):
                        saccs[c] = saccs[c] + raw_p[c] * lmask_ref[lv_p]
                pending = []
        if sb > 0:
            yield f"gla{sb - 1}.state"
        g_parts = []
        st = gstate_ref[...]
        for c in range(nch):
            r0 = c * GLA_C
            v_c = gv[r0:r0 + GLA_C]
            vbd = jnp.concatenate([jnp.where(vhead[h], v_c, zero_bf) for h in range(GLA_HEADS)], axis=0)
            b_c = bscr[r0:r0 + GLA_C, :]
            b_last = bscr[r0 + GLA_C - 1:r0 + GLA_C, :]
            q_in = _bf(gq[r0:r0 + GLA_C] * jnp.exp2(b_c))
            o_c = _dot(_bf(saccs[c]), vbd) + _dot_nt(q_in, _bf(st))
            k_out = _bf(gk[r0:r0 + GLA_C] * jnp.exp2(b_last - b_c))
            upd = _dot_tn(v_c, k_out)
            yield
            g_parts.append(o_c)
            st = st * jnp.exp2(b_last) + upd * smask_ref[...]
        gstate_ref[...] = st
        done.add(f"gla{sb}.state")
        g = jnp.concatenate(g_parts, axis=0)
        ms = _seg_mean(g * g, e64_ref)
        yield
        gate_g = _silu(pr[:, O_GG:O_GG + 256])
        mix_refs[sb][:, ATTN_W + RET_W:] = _bf(g * lax.rsqrt(ms + EPS) * gng_ref[...] * gate_g)
        done.add(f"gla{sb}.done")

    def gen_out(g):
        for sb in range(g * PGRP, (g + 1) * PGRP):
            yield f"attn{sb}.done"
            yield f"ret{sb}.done"
            yield f"gla{sb}.done"
        rws = slice(g * PGRP * BLK, (g + 1) * PGRP * BLK)
        half = D_MODEL // 2
        y0 = _dot(mix_grp[g][...], wout_ref[:, :half])
        yield
        y1 = _dot(mix_grp[g][...], wout_ref[:, half:])
        yield
        ssq = jnp.sum(y0 * y0, axis=-1, keepdims=True) + jnp.sum(y1 * y1, axis=-1, keepdims=True)
        inv = lax.rsqrt(ssq * (1.0 / D_MODEL) + EPS)
        gp = mod_ref[2:3, :] * post_ref[...]
        o_ref[0, rws, :half] = x_ref[0, rws, :half] + (y0 * inv) * gp[:, :half]
        yield
        o_ref[0, rws, half:] = x_ref[0, rws, half:] + (y1 * inv) * gp[:, half:]

    gens = []
    for g in range(NGRP):
        for sb in range(g * PGRP, (g + 1) * PGRP):
            gens += [gen_gla(sb), gen_attn(sb), gen_ret(sb)]
        gens += [gen_out(g), gen_proj(g)] + [gen_norm(sb) for sb in range(g * PGRP, (g + 1) * PGRP)]
    _run_interleaved(gens, done)


def _const_spec(shape):
    nd = len(shape)
    return pl.BlockSpec(shape, lambda b, t, s, _n=nd: (0,) * _n)


def _layer(l, x, mod, pre_gain, post_gain, w_in_b, w_out_b, sinks, gate_w_p, gate_b, gng, tables, consts):
    bsz, seq, d = x.shape
    tb = NSUB * BLK
    tok_spec = pl.BlockSpec((1, tb, LANES), lambda b, t, s: (b, t, 0))
    stacked = [pre_gain, post_gain, w_in_b, w_out_b, gate_w_p, gate_b, gng]

    def layer_spec(a):
        return pl.BlockSpec((None,) + a.shape[1:], lambda b, t, s, _n=a.ndim - 1: (l,) + (0,) * _n)

    in_specs = ([pl.BlockSpec((1, tb, d), lambda b, t, s: (b, t, 0)),
                 pl.BlockSpec((None, None, 3, d), lambda b, t, s: (l, b, 0, 0))]
                + [layer_spec(a) for a in stacked]
                + [tok_spec] * 4
                + [_const_spec(a.shape) for a in consts])
    scratch = ([pltpu.VMEM((PGRP * BLK, d), jnp.bfloat16)] * NGRP
               + [pltpu.VMEM((PGRP * BLK, N_PROJ), jnp.float32)] * NGRP
               + [pltpu.VMEM((PGRP * BLK, d), jnp.bfloat16)] * NGRP
               + [pltpu.VMEM((2, BLK, LANES), jnp.bfloat16),
                  pltpu.VMEM((2, BLK, LANES), jnp.bfloat16),
                  pltpu.VMEM((2, LANES, LANES), jnp.float32),
                  pltpu.VMEM((GLA_W, LANES), jnp.float32),
                  pltpu.VMEM((NSUB, BLK, LANES), jnp.float32)])
    grid_spec = pltpu.PrefetchScalarGridSpec(
        num_scalar_prefetch=1,
        grid=(bsz, seq // tb),
        in_specs=in_specs,
        out_specs=pl.BlockSpec((1, tb, d), lambda b, t, s: (b, t, 0)),
        scratch_shapes=scratch)
    return pl.pallas_call(
        _layer_kernel,
        out_shape=jax.ShapeDtypeStruct(x.shape, x.dtype),
        grid_spec=grid_spec,
        compiler_params=pltpu.CompilerParams(dimension_semantics=("arbitrary", "arbitrary"),
                                             vmem_limit_bytes=VMEM_LIMIT),
        name="hybrid_layer",
    )(sinks, x, mod, *stacked, *tables, *consts)


def kernel(x, c, positions, w_mod, b_mod, pre_norm_gain, post_norm_gain, w_in, attn_sinks, gla_gate_w,
           gla_gate_b, gla_norm_gain, w_out):
    depth = w_mod.shape[0]
    bsz = x.shape[0]
    f32, bf16 = jnp.float32, jnp.bfloat16

    rope_freq = ROPE_THETA ** (-jnp.arange(0, HEAD_DIM, 2, dtype=f32) / HEAD_DIM)
    ret_freq = 1.0 / (10000.0 ** jnp.linspace(0.0, 1.0, RET_DK // 2, dtype=f32))
    freq = jnp.concatenate([rope_freq, ret_freq, rope_freq, ret_freq]).reshape(1, LANES)
    mod, w_out_b, tables = _prologue(c, w_mod, b_mod, positions, freq, w_out)
    mod = mod.reshape(depth, bsz, 3, D_MODEL)

    consts = [jnp.asarray(_C["bias"]), jnp.asarray(_C["vones"], bf16), jnp.asarray(_C["dmask"]),
              jnp.asarray(_C["qdec"]), jnp.asarray(_C["kdec"]), jnp.asarray(_C["cdm"]), jnp.asarray(_C["rbm"]),
              jnp.asarray(_C["tri"], bf16), jnp.asarray(_C["lmask"]), jnp.asarray(_C["smask"]),
              jnp.asarray(_C["e64"], bf16)]
    w_in_b = w_in.astype(bf16)
    gate_w_p = jnp.pad(gla_gate_w, ((0, 0), (0, LANES - GLA_GATE_RANK), (0, 0))).astype(bf16)
    gng = jnp.tile(gla_norm_gain, (1, GLA_HEADS)).reshape(depth, 1, GLA_W)
    pre = pre_norm_gain.reshape(depth, 1, D_MODEL)
    post = post_norm_gain.reshape(depth, 1, D_MODEL)
    gate_b = gla_gate_b.reshape(depth, 1, LANES)

    for l in range(depth):
        x = _layer(l, x, mod, pre, post, w_in_b, w_out_b, attn_sinks[l], gate_w_p, gate_b, gng, tables, consts)
    return x
```

```python
import functools

import numpy as np
import jax
import jax.numpy as jnp
from jax import lax
from jax.experimental import pallas as pl
from jax.experimental.pallas import tpu as pltpu

D_MODEL = 1024
HEAD_DIM = 64
ATTN_HEADS = 8
ATTN_KV_HEADS = 2
WINDOW = 128
ROPE_THETA = 10000.0
RET_HEADS = 4
RET_DK = 64
RET_DV = 64
GLA_HEADS = 4
GLA_DK = 32
GLA_DV = 64
GLA_GATE_RANK = 16
GLA_GATE_NORMALIZER = 16.0
EPS = 1e-6

ATTN_W = ATTN_HEADS * HEAD_DIM
RET_W = RET_HEADS * RET_DV
GLA_W = GLA_HEADS * GLA_DV

LANES = 128
HALF = HEAD_DIM // 2
BLK = 128
NSUB = 8
PGRP = 8
NGRP = NSUB // PGRP
GLA_C = 64
GLA_LEVELS = 6
GLA_LEVELS_PER_STAGE = 7
NEG = -1e30
VMEM_LIMIT = 62 * 1024 * 1024
LOG2E = 1.4426950408889634

O_AQ, O_AK, O_AV, O_AG = 0, 512, 640, 768
O_RQ, O_RK, O_RV, O_RG = 1280, 1536, 1792, 2048
O_GQ, O_GK, O_GV, O_GG, O_GA = 2304, 2432, 2560, 2816, 3072
D_IN = O_GA + GLA_GATE_RANK
N_PROJ = 3200
PROJ_SEGMENTS = (("gga", O_GG, D_IN), ("gqkv", O_GQ, O_GG), ("akv", O_AK, O_AG), ("aq", O_AQ, O_AK),
                 ("rqk", O_RQ, O_RV), ("rvg", O_RV, O_GQ), ("ag", O_AG, O_RQ))


def _build_constants():
    f32 = np.float32
    lane = np.arange(LANES)
    grp = lane // HEAD_DIM

    i = np.arange(BLK)[:, None]
    j = np.arange(2 * BLK)[None, :]
    rel = i + BLK - j
    ok = (rel >= 0) & (rel < WINDOW)
    bias = np.stack([np.where(ok & (j >= BLK), 0.0, NEG), np.where(ok, 0.0, NEG)]).astype(f32)
    vones = np.zeros((2 * 2 * BLK, LANES), f32)
    for g in range(2):
        vones[g * 2 * BLK:(g + 1) * 2 * BLK] = (lane // HEAD_DIM == g)[None, :]

    kscale = RET_DK ** -0.5
    log_g = np.log(1.0 - 2.0 ** (-5.0 - np.arange(RET_HEADS, dtype=np.float64)))
    idx = np.arange(BLK, dtype=np.float64)
    diff = idx[:, None] - idx[None, :]
    dmask = np.zeros((2, BLK, 2 * BLK), f32)
    qdec = np.zeros((2, BLK, LANES), f32)
    kdec = np.zeros((2, BLK, LANES), f32)
    cdm = np.zeros((2, LANES, LANES), f32)
    rbm = ((np.arange(LANES)[:, None] // RET_DK) == (np.arange(LANES)[None, :] // RET_DV)).astype(f32)
    for p in range(2):
        for hh in range(2):
            lg = log_g[2 * p + hh]
            dmask[p, :, hh * BLK:(hh + 1) * BLK] = kscale * np.where(diff >= 0, np.exp(lg * np.maximum(diff, 0.0)), 0.0)
        lg_lane = log_g[2 * p + grp]
        qdec[p] = np.exp(lg_lane[None, :] * (idx[:, None] + 1.0))
        kdec[p] = kscale * np.exp(lg_lane[None, :] * (BLK - 1.0 - idx[:, None]))
        cdm[p] = np.exp(lg_lane * BLK)[:, None] * rbm

    t = np.arange(BLK)
    tri = (((t[:, None] // GLA_C) == (t[None, :] // GLA_C)) & (t[None, :] <= t[:, None])).astype(f32)
    ci = np.arange(GLA_C)[:, None]
    cj = np.arange(GLA_C)[None, :]
    x = ci ^ cj
    lvl = np.where(cj > ci, -1, np.where(x == 0, 0, np.floor(np.log2(np.maximum(x, 1))).astype(np.int64) + 1))
    lmask = np.stack([np.tile((lvl == lv).astype(f32), (1, GLA_HEADS)) for lv in range(GLA_LEVELS + 1)])
    smask = ((np.arange(GLA_W)[:, None] // GLA_DV) == (np.arange(LANES)[None, :] // GLA_DK)).astype(f32)
    e64 = ((np.arange(RET_W)[:, None] // RET_DV) == (np.arange(RET_W)[None, :] // RET_DV)).astype(f32) / RET_DV

    return dict(bias=bias, vones=vones, dmask=dmask, qdec=qdec, kdec=kdec, cdm=cdm, rbm=rbm, tri=tri,
                lmask=lmask, smask=smask, e64=e64)


_C = _build_constants()


def _dot(a, b):
    return jnp.dot(a, b, preferred_element_type=jnp.float32)


def _dot_nt(a, b):
    return lax.dot_general(a, b, (((1,), (1,)), ((), ())), preferred_element_type=jnp.float32)


def _dot_tn(a, b):
    return lax.dot_general(a, b, (((0,), (0,)), ((), ())), preferred_element_type=jnp.float32)


def _bf(x):
    return x.astype(jnp.bfloat16)


def _split_hi_lo(x):
    hi = _bf(x)
    lo = _bf(x - hi.astype(jnp.float32))
    return hi, lo


def _silu(x):
    half = 0.5 * x
    return half + half * jnp.tanh(half)


def _run_interleaved(gens, done):
    active = [[g, None] for g in gens]
    while active:
        progressed = False
        for item in list(active):
            if item[1] is not None and item[1] not in done:
                continue
            progressed = True
            try:
                need = next(item[0])
                while need is not None and need in done:
                    need = next(item[0])
                item[1] = need
            except StopIteration:
                active.remove(item)
        assert progressed, [item[1] for item in active]


MOD_TK = 256
TRIG_T = 512


def _mod_step(c_ref, w_ref, b_ref, o_ref, first):
    @pl.when(first)
    def _():
        o_ref[0] = jnp.broadcast_to(b_ref[0], o_ref.shape[1:])

    a = _silu(c_ref[...])
    a_hi, a_lo = _split_hi_lo(a)
    w_hi, w_lo = _split_hi_lo(w_ref[0])
    n = a.shape[0]
    both = _dot(jnp.concatenate([a_hi, a_lo], axis=0), w_hi)
    o_ref[0] += both[:n] + both[n:] + _dot(a_hi, w_lo)


def _trig_tile(pos_ref, freq_ref, out_refs, row0):
    ca_ref, sa_ref, cr_ref, sr_ref = out_refs
    half = TRIG_T // 2
    lane = lax.broadcasted_iota(jnp.int32, (half, LANES), 1)
    pos_a = pos_ref[row0:row0 + half, :]
    pos_b = pos_ref[row0 + half:row0 + TRIG_T, :]
    ang = jnp.where(lane < 2 * HALF, pos_a, pos_b).astype(jnp.float32) * freq_ref[...]
    c = jnp.cos(ang)
    s = jnp.sin(ang)
    quarter = lane // HALF
    sign = jnp.where((lane % HEAD_DIM) < HALF, -1.0, 1.0)
    rc = [c] + [pltpu.roll(c, k * HALF, axis=1) for k in (1, 2, 3)]
    rs = [s] + [pltpu.roll(s, k * HALF, axis=1) for k in (1, 2, 3)]

    def spread(r, src):
        out = r[(0 - src) % 4]
        for q in (1, 2, 3):
            out = jnp.where(quarter == q, r[(q - src) % 4], out)
        return out

    for tok, rws in ((0, slice(row0, row0 + half)), (1, slice(row0 + half, row0 + TRIG_T))):
        ca_ref[rws, :] = spread(rc, 2 * tok)
        sa_ref[rws, :] = spread(rs, 2 * tok) * sign
        cr_ref[rws, :] = spread(rc, 2 * tok + 1)
        sr_ref[rws, :] = spread(rs, 2 * tok + 1) * sign


def _prologue_kernel(c_ref, w_ref, b_ref, pos_ref, freq_ref, wo_ref, mod_ref, wob_ref, *table_refs, k_tiles, tiles):
    _mod_step(c_ref, w_ref, b_ref, mod_ref, pl.program_id(0) % k_tiles == 0)
    wob_ref[...] = _bf(wo_ref[...])
    for j in range(tiles):
        _trig_tile(pos_ref, freq_ref, table_refs, j * TRIG_T)


def _prologue(c, w_mod, b_mod, positions, freq, w_out):
    depth, d, n = w_mod.shape
    bsz, seq = positions.shape
    k_tiles = d // MOD_TK
    steps = depth * k_tiles
    tiles = (bsz * seq) // (TRIG_T * steps)
    rows = tiles * TRIG_T
    assert rows * steps == bsz * seq
    table = jax.ShapeDtypeStruct((bsz * seq, LANES), jnp.float32)
    tspec = pl.BlockSpec((rows, LANES), lambda i: (i, 0))
    wo_spec = pl.BlockSpec((1, w_out.shape[1] // k_tiles, w_out.shape[2]), lambda i: (i // k_tiles, i % k_tiles, 0))
    outs = pl.pallas_call(
        functools.partial(_prologue_kernel, k_tiles=k_tiles, tiles=tiles),
        out_shape=(jax.ShapeDtypeStruct((depth, bsz, n), jnp.float32),
                   jax.ShapeDtypeStruct(w_out.shape, jnp.bfloat16), table, table, table, table),
        grid=(steps,),
        in_specs=[pl.BlockSpec((bsz, MOD_TK), lambda i: (0, i % k_tiles)),
                  pl.BlockSpec((1, MOD_TK, n), lambda i: (i // k_tiles, i % k_tiles, 0)),
                  pl.BlockSpec((1, 1, n), lambda i: (i // k_tiles, 0, 0)),
                  pl.BlockSpec((rows, 1), lambda i: (i, 0)),
                  pl.BlockSpec((1, LANES), lambda i: (0, 0)),
                  wo_spec],
        out_specs=(pl.BlockSpec((1, bsz, n), lambda i: (i // k_tiles, 0, 0)), wo_spec, tspec, tspec, tspec, tspec),
        compiler_params=pltpu.CompilerParams(dimension_semantics=("arbitrary",), vmem_limit_bytes=VMEM_LIMIT),
        name="prologue",
    )(c, w_mod, b_mod.reshape(depth, 1, n), positions.reshape(bsz * seq, 1), freq, w_out)
    return outs[0], outs[1], tuple(t.reshape(bsz, seq, LANES) for t in outs[2:])


def _seg_mean(x, e_ref):
    return _dot(_bf(x), e_ref[...])


def _rot(v, c, s, first_half):
    partner = jnp.where(first_half, pltpu.roll(v, LANES - HALF, axis=1), pltpu.roll(v, HALF, axis=1))
    return v * c + partner * s


def _layer_kernel(sinks_ref,
                  x_ref, mod_ref, pre_ref, post_ref, win_ref, wout_ref, gw_ref, gb_ref, gng_ref,
                  ca_ref, sa_ref, cr_ref, sr_ref,
                  bias_ref, vones_ref, dmask_ref, qdec_ref, kdec_ref, cdm_ref, rbm_ref,
                  tri_ref, lmask_ref, smask_ref, e64_ref,
                  o_ref, *scratch):
    hb_grp = scratch[0:NGRP]
    proj_grp = scratch[NGRP:2 * NGRP]
    mix_grp = scratch[2 * NGRP:3 * NGRP]
    kprev_ref, vprev_ref, rstate_ref, gstate_ref, bscr_ref = scratch[3 * NGRP:]
    t = pl.program_id(1)

    def sub_view(refs, sb):
        return refs[sb // PGRP].at[pl.ds((sb % PGRP) * BLK, BLK)]

    hb_refs = [sub_view(hb_grp, sb) for sb in range(NSUB)]
    proj_refs = [sub_view(proj_grp, sb) for sb in range(NSUB)]
    mix_refs = [sub_view(mix_grp, sb) for sb in range(NSUB)]

    @pl.when(t == 0)
    def _():
        kprev_ref[...] = jnp.zeros_like(kprev_ref)
        vprev_ref[...] = jnp.zeros_like(vprev_ref)
        rstate_ref[...] = jnp.zeros_like(rstate_ref)
        gstate_ref[...] = jnp.zeros_like(gstate_ref)
        for g in range(NGRP):
            proj_grp[g][:, O_GA:] = jnp.zeros((PGRP * BLK, N_PROJ - O_GA), jnp.float32)

    lane = lax.broadcasted_iota(jnp.int32, (1, LANES), 1)
    lane2 = lax.broadcasted_iota(jnp.int32, (1, GLA_W), 1)
    zero_bf = jnp.zeros((), jnp.bfloat16)
    first_half = (lane % HEAD_DIM) < HALF
    lo_hi = [lane // HEAD_DIM == g for g in range(2)]
    hgrp = [lane // GLA_DK == h for h in range(GLA_HEADS)]
    vhead = [lane2 // GLA_DV == h for h in range(GLA_HEADS)]

    def rows(sb):
        return slice(sb * BLK, (sb + 1) * BLK)

    done = set()

    def gen_norm(sb):
        if sb >= PGRP:
            yield f"projgrp{sb // PGRP - 1}.started"
        x = x_ref[0, rows(sb), :]
        inv = lax.rsqrt(jnp.mean(x * x, axis=-1, keepdims=True) + EPS)
        hb_refs[sb][...] = _bf((x * inv) * (pre_ref[...] * (1.0 + mod_ref[1:2, :])) + mod_ref[0:1, :])
        done.add(f"norm{sb}")

    def gen_proj(g):
        subs = range(g * PGRP, (g + 1) * PGRP)
        for sb in subs:
            yield f"norm{sb}"
        if g > 0:
            yield f"projgrp{g - 1}.all"
        done.add(f"projgrp{g}.started")
        for name, a, b in PROJ_SEGMENTS:
            proj_grp[g][:, a:b] = _dot(hb_grp[g][...], win_ref[:, a:b])
            done.update(f"proj{sb}.{name}" for sb in subs)
            yield
        done.add(f"projgrp{g}.all")

    def gen_attn(sb):
        yield f"proj{sb}.akv"
        if sb > 0:
            yield f"attn{sb - 1}.kv"
        pr = proj_refs[sb]
        ca, sa = ca_ref[0, rows(sb), :], sa_ref[0, rows(sb), :]
        k_new = _rot(pr[:, O_AK:O_AK + 128], ca, sa, first_half) * (HEAD_DIM ** -0.5 * LOG2E)
        v_new = pr[:, O_AV:O_AV + 128]
        k_sw = pltpu.roll(k_new, HEAD_DIM, axis=1)
        v_sw = pltpu.roll(v_new, HEAD_DIM, axis=1)
        k2, v2 = [], []
        for g in range(2):
            k_g = _bf(jnp.where(lo_hi[g], k_new, k_sw))
            v_g = _bf(jnp.where(lo_hi[g], v_new, v_sw))
            kcat = jnp.concatenate([kprev_ref[g], k_g], axis=0)
            vcat = jnp.concatenate([vprev_ref[g], v_g], axis=0)
            kprev_ref[g] = k_g
            vprev_ref[g] = v_g
            k2.append(jnp.concatenate([jnp.where(lo_hi[h], kcat, zero_bf) for h in range(2)], axis=0))
            v2_g = jnp.concatenate([jnp.where(lo_hi[h], vcat, zero_bf) for h in range(2)], axis=0)
            v2.append(jnp.concatenate([v2_g, vones_ref[...]], axis=1))
        done.add(f"attn{sb}.kv")
        if sb == 0:
            bias = bias_ref[jnp.minimum(t, 1)]
        else:
            bias = bias_ref[1]

        def scores(g):
            q = [_rot(pr[:, O_AQ + m * 128:O_AQ + (m + 1) * 128], ca, sa, first_half) for m in (2 * g, 2 * g + 1)]
            return _dot_nt(_bf(jnp.concatenate(q, axis=0)), k2[g])

        yield f"proj{sb}.aq"
        s_g = scores(0)
        yield
        for g in range(2):
            ps, mxs = [], []
            for j in range(2):
                row_p, row_mx = [], []
                for h in range(2):
                    s = s_g[j * BLK:(j + 1) * BLK, h * 2 * BLK:(h + 1) * 2 * BLK] + bias
                    mx = jnp.maximum(jnp.max(s, axis=-1, keepdims=True), sinks_ref[2 * (2 * g + j) + h] * LOG2E)
                    row_p.append(_bf(jnp.exp2(s - mx)))
                    row_mx.append(mx)
                ps.append(jnp.concatenate(row_p, axis=1))
                mxs.append(row_mx)
            pv = _dot(jnp.concatenate(ps, axis=0), v2[g])
            if g == 0:
                s_g = scores(1)
            yield
            yield f"proj{sb}.ag"
            for j in range(2):
                m = 2 * g + j
                pv_m = pv[j * BLK:(j + 1) * BLK]
                mx_l = jnp.where(lo_hi[0], mxs[j][0], mxs[j][1])
                sink_l = jnp.where(lo_hi[0], sinks_ref[2 * m], sinks_ref[2 * m + 1]) * LOG2E
                den = pv_m[:, 128:] + jnp.exp2(sink_l - mx_l)
                gate_m = _silu(pr[:, O_AG + m * 128:O_AG + (m + 1) * 128])
                mix_refs[sb][:, m * 128:(m + 1) * 128] = _bf(pv_m[:, :128] / den * gate_m)
        done.add(f"attn{sb}.done")

    def gen_ret(sb):
        yield f"proj{sb}.rqk"
        yield f"proj{sb}.rvg"
        if sb > 0:
            yield f"ret{sb - 1}.state"
        pr = proj_refs[sb]
        cr, sr = cr_ref[0, rows(sb), :], sr_ref[0, rows(sb), :]
        held = []
        for p in range(2):
            q_p = _rot(pr[:, O_RQ + p * 128:O_RQ + (p + 1) * 128], cr, sr, first_half)
            k_p = _rot(pr[:, O_RK + p * 128:O_RK + (p + 1) * 128], cr, sr, first_half)
            v_p = _bf(pr[:, O_RV + p * 128:O_RV + (p + 1) * 128])
            k_pb = _bf(k_p)
            kbd = jnp.concatenate([jnp.where(lo_hi[h], k_pb, zero_bf) for h in range(2)], axis=0)
            vbd = jnp.concatenate([jnp.where(lo_hi[h], v_p, zero_bf) for h in range(2)], axis=0)
            sc_raw = _dot_nt(_bf(q_p), kbd)
            st = rstate_ref[p]
            inter = _dot(_bf(q_p * qdec_ref[p]), _bf(st))
            upd = _dot_tn(_bf(k_p * kdec_ref[p]), v_p)
            held.append((sc_raw, vbd, st, inter, upd))
        yield
        intras = []
        for p in range(2):
            sc_raw, vbd, st, inter, upd = held[p]
            intras.append(_dot(_bf(sc_raw * dmask_ref[p]), vbd) + inter)
            rstate_ref[p] = st * cdm_ref[p] + upd * rbm_ref[...]
        done.add(f"ret{sb}.state")
        yield
        r = jnp.concatenate(intras, axis=1)
        ms = _seg_mean(r * r, e64_ref)
        yield
        mix_refs[sb][:, ATTN_W:ATTN_W + RET_W] = _bf(r * lax.rsqrt(ms + EPS) * _silu(pr[:, O_RG:O_RG + 256]))
        done.add(f"ret{sb}.done")

    def gen_gla(sb):
        yield f"proj{sb}.gga"
        pr = proj_refs[sb]
        logits = _dot(_bf(pr[:, O_GA:O_GA + 128]), gw_ref[...])
        yield
        logits = logits + gb_ref[...]
        log_a = (jnp.minimum(logits, 0.0) - jnp.log(1.0 + jnp.exp(-jnp.abs(logits)))) * (LOG2E / GLA_GATE_NORMALIZER)
        la_hi, la_lo = _split_hi_lo(log_a)
        xr = _dot(tri_ref[...], jnp.concatenate([la_hi, la_lo], axis=1))
        yield
        yield f"proj{sb}.gqkv"
        bscr = bscr_ref.at[sb]
        bscr[...] = xr[:, :128] + xr[:, 128:]
        gq = pr[:, O_GQ:O_GQ + 128] * (GLA_DK ** -0.5)
        gk = pr[:, O_GK:O_GK + 128]
        gv = _bf(pr[:, O_GV:O_GV + 256])
        nch = BLK // GLA_C
        row = lax.broadcasted_iota(jnp.int32, (GLA_C, LANES), 0)

        def level_factor(c, lv):
            r0 = c * GLA_C
            b_c = bscr[r0:r0 + GLA_C, :]
            grp = 1 << lv
            if grp >= 8:
                mids = [bscr[r0 + g0 + grp // 2 - 1:r0 + g0 + grp // 2, :] for g0 in range(0, GLA_C, grp)]
                ref = jnp.concatenate([jnp.broadcast_to(m, (grp, LANES)) for m in mids], axis=0)
            elif grp == 4:
                r = row % 4
                ref = jnp.where(r == 0, pltpu.roll(b_c, GLA_C - 1, axis=0),
                                jnp.where(r == 1, b_c,
                                          jnp.where(r == 2, pltpu.roll(b_c, 1, axis=0), pltpu.roll(b_c, 2, axis=0))))
            else:
                ref = jnp.where(row % 2 == 1, pltpu.roll(b_c, 1, axis=0), b_c)
            return jnp.exp2(-jnp.abs(b_c - ref))

        saccs = [jnp.zeros((GLA_C, GLA_HEADS * GLA_C), jnp.float32) for _ in range(nch)]
        pending = []
        for lv in range(GLA_LEVELS + 1):
            raw = []
            for c in range(nch):
                r0 = c * GLA_C
                if lv == 0:
                    qt, kt = _bf(gq[r0:r0 + GLA_C]), _bf(gk[r0:r0 + GLA_C])
                else:
                    f = level_factor(c, lv)
                    qt, kt = _bf(gq[r0:r0 + GLA_C] * f), _bf(gk[r0:r0 + GLA_C] * f)
                kbd = jnp.concatenate([jnp.where(hgrp[h], kt, zero_bf) for h in range(GLA_HEADS)], axis=0)
                raw.append(_dot_nt(qt, kbd))
            pending.append((lv, raw))
            if len(pending) == GLA_LEVELS_PER_STAGE or lv == GLA_LEVELS:
                yield
                for lv_p, raw_p in pending:
                    for c in range(nch):
                        saccs[c] = saccs[c] + raw_p[c] * lmask_ref[lv_p]
                pending = []
        if sb > 0:
            yield f"gla{sb - 1}.state"
        g_parts = []
        st = gstate_ref[...]
        for c in range(nch):
            r0 = c * GLA_C
            v_c = gv[r0:r0 + GLA_C]
            vbd = jnp.concatenate([jnp.where(vhead[h], v_c, zero_bf) for h in range(GLA_HEADS)], axis=0)
            b_c = bscr[r0:r0 + GLA_C, :]
            b_last = bscr[r0 + GLA_C - 1:r0 + GLA_C, :]
            q_in = _bf(gq[r0:r0 + GLA_C] * jnp.exp2(b_c))
            o_c = _dot(_bf(saccs[c]), vbd) + _dot_nt(q_in, _bf(st))
            k_out = _bf(gk[r0:r0 + GLA_C] * jnp.exp2(b_last - b_c))
            upd = _dot_tn(v_c, k_out)
            yield
            g_parts.append(o_c)
            st = st * jnp.exp2(b_last) + upd * smask_ref[...]
        gstate_ref[...] = st
        done.add(f"gla{sb}.state")
        g = jnp.concatenate(g_parts, axis=0)
        ms = _seg_mean(g * g, e64_ref)
        yield
        gate_g = _silu(pr[:, O_GG:O_GG + 256])
        mix_refs[sb][:, ATTN_W + RET_W:] = _bf(g * lax.rsqrt(ms + EPS) * gng_ref[...] * gate_g)
        done.add(f"gla{sb}.done")

    def gen_out(g):
        for sb in range(g * PGRP, (g + 1) * PGRP):
            yield f"attn{sb}.done"
            yield f"ret{sb}.done"
            yield f"gla{sb}.done"
        rws = slice(g * PGRP * BLK, (g + 1) * PGRP * BLK)
        half = D_MODEL // 2
        y0 = _dot(mix_grp[g][...], wout_ref[:, :half])
        yield
        y1 = _dot(mix_grp[g][...], wout_ref[:, half:])
        yield
        ssq = jnp.sum(y0 * y0, axis=-1, keepdims=True) + jnp.sum(y1 * y1, axis=-1, keepdims=True)
        inv = lax.rsqrt(ssq * (1.0 / D_MODEL) + EPS)
        gp = mod_ref[2:3, :] * post_ref[...]
        o_ref[0, rws, :half] = x_ref[0, rws, :half] + (y0 * inv) * gp[:, :half]
        yield
        o_ref[0, rws, half:] = x_ref[0, rws, half:] + (y1 * inv) * gp[:, half:]

    gens = []
    for g in range(NGRP):
        for sb in range(g * PGRP, (g + 1) * PGRP):
            gens += [gen_gla(sb), gen_attn(sb), gen_ret(sb)]
        gens += [gen_out(g), gen_proj(g)] + [gen_norm(sb) for sb in range(g * PGRP, (g + 1) * PGRP)]
    _run_interleaved(gens, done)


def _const_spec(shape):
    nd = len(shape)
    return pl.BlockSpec(shape, lambda b, t, s, _n=nd: (0,) * _n)


def _layer(l, x, mod, pre_gain, post_gain, w_in_b, w_out_b, sinks, gate_w_p, gate_b, gng, tables, consts):
    bsz, seq, d = x.shape
    tb = NSUB * BLK
    tok_spec = pl.BlockSpec((1, tb, LANES), lambda b, t, s: (b, t, 0))
    stacked = [pre_gain, post_gain, w_in_b, w_out_b, gate_w_p, gate_b, gng]

    def layer_spec(a):
        return pl.BlockSpec((None,) + a.shape[1:], lambda b, t, s, _n=a.ndim - 1: (l,) + (0,) * _n)

    in_specs = ([pl.BlockSpec((1, tb, d), lambda b, t, s: (b, t, 0)),
                 pl.BlockSpec((None, None, 3, d), lambda b, t, s: (l, b, 0, 0))]
                + [layer_spec(a) for a in stacked]
                + [tok_spec] * 4
                + [_const_spec(a.shape) for a in consts])
    scratch = ([pltpu.VMEM((PGRP * BLK, d), jnp.bfloat16)] * NGRP
               + [pltpu.VMEM((PGRP * BLK, N_PROJ), jnp.float32)] * NGRP
               + [pltpu.VMEM((PGRP * BLK, d), jnp.bfloat16)] * NGRP
               + [pltpu.VMEM((2, BLK, LANES), jnp.bfloat16),
                  pltpu.VMEM((2, BLK, LANES), jnp.bfloat16),
                  pltpu.VMEM((2, LANES, LANES), jnp.float32),
                  pltpu.VMEM((GLA_W, LANES), jnp.float32),
                  pltpu.VMEM((NSUB, BLK, LANES), jnp.float32)])
    grid_spec = pltpu.PrefetchScalarGridSpec(
        num_scalar_prefetch=1,
        grid=(bsz, seq // tb),
        in_specs=in_specs,
        out_specs=pl.BlockSpec((1, tb, d), lambda b, t, s: (b, t, 0)),
        scratch_shapes=scratch)
    return pl.pallas_call(
        _layer_kernel,
        out_shape=jax.ShapeDtypeStruct(x.shape, x.dtype),
        grid_spec=grid_spec,
        compiler_params=pltpu.CompilerParams(dimension_semantics=("arbitrary", "arbitrary"),
                                             vmem_limit_bytes=VMEM_LIMIT),
        name="hybrid_layer",
    )(sinks, x, mod, *stacked, *tables, *consts)


def kernel(x, c, positions, w_mod, b_mod, pre_norm_gain, post_norm_gain, w_in, attn_sinks, gla_gate_w,
           gla_gate_b, gla_norm_gain, w_out):
    depth = w_mod.shape[0]
    bsz = x.shape[0]
    f32, bf16 = jnp.float32, jnp.bfloat16

    rope_freq = ROPE_THETA ** (-jnp.arange(0, HEAD_DIM, 2, dtype=f32) / HEAD_DIM)
    ret_freq = 1.0 / (10000.0 ** jnp.linspace(0.0, 1.0, RET_DK // 2, dtype=f32))
    freq = jnp.concatenate([rope_freq, ret_freq, rope_freq, ret_freq]).reshape(1, LANES)
    mod, w_out_b, tables = _prologue(c, w_mod, b_mod, positions, freq, w_out)
    mod = mod.reshape(depth, bsz, 3, D_MODEL)

    consts = [jnp.asarray(_C["bias"]), jnp.asarray(_C["vones"], bf16), jnp.asarray(_C["dmask"]),
              jnp.asarray(_C["qdec"]), jnp.asarray(_C["kdec"]), jnp.asarray(_C["cdm"]), jnp.asarray(_C["rbm"]),
              jnp.asarray(_C["tri"], bf16), jnp.asarray(_C["lmask"]), jnp.asarray(_C["smask"]),
              jnp.asarray(_C["e64"], bf16)]
    w_in_b = w_in.astype(bf16)
    gate_w_p = jnp.pad(gla_gate_w, ((0, 0), (0, LANES - GLA_GATE_RANK), (0, 0))).astype(bf16)
    gng = jnp.tile(gla_norm_gain, (1, GLA_HEADS)).reshape(depth, 1, GLA_W)
    pre = pre_norm_gain.reshape(depth, 1, D_MODEL)
    post = post_norm_gain.reshape(depth, 1, D_MODEL)
    gate_b = gla_gate_b.reshape(depth, 1, LANES)

    for l in range(depth):
        x = _layer(l, x, mod, pre, post, w_in_b, w_out_b, attn_sinks[l], gate_w_p, gate_b, gng, tables, consts)
    return x
```

```python
import functools

import numpy as np
import jax
import jax.numpy as jnp
from jax import lax
from jax.experimental import pallas as pl
from jax.experimental.pallas import tpu as pltpu

D_MODEL = 1024
HEAD_DIM = 64
ATTN_HEADS = 8
ATTN_KV_HEADS = 2
WINDOW = 128
ROPE_THETA = 10000.0
RET_HEADS = 4
RET_DK = 64
RET_DV = 64
GLA_HEADS = 4
GLA_DK = 32
GLA_DV = 64
GLA_GATE_RANK = 16
GLA_GATE_NORMALIZER = 16.0
EPS = 1e-6

ATTN_W = ATTN_HEADS * HEAD_DIM
RET_W = RET_HEADS * RET_DV
GLA_W = GLA_HEADS * GLA_DV

LANES = 128
HALF = HEAD_DIM // 2
BLK = 128
NSUB = 8
PGRP = 8
NGRP = NSUB // PGRP
OGRP = 4
GLA_C = 64
GLA_LEVELS = 6
GLA_LEVELS_PER_STAGE = 7
NEG = -1e30
VMEM_LIMIT = 62 * 1024 * 1024
LOG2E = 1.4426950408889634

O_AQ, O_AK, O_AV, O_AG = 0, 512, 640, 768
O_RQ, O_RK, O_RV, O_RG = 1280, 1536, 1792, 2048
O_GQ, O_GK, O_GV, O_GG, O_GA = 2304, 2432, 2560, 2816, 3072
D_IN = O_GA + GLA_GATE_RANK
N_PROJ = 3200
PROJ_SEGMENTS = (("gga", O_GG, D_IN), ("gqkv", O_GQ, O_GG), ("akv", O_AK, O_AG), ("aq", O_AQ, O_AK),
                 ("rqk", O_RQ, O_RV), ("rvg", O_RV, O_GQ), ("ag", O_AG, O_RQ))


def _build_constants():
    f32 = np.float32
    lane = np.arange(LANES)
    grp = lane // HEAD_DIM

    i = np.arange(BLK)[:, None]
    j = np.arange(2 * BLK)[None, :]
    rel = i + BLK - j
    ok = (rel >= 0) & (rel < WINDOW)
    bias = np.stack([np.where(ok & (j >= BLK), 0.0, NEG), np.where(ok, 0.0, NEG)]).astype(f32)
    vones = np.zeros((2 * 2 * BLK, LANES), f32)
    for g in range(2):
        vones[g * 2 * BLK:(g + 1) * 2 * BLK] = (lane // HEAD_DIM == g)[None, :]

    kscale = RET_DK ** -0.5
    log_g = np.log(1.0 - 2.0 ** (-5.0 - np.arange(RET_HEADS, dtype=np.float64)))
    idx = np.arange(BLK, dtype=np.float64)
    diff = idx[:, None] - idx[None, :]
    dmask = np.zeros((2, BLK, 2 * BLK), f32)
    qdec = np.zeros((2, BLK, LANES), f32)
    kdec = np.zeros((2, BLK, LANES), f32)
    cdm = np.zeros((2, LANES, LANES), f32)
    rbm = ((np.arange(LANES)[:, None] // RET_DK) == (np.arange(LANES)[None, :] // RET_DV)).astype(f32)
    for p in range(2):
        for hh in range(2):
            lg = log_g[2 * p + hh]
            dmask[p, :, hh * BLK:(hh + 1) * BLK] = kscale * np.where(diff >= 0, np.exp(lg * np.maximum(diff, 0.0)), 0.0)
        lg_lane = log_g[2 * p + grp]
        qdec[p] = np.exp(lg_lane[None, :] * (idx[:, None] + 1.0))
        kdec[p] = kscale * np.exp(lg_lane[None, :] * (BLK - 1.0 - idx[:, None]))
        cdm[p] = np.exp(lg_lane * BLK)[:, None] * rbm

    t = np.arange(BLK)
    tri = (((t[:, None] // GLA_C) == (t[None, :] // GLA_C)) & (t[None, :] <= t[:, None])).astype(f32)
    ci = np.arange(GLA_C)[:, None]
    cj = np.arange(GLA_C)[None, :]
    x = ci ^ cj
    lvl = np.where(cj > ci, -1, np.where(x == 0, 0, np.floor(np.log2(np.maximum(x, 1))).astype(np.int64) + 1))
    lmask = np.stack([np.tile((lvl == lv).astype(f32), (1, GLA_HEADS)) for lv in range(GLA_LEVELS + 1)])
    smask = ((np.arange(GLA_W)[:, None] // GLA_DV) == (np.arange(LANES)[None, :] // GLA_DK)).astype(f32)
    e64 = ((np.arange(RET_W)[:, None] // RET_DV) == (np.arange(RET_W)[None, :] // RET_DV)).astype(f32) / RET_DV

    return dict(bias=bias, vones=vones, dmask=dmask, qdec=qdec, kdec=kdec, cdm=cdm, rbm=rbm, tri=tri,
                lmask=lmask, smask=smask, e64=e64)


_C = _build_constants()


def _dot(a, b):
    return jnp.dot(a, b, preferred_element_type=jnp.float32)


def _dot_nt(a, b):
    return lax.dot_general(a, b, (((1,), (1,)), ((), ())), preferred_element_type=jnp.float32)


def _dot_tn(a, b):
    return lax.dot_general(a, b, (((0,), (0,)), ((), ())), preferred_element_type=jnp.float32)


def _bf(x):
    return x.astype(jnp.bfloat16)


def _split_hi_lo(x):
    hi = _bf(x)
    lo = _bf(x - hi.astype(jnp.float32))
    return hi, lo


def _silu(x):
    half = 0.5 * x
    return half + half * jnp.tanh(half)


def _run_interleaved(gens, done):
    active = [[g, None] for g in gens]
    while active:
        progressed = False
        for item in list(active):
            if item[1] is not None and item[1] not in done:
                continue
            progressed = True
            try:
                need = next(item[0])
                while need is not None and need in done:
                    need = next(item[0])
                item[1] = need
            except StopIteration:
                active.remove(item)
        assert progressed, [item[1] for item in active]


MOD_TK = 256
TRIG_T = 512


def _mod_step(c_ref, w_ref, b_ref, o_ref, first):
    @pl.when(first)
    def _():
        o_ref[0] = jnp.broadcast_to(b_ref[0], o_ref.shape[1:])

    a = _silu(c_ref[...])
    a_hi, a_lo = _split_hi_lo(a)
    w_hi, w_lo = _split_hi_lo(w_ref[0])
    n = a.shape[0]
    both = _dot(jnp.concatenate([a_hi, a_lo], axis=0), w_hi)
    o_ref[0] += both[:n] + both[n:] + _dot(a_hi, w_lo)


def _trig_tile(pos_ref, freq_ref, out_refs, row0):
    ca_ref, sa_ref, cr_ref, sr_ref = out_refs
    half = TRIG_T // 2
    lane = lax.broadcasted_iota(jnp.int32, (half, LANES), 1)
    pos_a = pos_ref[row0:row0 + half, :]
    pos_b = pos_ref[row0 + half:row0 + TRIG_T, :]
    ang = jnp.where(lane < 2 * HALF, pos_a, pos_b).astype(jnp.float32) * freq_ref[...]
    c = jnp.cos(ang)
    s = jnp.sin(ang)
    quarter = lane // HALF
    sign = jnp.where((lane % HEAD_DIM) < HALF, -1.0, 1.0)
    rc = [c] + [pltpu.roll(c, k * HALF, axis=1) for k in (1, 2, 3)]
    rs = [s] + [pltpu.roll(s, k * HALF, axis=1) for k in (1, 2, 3)]

    def spread(r, src):
        out = r[(0 - src) % 4]
        for q in (1, 2, 3):
            out = jnp.where(quarter == q, r[(q - src) % 4], out)
        return out

    for tok, rws in ((0, slice(row0, row0 + half)), (1, slice(row0 + half, row0 + TRIG_T))):
        ca_ref[rws, :] = spread(rc, 2 * tok)
        sa_ref[rws, :] = spread(rs, 2 * tok) * sign
        cr_ref[rws, :] = spread(rc, 2 * tok + 1)
        sr_ref[rws, :] = spread(rs, 2 * tok + 1) * sign


def _prologue_kernel(c_ref, w_ref, b_ref, pos_ref, freq_ref, wo_ref, mod_ref, wob_ref, *table_refs, k_tiles, tiles):
    _mod_step(c_ref, w_ref, b_ref, mod_ref, pl.program_id(0) % k_tiles == 0)
    wob_ref[...] = _bf(wo_ref[...])
    for j in range(tiles):
        _trig_tile(pos_ref, freq_ref, table_refs, j * TRIG_T)


def _prologue(c, w_mod, b_mod, positions, freq, w_out):
    depth, d, n = w_mod.shape
    bsz, seq = positions.shape
    k_tiles = d // MOD_TK
    steps = depth * k_tiles
    tiles = (bsz * seq) // (TRIG_T * steps)
    rows = tiles * TRIG_T
    assert rows * steps == bsz * seq
    table = jax.ShapeDtypeStruct((bsz * seq, LANES), jnp.float32)
    tspec = pl.BlockSpec((rows, LANES), lambda i: (i, 0))
    wo_spec = pl.BlockSpec((1, w_out.shape[1] // k_tiles, w_out.shape[2]), lambda i: (i // k_tiles, i % k_tiles, 0))
    outs = pl.pallas_call(
        functools.partial(_prologue_kernel, k_tiles=k_tiles, tiles=tiles),
        out_shape=(jax.ShapeDtypeStruct((depth, bsz, n), jnp.float32),
                   jax.ShapeDtypeStruct(w_out.shape, jnp.bfloat16), table, table, table, table),
        grid=(steps,),
        in_specs=[pl.BlockSpec((bsz, MOD_TK), lambda i: (0, i % k_tiles)),
                  pl.BlockSpec((1, MOD_TK, n), lambda i: (i // k_tiles, i % k_tiles, 0)),
                  pl.BlockSpec((1, 1, n), lambda i: (i // k_tiles, 0, 0)),
                  pl.BlockSpec((rows, 1), lambda i: (i, 0)),
                  pl.BlockSpec((1, LANES), lambda i: (0, 0)),
                  wo_spec],
        out_specs=(pl.BlockSpec((1, bsz, n), lambda i: (i // k_tiles, 0, 0)), wo_spec, tspec, tspec, tspec, tspec),
        compiler_params=pltpu.CompilerParams(dimension_semantics=("arbitrary",), vmem_limit_bytes=VMEM_LIMIT),
        name="prologue",
    )(c, w_mod, b_mod.reshape(depth, 1, n), positions.reshape(bsz * seq, 1), freq, w_out)
    return outs[0], outs[1], tuple(t.reshape(bsz, seq, LANES) for t in outs[2:])


def _seg_mean(x, e_ref):
    return _dot(_bf(x), e_ref[...])


def _rot(v, c, s, first_half):
    partner = jnp.where(first_half, pltpu.roll(v, LANES - HALF, axis=1), pltpu.roll(v, HALF, axis=1))
    return v * c + partner * s


def _layer_kernel(sinks_ref,
                  x_ref, mod_ref, pre_ref, post_ref, win_ref, wout_ref, gw_ref, gb_ref, gng_ref,
                  ca_ref, sa_ref, cr_ref, sr_ref,
                  bias_ref, vones_ref, dmask_ref, qdec_ref, kdec_ref, cdm_ref, rbm_ref,
                  tri_ref, lmask_ref, smask_ref, e64_ref,
                  o_ref, *scratch):
    hb_grp = scratch[0:NGRP]
    proj_grp = scratch[NGRP:2 * NGRP]
    mix_grp = scratch[2 * NGRP:3 * NGRP]
    kprev_ref, vprev_ref, rstate_ref, gstate_ref, bscr_ref = scratch[3 * NGRP:]
    t = pl.program_id(1)

    def sub_view(refs, sb):
        return refs[sb // PGRP].at[pl.ds((sb % PGRP) * BLK, BLK)]

    hb_refs = [sub_view(hb_grp, sb) for sb in range(NSUB)]
    proj_refs = [sub_view(proj_grp, sb) for sb in range(NSUB)]
    mix_refs = [sub_view(mix_grp, sb) for sb in range(NSUB)]

    @pl.when(t == 0)
    def _():
        kprev_ref[...] = jnp.zeros_like(kprev_ref)
        vprev_ref[...] = jnp.zeros_like(vprev_ref)
        rstate_ref[...] = jnp.zeros_like(rstate_ref)
        gstate_ref[...] = jnp.zeros_like(gstate_ref)
        for g in range(NGRP):
            proj_grp[g][:, O_GA:] = jnp.zeros((PGRP * BLK, N_PROJ - O_GA), jnp.float32)

    lane = lax.broadcasted_iota(jnp.int32, (1, LANES), 1)
    lane2 = lax.broadcasted_iota(jnp.int32, (1, GLA_W), 1)
    zero_bf = jnp.zeros((), jnp.bfloat16)
    first_half = (lane % HEAD_DIM) < HALF
    lo_hi = [lane // HEAD_DIM == g for g in range(2)]
    hgrp = [lane // GLA_DK == h for h in range(GLA_HEADS)]
    vhead = [lane2 // GLA_DV == h for h in range(GLA_HEADS)]

    def rows(sb):
        return slice(sb * BLK, (sb + 1) * BLK)

    done = set()

    def gen_norm(sb):
        if sb >= PGRP:
            yield f"projgrp{sb // PGRP - 1}.started"
        x = x_ref[0, rows(sb), :]
        inv = lax.rsqrt(jnp.mean(x * x, axis=-1, keepdims=True) + EPS)
        hb_refs[sb][...] = _bf((x * inv) * (pre_ref[...] * (1.0 + mod_ref[1:2, :])) + mod_ref[0:1, :])
        done.add(f"norm{sb}")

    def gen_proj(g):
        subs = range(g * PGRP, (g + 1) * PGRP)
        for sb in subs:
            yield f"norm{sb}"
        if g > 0:
            yield f"projgrp{g - 1}.all"
        done.add(f"projgrp{g}.started")
        for name, a, b in PROJ_SEGMENTS:
            proj_grp[g][:, a:b] = _dot(hb_grp[g][...], win_ref[:, a:b])
            done.update(f"proj{sb}.{name}" for sb in subs)
            yield
        done.add(f"projgrp{g}.all")

    def gen_attn(sb):
        yield f"proj{sb}.akv"
        if sb > 0:
            yield f"attn{sb - 1}.kv"
        pr = proj_refs[sb]
        ca, sa = ca_ref[0, rows(sb), :], sa_ref[0, rows(sb), :]
        k_new = _rot(pr[:, O_AK:O_AK + 128], ca, sa, first_half) * (HEAD_DIM ** -0.5 * LOG2E)
        v_new = pr[:, O_AV:O_AV + 128]
        k_sw = pltpu.roll(k_new, HEAD_DIM, axis=1)
        v_sw = pltpu.roll(v_new, HEAD_DIM, axis=1)
        k2, v2 = [], []
        for g in range(2):
            k_g = _bf(jnp.where(lo_hi[g], k_new, k_sw))
            v_g = _bf(jnp.where(lo_hi[g], v_new, v_sw))
            kcat = jnp.concatenate([kprev_ref[g], k_g], axis=0)
            vcat = jnp.concatenate([vprev_ref[g], v_g], axis=0)
            kprev_ref[g] = k_g
            vprev_ref[g] = v_g
            k2.append(jnp.concatenate([jnp.where(lo_hi[h], kcat, zero_bf) for h in range(2)], axis=0))
            v2_g = jnp.concatenate([jnp.where(lo_hi[h], vcat, zero_bf) for h in range(2)], axis=0)
            v2.append(jnp.concatenate([v2_g, vones_ref[...]], axis=1))
        done.add(f"attn{sb}.kv")
        if sb == 0:
            bias = bias_ref[jnp.minimum(t, 1)]
        else:
            bias = bias_ref[1]

        def scores(g):
            q = [_rot(pr[:, O_AQ + m * 128:O_AQ + (m + 1) * 128], ca, sa, first_half) for m in (2 * g, 2 * g + 1)]
            return _dot_nt(_bf(jnp.concatenate(q, axis=0)), k2[g])

        yield f"proj{sb}.aq"
        s_g = scores(0)
        yield
        for g in range(2):
            ps, mxs = [], []
            for j in range(2):
                row_p, row_mx = [], []
                for h in range(2):
                    s = s_g[j * BLK:(j + 1) * BLK, h * 2 * BLK:(h + 1) * 2 * BLK] + bias
                    mx = jnp.maximum(jnp.max(s, axis=-1, keepdims=True), sinks_ref[2 * (2 * g + j) + h] * LOG2E)
                    row_p.append(_bf(jnp.exp2(s - mx)))
                    row_mx.append(mx)
                ps.append(jnp.concatenate(row_p, axis=1))
                mxs.append(row_mx)
            pv = _dot(jnp.concatenate(ps, axis=0), v2[g])
            if g == 0:
                s_g = scores(1)
            yield
            yield f"proj{sb}.ag"
            for j in range(2):
                m = 2 * g + j
                pv_m = pv[j * BLK:(j + 1) * BLK]
                mx_l = jnp.where(lo_hi[0], mxs[j][0], mxs[j][1])
                sink_l = jnp.where(lo_hi[0], sinks_ref[2 * m], sinks_ref[2 * m + 1]) * LOG2E
                den = pv_m[:, 128:] + jnp.exp2(sink_l - mx_l)
                gate_m = _silu(pr[:, O_AG + m * 128:O_AG + (m + 1) * 128])
                mix_refs[sb][:, m * 128:(m + 1) * 128] = _bf(pv_m[:, :128] / den * gate_m)
        done.add(f"attn{sb}.done")

    def gen_ret(sb):
        yield f"proj{sb}.rqk"
        yield f"proj{sb}.rvg"
        if sb > 0:
            yield f"ret{sb - 1}.state"
        pr = proj_refs[sb]
        cr, sr = cr_ref[0, rows(sb), :], sr_ref[0, rows(sb), :]
        held = []
        for p in range(2):
            q_p = _rot(pr[:, O_RQ + p * 128:O_RQ + (p + 1) * 128], cr, sr, first_half)
            k_p = _rot(pr[:, O_RK + p * 128:O_RK + (p + 1) * 128], cr, sr, first_half)
            v_p = _bf(pr[:, O_RV + p * 128:O_RV + (p + 1) * 128])
            k_pb = _bf(k_p)
            kbd = jnp.concatenate([jnp.where(lo_hi[h], k_pb, zero_bf) for h in range(2)], axis=0)
            vbd = jnp.concatenate([jnp.where(lo_hi[h], v_p, zero_bf) for h in range(2)], axis=0)
            sc_raw = _dot_nt(_bf(q_p), kbd)
            st = rstate_ref[p]
            inter = _dot(_bf(q_p * qdec_ref[p]), _bf(st))
            upd = _dot_tn(_bf(k_p * kdec_ref[p]), v_p)
            held.append((sc_raw, vbd, st, inter, upd))
        yield
        intras = []
        for p in range(2):
            sc_raw, vbd, st, inter, upd = held[p]
            intras.append(_dot(_bf(sc_raw * dmask_ref[p]), vbd) + inter)
            rstate_ref[p] = st * cdm_ref[p] + upd * rbm_ref[...]
        done.add(f"ret{sb}.state")
        yield
        r = jnp.concatenate(intras, axis=1)
        ms = _seg_mean(r * r, e64_ref)
        yield
        mix_refs[sb][:, ATTN_W:ATTN_W + RET_W] = _bf(r * lax.rsqrt(ms + EPS) * _silu(pr[:, O_RG:O_RG + 256]))
        done.add(f"ret{sb}.done")

    def gen_gla(sb):
        yield f"proj{sb}.gga"
        pr = proj_refs[sb]
        logits = _dot(_bf(pr[:, O_GA:O_GA + 128]), gw_ref[...])
        yield
        logits = logits + gb_ref[...]
        log_a = (jnp.minimum(logits, 0.0) - jnp.log(1.0 + jnp.exp(-jnp.abs(logits)))) * (LOG2E / GLA_GATE_NORMALIZER)
        la_hi, la_lo = _split_hi_lo(log_a)
        xr = _dot(tri_ref[...], jnp.concatenate([la_hi, la_lo], axis=1))
        yield
        yield f"proj{sb}.gqkv"
        bscr = bscr_ref.at[sb]
        bscr[...] = xr[:, :128] + xr[:, 128:]
        gq = pr[:, O_GQ:O_GQ + 128] * (GLA_DK ** -0.5)
        gk = pr[:, O_GK:O_GK + 128]
        gv = _bf(pr[:, O_GV:O_GV + 256])
        nch = BLK // GLA_C
        row = lax.broadcasted_iota(jnp.int32, (GLA_C, LANES), 0)

        def level_factor(c, lv):
            r0 = c * GLA_C
            b_c = bscr[r0:r0 + GLA_C, :]
            grp = 1 << lv
            if grp >= 8:
                mids = [bscr[r0 + g0 + grp // 2 - 1:r0 + g0 + grp // 2, :] for g0 in range(0, GLA_C, grp)]
                ref = jnp.concatenate([jnp.broadcast_to(m, (grp, LANES)) for m in mids], axis=0)
            elif grp == 4:
                r = row % 4
                ref = jnp.where(r == 0, pltpu.roll(b_c, GLA_C - 1, axis=0),
                                jnp.where(r == 1, b_c,
                                          jnp.where(r == 2, pltpu.roll(b_c, 1, axis=0), pltpu.roll(b_c, 2, axis=0))))
            else:
                ref = jnp.where(row % 2 == 1, pltpu.roll(b_c, 1, axis=0), b_c)
            return jnp.exp2(-jnp.abs(b_c - ref))

        saccs = [jnp.zeros((GLA_C, GLA_HEADS * GLA_C), jnp.float32) for _ in range(nch)]
        pending = []
        for lv in range(GLA_LEVELS + 1):
            raw = []
            for c in range(nch):
                r0 = c * GLA_C
                if lv == 0:
                    qt, kt = _bf(gq[r0:r0 + GLA_C]), _bf(gk[r0:r0 + GLA_C])
                else:
                    f = level_factor(c, lv)
                    qt, kt = _bf(gq[r0:r0 + GLA_C] * f), _bf(gk[r0:r0 + GLA_C] * f)
                kbd = jnp.concatenate([jnp.where(hgrp[h], kt, zero_bf) for h in range(GLA_HEADS)], axis=0)
                raw.append(_dot_nt(qt, kbd))
            pending.append((lv, raw))
            if len(pending) == GLA_LEVELS_PER_STAGE or lv == GLA_LEVELS:
                yield
                for lv_p, raw_p in pending:
                    for c in range(nch):
                        saccs[c] = saccs[c] + raw_p[c] * lmask_ref[lv_p]
                pending = []
        if sb > 0:
            yield f"gla{sb - 1}.state"
        g_parts = []
        st = gstate_ref[...]
        for c in range(nch):
            r0 = c * GLA_C
            v_c = gv[r0:r0 + GLA_C]
            vbd = jnp.concatenate([jnp.where(vhead[h], v_c, zero_bf) for h in range(GLA_HEADS)], axis=0)
            b_c = bscr[r0:r0 + GLA_C, :]
            b_last = bscr[r0 + GLA_C - 1:r0 + GLA_C, :]
            q_in = _bf(gq[r0:r0 + GLA_C] * jnp.exp2(b_c))
            o_c = _dot(_bf(saccs[c]), vbd) + _dot_nt(q_in, _bf(st))
            k_out = _bf(gk[r0:r0 + GLA_C] * jnp.exp2(b_last - b_c))
            upd = _dot_tn(v_c, k_out)
            yield
            g_parts.append(o_c)
            st = st * jnp.exp2(b_last) + upd * smask_ref[...]
        gstate_ref[...] = st
        done.add(f"gla{sb}.state")
        g = jnp.concatenate(g_parts, axis=0)
        ms = _seg_mean(g * g, e64_ref)
        yield
        gate_g = _silu(pr[:, O_GG:O_GG + 256])
        mix_refs[sb][:, ATTN_W + RET_W:] = _bf(g * lax.rsqrt(ms + EPS) * gng_ref[...] * gate_g)
        done.add(f"gla{sb}.done")

    def gen_out(og):
        sb0 = og * OGRP
        for sb in range(sb0, sb0 + OGRP):
            yield f"attn{sb}.done"
            yield f"ret{sb}.done"
            yield f"gla{sb}.done"
        rws = slice(sb0 * BLK, (sb0 + OGRP) * BLK)
        mix = mix_grp[sb0 // PGRP].at[pl.ds((sb0 % PGRP) * BLK, OGRP * BLK)]
        half = D_MODEL // 2
        y0 = _dot(mix[...], wout_ref[:, :half])
        yield
        y1 = _dot(mix[...], wout_ref[:, half:])
        yield
        ssq = jnp.sum(y0 * y0, axis=-1, keepdims=True) + jnp.sum(y1 * y1, axis=-1, keepdims=True)
        inv = lax.rsqrt(ssq * (1.0 / D_MODEL) + EPS)
        gp = mod_ref[2:3, :] * post_ref[...]
        o_ref[0, rws, :half] = x_ref[0, rws, :half] + (y0 * inv) * gp[:, :half]
        yield
        o_ref[0, rws, half:] = x_ref[0, rws, half:] + (y1 * inv) * gp[:, half:]

    gens = []
    for g in range(NGRP):
        for sb in range(g * PGRP, (g + 1) * PGRP):
            gens += [gen_gla(sb), gen_attn(sb), gen_ret(sb)]
        gens += [gen_out(og) for og in range(g * PGRP // OGRP, (g + 1) * PGRP // OGRP)]
        gens += [gen_proj(g)] + [gen_norm(sb) for sb in range(g * PGRP, (g + 1) * PGRP)]
    _run_interleaved(gens, done)


def _const_spec(shape):
    nd = len(shape)
    return pl.BlockSpec(shape, lambda b, t, s, _n=nd: (0,) * _n)


def _layer(l, x, mod, pre_gain, post_gain, w_in_b, w_out_b, sinks, gate_w_p, gate_b, gng, tables, consts):
    bsz, seq, d = x.shape
    tb = NSUB * BLK
    tok_spec = pl.BlockSpec((1, tb, LANES), lambda b, t, s: (b, t, 0))
    stacked = [pre_gain, post_gain, w_in_b, w_out_b, gate_w_p, gate_b, gng]

    def layer_spec(a):
        return pl.BlockSpec((None,) + a.shape[1:], lambda b, t, s, _n=a.ndim - 1: (l,) + (0,) * _n)

    in_specs = ([pl.BlockSpec((1, tb, d), lambda b, t, s: (b, t, 0)),
                 pl.BlockSpec((None, None, 3, d), lambda b, t, s: (l, b, 0, 0))]
                + [layer_spec(a) for a in stacked]
                + [tok_spec] * 4
                + [_const_spec(a.shape) for a in consts])
    scratch = ([pltpu.VMEM((PGRP * BLK, d), jnp.bfloat16)] * NGRP
               + [pltpu.VMEM((PGRP * BLK, N_PROJ), jnp.float32)] * NGRP
               + [pltpu.VMEM((PGRP * BLK, d), jnp.bfloat16)] * NGRP
               + [pltpu.VMEM((2, BLK, LANES), jnp.bfloat16),
                  pltpu.VMEM((2, BLK, LANES), jnp.bfloat16),
                  pltpu.VMEM((2, LANES, LANES), jnp.float32),
                  pltpu.VMEM((GLA_W, LANES), jnp.float32),
                  pltpu.VMEM((NSUB, BLK, LANES), jnp.float32)])
    grid_spec = pltpu.PrefetchScalarGridSpec(
        num_scalar_prefetch=1,
        grid=(bsz, seq // tb),
        in_specs=in_specs,
        out_specs=pl.BlockSpec((1, tb, d), lambda b, t, s: (b, t, 0)),
        scratch_shapes=scratch)
    return pl.pallas_call(
        _layer_kernel,
        out_shape=jax.ShapeDtypeStruct(x.shape, x.dtype),
        grid_spec=grid_spec,
        compiler_params=pltpu.CompilerParams(dimension_semantics=("arbitrary", "arbitrary"),
                                             vmem_limit_bytes=VMEM_LIMIT),
        name="hybrid_layer",
    )(sinks, x, mod, *stacked, *tables, *consts)


def kernel(x, c, positions, w_mod, b_mod, pre_norm_gain, post_norm_gain, w_in, attn_sinks, gla_gate_w,
           gla_gate_b, gla_norm_gain, w_out):
    depth = w_mod.shape[0]
    bsz = x.shape[0]
    f32, bf16 = jnp.float32, jnp.bfloat16

    rope_freq = ROPE_THETA ** (-jnp.arange(0, HEAD_DIM, 2, dtype=f32) / HEAD_DIM)
    ret_freq = 1.0 / (10000.0 ** jnp.linspace(0.0, 1.0, RET_DK // 2, dtype=f32))
    freq = jnp.concatenate([rope_freq, ret_freq, rope_freq, ret_freq]).reshape(1, LANES)
    mod, w_out_b, tables = _prologue(c, w_mod, b_mod, positions, freq, w_out)
    mod = mod.reshape(depth, bsz, 3, D_MODEL)

    consts = [jnp.asarray(_C["bias"]), jnp.asarray(_C["vones"], bf16), jnp.asarray(_C["dmask"]),
              jnp.asarray(_C["qdec"]), jnp.asarray(_C["kdec"]), jnp.asarray(_C["cdm"]), jnp.asarray(_C["rbm"]),
              jnp.asarray(_C["tri"], bf16), jnp.asarray(_C["lmask"]), jnp.asarray(_C["smask"]),
              jnp.asarray(_C["e64"], bf16)]
    w_in_b = w_in.astype(bf16)
    gate_w_p = jnp.pad(gla_gate_w, ((0, 0), (0, LANES - GLA_GATE_RANK), (0, 0))).astype(bf16)
    gng = jnp.tile(gla_norm_gain, (1, GLA_HEADS)).reshape(depth, 1, GLA_W)
    pre = pre_norm_gain.reshape(depth, 1, D_MODEL)
    post = post_norm_gain.reshape(depth, 1, D_MODEL)
    gate_b = gla_gate_b.reshape(depth, 1, LANES)

    for l in range(depth):
        x = _layer(l, x, mod, pre, post, w_in_b, w_out_b, attn_sinks[l], gate_w_p, gate_b, gng, tables, consts)
    return x
```

```python
import functools

import numpy as np
import jax
import jax.numpy as jnp
from jax import lax
from jax.experimental import pallas as pl
from jax.experimental.pallas import tpu as pltpu

D_MODEL = 1024
HEAD_DIM = 64
ATTN_HEADS = 8
ATTN_KV_HEADS = 2
WINDOW = 128
ROPE_THETA = 10000.0
RET_HEADS = 4
RET_DK = 64
RET_DV = 64
GLA_HEADS = 4
GLA_DK = 32
GLA_DV = 64
GLA_GATE_RANK = 16
GLA_GATE_NORMALIZER = 16.0
EPS = 1e-6

ATTN_W = ATTN_HEADS * HEAD_DIM
RET_W = RET_HEADS * RET_DV
GLA_W = GLA_HEADS * GLA_DV

LANES = 128
HALF = HEAD_DIM // 2
BLK = 128
NSUB = 8
PGRP = 8
NGRP = NSUB // PGRP
OGRP = 2
GLA_C = 64
GLA_LEVELS = 6
GLA_LEVELS_PER_STAGE = 7
NEG = -1e30
VMEM_LIMIT = 62 * 1024 * 1024
LOG2E = 1.4426950408889634

O_AQ, O_AK, O_AV, O_AG = 0, 512, 640, 768
O_RQ, O_RK, O_RV, O_RG = 1280, 1536, 1792, 2048
O_GQ, O_GK, O_GV, O_GG, O_GA = 2304, 2432, 2560, 2816, 3072
D_IN = O_GA + GLA_GATE_RANK
N_PROJ = 3200
PROJ_SEGMENTS = (("gga", O_GG, D_IN), ("gqkv", O_GQ, O_GG), ("akv", O_AK, O_AG), ("aq", O_AQ, O_AK),
                 ("rqk", O_RQ, O_RV), ("rvg", O_RV, O_GQ), ("ag", O_AG, O_RQ))


def _build_constants():
    f32 = np.float32
    lane = np.arange(LANES)
    grp = lane // HEAD_DIM

    i = np.arange(BLK)[:, None]
    j = np.arange(2 * BLK)[None, :]
    rel = i + BLK - j
    ok = (rel >= 0) & (rel < WINDOW)
    bias = np.stack([np.where(ok & (j >= BLK), 0.0, NEG), np.where(ok, 0.0, NEG)]).astype(f32)
    vones = np.zeros((2 * 2 * BLK, LANES), f32)
    for g in range(2):
        vones[g * 2 * BLK:(g + 1) * 2 * BLK] = (lane // HEAD_DIM == g)[None, :]

    kscale = RET_DK ** -0.5
    log_g = np.log(1.0 - 2.0 ** (-5.0 - np.arange(RET_HEADS, dtype=np.float64)))
    idx = np.arange(BLK, dtype=np.float64)
    diff = idx[:, None] - idx[None, :]
    dmask = np.zeros((2, BLK, 2 * BLK), f32)
    qdec = np.zeros((2, BLK, LANES), f32)
    kdec = np.zeros((2, BLK, LANES), f32)
    cdm = np.zeros((2, LANES, LANES), f32)
    rbm = ((np.arange(LANES)[:, None] // RET_DK) == (np.arange(LANES)[None, :] // RET_DV)).astype(f32)
    for p in range(2):
        for hh in range(2):
            lg = log_g[2 * p + hh]
            dmask[p, :, hh * BLK:(hh + 1) * BLK] = kscale * np.where(diff >= 0, np.exp(lg * np.maximum(diff, 0.0)), 0.0)
        lg_lane = log_g[2 * p + grp]
        qdec[p] = np.exp(lg_lane[None, :] * (idx[:, None] + 1.0))
        kdec[p] = kscale * np.exp(lg_lane[None, :] * (BLK - 1.0 - idx[:, None]))
        cdm[p] = np.exp(lg_lane * BLK)[:, None] * rbm

    t = np.arange(BLK)
    tri = (((t[:, None] // GLA_C) == (t[None, :] // GLA_C)) & (t[None, :] <= t[:, None])).astype(f32)
    ci = np.arange(GLA_C)[:, None]
    cj = np.arange(GLA_C)[None, :]
    x = ci ^ cj
    lvl = np.where(cj > ci, -1, np.where(x == 0, 0, np.floor(np.log2(np.maximum(x, 1))).astype(np.int64) + 1))
    lmask = np.stack([np.tile((lvl == lv).astype(f32), (1, GLA_HEADS)) for lv in range(GLA_LEVELS + 1)])
    smask = ((np.arange(GLA_W)[:, None] // GLA_DV) == (np.arange(LANES)[None, :] // GLA_DK)).astype(f32)
    e64 = ((np.arange(RET_W)[:, None] // RET_DV) == (np.arange(RET_W)[None, :] // RET_DV)).astype(f32) / RET_DV

    return dict(bias=bias, vones=vones, dmask=dmask, qdec=qdec, kdec=kdec, cdm=cdm, rbm=rbm, tri=tri,
                lmask=lmask, smask=smask, e64=e64)


_C = _build_constants()


def _dot(a, b):
    return jnp.dot(a, b, preferred_element_type=jnp.float32)


def _dot_nt(a, b):
    return lax.dot_general(a, b, (((1,), (1,)), ((), ())), preferred_element_type=jnp.float32)


def _dot_tn(a, b):
    return lax.dot_general(a, b, (((0,), (0,)), ((), ())), preferred_element_type=jnp.float32)


def _bf(x):
    return x.astype(jnp.bfloat16)


def _split_hi_lo(x):
    hi = _bf(x)
    lo = _bf(x - hi.astype(jnp.float32))
    return hi, lo


def _silu(x):
    half = 0.5 * x
    return half + half * jnp.tanh(half)


def _run_interleaved(gens, done):
    active = [[g, None] for g in gens]
    while active:
        progressed = False
        for item in list(active):
            if item[1] is not None and item[1] not in done:
                continue
            progressed = True
            try:
                need = next(item[0])
                while need is not None and need in done:
                    need = next(item[0])
                item[1] = need
            except StopIteration:
                active.remove(item)
        assert progressed, [item[1] for item in active]


MOD_TK = 256
TRIG_T = 512


def _mod_step(c_ref, w_ref, b_ref, o_ref, first):
    @pl.when(first)
    def _():
        o_ref[0] = jnp.broadcast_to(b_ref[0], o_ref.shape[1:])

    a = _silu(c_ref[...])
    a_hi, a_lo = _split_hi_lo(a)
    w_hi, w_lo = _split_hi_lo(w_ref[0])
    n = a.shape[0]
    both = _dot(jnp.concatenate([a_hi, a_lo], axis=0), w_hi)
    o_ref[0] += both[:n] + both[n:] + _dot(a_hi, w_lo)


def _trig_tile(pos_ref, freq_ref, out_refs, row0):
    ca_ref, sa_ref, cr_ref, sr_ref = out_refs
    half = TRIG_T // 2
    lane = lax.broadcasted_iota(jnp.int32, (half, LANES), 1)
    pos_a = pos_ref[row0:row0 + half, :]
    pos_b = pos_ref[row0 + half:row0 + TRIG_T, :]
    ang = jnp.where(lane < 2 * HALF, pos_a, pos_b).astype(jnp.float32) * freq_ref[...]
    c = jnp.cos(ang)
    s = jnp.sin(ang)
    quarter = lane // HALF
    sign = jnp.where((lane % HEAD_DIM) < HALF, -1.0, 1.0)
    rc = [c] + [pltpu.roll(c, k * HALF, axis=1) for k in (1, 2, 3)]
    rs = [s] + [pltpu.roll(s, k * HALF, axis=1) for k in (1, 2, 3)]

    def spread(r, src):
        out = r[(0 - src) % 4]
        for q in (1, 2, 3):
            out = jnp.where(quarter == q, r[(q - src) % 4], out)
        return out

    for tok, rws in ((0, slice(row0, row0 + half)), (1, slice(row0 + half, row0 + TRIG_T))):
        ca_ref[rws, :] = spread(rc, 2 * tok)
        sa_ref[rws, :] = spread(rs, 2 * tok) * sign
        cr_ref[rws, :] = spread(rc, 2 * tok + 1)
        sr_ref[rws, :] = spread(rs, 2 * tok + 1) * sign


def _prologue_kernel(c_ref, w_ref, b_ref, pos_ref, freq_ref, wo_ref, mod_ref, wob_ref, *table_refs, k_tiles, tiles):
    _mod_step(c_ref, w_ref, b_ref, mod_ref, pl.program_id(0) % k_tiles == 0)
    wob_ref[...] = _bf(wo_ref[...])
    for j in range(tiles):
        _trig_tile(pos_ref, freq_ref, table_refs, j * TRIG_T)


def _prologue(c, w_mod, b_mod, positions, freq, w_out):
    depth, d, n = w_mod.shape
    bsz, seq = positions.shape
    k_tiles = d // MOD_TK
    steps = depth * k_tiles
    tiles = (bsz * seq) // (TRIG_T * steps)
    rows = tiles * TRIG_T
    assert rows * steps == bsz * seq
    table = jax.ShapeDtypeStruct((bsz * seq, LANES), jnp.float32)
    tspec = pl.BlockSpec((rows, LANES), lambda i: (i, 0))
    wo_spec = pl.BlockSpec((1, w_out.shape[1] // k_tiles, w_out.shape[2]), lambda i: (i // k_tiles, i % k_tiles, 0))
    outs = pl.pallas_call(
        functools.partial(_prologue_kernel, k_tiles=k_tiles, tiles=tiles),
        out_shape=(jax.ShapeDtypeStruct((depth, bsz, n), jnp.float32),
                   jax.ShapeDtypeStruct(w_out.shape, jnp.bfloat16), table, table, table, table),
        grid=(steps,),
        in_specs=[pl.BlockSpec((bsz, MOD_TK), lambda i: (0, i % k_tiles)),
                  pl.BlockSpec((1, MOD_TK, n), lambda i: (i // k_tiles, i % k_tiles, 0)),
                  pl.BlockSpec((1, 1, n), lambda i: (i // k_tiles, 0, 0)),
                  pl.BlockSpec((rows, 1), lambda i: (i, 0)),
                  pl.BlockSpec((1, LANES), lambda i: (0, 0)),
                  wo_spec],
        out_specs=(pl.BlockSpec((1, bsz, n), lambda i: (i // k_tiles, 0, 0)), wo_spec, tspec, tspec, tspec, tspec),
        compiler_params=pltpu.CompilerParams(dimension_semantics=("arbitrary",), vmem_limit_bytes=VMEM_LIMIT),
        name="prologue",
    )(c, w_mod, b_mod.reshape(depth, 1, n), positions.reshape(bsz * seq, 1), freq, w_out)
    return outs[0], outs[1], tuple(t.reshape(bsz, seq, LANES) for t in outs[2:])


def _seg_mean(x, e_ref):
    return _dot(_bf(x), e_ref[...])


def _rot(v, c, s, first_half):
    partner = jnp.where(first_half, pltpu.roll(v, LANES - HALF, axis=1), pltpu.roll(v, HALF, axis=1))
    return v * c + partner * s


def _layer_kernel(sinks_ref,
                  x_ref, mod_ref, pre_ref, post_ref, win_ref, wout_ref, gw_ref, gb_ref, gng_ref,
                  ca_ref, sa_ref, cr_ref, sr_ref,
                  bias_ref, vones_ref, dmask_ref, qdec_ref, kdec_ref, cdm_ref, rbm_ref,
                  tri_ref, lmask_ref, smask_ref, e64_ref,
                  o_ref, *scratch):
    hb_grp = scratch[0:NGRP]
    proj_grp = scratch[NGRP:2 * NGRP]
    mix_grp = scratch[2 * NGRP:3 * NGRP]
    kprev_ref, vprev_ref, rstate_ref, gstate_ref, bscr_ref = scratch[3 * NGRP:]
    t = pl.program_id(1)

    def sub_view(refs, sb):
        return refs[sb // PGRP].at[pl.ds((sb % PGRP) * BLK, BLK)]

    hb_refs = [sub_view(hb_grp, sb) for sb in range(NSUB)]
    proj_refs = [sub_view(proj_grp, sb) for sb in range(NSUB)]
    mix_refs = [sub_view(mix_grp, sb) for sb in range(NSUB)]

    @pl.when(t == 0)
    def _():
        kprev_ref[...] = jnp.zeros_like(kprev_ref)
        vprev_ref[...] = jnp.zeros_like(vprev_ref)
        rstate_ref[...] = jnp.zeros_like(rstate_ref)
        gstate_ref[...] = jnp.zeros_like(gstate_ref)
        for g in range(NGRP):
            proj_grp[g][:, O_GA:] = jnp.zeros((PGRP * BLK, N_PROJ - O_GA), jnp.float32)

    lane = lax.broadcasted_iota(jnp.int32, (1, LANES), 1)
    lane2 = lax.broadcasted_iota(jnp.int32, (1, GLA_W), 1)
    zero_bf = jnp.zeros((), jnp.bfloat16)
    first_half = (lane % HEAD_DIM) < HALF
    lo_hi = [lane // HEAD_DIM == g for g in range(2)]
    hgrp = [lane // GLA_DK == h for h in range(GLA_HEADS)]
    vhead = [lane2 // GLA_DV == h for h in range(GLA_HEADS)]

    def rows(sb):
        return slice(sb * BLK, (sb + 1) * BLK)

    done = set()

    def gen_norm(sb):
        if sb >= PGRP:
            yield f"projgrp{sb // PGRP - 1}.started"
        x = x_ref[0, rows(sb), :]
        inv = lax.rsqrt(jnp.mean(x * x, axis=-1, keepdims=True) + EPS)
        hb_refs[sb][...] = _bf((x * inv) * (pre_ref[...] * (1.0 + mod_ref[1:2, :])) + mod_ref[0:1, :])
        done.add(f"norm{sb}")

    def gen_proj(g):
        subs = range(g * PGRP, (g + 1) * PGRP)
        for sb in subs:
            yield f"norm{sb}"
        if g > 0:
            yield f"projgrp{g - 1}.all"
        done.add(f"projgrp{g}.started")
        for name, a, b in PROJ_SEGMENTS:
            proj_grp[g][:, a:b] = _dot(hb_grp[g][...], win_ref[:, a:b])
            done.update(f"proj{sb}.{name}" for sb in subs)
            yield
        done.add(f"projgrp{g}.all")

    def gen_attn(sb):
        yield f"proj{sb}.akv"
        if sb > 0:
            yield f"attn{sb - 1}.kv"
        pr = proj_refs[sb]
        ca, sa = ca_ref[0, rows(sb), :], sa_ref[0, rows(sb), :]
        k_new = _rot(pr[:, O_AK:O_AK + 128], ca, sa, first_half) * (HEAD_DIM ** -0.5 * LOG2E)
        v_new = pr[:, O_AV:O_AV + 128]
        k_sw = pltpu.roll(k_new, HEAD_DIM, axis=1)
        v_sw = pltpu.roll(v_new, HEAD_DIM, axis=1)
        k2, v2 = [], []
        for g in range(2):
            k_g = _bf(jnp.where(lo_hi[g], k_new, k_sw))
            v_g = _bf(jnp.where(lo_hi[g], v_new, v_sw))
            kcat = jnp.concatenate([kprev_ref[g], k_g], axis=0)
            vcat = jnp.concatenate([vprev_ref[g], v_g], axis=0)
            kprev_ref[g] = k_g
            vprev_ref[g] = v_g
            k2.append(jnp.concatenate([jnp.where(lo_hi[h], kcat, zero_bf) for h in range(2)], axis=0))
            v2_g = jnp.concatenate([jnp.where(lo_hi[h], vcat, zero_bf) for h in range(2)], axis=0)
            v2.append(jnp.concatenate([v2_g, vones_ref[...]], axis=1))
        done.add(f"attn{sb}.kv")
        if sb == 0:
            bias = bias_ref[jnp.minimum(t, 1)]
        else:
            bias = bias_ref[1]

        def scores(g):
            q = [_rot(pr[:, O_AQ + m * 128:O_AQ + (m + 1) * 128], ca, sa, first_half) for m in (2 * g, 2 * g + 1)]
            return _dot_nt(_bf(jnp.concatenate(q, axis=0)), k2[g])

        yield f"proj{sb}.aq"
        s_g = scores(0)
        yield
        for g in range(2):
            ps, mxs = [], []
            for j in range(2):
                row_p, row_mx = [], []
                for h in range(2):
                    s = s_g[j * BLK:(j + 1) * BLK, h * 2 * BLK:(h + 1) * 2 * BLK] + bias
                    mx = jnp.maximum(jnp.max(s, axis=-1, keepdims=True), sinks_ref[2 * (2 * g + j) + h] * LOG2E)
                    row_p.append(_bf(jnp.exp2(s - mx)))
                    row_mx.append(mx)
                ps.append(jnp.concatenate(row_p, axis=1))
                mxs.append(row_mx)
            pv = _dot(jnp.concatenate(ps, axis=0), v2[g])
            if g == 0:
                s_g = scores(1)
            yield
            yield f"proj{sb}.ag"
            for j in range(2):
                m = 2 * g + j
                pv_m = pv[j * BLK:(j + 1) * BLK]
                mx_l = jnp.where(lo_hi[0], mxs[j][0], mxs[j][1])
                sink_l = jnp.where(lo_hi[0], sinks_ref[2 * m], sinks_ref[2 * m + 1]) * LOG2E
                den = pv_m[:, 128:] + jnp.exp2(sink_l - mx_l)
                gate_m = _silu(pr[:, O_AG + m * 128:O_AG + (m + 1) * 128])
                mix_refs[sb][:, m * 128:(m + 1) * 128] = _bf(pv_m[:, :128] / den * gate_m)
        done.add(f"attn{sb}.done")

    def gen_ret(sb):
        yield f"proj{sb}.rqk"
        yield f"proj{sb}.rvg"
        if sb > 0:
            yield f"ret{sb - 1}.state"
        pr = proj_refs[sb]
        cr, sr = cr_ref[0, rows(sb), :], sr_ref[0, rows(sb), :]
        held = []
        for p in range(2):
            q_p = _rot(pr[:, O_RQ + p * 128:O_RQ + (p + 1) * 128], cr, sr, first_half)
            k_p = _rot(pr[:, O_RK + p * 128:O_RK + (p + 1) * 128], cr, sr, first_half)
            v_p = _bf(pr[:, O_RV + p * 128:O_RV + (p + 1) * 128])
            k_pb = _bf(k_p)
            kbd = jnp.concatenate([jnp.where(lo_hi[h], k_pb, zero_bf) for h in range(2)], axis=0)
            vbd = jnp.concatenate([jnp.where(lo_hi[h], v_p, zero_bf) for h in range(2)], axis=0)
            sc_raw = _dot_nt(_bf(q_p), kbd)
            st = rstate_ref[p]
            inter = _dot(_bf(q_p * qdec_ref[p]), _bf(st))
            upd = _dot_tn(_bf(k_p * kdec_ref[p]), v_p)
            held.append((sc_raw, vbd, st, inter, upd))
        yield
        intras = []
        for p in range(2):
            sc_raw, vbd, st, inter, upd = held[p]
            intras.append(_dot(_bf(sc_raw * dmask_ref[p]), vbd) + inter)
            rstate_ref[p] = st * cdm_ref[p] + upd * rbm_ref[...]
        done.add(f"ret{sb}.state")
        yield
        r = jnp.concatenate(intras, axis=1)
        ms = _seg_mean(r * r, e64_ref)
        yield
        mix_refs[sb][:, ATTN_W:ATTN_W + RET_W] = _bf(r * lax.rsqrt(ms + EPS) * _silu(pr[:, O_RG:O_RG + 256]))
        done.add(f"ret{sb}.done")

    def gen_gla(sb):
        yield f"proj{sb}.gga"
        pr = proj_refs[sb]
        logits = _dot(_bf(pr[:, O_GA:O_GA + 128]), gw_ref[...])
        yield
        logits = logits + gb_ref[...]
        log_a = (jnp.minimum(logits, 0.0) - jnp.log(1.0 + jnp.exp(-jnp.abs(logits)))) * (LOG2E / GLA_GATE_NORMALIZER)
        la_hi, la_lo = _split_hi_lo(log_a)
        xr = _dot(tri_ref[...], jnp.concatenate([la_hi, la_lo], axis=1))
        yield
        yield f"proj{sb}.gqkv"
        bscr = bscr_ref.at[sb]
        bscr[...] = xr[:, :128] + xr[:, 128:]
        gq = pr[:, O_GQ:O_GQ + 128] * (GLA_DK ** -0.5)
        gk = pr[:, O_GK:O_GK + 128]
        gv = _bf(pr[:, O_GV:O_GV + 256])
        nch = BLK // GLA_C
        row = lax.broadcasted_iota(jnp.int32, (GLA_C, LANES), 0)

        def level_factor(c, lv):
            r0 = c * GLA_C
            b_c = bscr[r0:r0 + GLA_C, :]
            grp = 1 << lv
            if grp >= 8:
                mids = [bscr[r0 + g0 + grp // 2 - 1:r0 + g0 + grp // 2, :] for g0 in range(0, GLA_C, grp)]
                ref = jnp.concatenate([jnp.broadcast_to(m, (grp, LANES)) for m in mids], axis=0)
            elif grp == 4:
                r = row % 4
                ref = jnp.where(r == 0, pltpu.roll(b_c, GLA_C - 1, axis=0),
                                jnp.where(r == 1, b_c,
                                          jnp.where(r == 2, pltpu.roll(b_c, 1, axis=0), pltpu.roll(b_c, 2, axis=0))))
            else:
                ref = jnp.where(row % 2 == 1, pltpu.roll(b_c, 1, axis=0), b_c)
            return jnp.exp2(-jnp.abs(b_c - ref))

        saccs = [jnp.zeros((GLA_C, GLA_HEADS * GLA_C), jnp.float32) for _ in range(nch)]
        pending = []
        for lv in range(GLA_LEVELS + 1):
            raw = []
            for c in range(nch):
                r0 = c * GLA_C
                if lv == 0:
                    qt, kt = _bf(gq[r0:r0 + GLA_C]), _bf(gk[r0:r0 + GLA_C])
                else:
                    f = level_factor(c, lv)
                    qt, kt = _bf(gq[r0:r0 + GLA_C] * f), _bf(gk[r0:r0 + GLA_C] * f)
                kbd = jnp.concatenate([jnp.where(hgrp[h], kt, zero_bf) for h in range(GLA_HEADS)], axis=0)
                raw.append(_dot_nt(qt, kbd))
            pending.append((lv, raw))
            if len(pending) == GLA_LEVELS_PER_STAGE or lv == GLA_LEVELS:
                yield
                for lv_p, raw_p in pending:
                    for c in range(nch):
                        saccs[c] = saccs[c] + raw_p[c] * lmask_ref[lv_p]
                pending = []
        if sb > 0:
            yield f"gla{sb - 1}.state"
        g_parts = []
        st = gstate_ref[...]
        for c in range(nch):
            r0 = c * GLA_C
            v_c = gv[r0:r0 + GLA_C]
            vbd = jnp.concatenate([jnp.where(vhead[h], v_c, zero_bf) for h in range(GLA_HEADS)], axis=0)
            b_c = bscr[r0:r0 + GLA_C, :]
            b_last = bscr[r0 + GLA_C - 1:r0 + GLA_C, :]
            q_in = _bf(gq[r0:r0 + GLA_C] * jnp.exp2(b_c))
            o_c = _dot(_bf(saccs[c]), vbd) + _dot_nt(q_in, _bf(st))
            k_out = _bf(gk[r0:r0 + GLA_C] * jnp.exp2(b_last - b_c))
            upd = _dot_tn(v_c, k_out)
            yield
            g_parts.append(o_c)
            st = st * jnp.exp2(b_last) + upd * smask_ref[...]
        gstate_ref[...] = st
        done.add(f"gla{sb}.state")
        g = jnp.concatenate(g_parts, axis=0)
        ms = _seg_mean(g * g, e64_ref)
        yield
        gate_g = _silu(pr[:, O_GG:O_GG + 256])
        mix_refs[sb][:, ATTN_W + RET_W:] = _bf(g * lax.rsqrt(ms + EPS) * gng_ref[...] * gate_g)
        done.add(f"gla{sb}.done")

    def gen_out(og):
        sb0 = og * OGRP
        for sb in range(sb0, sb0 + OGRP):
            yield f"attn{sb}.done"
            yield f"ret{sb}.done"
            yield f"gla{sb}.done"
        rws = slice(sb0 * BLK, (sb0 + OGRP) * BLK)
        mix = mix_grp[sb0 // PGRP].at[pl.ds((sb0 % PGRP) * BLK, OGRP * BLK)]
        half = D_MODEL // 2
        y0 = _dot(mix[...], wout_ref[:, :half])
        yield
        y1 = _dot(mix[...], wout_ref[:, half:])
        yield
        ssq = jnp.sum(y0 * y0, axis=-1, keepdims=True) + jnp.sum(y1 * y1, axis=-1, keepdims=True)
        inv = lax.rsqrt(ssq * (1.0 / D_MODEL) + EPS)
        gp = mod_ref[2:3, :] * post_ref[...]
        o_ref[0, rws, :half] = x_ref[0, rws, :half] + (y0 * inv) * gp[:, :half]
        yield
        o_ref[0, rws, half:] = x_ref[0, rws, half:] + (y1 * inv) * gp[:, half:]

    gens = []
    for g in range(NGRP):
        for sb in range(g * PGRP, (g + 1) * PGRP):
            gens += [gen_gla(sb), gen_attn(sb), gen_ret(sb)]
        gens += [gen_out(og) for og in range(g * PGRP // OGRP, (g + 1) * PGRP // OGRP)]
        gens += [gen_proj(g)] + [gen_norm(sb) for sb in range(g * PGRP, (g + 1) * PGRP)]
    _run_interleaved(gens, done)


def _const_spec(shape):
    nd = len(shape)
    return pl.BlockSpec(shape, lambda b, t, s, _n=nd: (0,) * _n)


def _layer(l, x, mod, pre_gain, post_gain, w_in_b, w_out_b, sinks, gate_w_p, gate_b, gng, tables, consts):
    bsz, seq, d = x.shape
    tb = NSUB * BLK
    tok_spec = pl.BlockSpec((1, tb, LANES), lambda b, t, s: (b, t, 0))
    stacked = [pre_gain, post_gain, w_in_b, w_out_b, gate_w_p, gate_b, gng]

    def layer_spec(a):
        return pl.BlockSpec((None,) + a.shape[1:], lambda b, t, s, _n=a.ndim - 1: (l,) + (0,) * _n)

    in_specs = ([pl.BlockSpec((1, tb, d), lambda b, t, s: (b, t, 0)),
                 pl.BlockSpec((None, None, 3, d), lambda b, t, s: (l, b, 0, 0))]
                + [layer_spec(a) for a in stacked]
                + [tok_spec] * 4
                + [_const_spec(a.shape) for a in consts])
    scratch = ([pltpu.VMEM((PGRP * BLK, d), jnp.bfloat16)] * NGRP
               + [pltpu.VMEM((PGRP * BLK, N_PROJ), jnp.float32)] * NGRP
               + [pltpu.VMEM((PGRP * BLK, d), jnp.bfloat16)] * NGRP
               + [pltpu.VMEM((2, BLK, LANES), jnp.bfloat16),
                  pltpu.VMEM((2, BLK, LANES), jnp.bfloat16),
                  pltpu.VMEM((2, LANES, LANES), jnp.float32),
                  pltpu.VMEM((GLA_W, LANES), jnp.float32),
                  pltpu.VMEM((NSUB, BLK, LANES), jnp.float32)])
    grid_spec = pltpu.PrefetchScalarGridSpec(
        num_scalar_prefetch=1,
        grid=(bsz, seq // tb),
        in_specs=in_specs,
        out_specs=pl.BlockSpec((1, tb, d), lambda b, t, s: (b, t, 0)),
        scratch_shapes=scratch)
    return pl.pallas_call(
        _layer_kernel,
        out_shape=jax.ShapeDtypeStruct(x.shape, x.dtype),
        grid_spec=grid_spec,
        compiler_params=pltpu.CompilerParams(dimension_semantics=("arbitrary", "arbitrary"),
                                             vmem_limit_bytes=VMEM_LIMIT),
        name="hybrid_layer",
    )(sinks, x, mod, *stacked, *tables, *consts)


def kernel(x, c, positions, w_mod, b_mod, pre_norm_gain, post_norm_gain, w_in, attn_sinks, gla_gate_w,
           gla_gate_b, gla_norm_gain, w_out):
    depth = w_mod.shape[0]
    bsz = x.shape[0]
    f32, bf16 = jnp.float32, jnp.bfloat16

    rope_freq = ROPE_THETA ** (-jnp.arange(0, HEAD_DIM, 2, dtype=f32) / HEAD_DIM)
    ret_freq = 1.0 / (10000.0 ** jnp.linspace(0.0, 1.0, RET_DK // 2, dtype=f32))
    freq = jnp.concatenate([rope_freq, ret_freq, rope_freq, ret_freq]).reshape(1, LANES)
    mod, w_out_b, tables = _prologue(c, w_mod, b_mod, positions, freq, w_out)
    mod = mod.reshape(depth, bsz, 3, D_MODEL)

    consts = [jnp.asarray(_C["bias"]), jnp.asarray(_C["vones"], bf16), jnp.asarray(_C["dmask"]),
              jnp.asarray(_C["qdec"]), jnp.asarray(_C["kdec"]), jnp.asarray(_C["cdm"]), jnp.asarray(_C["rbm"]),
              jnp.asarray(_C["tri"], bf16), jnp.asarray(_C["lmask"]), jnp.asarray(_C["smask"]),
              jnp.asarray(_C["e64"], bf16)]
    w_in_b = w_in.astype(bf16)
    gate_w_p = jnp.pad(gla_gate_w, ((0, 0), (0, LANES - GLA_GATE_RANK), (0, 0))).astype(bf16)
    gng = jnp.tile(gla_norm_gain, (1, GLA_HEADS)).reshape(depth, 1, GLA_W)
    pre = pre_norm_gain.reshape(depth, 1, D_MODEL)
    post = post_norm_gain.reshape(depth, 1, D_MODEL)
    gate_b = gla_gate_b.reshape(depth, 1, LANES)

    for l in range(depth):
        x = _layer(l, x, mod, pre, post, w_in_b, w_out_b, attn_sinks[l], gate_w_p, gate_b, gng, tables, consts)
    return x
```
